```python
import math
import jax, jax.numpy as jnp
from jax import lax
import numpy as np


D_MODEL = 1024
BATCH = 16
SEQ = 4096
DEPTH = 4

CONV_DIM = 512
CONV_WIDTH = 3
NSA_HEADS = 8
NSA_KV_GROUPS = 2
NSA_HPG = NSA_HEADS // NSA_KV_GROUPS
NSA_HEAD_DIM = 64
NSA_DIM = NSA_HEADS * NSA_HEAD_DIM
CMP_LEN = 32
CMP_STRIDE = 16
CMP_HIDDEN = 256
SLC_LEN = 64
SLC_TOPN = 8
SLC_FORCE_BONUS = 1e6
WINDOW = 512
NSA_QBLOCK = 64
GDN_HEADS = 4
GDN_HEAD_DIM = 128
GDN_DIM = GDN_HEADS * GDN_HEAD_DIM
GDN_CONV = 4
GDN_CHUNK = 64
N_BRANCH = 3
BRANCH_DIM = 512
IN_SPLITS = (3 * CONV_DIM, NSA_DIM, 6 * NSA_KV_GROUPS * NSA_HEAD_DIM, 3 * NSA_HEADS, 3 * GDN_DIM, GDN_DIM, GDN_HEADS, GDN_HEADS)
D_IN = 3 * CONV_DIM + NSA_DIM + 6 * NSA_KV_GROUPS * NSA_HEAD_DIM + 3 * NSA_HEADS + 4 * GDN_DIM + 2 * GDN_HEADS
MOE_GROUPS = 4
EXPERTS_PER_GROUP = 8
N_EXPERTS = MOE_GROUPS * EXPERTS_PER_GROUP
TOPK_IN_GROUP = 2
EXPERT_FF = 512
MOE_BLOCK = 256
RMS_EPS = 1e-6
NEG_INF = -1e30

kernel_name = 'hybrid_conv_nsa_gdn_hmoe'


def split_last(u, sizes):
    out, start = [], 0
    for n in sizes:
        out.append(u[..., start:start + n])
        start += n
    return out


def rms_norm(x, gain):
    xf = x.astype(jnp.float32)
    y = xf * lax.rsqrt(jnp.mean(xf * xf, axis=-1, keepdims=True) + RMS_EPS)
    return (y * gain.astype(jnp.float32)).astype(x.dtype)


def l2_norm(x):
    return x * lax.rsqrt(jnp.sum(x * x, axis=-1, keepdims=True) + RMS_EPS)


def causal_depthwise_conv(x, w):
    k, c = w.shape
    return lax.conv_general_dilated(x, w[:, None, :].astype(x.dtype), window_strides=(1,), padding=[(k - 1, 0)], dimension_numbers=('NWC', 'WIO', 'NWC'), feature_group_count=c)


def alibi_slopes(n):
    return 2.0 ** (-8.0 * jnp.arange(1, n + 1, dtype=jnp.float32) / n)


def short_conv_mixer(u_a, conv_w):
    b_gate, c_gate, x_in = split_last(u_a, (CONV_DIM, CONV_DIM, CONV_DIM))
    return b_gate * causal_depthwise_conv(c_gate * x_in, conv_w)


def nsa_compress(kv, pe, w1, w2):
    b, s, g, d = kv.shape
    n_cmp = (s - CMP_LEN) // CMP_STRIDE + 1
    idx = jnp.arange(n_cmp)[:, None] * CMP_STRIDE + jnp.arange(CMP_LEN)[None, :]
    blocks = kv[:, idx] + pe[:, None, :]
    flat = jnp.moveaxis(blocks, 3, 2).reshape(b, n_cmp, g, CMP_LEN * d)
    return jax.nn.gelu(flat @ w1) @ w2


def nsa_mixer(u_q, u_kv, u_g, qk_gain, cmp_pe, cmp_w1, cmp_w2):
    b, s, _ = u_q.shape
    g, hpg, d, qbl = NSA_KV_GROUPS, NSA_HPG, NSA_HEAD_DIM, NSA_QBLOCK
    f32 = jnp.float32
    scale = d ** -0.5
    q = rms_norm(u_q.reshape(b, s, g, hpg, d), qk_gain[0])
    kv = u_kv.reshape(b, s, 6, g, d)
    k_c = rms_norm(nsa_compress(kv[:, :, 0], cmp_pe[0], cmp_w1[0], cmp_w2[0]), qk_gain[1])
    v_c = nsa_compress(kv[:, :, 1], cmp_pe[1], cmp_w1[1], cmp_w2[1])
    n_cmp = k_c.shape[1]
    n_slc = s // SLC_LEN
    top_n = min(SLC_TOPN, n_slc)
    k_s = rms_norm(kv[:, :, 2], qk_gain[2]).reshape(b, n_slc, SLC_LEN, g, d).transpose(0, 3, 1, 2, 4)
    v_s = kv[:, :, 3].reshape(b, n_slc, SLC_LEN, g, d).transpose(0, 3, 1, 2, 4)
    pad = ((0, 0), (WINDOW, 0), (0, 0), (0, 0))
    k_w = jnp.pad(rms_norm(kv[:, :, 4], qk_gain[3]), pad)
    v_w = jnp.pad(kv[:, :, 5], pad)
    gates = jax.nn.sigmoid(u_g.astype(f32)).reshape(b, s, 3, g, hpg)
    slopes = alibi_slopes(NSA_HEADS).reshape(g, hpg)
    cmp_lo = jnp.arange(n_cmp) * CMP_STRIDE
    cmp_end = cmp_lo + CMP_LEN - 1
    slc_lo = jnp.arange(n_slc) * SLC_LEN
    overlap = ((cmp_lo[:, None] < slc_lo[None, :] + SLC_LEN) & (cmp_lo[:, None] + CMP_LEN > slc_lo[None, :])).astype(f32)
    n_qb = s // qbl
    q_blocks = jnp.moveaxis(q.reshape(b, n_qb, qbl, g, hpg, d), 1, 0)
    g_blocks = jnp.moveaxis(gates.reshape(b, n_qb, qbl, 3, g, hpg), 1, 0)
    bi = jnp.arange(b)[:, None, None, None]
    gi = jnp.arange(g)[None, :, None, None]
    j_idx = jnp.arange(n_slc)

    def query_block(args):
        qb, gb, blk = args
        q0 = blk * qbl
        t = q0 + jnp.arange(qbl)
        dist_c = (t[:, None] - cmp_end[None, :]).astype(f32)
        ok_c = dist_c >= 0
        s_c = jnp.einsum('bqghd,bcgd->bghqc', qb, k_c).astype(f32) * scale
        s_c = jnp.where(ok_c, s_c - slopes[:, :, None, None] * dist_c, NEG_INF)
        p_c = jax.nn.softmax(s_c, axis=-1) * jnp.any(ok_c, axis=-1)[:, None].astype(f32)
        o_c = jnp.einsum('bghqc,bcgd->bqghd', p_c.astype(v_c.dtype), v_c)
        imp = jnp.einsum('bghqc,cj->bgqj', p_c, overlap)
        jt = (t // SLC_LEN)[:, None]
        forced = (j_idx[None, :] == 0) | (j_idx[None, :] == jt) | (j_idx[None, :] == jt - 1)
        score = jnp.where(j_idx[None, :] <= jt, imp + jnp.where(forced, SLC_FORCE_BONUS, 0.0), NEG_INF)
        _, sel = lax.top_k(score, top_n)
        k_sel = k_s[bi, gi, sel].reshape(b, g, qbl, top_n * SLC_LEN, d)
        v_sel = v_s[bi, gi, sel].reshape(b, g, qbl, top_n * SLC_LEN, d)
        pos_s = (sel[..., None] * SLC_LEN + jnp.arange(SLC_LEN)).reshape(b, g, qbl, top_n * SLC_LEN)
        dist_s = (t[None, None, :, None] - pos_s).astype(f32)[:, :, None]
        s_s = jnp.einsum('bqghd,bgqkd->bghqk', qb, k_sel).astype(f32) * scale
        s_s = jnp.where(dist_s >= 0, s_s - slopes[None, :, :, None, None] * dist_s, NEG_INF)
        p_s = jax.nn.softmax(s_s, axis=-1)
        o_s = jnp.einsum('bghqk,bgqkd->bqghd', p_s.astype(v_sel.dtype), v_sel)
        k_wb = lax.dynamic_slice_in_dim(k_w, q0, WINDOW + qbl, axis=1)
        v_wb = lax.dynamic_slice_in_dim(v_w, q0, WINDOW + qbl, axis=1)
        pos_w = q0 - WINDOW + jnp.arange(WINDOW + qbl)
        dist_w = t[:, None] - pos_w[None, :]
        ok_w = (dist_w >= 0) & (dist_w < WINDOW) & (pos_w[None, :] >= 0)
        s_w = jnp.einsum('bqghd,bkgd->bghqk', qb, k_wb).astype(f32) * scale
        s_w = jnp.where(ok_w, s_w - slopes[:, :, None, None] * dist_w.astype(f32), NEG_INF)
        p_w = jax.nn.softmax(s_w, axis=-1)
        o_w = jnp.einsum('bghqk,bkgd->bqghd', p_w.astype(v_wb.dtype), v_wb)
        o = gb[:, :, 0, :, :, None] * o_c + gb[:, :, 1, :, :, None] * o_s + gb[:, :, 2, :, :, None] * o_w
        return o.reshape(b, qbl, NSA_DIM).astype(qb.dtype)

    out = lax.map(query_block, (q_blocks, g_blocks, jnp.arange(n_qb)))
    return jnp.moveaxis(out, 0, 1).reshape(b, s, NSA_DIM)


def chunk_gated_delta_rule(q, k, v, g, beta):
    b, s, h, dk = q.shape
    dv = v.shape[-1]
    c = GDN_CHUNK
    n = s // c

    def to_chunks(a):
        return jnp.moveaxis(a.reshape((b, n, c, h) + a.shape[3:]), 3, 1)

    q, k, v, beta = to_chunks(q), to_chunks(k), to_chunks(v), to_chunks(beta)
    gc = jnp.cumsum(to_chunks(g), axis=-1)
    tri = jnp.tril(jnp.ones((c, c), dtype=bool))
    decay = jnp.where(tri, jnp.exp(jnp.where(tri, gc[..., :, None] - gc[..., None, :], 0.0)), 0.0)
    k_beta = k * beta[..., None]
    m = jnp.einsum('bhncd,bhnsd->bhncs', k_beta, k) * decay
    t_inv = lax.linalg.triangular_solve(m, jnp.broadcast_to(jnp.eye(c, dtype=m.dtype), m.shape), left_side=True, lower=True, unit_diagonal=True)
    u = t_inv @ (v * beta[..., None])
    w = t_inv @ (k_beta * jnp.exp(gc)[..., None])
    attn = jnp.einsum('bhncd,bhnsd->bhncs', q, k) * decay
    q_dec = q * jnp.exp(gc)[..., None]
    g_last = gc[..., -1:]
    k_dec = k * jnp.exp(g_last - gc)[..., None]
    chunk_decay = jnp.exp(g_last[..., 0])

    def step(state, xs):
        q_i, k_i, u_i, w_i, a_i, d_i = xs
        v_new = u_i - jnp.einsum('bhcd,bhde->bhce', w_i, state)
        o_i = jnp.einsum('bhcd,bhde->bhce', q_i, state) + jnp.einsum('bhcs,bhse->bhce', a_i, v_new)
        state = state * d_i[..., None, None] + jnp.einsum('bhcd,bhce->bhde', k_i, v_new)
        return state, o_i

    xs = tuple(jnp.moveaxis(a, 2, 0) for a in (q_dec, k_dec, u, w, attn, chunk_decay))
    _, o = lax.scan(step, jnp.zeros((b, h, dk, dv), q.dtype), xs)
    return o.transpose(1, 0, 3, 2, 4).reshape(b, s, h, dv)


def gated_deltanet_mixer(u_qkv, u_z, u_b, u_a, conv_w, a_log, dt_bias, out_gain):
    b, s, _ = u_qkv.shape
    f32 = jnp.float32
    qkv = jax.nn.silu(causal_depthwise_conv(u_qkv, conv_w)).astype(f32).reshape(b, s, 3, GDN_HEADS, GDN_HEAD_DIM)
    q = l2_norm(qkv[:, :, 0]) * GDN_HEAD_DIM ** -0.5
    k = l2_norm(qkv[:, :, 1])
    v = qkv[:, :, 2]
    beta = jax.nn.sigmoid(u_b.astype(f32))
    g = -jnp.exp(a_log.astype(f32)) * jax.nn.softplus(u_a.astype(f32) + dt_bias.astype(f32))
    o = chunk_gated_delta_rule(q, k, v, g, beta)
    o = rms_norm(o, out_gain) * jax.nn.silu(u_z.astype(f32).reshape(b, s, GDN_HEADS, GDN_HEAD_DIM))
    return o.reshape(b, s, GDN_DIM).astype(u_qkv.dtype)


def hybrid_mixer(h, w_in, conv_a_w, nsa_qk_gain, cmp_pe, cmp_w1, cmp_w2, gdn_conv_w, gdn_a_log, gdn_dt_bias, gdn_out_gain, w_branch, w_gate, b_gate, w_out):
    b, s, dm = h.shape
    u = h @ w_in
    u_a, u_nq, u_nkv, u_ng, u_gqkv, u_gz, u_gb, u_ga = split_last(u, IN_SPLITS)
    y_a = short_conv_mixer(u_a, conv_a_w)
    y_n = nsa_mixer(u_nq, u_nkv, u_ng, nsa_qk_gain, cmp_pe, cmp_w1, cmp_w2)
    y_g = gated_deltanet_mixer(u_gqkv, u_gz, u_gb, u_ga, gdn_conv_w, gdn_a_log, gdn_dt_bias, gdn_out_gain)
    branches = jnp.einsum('bsrc,rcd->bsrd', jnp.stack([y_a, y_n, y_g], axis=2), w_branch)
    gates = jax.nn.sigmoid((h @ w_gate + b_gate).astype(jnp.float32)).reshape(b, s, N_BRANCH, dm)
    merged = jnp.sum(gates * branches, axis=2).astype(h.dtype)
    return merged @ w_out


def hier_moe(h, w_rg, b_rg, w_re, b_re, w_eg, w_eu, w_ed):
    b, s, dm = h.shape
    n_tok = b * s
    n_asg = n_tok * TOPK_IN_GROUP
    xt = h.reshape(n_tok, dm)
    lg = (xt @ w_rg + b_rg).astype(jnp.float32)
    grp = jnp.argmax(lg, axis=-1)
    p_grp = jnp.take_along_axis(jax.nn.softmax(lg, axis=-1), grp[:, None], axis=-1)
    le = (xt @ w_re + b_re).astype(jnp.float32).reshape(n_tok, MOE_GROUPS, EXPERTS_PER_GROUP)
    le = jnp.take_along_axis(le, grp[:, None, None], axis=1)[:, 0]
    top_p, top_i = lax.top_k(jax.nn.softmax(le, axis=-1), TOPK_IN_GROUP)
    gate = top_p / jnp.sum(top_p, axis=-1, keepdims=True) * p_grp
    expert = (grp[:, None] * EXPERTS_PER_GROUP + top_i).reshape(n_asg)
    token = jnp.arange(n_asg) // TOPK_IN_GROUP
    order = jnp.argsort(expert)
    e_s, tok_s, gate_s = expert[order], token[order], gate.reshape(n_asg)[order]
    counts = jnp.bincount(expert, length=N_EXPERTS)
    n_blk = (counts + MOE_BLOCK - 1) // MOE_BLOCK
    start = jnp.cumsum(counts) - counts
    pad_start = (jnp.cumsum(n_blk) - n_blk) * MOE_BLOCK
    dest = pad_start[e_s] + jnp.arange(n_asg) - start[e_s]
    total_blk = n_asg // MOE_BLOCK + N_EXPERTS
    buf = jnp.zeros((total_blk * MOE_BLOCK, dm), h.dtype).at[dest].set(xt[tok_s])
    blk_expert = jnp.repeat(jnp.arange(N_EXPERTS), n_blk, total_repeat_length=total_blk)

    def expert_block(args):
        xb, e = args
        return (jax.nn.silu(xb @ w_eg[e]) * (xb @ w_eu[e])) @ w_ed[e]

    yb = lax.map(expert_block, (buf.reshape(total_blk, MOE_BLOCK, dm), blk_expert))
    y = yb.reshape(total_blk * MOE_BLOCK, dm)[dest] * gate_s[:, None].astype(h.dtype)
    return jax.ops.segment_sum(y, tok_s, num_segments=n_tok).reshape(b, s, dm)


def setup_inputs(seed: int = 0) -> dict:
    key = jax.random.key(seed)
    ks = jax.random.split(key, 24)
    f32 = jnp.float32

    def nrm(k, shape, scale):
        return jax.random.normal(k, shape, f32) * scale

    out_scale = (2 * DEPTH) ** -0.5
    dt = jnp.exp(jax.random.uniform(ks[10], (DEPTH, GDN_HEADS), f32, math.log(1e-3), math.log(1e-1)))
    return {
        'x': nrm(ks[0], (BATCH, SEQ, D_MODEL), 1.0),
        'norm_mix': 1.0 + nrm(ks[1], (DEPTH, D_MODEL), 0.02),
        'w_in': nrm(ks[2], (DEPTH, D_MODEL, D_IN), D_MODEL ** -0.5),
        'conv_a_w': nrm(ks[3], (DEPTH, CONV_WIDTH, CONV_DIM), CONV_WIDTH ** -0.5),
        'nsa_qk_gain': 1.0 + nrm(ks[4], (DEPTH, 4, NSA_HEAD_DIM), 0.02),
        'cmp_pe': nrm(ks[5], (DEPTH, 2, CMP_LEN, NSA_HEAD_DIM), 0.02),
        'cmp_w1': nrm(ks[6], (DEPTH, 2, CMP_LEN * NSA_HEAD_DIM, CMP_HIDDEN), (CMP_LEN * NSA_HEAD_DIM) ** -0.5),
        'cmp_w2': nrm(ks[7], (DEPTH, 2, CMP_HIDDEN, NSA_HEAD_DIM), CMP_HIDDEN ** -0.5),
        'gdn_conv_w': nrm(ks[8], (DEPTH, GDN_CONV, 3 * GDN_DIM), GDN_CONV ** -0.5),
        'gdn_a_log': jnp.log(jax.random.uniform(ks[9], (DEPTH, GDN_HEADS), f32, 1.0, 16.0)),
        'gdn_dt_bias': dt + jnp.log(-jnp.expm1(-dt)),
        'gdn_out_gain': 1.0 + nrm(ks[11], (DEPTH, GDN_HEAD_DIM), 0.02),
        'w_branch': nrm(ks[12], (DEPTH, N_BRANCH, BRANCH_DIM, D_MODEL), BRANCH_DIM ** -0.5),
        'w_gate': nrm(ks[13], (DEPTH, D_MODEL, N_BRANCH * D_MODEL), D_MODEL ** -0.5),
        'b_gate': nrm(ks[14], (DEPTH, N_BRANCH * D_MODEL), 0.01),
        'w_out': nrm(ks[15], (DEPTH, D_MODEL, D_MODEL), D_MODEL ** -0.5 * out_scale),
        'norm_ffn': 1.0 + nrm(ks[16], (DEPTH, D_MODEL), 0.02),
        'w_router_group': nrm(ks[17], (DEPTH, D_MODEL, MOE_GROUPS), D_MODEL ** -0.5),
        'b_router_group': nrm(ks[18], (DEPTH, MOE_GROUPS), 0.01),
        'w_router_expert': nrm(ks[19], (DEPTH, D_MODEL, N_EXPERTS), D_MODEL ** -0.5),
        'b_router_expert': nrm(ks[20], (DEPTH, N_EXPERTS), 0.01),
        'w_expert_gate': nrm(ks[21], (DEPTH, N_EXPERTS, D_MODEL, EXPERT_FF), D_MODEL ** -0.5),
        'w_expert_up': nrm(ks[22], (DEPTH, N_EXPERTS, D_MODEL, EXPERT_FF), D_MODEL ** -0.5),
        'w_expert_down': nrm(ks[23], (DEPTH, N_EXPERTS, EXPERT_FF, D_MODEL), EXPERT_FF ** -0.5 * out_scale),
    }


def reference(x, norm_mix, w_in, conv_a_w, nsa_qk_gain, cmp_pe, cmp_w1, cmp_w2, gdn_conv_w, gdn_a_log, gdn_dt_bias, gdn_out_gain, w_branch, w_gate, b_gate, w_out, norm_ffn, w_router_group, b_router_group, w_router_expert, b_router_expert, w_expert_gate, w_expert_up, w_expert_down):
    for l in range(DEPTH):
        h = rms_norm(x, norm_mix[l])
        x = x + hybrid_mixer(h, w_in[l], conv_a_w[l], nsa_qk_gain[l], cmp_pe[l], cmp_w1[l], cmp_w2[l], gdn_conv_w[l], gdn_a_log[l], gdn_dt_bias[l], gdn_out_gain[l], w_branch[l], w_gate[l], b_gate[l], w_out[l])
        h = rms_norm(x, norm_ffn[l])
        x = x + hier_moe(h, w_router_group[l], b_router_group[l], w_router_expert[l], b_router_expert[l], w_expert_gate[l], w_expert_up[l], w_expert_down[l])
    return x
```

```python
import functools
import math

import jax
import jax.numpy as jnp
from jax import lax
from jax.experimental import pallas as pl
from jax.experimental.pallas import tpu as pltpu

D_MODEL = 1024
DEPTH = 4
CONV_DIM = 512
CONV_WIDTH = 3
NSA_HEADS = 8
NSA_KV_GROUPS = 2
NSA_HPG = NSA_HEADS // NSA_KV_GROUPS
NSA_HEAD_DIM = 64
NSA_DIM = NSA_HEADS * NSA_HEAD_DIM
CMP_LEN = 32
CMP_STRIDE = 16
CMP_HIDDEN = 256
SLC_LEN = 64
SLC_TOPN = 8
SLC_FORCE_BONUS = 1e6
WINDOW = 512
NSA_QBLOCK = 64
GDN_HEADS = 4
GDN_HEAD_DIM = 128
GDN_DIM = GDN_HEADS * GDN_HEAD_DIM
GDN_CONV = 4
GDN_CHUNK = 64
N_BRANCH = 3
BRANCH_DIM = 512
IN_SPLITS = (3 * CONV_DIM, NSA_DIM, 6 * NSA_KV_GROUPS * NSA_HEAD_DIM, 3 * NSA_HEADS, 3 * GDN_DIM, GDN_DIM, GDN_HEADS, GDN_HEADS)
D_IN = sum(IN_SPLITS)
MOE_GROUPS = 4
EXPERTS_PER_GROUP = 8
N_EXPERTS = MOE_GROUPS * EXPERTS_PER_GROUP
TOPK_IN_GROUP = 2
EXPERT_FF = 512
MOE_BLOCK = 256
RMS_EPS = 1e-6
NEG_INF = -1e30


def _norm_proj_kernel(x_ref, g_ref, w_ref, o_ref, h_ref):
    @pl.when(pl.program_id(1) == 0)
    def _():
        x = x_ref[...]
        y = x * lax.rsqrt(jnp.mean(x * x, axis=-1, keepdims=True) + RMS_EPS)
        h_ref[...] = (y * g_ref[...]).astype(jnp.bfloat16)

    o_ref[...] = jnp.dot(h_ref[...], w_ref[...], preferred_element_type=jnp.float32)


def norm_proj(x2d, gain, w, tm=512, tn=512):
    t, d = x2d.shape
    n = w.shape[1]
    n_pad = -(-n // tn) * tn
    wb = jnp.pad(w.astype(jnp.bfloat16), ((0, 0), (0, n_pad - n)))
    out = pl.pallas_call(
        _norm_proj_kernel,
        grid=(t // tm, n_pad // tn),
        in_specs=[
            pl.BlockSpec((tm, d), lambda i, j: (i, 0)),
            pl.BlockSpec((1, d), lambda i, j: (0, 0)),
            pl.BlockSpec((d, tn), lambda i, j: (0, j)),
        ],
        out_specs=pl.BlockSpec((tm, tn), lambda i, j: (i, j)),
        out_shape=jax.ShapeDtypeStruct((t, n_pad), jnp.float32),
        scratch_shapes=[pltpu.VMEM((tm, d), jnp.bfloat16)],
        compiler_params=pltpu.CompilerParams(dimension_semantics=("arbitrary", "arbitrary")),
        name="norm_proj",
    )(x2d, gain.reshape(1, d), wb)
    return out[:, :n]


def split_last(u, sizes):
    out, start = [], 0
    for n in sizes:
        out.append(u[..., start:start + n])
        start += n
    return out


def rms_norm(x, gain):
    xf = x.astype(jnp.float32)
    y = xf * lax.rsqrt(jnp.mean(xf * xf, axis=-1, keepdims=True) + RMS_EPS)
    return (y * gain.astype(jnp.float32)).astype(x.dtype)


def l2_norm(x):
    return x * lax.rsqrt(jnp.sum(x * x, axis=-1, keepdims=True) + RMS_EPS)


def causal_depthwise_conv(x, w):
    k, c = w.shape
    return lax.conv_general_dilated(x, w[:, None, :].astype(x.dtype), window_strides=(1,), padding=[(k - 1, 0)], dimension_numbers=('NWC', 'WIO', 'NWC'), feature_group_count=c)


def alibi_slopes(n):
    return 2.0 ** (-8.0 * jnp.arange(1, n + 1, dtype=jnp.float32) / n)


def short_conv_mixer(u_a, conv_w):
    b_gate, c_gate, x_in = split_last(u_a, (CONV_DIM, CONV_DIM, CONV_DIM))
    return b_gate * causal_depthwise_conv(c_gate * x_in, conv_w)


def nsa_compress(kv, pe, w1, w2):
    b, s, g, d = kv.shape
    n_cmp = (s - CMP_LEN) // CMP_STRIDE + 1
    idx = jnp.arange(n_cmp)[:, None] * CMP_STRIDE + jnp.arange(CMP_LEN)[None, :]
    blocks = kv[:, idx] + pe[:, None, :]
    flat = jnp.moveaxis(blocks, 3, 2).reshape(b, n_cmp, g, CMP_LEN * d)
    return jax.nn.gelu(flat @ w1) @ w2


def nsa_mixer(u_q, u_kv, u_g, qk_gain, cmp_pe, cmp_w1, cmp_w2):
    b, s, _ = u_q.shape
    g, hpg, d, qbl = NSA_KV_GROUPS, NSA_HPG, NSA_HEAD_DIM, NSA_QBLOCK
    f32 = jnp.float32
    scale = d ** -0.5
    q = rms_norm(u_q.reshape(b, s, g, hpg, d), qk_gain[0])
    kv = u_kv.reshape(b, s, 6, g, d)
    k_c = rms_norm(nsa_compress(kv[:, :, 0], cmp_pe[0], cmp_w1[0], cmp_w2[0]), qk_gain[1])
    v_c = nsa_compress(kv[:, :, 1], cmp_pe[1], cmp_w1[1], cmp_w2[1])
    n_cmp = k_c.shape[1]
    n_slc = s // SLC_LEN
    top_n = min(SLC_TOPN, n_slc)
    k_s = rms_norm(kv[:, :, 2], qk_gain[2]).reshape(b, n_slc, SLC_LEN, g, d).transpose(0, 3, 1, 2, 4)
    v_s = kv[:, :, 3].reshape(b, n_slc, SLC_LEN, g, d).transpose(0, 3, 1, 2, 4)
    pad = ((0, 0), (WINDOW, 0), (0, 0), (0, 0))
    k_w = jnp.pad(rms_norm(kv[:, :, 4], qk_gain[3]), pad)
    v_w = jnp.pad(kv[:, :, 5], pad)
    gates = jax.nn.sigmoid(u_g.astype(f32)).reshape(b, s, 3, g, hpg)
    slopes = alibi_slopes(NSA_HEADS).reshape(g, hpg)
    cmp_lo = jnp.arange(n_cmp) * CMP_STRIDE
    cmp_end = cmp_lo + CMP_LEN - 1
    slc_lo = jnp.arange(n_slc) * SLC_LEN
    overlap = ((cmp_lo[:, None] < slc_lo[None, :] + SLC_LEN) & (cmp_lo[:, None] + CMP_LEN > slc_lo[None, :])).astype(f32)
    n_qb = s // qbl
    q_blocks = jnp.moveaxis(q.reshape(b, n_qb, qbl, g, hpg, d), 1, 0)
    g_blocks = jnp.moveaxis(gates.reshape(b, n_qb, qbl, 3, g, hpg), 1, 0)
    bi = jnp.arange(b)[:, None, None, None]
    gi = jnp.arange(g)[None, :, None, None]
    j_idx = jnp.arange(n_slc)

    def query_block(args):
        qb, gb, blk = args
        q0 = blk * qbl
        t = q0 + jnp.arange(qbl)
        dist_c = (t[:, None] - cmp_end[None, :]).astype(f32)
        ok_c = dist_c >= 0
        s_c = jnp.einsum('bqghd,bcgd->bghqc', qb, k_c).astype(f32) * scale
        s_c = jnp.where(ok_c, s_c - slopes[:, :, None, None] * dist_c, NEG_INF)
        p_c = jax.nn.softmax(s_c, axis=-1) * jnp.any(ok_c, axis=-1)[:, None].astype(f32)
        o_c = jnp.einsum('bghqc,bcgd->bqghd', p_c.astype(v_c.dtype), v_c)
        imp = jnp.einsum('bghqc,cj->bgqj', p_c, overlap)
        jt = (t // SLC_LEN)[:, None]
        forced = (j_idx[None, :] == 0) | (j_idx[None, :] == jt) | (j_idx[None, :] == jt - 1)
        score = jnp.where(j_idx[None, :] <= jt, imp + jnp.where(forced, SLC_FORCE_BONUS, 0.0), NEG_INF)
        _, sel = lax.top_k(score, top_n)
        k_sel = k_s[bi, gi, sel].reshape(b, g, qbl, top_n * SLC_LEN, d)
        v_sel = v_s[bi, gi, sel].reshape(b, g, qbl, top_n * SLC_LEN, d)
        pos_s = (sel[..., None] * SLC_LEN + jnp.arange(SLC_LEN)).reshape(b, g, qbl, top_n * SLC_LEN)
        dist_s = (t[None, None, :, None] - pos_s).astype(f32)[:, :, None]
        s_s = jnp.einsum('bqghd,bgqkd->bghqk', qb, k_sel).astype(f32) * scale
        s_s = jnp.where(dist_s >= 0, s_s - slopes[None, :, :, None, None] * dist_s, NEG_INF)
        p_s = jax.nn.softmax(s_s, axis=-1)
        o_s = jnp.einsum('bghqk,bgqkd->bqghd', p_s.astype(v_sel.dtype), v_sel)
        k_wb = lax.dynamic_slice_in_dim(k_w, q0, WINDOW + qbl, axis=1)
        v_wb = lax.dynamic_slice_in_dim(v_w, q0, WINDOW + qbl, axis=1)
        pos_w = q0 - WINDOW + jnp.arange(WINDOW + qbl)
        dist_w = t[:, None] - pos_w[None, :]
        ok_w = (dist_w >= 0) & (dist_w < WINDOW) & (pos_w[None, :] >= 0)
        s_w = jnp.einsum('bqghd,bkgd->bghqk', qb, k_wb).astype(f32) * scale
        s_w = jnp.where(ok_w, s_w - slopes[:, :, None, None] * dist_w.astype(f32), NEG_INF)
        p_w = jax.nn.softmax(s_w, axis=-1)
        o_w = jnp.einsum('bghqk,bkgd->bqghd', p_w.astype(v_wb.dtype), v_wb)
        o = gb[:, :, 0, :, :, None] * o_c + gb[:, :, 1, :, :, None] * o_s + gb[:, :, 2, :, :, None] * o_w
        return o.reshape(b, qbl, NSA_DIM).astype(qb.dtype)

    out = lax.map(query_block, (q_blocks, g_blocks, jnp.arange(n_qb)))
    return jnp.moveaxis(out, 0, 1).reshape(b, s, NSA_DIM)


def chunk_gated_delta_rule(q, k, v, g, beta):
    b, s, h, dk = q.shape
    dv = v.shape[-1]
    c = GDN_CHUNK
    n = s // c

    def to_chunks(a):
        return jnp.moveaxis(a.reshape((b, n, c, h) + a.shape[3:]), 3, 1)

    q, k, v, beta = to_chunks(q), to_chunks(k), to_chunks(v), to_chunks(beta)
    gc = jnp.cumsum(to_chunks(g), axis=-1)
    tri = jnp.tril(jnp.ones((c, c), dtype=bool))
    decay = jnp.where(tri, jnp.exp(jnp.where(tri, gc[..., :, None] - gc[..., None, :], 0.0)), 0.0)
    k_beta = k * beta[..., None]
    m = jnp.einsum('bhncd,bhnsd->bhncs', k_beta, k) * decay
    t_inv = lax.linalg.triangular_solve(m, jnp.broadcast_to(jnp.eye(c, dtype=m.dtype), m.shape), left_side=True, lower=True, unit_diagonal=True)
    u = t_inv @ (v * beta[..., None])
    w = t_inv @ (k_beta * jnp.exp(gc)[..., None])
    attn = jnp.einsum('bhncd,bhnsd->bhncs', q, k) * decay
    q_dec = q * jnp.exp(gc)[..., None]
    g_last = gc[..., -1:]
    k_dec = k * jnp.exp(g_last - gc)[..., None]
    chunk_decay = jnp.exp(g_last[..., 0])

    def step(state, xs):
        q_i, k_i, u_i, w_i, a_i, d_i = xs
        v_new = u_i - jnp.einsum('bhcd,bhde->bhce', w_i, state)
        o_i = jnp.einsum('bhcd,bhde->bhce', q_i, state) + jnp.einsum('bhcs,bhse->bhce', a_i, v_new)
        state = state * d_i[..., None, None] + jnp.einsum('bhcd,bhce->bhde', k_i, v_new)
        return state, o_i

    xs = tuple(jnp.moveaxis(a, 2, 0) for a in (q_dec, k_dec, u, w, attn, chunk_decay))
    _, o = lax.scan(step, jnp.zeros((b, h, dk, dv), q.dtype), xs)
    return o.transpose(1, 0, 3, 2, 4).reshape(b, s, h, dv)


def gated_deltanet_mixer(u_qkv, u_z, u_b, u_a, conv_w, a_log, dt_bias, out_gain):
    b, s, _ = u_qkv.shape
    f32 = jnp.float32
    qkv = jax.nn.silu(causal_depthwise_conv(u_qkv, conv_w)).astype(f32).reshape(b, s, 3, GDN_HEADS, GDN_HEAD_DIM)
    q = l2_norm(qkv[:, :, 0]) * GDN_HEAD_DIM ** -0.5
    k = l2_norm(qkv[:, :, 1])
    v = qkv[:, :, 2]
    beta = jax.nn.sigmoid(u_b.astype(f32))
    g = -jnp.exp(a_log.astype(f32)) * jax.nn.softplus(u_a.astype(f32) + dt_bias.astype(f32))
    o = chunk_gated_delta_rule(q, k, v, g, beta)
    o = rms_norm(o, out_gain) * jax.nn.silu(u_z.astype(f32).reshape(b, s, GDN_HEADS, GDN_HEAD_DIM))
    return o.reshape(b, s, GDN_DIM).astype(u_qkv.dtype)


def hybrid_mixer(x, norm_gain, w_in, conv_a_w, nsa_qk_gain, cmp_pe, cmp_w1, cmp_w2, gdn_conv_w, gdn_a_log, gdn_dt_bias, gdn_out_gain, w_branch, w_gate, b_gate, w_out):
    b, s, dm = x.shape
    h = rms_norm(x, norm_gain)
    u = norm_proj(x.reshape(b * s, dm), norm_gain, w_in).reshape(b, s, D_IN)
    u_a, u_nq, u_nkv, u_ng, u_gqkv, u_gz, u_gb, u_ga = split_last(u, IN_SPLITS)
    y_a = short_conv_mixer(u_a, conv_a_w)
    y_n = nsa_mixer(u_nq, u_nkv, u_ng, nsa_qk_gain, cmp_pe, cmp_w1, cmp_w2)
    y_g = gated_deltanet_mixer(u_gqkv, u_gz, u_gb, u_ga, gdn_conv_w, gdn_a_log, gdn_dt_bias, gdn_out_gain)
    branches = jnp.einsum('bsrc,rcd->bsrd', jnp.stack([y_a, y_n, y_g], axis=2), w_branch)
    gates = jax.nn.sigmoid((h @ w_gate + b_gate).astype(jnp.float32)).reshape(b, s, N_BRANCH, dm)
    merged = jnp.sum(gates * branches, axis=2).astype(h.dtype)
    return merged @ w_out


def hier_moe(h, w_rg, b_rg, w_re, b_re, w_eg, w_eu, w_ed):
    b, s, dm = h.shape
    n_tok = b * s
    n_asg = n_tok * TOPK_IN_GROUP
    xt = h.reshape(n_tok, dm)
    lg = (xt @ w_rg + b_rg).astype(jnp.float32)
    grp = jnp.argmax(lg, axis=-1)
    p_grp = jnp.take_along_axis(jax.nn.softmax(lg, axis=-1), grp[:, None], axis=-1)
    le = (xt @ w_re + b_re).astype(jnp.float32).reshape(n_tok, MOE_GROUPS, EXPERTS_PER_GROUP)
    le = jnp.take_along_axis(le, grp[:, None, None], axis=1)[:, 0]
    top_p, top_i = lax.top_k(jax.nn.softmax(le, axis=-1), TOPK_IN_GROUP)
    gate = top_p / jnp.sum(top_p, axis=-1, keepdims=True) * p_grp
    expert = (grp[:, None] * EXPERTS_PER_GROUP + top_i).reshape(n_asg)
    token = jnp.arange(n_asg) // TOPK_IN_GROUP
    order = jnp.argsort(expert)
    e_s, tok_s, gate_s = expert[order], token[order], gate.reshape(n_asg)[order]
    counts = jnp.bincount(expert, length=N_EXPERTS)
    n_blk = (counts + MOE_BLOCK - 1) // MOE_BLOCK
    start = jnp.cumsum(counts) - counts
    pad_start = (jnp.cumsum(n_blk) - n_blk) * MOE_BLOCK
    dest = pad_start[e_s] + jnp.arange(n_asg) - start[e_s]
    total_blk = n_asg // MOE_BLOCK + N_EXPERTS
    buf = jnp.zeros((total_blk * MOE_BLOCK, dm), h.dtype).at[dest].set(xt[tok_s])
    blk_expert = jnp.repeat(jnp.arange(N_EXPERTS), n_blk, total_repeat_length=total_blk)

    def expert_block(args):
        xb, e = args
        return (jax.nn.silu(xb @ w_eg[e]) * (xb @ w_eu[e])) @ w_ed[e]

    yb = lax.map(expert_block, (buf.reshape(total_blk, MOE_BLOCK, dm), blk_expert))
    y = yb.reshape(total_blk * MOE_BLOCK, dm)[dest] * gate_s[:, None].astype(h.dtype)
    return jax.ops.segment_sum(y, tok_s, num_segments=n_tok).reshape(b, s, dm)


def kernel(x, norm_mix, w_in, conv_a_w, nsa_qk_gain, cmp_pe, cmp_w1, cmp_w2, gdn_conv_w, gdn_a_log, gdn_dt_bias, gdn_out_gain, w_branch, w_gate, b_gate, w_out, norm_ffn, w_router_group, b_router_group, w_router_expert, b_router_expert, w_expert_gate, w_expert_up, w_expert_down):
    for l in range(DEPTH):
        x = x + hybrid_mixer(x, norm_mix[l], w_in[l], conv_a_w[l], nsa_qk_gain[l], cmp_pe[l], cmp_w1[l], cmp_w2[l], gdn_conv_w[l], gdn_a_log[l], gdn_dt_bias[l], gdn_out_gain[l], w_branch[l], w_gate[l], b_gate[l], w_out[l])
        h = rms_norm(x, norm_ffn[l])
        x = x + hier_moe(h, w_router_group[l], b_router_group[l], w_router_expert[l], b_router_expert[l], w_expert_gate[l], w_expert_up[l], w_expert_down[l])
    return x
```

```python
import functools
import math

import jax
import jax.numpy as jnp
from jax import lax
from jax.experimental import pallas as pl
from jax.experimental.pallas import tpu as pltpu

D_MODEL = 1024
DEPTH = 4
CONV_DIM = 512
CONV_WIDTH = 3
NSA_HEADS = 8
NSA_KV_GROUPS = 2
NSA_HPG = NSA_HEADS // NSA_KV_GROUPS
NSA_HEAD_DIM = 64
NSA_DIM = NSA_HEADS * NSA_HEAD_DIM
CMP_LEN = 32
CMP_STRIDE = 16
CMP_HIDDEN = 256
SLC_LEN = 64
SLC_TOPN = 8
SLC_FORCE_BONUS = 1e6
WINDOW = 512
NSA_QBLOCK = 64
GDN_HEADS = 4
GDN_HEAD_DIM = 128
GDN_DIM = GDN_HEADS * GDN_HEAD_DIM
GDN_CONV = 4
GDN_CHUNK = 64
N_BRANCH = 3
BRANCH_DIM = 512
IN_SPLITS = (3 * CONV_DIM, NSA_DIM, 6 * NSA_KV_GROUPS * NSA_HEAD_DIM, 3 * NSA_HEADS, 3 * GDN_DIM, GDN_DIM, GDN_HEADS, GDN_HEADS)
D_IN = sum(IN_SPLITS)
COL_NSA_Q = IN_SPLITS[0]
COL_NSA_KV = COL_NSA_Q + IN_SPLITS[1]
COL_NSA_GATE = COL_NSA_KV + IN_SPLITS[2]
MOE_GROUPS = 4
EXPERTS_PER_GROUP = 8
N_EXPERTS = MOE_GROUPS * EXPERTS_PER_GROUP
TOPK_IN_GROUP = 2
EXPERT_FF = 512
MOE_BLOCK = 256
RMS_EPS = 1e-6
NEG_INF = -1e30

F32 = jnp.float32
BF16 = jnp.bfloat16


def _norm_proj_kernel(x_ref, g_ref, w_ref, o_ref, h_ref):
    @pl.when(pl.program_id(1) == 0)
    def _():
        x = x_ref[...]
        y = x * lax.rsqrt(jnp.mean(x * x, axis=-1, keepdims=True) + RMS_EPS)
        h_ref[...] = (y * g_ref[...]).astype(BF16)

    o_ref[...] = jnp.dot(h_ref[...], w_ref[...], preferred_element_type=F32)


def norm_proj(x2d, gain, w, tm=512, tn=512):
    t, d = x2d.shape
    n = w.shape[1]
    n_blk = pl.cdiv(n, tn)
    wb = jnp.pad(w.astype(BF16), ((0, 0), (0, n_blk * tn - n)))
    return pl.pallas_call(
        _norm_proj_kernel,
        grid=(t // tm, n_blk),
        in_specs=[
            pl.BlockSpec((tm, d), lambda i, j: (i, 0)),
            pl.BlockSpec((1, d), lambda i, j: (0, 0)),
            pl.BlockSpec((d, tn), lambda i, j: (0, j)),
        ],
        out_specs=pl.BlockSpec((tm, tn), lambda i, j: (i, j)),
        out_shape=jax.ShapeDtypeStruct((t, n), F32),
        scratch_shapes=[pltpu.VMEM((tm, d), BF16)],
        compiler_params=pltpu.CompilerParams(dimension_semantics=("arbitrary", "arbitrary")),
        name="norm_proj",
    )(x2d, gain.reshape(1, d), wb)


KV_LANES = NSA_KV_GROUPS * NSA_HEAD_DIM
N_WIN_KEYS = WINDOW + NSA_QBLOCK
SEL_CHUNK = 8
M_INIT = -1e29


def _dot(a, b):
    return jnp.dot(a, b, preferred_element_type=F32)


def _dot_nt(a, b):
    return lax.dot_general(a, b, (((1,), (1,)), ((), ())), preferred_element_type=F32)


def _split_dot(x, m):
    hi = x.astype(BF16)
    lo = (x - hi.astype(F32)).astype(BF16)
    return _dot(hi, m) + _dot(lo, m)


def _head_slope(g, h):
    return 2.0 ** (-8.0 * (g * NSA_HPG + h + 1) / NSA_HEADS)


def _group_rms(x, bd, gain):
    ms = _split_dot(x * x, bd)
    return x * lax.rsqrt(ms + RMS_EPS) * gain


def _nsa_prep_kernel(uq_ref, uks_ref, uvs_ref, ukw_ref, uvw_ref, gq_ref, gks_ref, gkw_ref, bdq_ref, bdk_ref,
                     q_ref, ks_ref, vs_ref, kw_ref, vw_ref):
    scale = NSA_HEAD_DIM ** -0.5
    q_ref[...] = (_group_rms(uq_ref[...], bdq_ref[...], gq_ref[...]) * scale).astype(BF16)
    ks_ref[...] = _group_rms(uks_ref[...], bdk_ref[...], gks_ref[...]).astype(BF16)
    kw_ref[...] = _group_rms(ukw_ref[...], bdk_ref[...], gkw_ref[...]).astype(BF16)
    vs_ref[...] = uvs_ref[...].astype(BF16)
    vw_ref[...] = uvw_ref[...].astype(BF16)


def _block_diag_mean(n):
    i = jnp.arange(n) // NSA_HEAD_DIM
    return ((i[:, None] == i[None, :]).astype(F32) / NSA_HEAD_DIM).astype(BF16)


def nsa_prep(u, qk_gain, col_q, col_kv, tm=512):
    t = u.shape[0]
    assert t % tm == 0 and col_q % NSA_DIM == 0 and col_kv % KV_LANES == 0
    qb = col_q // NSA_DIM
    kb = col_kv // KV_LANES
    kv_spec = lambda j: pl.BlockSpec((tm, KV_LANES), lambda i, j=j: (i, kb + j))
    full = lambda r, c: pl.BlockSpec((r, c), lambda i: (0, 0))
    row = lambda c: pl.BlockSpec((tm, c), lambda i: (i, 0))
    gq = jnp.tile(qk_gain[0], NSA_HEADS).reshape(1, NSA_DIM)
    gks = jnp.tile(qk_gain[2], NSA_KV_GROUPS).reshape(1, KV_LANES)
    gkw = jnp.tile(qk_gain[3], NSA_KV_GROUPS).reshape(1, KV_LANES)
    return pl.pallas_call(
        _nsa_prep_kernel,
        grid=(t // tm,),
        in_specs=[pl.BlockSpec((tm, NSA_DIM), lambda i: (i, qb)), kv_spec(2), kv_spec(3), kv_spec(4), kv_spec(5),
                  full(1, NSA_DIM), full(1, KV_LANES), full(1, KV_LANES), full(NSA_DIM, NSA_DIM), full(KV_LANES, KV_LANES)],
        out_specs=[row(NSA_DIM), row(KV_LANES), row(KV_LANES), row(KV_LANES), row(KV_LANES)],
        out_shape=[jax.ShapeDtypeStruct((t, NSA_DIM), BF16)] + [jax.ShapeDtypeStruct((t, KV_LANES), BF16)] * 4,
        compiler_params=pltpu.CompilerParams(dimension_semantics=("arbitrary",)),
        name="nsa_prep",
    )(u, u, u, u, u, gq, gks, gkw, _block_diag_mean(NSA_DIM), _block_diag_mean(KV_LANES))


def _gelu_tanh(x):
    return 0.5 * x * (1.0 + jnp.tanh(0.7978845608028654 * (x + 0.044715 * x * x * x)))


def _nsa_compress_kernel(uk_ref, uv_ref, pe_ref, w1_ref, w2_ref, gk_ref, bd_ref, kc_ref, vc_ref):
    n_row = uk_ref.shape[1] // CMP_STRIDE
    d = NSA_HEAD_DIM
    for kv, (src, dst) in enumerate(((uk_ref, kc_ref), (uv_ref, vc_ref))):
        top = [jnp.zeros((n_row, CMP_HIDDEN), F32) for _ in range(NSA_KV_GROUPS)]
        bot = [jnp.zeros((n_row, CMP_HIDDEN), F32) for _ in range(NSA_KV_GROUPS)]
        for l in range(CMP_STRIDE):
            x2 = src[0, pl.ds(l, n_row, stride=CMP_STRIDE), :]
            l2 = l + CMP_STRIDE
            for g in range(NSA_KV_GROUPS):
                x = x2[:, g * d:(g + 1) * d]
                top[g] += _dot((x + pe_ref[kv, l:l + 1, :]).astype(BF16), w1_ref[kv, l * d:(l + 1) * d, :])
                bot[g] += _dot((x + pe_ref[kv, l2:l2 + 1, :]).astype(BF16), w1_ref[kv, l2 * d:(l2 + 1) * d, :])
        outs = []
        for g in range(NSA_KV_GROUPS):
            hid = top[g] + pltpu.roll(bot[g], n_row - 1, 0)
            outs.append(_dot(_gelu_tanh(hid).astype(BF16), w2_ref[kv]))
        y = jnp.concatenate(outs, axis=1)
        if kv == 0:
            y = _group_rms(y, bd_ref[...], gk_ref[...])
        dst[0] = y.astype(BF16)


def nsa_compress(u3, qk_gain, cmp_pe, cmp_w1, cmp_w2, col_kv):
    b, s, _ = u3.shape
    kb = col_kv // KV_LANES
    n_row = s // CMP_STRIDE
    full = lambda *shape: pl.BlockSpec(shape, lambda i: (0,) * len(shape))
    gk = jnp.tile(qk_gain[1], NSA_KV_GROUPS).reshape(1, KV_LANES)
    out_spec = pl.BlockSpec((1, n_row, KV_LANES), lambda i: (i, 0, 0))
    return pl.pallas_call(
        _nsa_compress_kernel,
        grid=(b,),
        in_specs=[pl.BlockSpec((1, s, KV_LANES), lambda i: (i, 0, kb)),
                  pl.BlockSpec((1, s, KV_LANES), lambda i: (i, 0, kb + 1)),
                  full(2, CMP_LEN, NSA_HEAD_DIM), full(2, CMP_LEN * NSA_HEAD_DIM, CMP_HIDDEN),
                  full(2, CMP_HIDDEN, NSA_HEAD_DIM), full(1, KV_LANES), full(KV_LANES, KV_LANES)],
        out_specs=[out_spec, out_spec],
        out_shape=[jax.ShapeDtypeStruct((b, n_row, KV_LANES), BF16)] * 2,
        compiler_params=pltpu.CompilerParams(dimension_semantics=("arbitrary",)),
        name="nsa_compress",
    )(u3, u3, cmp_pe, cmp_w1.astype(BF16), cmp_w2.astype(BF16), gk, _block_diag_mean(KV_LANES))


def _stack_heads(q, g):
    d = NSA_HEAD_DIM
    base = g * NSA_HPG * d
    return jnp.concatenate([q[:, base + h * d: base + (h + 1) * d] for h in range(NSA_HPG)], axis=0)


def _nsa_select_kernel(q_ref, kc_ref, vc_ref, ov_ref, oc_ref, sel_ref, flag_ref):
    qt = pl.program_id(1)
    qbl, d = NSA_QBLOCK, NSA_HEAD_DIM
    n_key = kc_ref.shape[1]
    n_slc = ov_ref.shape[1]
    t = qt * qbl + lax.broadcasted_iota(jnp.int32, (qbl, n_key), 0)
    c = lax.broadcasted_iota(jnp.int32, (qbl, n_key), 1)
    dist = (t - (c * CMP_STRIDE + CMP_LEN - 1)).astype(F32)
    ok = dist >= 0
    j = lax.broadcasted_iota(jnp.int32, (qbl, n_slc), 1)
    forced = (j == 0) | (j == qt) | (j == qt - 1)
    q = q_ref[0]
    flags = []
    for g in range(NSA_KV_GROUPS):
        kc = kc_ref[0, :, g * d:(g + 1) * d]
        vc = vc_ref[0, :, g * d:(g + 1) * d]
        s_all = _dot_nt(_stack_heads(q, g), kc)
        p_sum = jnp.zeros((qbl, n_key), F32)
        for h in range(NSA_HPG):
            s = jnp.where(ok, s_all[h * qbl:(h + 1) * qbl] - _head_slope(g, h) * dist, NEG_INF)
            m = jnp.max(s, axis=-1, keepdims=True)
            e = jnp.where(ok, jnp.exp(s - m), 0.0)
            l = jnp.sum(e, axis=-1, keepdims=True)
            p = e * jnp.where(l > 0, 1.0 / l, 0.0)
            col = (g * NSA_HPG + h) * d
            oc_ref[0, :, col:col + d] = _dot(p.astype(BF16), vc)
            p_sum += p
        imp = _split_dot(p_sum, ov_ref[...])
        score = jnp.where(j <= qt, imp + jnp.where(forced, SLC_FORCE_BONUS, 0.0), NEG_INF)
        sel = jnp.zeros((qbl, n_slc), F32)
        for _ in range(min(SLC_TOPN, n_slc)):
            m = jnp.max(score, axis=-1, keepdims=True)
            first = jnp.min(jnp.where(score == m, j, n_slc), axis=-1, keepdims=True)
            pick = j == first
            sel = jnp.where(pick, 1.0, sel)
            score = jnp.where(pick, -3e38, score)
        sel_ref[0, :, g * n_slc:(g + 1) * n_slc] = sel.astype(BF16)
        flags.append(jnp.max(sel, axis=0, keepdims=True))
    flag_ref[0, 0] = jnp.broadcast_to(jnp.concatenate(flags, axis=1), flag_ref.shape[2:])


def nsa_select(q3, k_cmp, v_cmp):
    b, s, _ = q3.shape
    n_qt = s // NSA_QBLOCK
    n_slc = s // SLC_LEN
    n_key = k_cmp.shape[1]
    c_lo = jnp.arange(n_key) * CMP_STRIDE
    j_lo = jnp.arange(n_slc) * SLC_LEN
    overlap = ((c_lo[:, None] < j_lo[None, :] + SLC_LEN) & (c_lo[:, None] + CMP_LEN > j_lo[None, :])).astype(BF16)
    tile = lambda c: pl.BlockSpec((1, NSA_QBLOCK, c), lambda i, t: (i, t, 0))
    per_b = pl.BlockSpec((1, n_key, KV_LANES), lambda i, t: (i, 0, 0))
    return pl.pallas_call(
        _nsa_select_kernel,
        grid=(b, n_qt),
        in_specs=[tile(NSA_DIM), per_b, per_b, pl.BlockSpec((n_key, n_slc), lambda i, t: (0, 0))],
        out_specs=[tile(NSA_DIM), tile(2 * n_slc), pl.BlockSpec((1, 1, 8, 2 * n_slc), lambda i, t: (i, t, 0, 0))],
        out_shape=[jax.ShapeDtypeStruct((b, s, NSA_DIM), F32), jax.ShapeDtypeStruct((b, s, 2 * n_slc), BF16),
                   jax.ShapeDtypeStruct((b, n_qt, 8, 2 * n_slc), F32)],
        compiler_params=pltpu.CompilerParams(dimension_semantics=("arbitrary", "arbitrary")),
        name="nsa_select",
    )(q3, k_cmp, v_cmp, overlap)


def _nsa_attend_kernel(bits_ref, q_ref, ks_ref, vs_ref, kw_ref, vw_ref, sel_ref, oc_ref, gate_ref, out_ref,
                       list_ref, ksel_ref, vsel_ref):
    bi, qt = pl.program_id(0), pl.program_id(1)
    n_qt = pl.num_programs(1)
    qbl, d, ch = NSA_QBLOCK, NSA_HEAD_DIM, SEL_CHUNK
    n_slc = sel_ref.shape[2] // NSA_KV_GROUPS
    n_word = (n_slc + 31) // 32
    q = q_ref[0]
    gates = jax.nn.sigmoid(gate_ref[0])
    lane = lax.broadcasted_iota(jnp.int32, (1, ch * SLC_LEN), 1)
    slot_of_lane = lane // SLC_LEN
    t_sel = qt * qbl + lax.broadcasted_iota(jnp.int32, (qbl, ch * SLC_LEN), 0)
    j_iota = lax.broadcasted_iota(jnp.int32, (n_slc, ch * SLC_LEN), 0)
    win_start = jnp.maximum(qt - WINDOW // qbl, 0) * qbl
    t_win = qt * qbl + lax.broadcasted_iota(jnp.int32, (qbl, N_WIN_KEYS), 0)
    dist_win = t_win - (win_start + lax.broadcasted_iota(jnp.int32, (qbl, N_WIN_KEYS), 1))
    ok_win = (dist_win >= 0) & (dist_win < WINDOW)
    dist_win = dist_win.astype(F32)

    for g in range(NSA_KV_GROUPS):
        lanes = slice(g * d, (g + 1) * d)
        qs = _stack_heads(q, g)

        kw = kw_ref[0, pl.ds(pl.multiple_of(win_start, qbl), N_WIN_KEYS), lanes]
        vw = vw_ref[0, pl.ds(pl.multiple_of(win_start, qbl), N_WIN_KEYS), lanes]
        s_all = _dot_nt(qs, kw)
        o_win = []
        for h in range(NSA_HPG):
            s = jnp.where(ok_win, s_all[h * qbl:(h + 1) * qbl] - _head_slope(g, h) * dist_win, NEG_INF)
            e = jnp.exp(s - jnp.max(s, axis=-1, keepdims=True))
            p = e * (1.0 / jnp.sum(e, axis=-1, keepdims=True))
            o_win.append(_dot(p.astype(BF16), vw))

        word0 = ((bi * n_qt + qt) * NSA_KV_GROUPS + g) * n_word

        def decode(jj, n, word0=word0):
            bit = (bits_ref[word0 + jj // 32] >> (jj % 32)) & 1
            list_ref[n] = jj
            return n + bit

        n_sel = lax.fori_loop(0, n_slc, decode, 0)
        sel_g = sel_ref[0, :, g * n_slc:(g + 1) * n_slc]

        def chunk(c, carry, lanes=lanes, qs=qs, sel_g=sel_g, g=g, n_sel=n_sel):
            m_old, l_old, acc = carry
            j_row = jnp.full((1, ch * SLC_LEN), -1, jnp.int32)
            for slot in range(ch):
                idx = c * ch + slot
                valid = idx < n_sel
                jb = jnp.where(valid, list_ref[jnp.minimum(idx, n_slc - 1)], 0)
                rows = pl.ds(pl.multiple_of(jb * SLC_LEN, SLC_LEN), SLC_LEN)
                ksel_ref[slot * SLC_LEN:(slot + 1) * SLC_LEN, :] = ks_ref[0, rows, lanes]
                vsel_ref[slot * SLC_LEN:(slot + 1) * SLC_LEN, :] = vs_ref[0, rows, lanes]
                j_row = jnp.where(slot_of_lane == slot, jnp.where(valid, jb, -1), j_row)
            s_all = _dot_nt(qs, ksel_ref[...])
            chosen = _dot(sel_g, (j_iota == j_row).astype(BF16))
            dist = t_sel - (j_row * SLC_LEN + lane % SLC_LEN)
            ok = (chosen > 0.5) & (dist >= 0)
            dist = dist.astype(F32)
            ps, ms, ls = [], [], []
            for h in range(NSA_HPG):
                rows_h = slice(h * qbl, (h + 1) * qbl)
                s = jnp.where(ok, s_all[rows_h] - _head_slope(g, h) * dist, NEG_INF)
                m_new = jnp.maximum(m_old[rows_h], jnp.max(s, axis=-1, keepdims=True))
                p = jnp.exp(s - m_new)
                alpha = jnp.exp(m_old[rows_h] - m_new)
                ls.append(alpha * l_old[rows_h] + jnp.sum(p, axis=-1, keepdims=True))
                ms.append(m_new)
                ps.append(p.astype(BF16))
            m_new = jnp.concatenate(ms, axis=0)
            alpha = jnp.exp(m_old - m_new)
            acc = alpha * acc + _dot(jnp.concatenate(ps, axis=0), vsel_ref[...])
            return m_new, jnp.concatenate(ls, axis=0), acc

        init = (jnp.full((NSA_HPG * qbl, 1), M_INIT, F32), jnp.zeros((NSA_HPG * qbl, 1), F32),
                jnp.zeros((NSA_HPG * qbl, d), F32))
        _, l_fin, acc = lax.fori_loop(0, (n_sel + ch - 1) // ch, chunk, init)
        o_sel = acc * (1.0 / l_fin)

        for h in range(NSA_HPG):
            hh = g * NSA_HPG + h
            col = hh * d
            out_ref[0, :, col:col + d] = (gates[:, hh:hh + 1] * oc_ref[0, :, col:col + d]
                                          + gates[:, NSA_HEADS + hh:NSA_HEADS + hh + 1] * o_sel[h * qbl:(h + 1) * qbl]
                                          + gates[:, 2 * NSA_HEADS + hh:2 * NSA_HEADS + hh + 1] * o_win[h])


def nsa_attend(bits, q3, ks, vs, kw, vw, sel, o_cmp, u3, col_gate):
    b, s, _ = q3.shape
    n_qt = s // NSA_QBLOCK
    assert col_gate % 128 == 0 and s >= N_WIN_KEYS
    gb = col_gate // 128
    tile = lambda c: pl.BlockSpec((1, NSA_QBLOCK, c), lambda i, t, bits: (i, t, 0))
    per_b = pl.BlockSpec((1, s, KV_LANES), lambda i, t, bits: (i, 0, 0))
    grid_spec = pltpu.PrefetchScalarGridSpec(
        num_scalar_prefetch=1,
        grid=(b, n_qt),
        in_specs=[tile(NSA_DIM), per_b, per_b, per_b, per_b, tile(sel.shape[2]), tile(NSA_DIM),
                  pl.BlockSpec((1, NSA_QBLOCK, 128), lambda i, t, bits: (i, t, gb))],
        out_specs=tile(NSA_DIM),
        scratch_shapes=[pltpu.SMEM((s // SLC_LEN,), jnp.int32),
                        pltpu.VMEM((SEL_CHUNK * SLC_LEN, NSA_HEAD_DIM), BF16),
                        pltpu.VMEM((SEL_CHUNK * SLC_LEN, NSA_HEAD_DIM), BF16)],
    )
    return pl.pallas_call(
        _nsa_attend_kernel,
        grid_spec=grid_spec,
        out_shape=jax.ShapeDtypeStruct((b, s, NSA_DIM), F32),
        compiler_params=pltpu.CompilerParams(dimension_semantics=("arbitrary", "arbitrary")),
        name="nsa_attend",
    )(bits, q3, ks, vs, kw, vw, sel, o_cmp, u3)


def _pack_union_bits(flags, n_slc):
    b, n_qt = flags.shape[:2]
    n_word = (n_slc + 31) // 32
    f = flags[:, :, 0, :].reshape(b, n_qt, NSA_KV_GROUPS, n_slc) > 0.5
    f = jnp.pad(f, ((0, 0), (0, 0), (0, 0), (0, n_word * 32 - n_slc))).reshape(b, n_qt, NSA_KV_GROUPS, n_word, 32)
    words = jnp.sum(f.astype(jnp.uint32) << jnp.arange(32, dtype=jnp.uint32), axis=-1, dtype=jnp.uint32)
    return lax.bitcast_convert_type(words, jnp.int32).reshape(-1)


def nsa_mixer_pallas(u3, qk_gain, cmp_pe, cmp_w1, cmp_w2, col_q, col_kv, col_gate):
    b, s, d_in = u3.shape
    q, ks, vs, kw, vw = nsa_prep(u3.reshape(b * s, d_in), qk_gain, col_q, col_kv)
    k_cmp, v_cmp = nsa_compress(u3, qk_gain, cmp_pe, cmp_w1, cmp_w2, col_kv)
    q3 = q.reshape(b, s, NSA_DIM)
    r3 = lambda a: a.reshape(b, s, KV_LANES)
    o_cmp, sel, flags = nsa_select(q3, k_cmp, v_cmp)
    bits = _pack_union_bits(flags, s // SLC_LEN)
    return nsa_attend(bits, q3, r3(ks), r3(vs), r3(kw), r3(vw), sel, o_cmp, u3, col_gate)


def split_last(u, sizes):
    out, start = [], 0
    for n in sizes:
        out.append(u[..., start:start + n])
        start += n
    return out


def rms_norm(x, gain):
    xf = x.astype(jnp.float32)
    y = xf * lax.rsqrt(jnp.mean(xf * xf, axis=-1, keepdims=True) + RMS_EPS)
    return (y * gain.astype(jnp.float32)).astype(x.dtype)


def l2_norm(x):
    return x * lax.rsqrt(jnp.sum(x * x, axis=-1, keepdims=True) + RMS_EPS)


def causal_depthwise_conv(x, w):
    k, c = w.shape
    return lax.conv_general_dilated(x, w[:, None, :].astype(x.dtype), window_strides=(1,), padding=[(k - 1, 0)], dimension_numbers=('NWC', 'WIO', 'NWC'), feature_group_count=c)


def short_conv_mixer(u_a, conv_w):
    b_gate, c_gate, x_in = split_last(u_a, (CONV_DIM, CONV_DIM, CONV_DIM))
    return b_gate * causal_depthwise_conv(c_gate * x_in, conv_w)


def chunk_gated_delta_rule(q, k, v, g, beta):
    b, s, h, dk = q.shape
    dv = v.shape[-1]
    c = GDN_CHUNK
    n = s // c

    def to_chunks(a):
        return jnp.moveaxis(a.reshape((b, n, c, h) + a.shape[3:]), 3, 1)

    q, k, v, beta = to_chunks(q), to_chunks(k), to_chunks(v), to_chunks(beta)
    gc = jnp.cumsum(to_chunks(g), axis=-1)
    tri = jnp.tril(jnp.ones((c, c), dtype=bool))
    decay = jnp.where(tri, jnp.exp(jnp.where(tri, gc[..., :, None] - gc[..., None, :], 0.0)), 0.0)
    k_beta = k * beta[..., None]
    m = jnp.einsum('bhncd,bhnsd->bhncs', k_beta, k) * decay
    t_inv = lax.linalg.triangular_solve(m, jnp.broadcast_to(jnp.eye(c, dtype=m.dtype), m.shape), left_side=True, lower=True, unit_diagonal=True)
    u = t_inv @ (v * beta[..., None])
    w = t_inv @ (k_beta * jnp.exp(gc)[..., None])
    attn = jnp.einsum('bhncd,bhnsd->bhncs', q, k) * decay
    q_dec = q * jnp.exp(gc)[..., None]
    g_last = gc[..., -1:]
    k_dec = k * jnp.exp(g_last - gc)[..., None]
    chunk_decay = jnp.exp(g_last[..., 0])

    def step(state, xs):
        q_i, k_i, u_i, w_i, a_i, d_i = xs
        v_new = u_i - jnp.einsum('bhcd,bhde->bhce', w_i, state)
        o_i = jnp.einsum('bhcd,bhde->bhce', q_i, state) + jnp.einsum('bhcs,bhse->bhce', a_i, v_new)
        state = state * d_i[..., None, None] + jnp.einsum('bhcd,bhce->bhde', k_i, v_new)
        return state, o_i

    xs = tuple(jnp.moveaxis(a, 2, 0) for a in (q_dec, k_dec, u, w, attn, chunk_decay))
    _, o = lax.scan(step, jnp.zeros((b, h, dk, dv), q.dtype), xs)
    return o.transpose(1, 0, 3, 2, 4).reshape(b, s, h, dv)


def gated_deltanet_mixer(u_qkv, u_z, u_b, u_a, conv_w, a_log, dt_bias, out_gain):
    b, s, _ = u_qkv.shape
    f32 = jnp.float32
    qkv = jax.nn.silu(causal_depthwise_conv(u_qkv, conv_w)).astype(f32).reshape(b, s, 3, GDN_HEADS, GDN_HEAD_DIM)
    q = l2_norm(qkv[:, :, 0]) * GDN_HEAD_DIM ** -0.5
    k = l2_norm(qkv[:, :, 1])
    v = qkv[:, :, 2]
    beta = jax.nn.sigmoid(u_b.astype(f32))
    g = -jnp.exp(a_log.astype(f32)) * jax.nn.softplus(u_a.astype(f32) + dt_bias.astype(f32))
    o = chunk_gated_delta_rule(q, k, v, g, beta)
    o = rms_norm(o, out_gain) * jax.nn.silu(u_z.astype(f32).reshape(b, s, GDN_HEADS, GDN_HEAD_DIM))
    return o.reshape(b, s, GDN_DIM).astype(u_qkv.dtype)


def hybrid_mixer(x, norm_gain, w_in, conv_a_w, nsa_qk_gain, cmp_pe, cmp_w1, cmp_w2, gdn_conv_w, gdn_a_log, gdn_dt_bias, gdn_out_gain, w_branch, w_gate, b_gate, w_out):
    b, s, dm = x.shape
    h = rms_norm(x, norm_gain)
    u = norm_proj(x.reshape(b * s, dm), norm_gain, w_in).reshape(b, s, D_IN)
    u_a, u_nq, u_nkv, u_ng, u_gqkv, u_gz, u_gb, u_ga = split_last(u, IN_SPLITS)
    y_a = short_conv_mixer(u_a, conv_a_w)
    y_n = nsa_mixer_pallas(u, nsa_qk_gain, cmp_pe, cmp_w1, cmp_w2, COL_NSA_Q, COL_NSA_KV, COL_NSA_GATE)
    y_g = gated_deltanet_mixer(u_gqkv, u_gz, u_gb, u_ga, gdn_conv_w, gdn_a_log, gdn_dt_bias, gdn_out_gain)
    branches = jnp.einsum('bsrc,rcd->bsrd', jnp.stack([y_a, y_n, y_g], axis=2), w_branch)
    gates = jax.nn.sigmoid((h @ w_gate + b_gate).astype(jnp.float32)).reshape(b, s, N_BRANCH, dm)
    merged = jnp.sum(gates * branches, axis=2).astype(h.dtype)
    return merged @ w_out


def hier_moe(h, w_rg, b_rg, w_re, b_re, w_eg, w_eu, w_ed):
    b, s, dm = h.shape
    n_tok = b * s
    n_asg = n_tok * TOPK_IN_GROUP
    xt = h.reshape(n_tok, dm)
    lg = (xt @ w_rg + b_rg).astype(jnp.float32)
    grp = jnp.argmax(lg, axis=-1)
    p_grp = jnp.take_along_axis(jax.nn.softmax(lg, axis=-1), grp[:, None], axis=-1)
    le = (xt @ w_re + b_re).astype(jnp.float32).reshape(n_tok, MOE_GROUPS, EXPERTS_PER_GROUP)
    le = jnp.take_along_axis(le, grp[:, None, None], axis=1)[:, 0]
    top_p, top_i = lax.top_k(jax.nn.softmax(le, axis=-1), TOPK_IN_GROUP)
    gate = top_p / jnp.sum(top_p, axis=-1, keepdims=True) * p_grp
    expert = (grp[:, None] * EXPERTS_PER_GROUP + top_i).reshape(n_asg)
    token = jnp.arange(n_asg) // TOPK_IN_GROUP
    order = jnp.argsort(expert)
    e_s, tok_s, gate_s = expert[order], token[order], gate.reshape(n_asg)[order]
    counts = jnp.bincount(expert, length=N_EXPERTS)
    n_blk = (counts + MOE_BLOCK - 1) // MOE_BLOCK
    start = jnp.cumsum(counts) - counts
    pad_start = (jnp.cumsum(n_blk) - n_blk) * MOE_BLOCK
    dest = pad_start[e_s] + jnp.arange(n_asg) - start[e_s]
    total_blk = n_asg // MOE_BLOCK + N_EXPERTS
    buf = jnp.zeros((total_blk * MOE_BLOCK, dm), h.dtype).at[dest].set(xt[tok_s])
    blk_expert = jnp.repeat(jnp.arange(N_EXPERTS), n_blk, total_repeat_length=total_blk)

    def expert_block(args):
        xb, e = args
        return (jax.nn.silu(xb @ w_eg[e]) * (xb @ w_eu[e])) @ w_ed[e]

    yb = lax.map(expert_block, (buf.reshape(total_blk, MOE_BLOCK, dm), blk_expert))
    y = yb.reshape(total_blk * MOE_BLOCK, dm)[dest] * gate_s[:, None].astype(h.dtype)
    return jax.ops.segment_sum(y, tok_s, num_segments=n_tok).reshape(b, s, dm)


def kernel(x, norm_mix, w_in, conv_a_w, nsa_qk_gain, cmp_pe, cmp_w1, cmp_w2, gdn_conv_w, gdn_a_log, gdn_dt_bias, gdn_out_gain, w_branch, w_gate, b_gate, w_out, norm_ffn, w_router_group, b_router_group, w_router_expert, b_router_expert, w_expert_gate, w_expert_up, w_expert_down):
    for l in range(DEPTH):
        x = x + hybrid_mixer(x, norm_mix[l], w_in[l], conv_a_w[l], nsa_qk_gain[l], cmp_pe[l], cmp_w1[l], cmp_w2[l], gdn_conv_w[l], gdn_a_log[l], gdn_dt_bias[l], gdn_out_gain[l], w_branch[l], w_gate[l], b_gate[l], w_out[l])
        h = rms_norm(x, norm_ffn[l])
        x = x + hier_moe(h, w_router_group[l], b_router_group[l], w_router_expert[l], b_router_expert[l], w_expert_gate[l], w_expert_up[l], w_expert_down[l])
    return x
```

```python
import functools
import math

import jax
import jax.numpy as jnp
from jax import lax
from jax.experimental import pallas as pl
from jax.experimental.pallas import tpu as pltpu

D_MODEL = 1024
DEPTH = 4
CONV_DIM = 512
CONV_WIDTH = 3
NSA_HEADS = 8
NSA_KV_GROUPS = 2
NSA_HPG = NSA_HEADS // NSA_KV_GROUPS
NSA_HEAD_DIM = 64
NSA_DIM = NSA_HEADS * NSA_HEAD_DIM
CMP_LEN = 32
CMP_STRIDE = 16
CMP_HIDDEN = 256
SLC_LEN = 64
SLC_TOPN = 8
SLC_FORCE_BONUS = 1e6
WINDOW = 512
NSA_QBLOCK = 64
GDN_HEADS = 4
GDN_HEAD_DIM = 128
GDN_DIM = GDN_HEADS * GDN_HEAD_DIM
GDN_CONV = 4
GDN_CHUNK = 64
N_BRANCH = 3
BRANCH_DIM = 512
IN_SPLITS = (3 * CONV_DIM, NSA_DIM, 6 * NSA_KV_GROUPS * NSA_HEAD_DIM, 3 * NSA_HEADS, 3 * GDN_DIM, GDN_DIM, GDN_HEADS, GDN_HEADS)
D_IN = sum(IN_SPLITS)
COL_NSA_Q = IN_SPLITS[0]
COL_NSA_KV = COL_NSA_Q + IN_SPLITS[1]
COL_NSA_GATE = COL_NSA_KV + IN_SPLITS[2]
MOE_GROUPS = 4
EXPERTS_PER_GROUP = 8
N_EXPERTS = MOE_GROUPS * EXPERTS_PER_GROUP
TOPK_IN_GROUP = 2
EXPERT_FF = 512
MOE_BLOCK = 256
RMS_EPS = 1e-6
NEG_INF = -1e30

F32 = jnp.float32
BF16 = jnp.bfloat16


def _norm_proj_kernel(x_ref, g_ref, w_ref, o_ref, h_ref):
    @pl.when(pl.program_id(1) == 0)
    def _():
        x = x_ref[...]
        y = x * lax.rsqrt(jnp.mean(x * x, axis=-1, keepdims=True) + RMS_EPS)
        h_ref[...] = (y * g_ref[...]).astype(BF16)

    o_ref[...] = jnp.dot(h_ref[...], w_ref[...], preferred_element_type=F32)


def norm_proj(x2d, gain, w, tm=512, tn=512):
    t, d = x2d.shape
    n = w.shape[1]
    n_blk = pl.cdiv(n, tn)
    wb = jnp.pad(w.astype(BF16), ((0, 0), (0, n_blk * tn - n)))
    return pl.pallas_call(
        _norm_proj_kernel,
        grid=(t // tm, n_blk),
        in_specs=[
            pl.BlockSpec((tm, d), lambda i, j: (i, 0)),
            pl.BlockSpec((1, d), lambda i, j: (0, 0)),
            pl.BlockSpec((d, tn), lambda i, j: (0, j)),
        ],
        out_specs=pl.BlockSpec((tm, tn), lambda i, j: (i, j)),
        out_shape=jax.ShapeDtypeStruct((t, n), F32),
        scratch_shapes=[pltpu.VMEM((tm, d), BF16)],
        compiler_params=pltpu.CompilerParams(dimension_semantics=("arbitrary", "arbitrary")),
        name="norm_proj",
    )(x2d, gain.reshape(1, d), wb)


KV_LANES = NSA_KV_GROUPS * NSA_HEAD_DIM
N_WIN_KEYS = WINDOW + NSA_QBLOCK
SEL_CHUNK = 8
M_INIT = -1e29


def _dot(a, b):
    return jnp.dot(a, b, preferred_element_type=F32)


def _dot_nt(a, b):
    return lax.dot_general(a, b, (((1,), (1,)), ((), ())), preferred_element_type=F32)


def _split_dot(x, m):
    hi = x.astype(BF16)
    lo = (x - hi.astype(F32)).astype(BF16)
    return _dot(hi, m) + _dot(lo, m)


def _head_slope(g, h):
    return 2.0 ** (-8.0 * (g * NSA_HPG + h + 1) / NSA_HEADS)


def _group_rms(x, bd, gain):
    ms = _split_dot(x * x, bd)
    return x * lax.rsqrt(ms + RMS_EPS) * gain


def _nsa_prep_kernel(uq_ref, uks_ref, uvs_ref, ukw_ref, uvw_ref, gq_ref, gks_ref, gkw_ref, bdq_ref, bdk_ref,
                     q_ref, ks_ref, vs_ref, kw_ref, vw_ref):
    scale = NSA_HEAD_DIM ** -0.5
    q_ref[...] = (_group_rms(uq_ref[...], bdq_ref[...], gq_ref[...]) * scale).astype(BF16)
    ks_ref[...] = _group_rms(uks_ref[...], bdk_ref[...], gks_ref[...]).astype(BF16)
    kw_ref[...] = _group_rms(ukw_ref[...], bdk_ref[...], gkw_ref[...]).astype(BF16)
    vs_ref[...] = uvs_ref[...].astype(BF16)
    vw_ref[...] = uvw_ref[...].astype(BF16)


def _block_diag_mean(n):
    i = jnp.arange(n) // NSA_HEAD_DIM
    return ((i[:, None] == i[None, :]).astype(F32) / NSA_HEAD_DIM).astype(BF16)


def nsa_prep(u, qk_gain, col_q, col_kv, tm=512):
    t = u.shape[0]
    assert t % tm == 0 and col_q % NSA_DIM == 0 and col_kv % KV_LANES == 0
    qb = col_q // NSA_DIM
    kb = col_kv // KV_LANES
    kv_spec = lambda j: pl.BlockSpec((tm, KV_LANES), lambda i, j=j: (i, kb + j))
    full = lambda r, c: pl.BlockSpec((r, c), lambda i: (0, 0))
    row = lambda c: pl.BlockSpec((tm, c), lambda i: (i, 0))
    gq = jnp.tile(qk_gain[0], NSA_HEADS).reshape(1, NSA_DIM)
    gks = jnp.tile(qk_gain[2], NSA_KV_GROUPS).reshape(1, KV_LANES)
    gkw = jnp.tile(qk_gain[3], NSA_KV_GROUPS).reshape(1, KV_LANES)
    return pl.pallas_call(
        _nsa_prep_kernel,
        grid=(t // tm,),
        in_specs=[pl.BlockSpec((tm, NSA_DIM), lambda i: (i, qb)), kv_spec(2), kv_spec(3), kv_spec(4), kv_spec(5),
                  full(1, NSA_DIM), full(1, KV_LANES), full(1, KV_LANES), full(NSA_DIM, NSA_DIM), full(KV_LANES, KV_LANES)],
        out_specs=[row(NSA_DIM), row(KV_LANES), row(KV_LANES), row(KV_LANES), row(KV_LANES)],
        out_shape=[jax.ShapeDtypeStruct((t, NSA_DIM), BF16)] + [jax.ShapeDtypeStruct((t, KV_LANES), BF16)] * 4,
        compiler_params=pltpu.CompilerParams(dimension_semantics=("arbitrary",)),
        name="nsa_prep",
    )(u, u, u, u, u, gq, gks, gkw, _block_diag_mean(NSA_DIM), _block_diag_mean(KV_LANES))


def _gelu_tanh(x):
    return 0.5 * x * (1.0 + jnp.tanh(0.7978845608028654 * (x + 0.044715 * x * x * x)))


def _nsa_compress_kernel(uk_ref, uv_ref, pe_ref, w1_ref, w2_ref, gk_ref, bd_ref, kc_ref, vc_ref):
    n_row = uk_ref.shape[1] // CMP_STRIDE
    d = NSA_HEAD_DIM
    for kv, (src, dst) in enumerate(((uk_ref, kc_ref), (uv_ref, vc_ref))):
        top = [jnp.zeros((n_row, CMP_HIDDEN), F32) for _ in range(NSA_KV_GROUPS)]
        bot = [jnp.zeros((n_row, CMP_HIDDEN), F32) for _ in range(NSA_KV_GROUPS)]
        for l in range(CMP_STRIDE):
            x2 = src[0, pl.ds(l, n_row, stride=CMP_STRIDE), :]
            l2 = l + CMP_STRIDE
            for g in range(NSA_KV_GROUPS):
                x = x2[:, g * d:(g + 1) * d]
                top[g] += _dot((x + pe_ref[kv, l:l + 1, :]).astype(BF16), w1_ref[kv, l * d:(l + 1) * d, :])
                bot[g] += _dot((x + pe_ref[kv, l2:l2 + 1, :]).astype(BF16), w1_ref[kv, l2 * d:(l2 + 1) * d, :])
        outs = []
        for g in range(NSA_KV_GROUPS):
            hid = top[g] + pltpu.roll(bot[g], n_row - 1, 0)
            outs.append(_dot(_gelu_tanh(hid).astype(BF16), w2_ref[kv]))
        y = jnp.concatenate(outs, axis=1)
        if kv == 0:
            y = _group_rms(y, bd_ref[...], gk_ref[...])
        dst[0] = y.astype(BF16)


def nsa_compress(u3, qk_gain, cmp_pe, cmp_w1, cmp_w2, col_kv):
    b, s, _ = u3.shape
    kb = col_kv // KV_LANES
    n_row = s // CMP_STRIDE
    full = lambda *shape: pl.BlockSpec(shape, lambda i: (0,) * len(shape))
    gk = jnp.tile(qk_gain[1], NSA_KV_GROUPS).reshape(1, KV_LANES)
    out_spec = pl.BlockSpec((1, n_row, KV_LANES), lambda i: (i, 0, 0))
    return pl.pallas_call(
        _nsa_compress_kernel,
        grid=(b,),
        in_specs=[pl.BlockSpec((1, s, KV_LANES), lambda i: (i, 0, kb)),
                  pl.BlockSpec((1, s, KV_LANES), lambda i: (i, 0, kb + 1)),
                  full(2, CMP_LEN, NSA_HEAD_DIM), full(2, CMP_LEN * NSA_HEAD_DIM, CMP_HIDDEN),
                  full(2, CMP_HIDDEN, NSA_HEAD_DIM), full(1, KV_LANES), full(KV_LANES, KV_LANES)],
        out_specs=[out_spec, out_spec],
        out_shape=[jax.ShapeDtypeStruct((b, n_row, KV_LANES), BF16)] * 2,
        compiler_params=pltpu.CompilerParams(dimension_semantics=("arbitrary",)),
        name="nsa_compress",
    )(u3, u3, cmp_pe, cmp_w1.astype(BF16), cmp_w2.astype(BF16), gk, _block_diag_mean(KV_LANES))


def _stack_heads(q, g):
    d = NSA_HEAD_DIM
    base = g * NSA_HPG * d
    return jnp.concatenate([q[:, base + h * d: base + (h + 1) * d] for h in range(NSA_HPG)], axis=0)


def _nsa_select_kernel(q_ref, kc_ref, vc_ref, ov_ref, oc_ref, sel_ref, flag_ref):
    qt = pl.program_id(1)
    qbl, d = NSA_QBLOCK, NSA_HEAD_DIM
    n_key = kc_ref.shape[1]
    n_slc = ov_ref.shape[1]
    t = qt * qbl + lax.broadcasted_iota(jnp.int32, (qbl, n_key), 0)
    c = lax.broadcasted_iota(jnp.int32, (qbl, n_key), 1)
    dist = (t - (c * CMP_STRIDE + CMP_LEN - 1)).astype(F32)
    ok = dist >= 0
    j = lax.broadcasted_iota(jnp.int32, (qbl, n_slc), 1)
    forced = (j == 0) | (j == qt) | (j == qt - 1)
    q = q_ref[0]
    flags = []
    for g in range(NSA_KV_GROUPS):
        kc = kc_ref[0, :, g * d:(g + 1) * d]
        vc = vc_ref[0, :, g * d:(g + 1) * d]
        s_all = _dot_nt(_stack_heads(q, g), kc)
        p_sum = jnp.zeros((qbl, n_key), F32)
        for h in range(NSA_HPG):
            s = jnp.where(ok, s_all[h * qbl:(h + 1) * qbl] - _head_slope(g, h) * dist, NEG_INF)
            m = jnp.max(s, axis=-1, keepdims=True)
            e = jnp.where(ok, jnp.exp(s - m), 0.0)
            l = jnp.sum(e, axis=-1, keepdims=True)
            p = e * jnp.where(l > 0, 1.0 / l, 0.0)
            col = (g * NSA_HPG + h) * d
            oc_ref[0, :, col:col + d] = _dot(p.astype(BF16), vc)
            p_sum += p
        imp = _split_dot(p_sum, ov_ref[...])
        score = jnp.where(j <= qt, imp + jnp.where(forced, SLC_FORCE_BONUS, 0.0), NEG_INF)
        sel = jnp.zeros((qbl, n_slc), F32)
        for _ in range(min(SLC_TOPN, n_slc)):
            m = jnp.max(score, axis=-1, keepdims=True)
            first = jnp.min(jnp.where(score == m, j, n_slc), axis=-1, keepdims=True)
            pick = j == first
            sel = jnp.where(pick, 1.0, sel)
            score = jnp.where(pick, -3e38, score)
        sel_ref[0, :, g * n_slc:(g + 1) * n_slc] = sel.astype(BF16)
        flags.append(jnp.max(sel, axis=0, keepdims=True))
    flag_ref[0, 0] = jnp.broadcast_to(jnp.concatenate(flags, axis=1), flag_ref.shape[2:])


def nsa_select(q3, k_cmp, v_cmp):
    b, s, _ = q3.shape
    n_qt = s // NSA_QBLOCK
    n_slc = s // SLC_LEN
    n_key = k_cmp.shape[1]
    c_lo = jnp.arange(n_key) * CMP_STRIDE
    j_lo = jnp.arange(n_slc) * SLC_LEN
    overlap = ((c_lo[:, None] < j_lo[None, :] + SLC_LEN) & (c_lo[:, None] + CMP_LEN > j_lo[None, :])).astype(BF16)
    tile = lambda c: pl.BlockSpec((1, NSA_QBLOCK, c), lambda i, t: (i, t, 0))
    per_b = pl.BlockSpec((1, n_key, KV_LANES), lambda i, t: (i, 0, 0))
    return pl.pallas_call(
        _nsa_select_kernel,
        grid=(b, n_qt),
        in_specs=[tile(NSA_DIM), per_b, per_b, pl.BlockSpec((n_key, n_slc), lambda i, t: (0, 0))],
        out_specs=[tile(NSA_DIM), tile(2 * n_slc), pl.BlockSpec((1, 1, 8, 2 * n_slc), lambda i, t: (i, t, 0, 0))],
        out_shape=[jax.ShapeDtypeStruct((b, s, NSA_DIM), F32), jax.ShapeDtypeStruct((b, s, 2 * n_slc), BF16),
                   jax.ShapeDtypeStruct((b, n_qt, 8, 2 * n_slc), F32)],
        compiler_params=pltpu.CompilerParams(dimension_semantics=("arbitrary", "arbitrary")),
        name="nsa_select",
    )(q3, k_cmp, v_cmp, overlap)


def _nsa_attend_kernel(bits_ref, q_ref, ks_ref, vs_ref, kw_ref, vw_ref, sel_ref, oc_ref, gate_ref, out_ref,
                       list_ref, ksel_ref, vsel_ref):
    bi, qt = pl.program_id(0), pl.program_id(1)
    n_qt = pl.num_programs(1)
    qbl, d, ch = NSA_QBLOCK, NSA_HEAD_DIM, SEL_CHUNK
    n_slc = sel_ref.shape[2] // NSA_KV_GROUPS
    n_word = (n_slc + 31) // 32
    q = q_ref[0]
    gates = jax.nn.sigmoid(gate_ref[0])
    lane = lax.broadcasted_iota(jnp.int32, (1, ch * SLC_LEN), 1)
    slot_of_lane = lane // SLC_LEN
    t_sel = qt * qbl + lax.broadcasted_iota(jnp.int32, (qbl, ch * SLC_LEN), 0)
    j_iota = lax.broadcasted_iota(jnp.int32, (n_slc, ch * SLC_LEN), 0)
    win_start = jnp.maximum(qt - WINDOW // qbl, 0) * qbl
    t_win = qt * qbl + lax.broadcasted_iota(jnp.int32, (qbl, N_WIN_KEYS), 0)
    dist_win = t_win - (win_start + lax.broadcasted_iota(jnp.int32, (qbl, N_WIN_KEYS), 1))
    ok_win = (dist_win >= 0) & (dist_win < WINDOW)
    dist_win = dist_win.astype(F32)

    for g in range(NSA_KV_GROUPS):
        lanes = slice(g * d, (g + 1) * d)
        qs = _stack_heads(q, g)

        kw = kw_ref[0, pl.ds(pl.multiple_of(win_start, qbl), N_WIN_KEYS), lanes]
        vw = vw_ref[0, pl.ds(pl.multiple_of(win_start, qbl), N_WIN_KEYS), lanes]
        s_all = _dot_nt(qs, kw)
        o_win = []
        for h in range(NSA_HPG):
            s = jnp.where(ok_win, s_all[h * qbl:(h + 1) * qbl] - _head_slope(g, h) * dist_win, NEG_INF)
            e = jnp.exp(s - jnp.max(s, axis=-1, keepdims=True))
            p = e * (1.0 / jnp.sum(e, axis=-1, keepdims=True))
            o_win.append(_dot(p.astype(BF16), vw))

        word0 = ((bi * n_qt + qt) * NSA_KV_GROUPS + g) * n_word

        def decode(jj, n, word0=word0):
            bit = (bits_ref[word0 + jj // 32] >> (jj % 32)) & 1
            list_ref[n] = jj
            return n + bit

        n_sel = lax.fori_loop(0, n_slc, decode, 0)
        sel_g = sel_ref[0, :, g * n_slc:(g + 1) * n_slc]

        def chunk(c, carry, lanes=lanes, qs=qs, sel_g=sel_g, g=g, n_sel=n_sel):
            m_old, l_old, acc = carry
            j_row = jnp.full((1, ch * SLC_LEN), -1, jnp.int32)
            for slot in range(ch):
                idx = c * ch + slot
                valid = idx < n_sel
                jb = jnp.where(valid, list_ref[jnp.minimum(idx, n_slc - 1)], 0)
                rows = pl.ds(pl.multiple_of(jb * SLC_LEN, SLC_LEN), SLC_LEN)
                ksel_ref[slot * SLC_LEN:(slot + 1) * SLC_LEN, :] = ks_ref[0, rows, lanes]
                vsel_ref[slot * SLC_LEN:(slot + 1) * SLC_LEN, :] = vs_ref[0, rows, lanes]
                j_row = jnp.where(slot_of_lane == slot, jnp.where(valid, jb, -1), j_row)
            s_all = _dot_nt(qs, ksel_ref[...])
            chosen = _dot(sel_g, (j_iota == j_row).astype(BF16))
            dist = t_sel - (j_row * SLC_LEN + lane % SLC_LEN)
            ok = (chosen > 0.5) & (dist >= 0)
            dist = dist.astype(F32)
            ps, ms, ls = [], [], []
            for h in range(NSA_HPG):
                rows_h = slice(h * qbl, (h + 1) * qbl)
                s = jnp.where(ok, s_all[rows_h] - _head_slope(g, h) * dist, NEG_INF)
                m_new = jnp.maximum(m_old[rows_h], jnp.max(s, axis=-1, keepdims=True))
                p = jnp.exp(s - m_new)
                alpha = jnp.exp(m_old[rows_h] - m_new)
                ls.append(alpha * l_old[rows_h] + jnp.sum(p, axis=-1, keepdims=True))
                ms.append(m_new)
                ps.append(p.astype(BF16))
            m_new = jnp.concatenate(ms, axis=0)
            alpha = jnp.exp(m_old - m_new)
            acc = alpha * acc + _dot(jnp.concatenate(ps, axis=0), vsel_ref[...])
            return m_new, jnp.concatenate(ls, axis=0), acc

        init = (jnp.full((NSA_HPG * qbl, 1), M_INIT, F32), jnp.zeros((NSA_HPG * qbl, 1), F32),
                jnp.zeros((NSA_HPG * qbl, d), F32))
        _, l_fin, acc = lax.fori_loop(0, (n_sel + ch - 1) // ch, chunk, init)
        o_sel = acc * (1.0 / l_fin)

        for h in range(NSA_HPG):
            hh = g * NSA_HPG + h
            col = hh * d
            out_ref[0, :, col:col + d] = (gates[:, hh:hh + 1] * oc_ref[0, :, col:col + d]
                                          + gates[:, NSA_HEADS + hh:NSA_HEADS + hh + 1] * o_sel[h * qbl:(h + 1) * qbl]
                                          + gates[:, 2 * NSA_HEADS + hh:2 * NSA_HEADS + hh + 1] * o_win[h])


def nsa_attend(bits, q3, ks, vs, kw, vw, sel, o_cmp, u3, col_gate):
    b, s, _ = q3.shape
    n_qt = s // NSA_QBLOCK
    assert col_gate % 128 == 0 and s >= N_WIN_KEYS
    gb = col_gate // 128
    tile = lambda c: pl.BlockSpec((1, NSA_QBLOCK, c), lambda i, t, bits: (i, t, 0))
    per_b = pl.BlockSpec((1, s, KV_LANES), lambda i, t, bits: (i, 0, 0))
    grid_spec = pltpu.PrefetchScalarGridSpec(
        num_scalar_prefetch=1,
        grid=(b, n_qt),
        in_specs=[tile(NSA_DIM), per_b, per_b, per_b, per_b, tile(sel.shape[2]), tile(NSA_DIM),
                  pl.BlockSpec((1, NSA_QBLOCK, 128), lambda i, t, bits: (i, t, gb))],
        out_specs=tile(NSA_DIM),
        scratch_shapes=[pltpu.SMEM((s // SLC_LEN,), jnp.int32),
                        pltpu.VMEM((SEL_CHUNK * SLC_LEN, NSA_HEAD_DIM), BF16),
                        pltpu.VMEM((SEL_CHUNK * SLC_LEN, NSA_HEAD_DIM), BF16)],
    )
    return pl.pallas_call(
        _nsa_attend_kernel,
        grid_spec=grid_spec,
        out_shape=jax.ShapeDtypeStruct((b, s, NSA_DIM), F32),
        compiler_params=pltpu.CompilerParams(dimension_semantics=("arbitrary", "arbitrary")),
        name="nsa_attend",
    )(bits, q3, ks, vs, kw, vw, sel, o_cmp, u3)


def _pack_union_bits(flags, n_slc):
    b, n_qt = flags.shape[:2]
    n_word = (n_slc + 31) // 32
    f = flags[:, :, 0, :].reshape(b, n_qt, NSA_KV_GROUPS, n_slc) > 0.5
    f = jnp.pad(f, ((0, 0), (0, 0), (0, 0), (0, n_word * 32 - n_slc))).reshape(b, n_qt, NSA_KV_GROUPS, n_word, 32)
    words = jnp.sum(f.astype(jnp.uint32) << jnp.arange(32, dtype=jnp.uint32), axis=-1, dtype=jnp.uint32)
    return lax.bitcast_convert_type(words, jnp.int32).reshape(-1)


def nsa_mixer_pallas(u3, qk_gain, cmp_pe, cmp_w1, cmp_w2, col_q, col_kv, col_gate):
    b, s, d_in = u3.shape
    q, ks, vs, kw, vw = nsa_prep(u3.reshape(b * s, d_in), qk_gain, col_q, col_kv)
    k_cmp, v_cmp = nsa_compress(u3, qk_gain, cmp_pe, cmp_w1, cmp_w2, col_kv)
    q3 = q.reshape(b, s, NSA_DIM)
    r3 = lambda a: a.reshape(b, s, KV_LANES)
    o_cmp, sel, flags = nsa_select(q3, k_cmp, v_cmp)
    bits = _pack_union_bits(flags, s // SLC_LEN)
    return nsa_attend(bits, q3, r3(ks), r3(vs), r3(kw), r3(vw), sel, o_cmp, u3, col_gate)


def split_last(u, sizes):
    out, start = [], 0
    for n in sizes:
        out.append(u[..., start:start + n])
        start += n
    return out


def rms_norm(x, gain):
    xf = x.astype(jnp.float32)
    y = xf * lax.rsqrt(jnp.mean(xf * xf, axis=-1, keepdims=True) + RMS_EPS)
    return (y * gain.astype(jnp.float32)).astype(x.dtype)


def l2_norm(x):
    return x * lax.rsqrt(jnp.sum(x * x, axis=-1, keepdims=True) + RMS_EPS)


def causal_depthwise_conv(x, w):
    k, c = w.shape
    return lax.conv_general_dilated(x, w[:, None, :].astype(x.dtype), window_strides=(1,), padding=[(k - 1, 0)], dimension_numbers=('NWC', 'WIO', 'NWC'), feature_group_count=c)


def short_conv_mixer(u_a, conv_w):
    b_gate, c_gate, x_in = split_last(u_a, (CONV_DIM, CONV_DIM, CONV_DIM))
    return b_gate * causal_depthwise_conv(c_gate * x_in, conv_w)


def chunk_gated_delta_rule(q, k, v, g, beta):
    b, s, h, dk = q.shape
    dv = v.shape[-1]
    c = GDN_CHUNK
    n = s // c

    def to_chunks(a):
        return jnp.moveaxis(a.reshape((b, n, c, h) + a.shape[3:]), 3, 1)

    q, k, v, beta = to_chunks(q), to_chunks(k), to_chunks(v), to_chunks(beta)
    gc = jnp.cumsum(to_chunks(g), axis=-1)
    tri = jnp.tril(jnp.ones((c, c), dtype=bool))
    decay = jnp.where(tri, jnp.exp(jnp.where(tri, gc[..., :, None] - gc[..., None, :], 0.0)), 0.0)
    k_beta = k * beta[..., None]
    m = jnp.einsum('bhncd,bhnsd->bhncs', k_beta, k) * decay
    t_inv = lax.linalg.triangular_solve(m, jnp.broadcast_to(jnp.eye(c, dtype=m.dtype), m.shape), left_side=True, lower=True, unit_diagonal=True)
    u = t_inv @ (v * beta[..., None])
    w = t_inv @ (k_beta * jnp.exp(gc)[..., None])
    attn = jnp.einsum('bhncd,bhnsd->bhncs', q, k) * decay
    q_dec = q * jnp.exp(gc)[..., None]
    g_last = gc[..., -1:]
    k_dec = k * jnp.exp(g_last - gc)[..., None]
    chunk_decay = jnp.exp(g_last[..., 0])

    def step(state, xs):
        q_i, k_i, u_i, w_i, a_i, d_i = xs
        v_new = u_i - jnp.einsum('bhcd,bhde->bhce', w_i, state)
        o_i = jnp.einsum('bhcd,bhde->bhce', q_i, state) + jnp.einsum('bhcs,bhse->bhce', a_i, v_new)
        state = state * d_i[..., None, None] + jnp.einsum('bhcd,bhce->bhde', k_i, v_new)
        return state, o_i

    xs = tuple(jnp.moveaxis(a, 2, 0) for a in (q_dec, k_dec, u, w, attn, chunk_decay))
    _, o = lax.scan(step, jnp.zeros((b, h, dk, dv), q.dtype), xs)
    return o.transpose(1, 0, 3, 2, 4).reshape(b, s, h, dv)


def gated_deltanet_mixer(u_qkv, u_z, u_b, u_a, conv_w, a_log, dt_bias, out_gain):
    b, s, _ = u_qkv.shape
    f32 = jnp.float32
    qkv = jax.nn.silu(causal_depthwise_conv(u_qkv, conv_w)).astype(f32).reshape(b, s, 3, GDN_HEADS, GDN_HEAD_DIM)
    q = l2_norm(qkv[:, :, 0]) * GDN_HEAD_DIM ** -0.5
    k = l2_norm(qkv[:, :, 1])
    v = qkv[:, :, 2]
    beta = jax.nn.sigmoid(u_b.astype(f32))
    g = -jnp.exp(a_log.astype(f32)) * jax.nn.softplus(u_a.astype(f32) + dt_bias.astype(f32))
    o = chunk_gated_delta_rule(q, k, v, g, beta)
    o = rms_norm(o, out_gain) * jax.nn.silu(u_z.astype(f32).reshape(b, s, GDN_HEADS, GDN_HEAD_DIM))
    return o.reshape(b, s, GDN_DIM).astype(u_qkv.dtype)


def hybrid_mixer(x, norm_gain, w_in, conv_a_w, nsa_qk_gain, cmp_pe, cmp_w1, cmp_w2, gdn_conv_w, gdn_a_log, gdn_dt_bias, gdn_out_gain, w_branch, w_gate, b_gate, w_out):
    b, s, dm = x.shape
    h = rms_norm(x, norm_gain)
    u = norm_proj(x.reshape(b * s, dm), norm_gain, w_in).reshape(b, s, D_IN)
    u_a, u_nq, u_nkv, u_ng, u_gqkv, u_gz, u_gb, u_ga = split_last(u, IN_SPLITS)
    y_a = short_conv_mixer(u_a, conv_a_w)
    y_n = nsa_mixer_pallas(u, nsa_qk_gain, cmp_pe, cmp_w1, cmp_w2, COL_NSA_Q, COL_NSA_KV, COL_NSA_GATE)
    y_g = gated_deltanet_mixer(u_gqkv, u_gz, u_gb, u_ga, gdn_conv_w, gdn_a_log, gdn_dt_bias, gdn_out_gain)
    branches = jnp.einsum('bsrc,rcd->bsrd', jnp.stack([y_a, y_n, y_g], axis=2), w_branch)
    gates = jax.nn.sigmoid((h @ w_gate + b_gate).astype(jnp.float32)).reshape(b, s, N_BRANCH, dm)
    merged = jnp.sum(gates * branches, axis=2).astype(h.dtype)
    return merged @ w_out


MOE_ROWS = 512
ROUTE_LANES = 128
N_ROUTER = MOE_GROUPS + N_EXPERTS


def _moe_route_kernel(x_ref, g_ref, wr_ref, br_ref, tri_ref, h_ref, route_ref, cnt_ref, run_ref):
    @pl.when(pl.program_id(0) == 0)
    def _():
        run_ref[...] = jnp.zeros_like(run_ref)

    x = x_ref[...]
    h = x * lax.rsqrt(jnp.mean(x * x, axis=-1, keepdims=True) + RMS_EPS) * g_ref[...]
    h_ref[...] = h
    logits = _dot(h.astype(BF16), wr_ref[...]) + br_ref[...]
    lane = lax.broadcasted_iota(jnp.int32, logits.shape, 1)
    first_of = lambda hit: jnp.min(jnp.where(hit, lane, ROUTE_LANES), axis=-1, keepdims=True)
    is_grp = lane < MOE_GROUPS
    lg = jnp.where(is_grp, logits, NEG_INF)
    m_g = jnp.max(lg, axis=-1, keepdims=True)
    grp = first_of(lg == m_g)
    p_grp = 1.0 / jnp.sum(jnp.where(is_grp, jnp.exp(lg - m_g), 0.0), axis=-1, keepdims=True)
    lo = MOE_GROUPS + grp * EXPERTS_PER_GROUP
    le = jnp.where((lane >= lo) & (lane < lo + EXPERTS_PER_GROUP), logits, NEG_INF)
    m1 = jnp.max(le, axis=-1, keepdims=True)
    i1 = first_of(le == m1)
    le2 = jnp.where(lane == i1, NEG_INF, le)
    m2 = jnp.max(le2, axis=-1, keepdims=True)
    i2 = first_of(le2 == m2)
    r = jnp.exp(m2 - m1)
    g1 = p_grp / (1.0 + r)
    g2 = p_grp * r / (1.0 + r)
    e1 = i1 - MOE_GROUPS
    e2 = i2 - MOE_GROUPS
    hit1 = lane == e1
    hit2 = lane == e2
    onehot = (hit1 | hit2).astype(BF16)
    before = _dot(tri_ref[...], onehot) + run_ref[...]
    r1 = jnp.sum(jnp.where(hit1, before, 0.0), axis=-1, keepdims=True)
    r2 = jnp.sum(jnp.where(hit2, before, 0.0), axis=-1, keepdims=True)
    run_ref[...] += jnp.sum(onehot.astype(F32), axis=0, keepdims=True)
    rec = jnp.zeros(logits.shape, F32)
    for k, v in enumerate((e1.astype(F32), e2.astype(F32), r1, r2, g1, g2)):
        rec = jnp.where(lane == k, v, rec)
    route_ref[...] = rec
    cnt_ref[...] = jnp.broadcast_to(run_ref[...], cnt_ref.shape)


def moe_route(x2d, gain, w_rg, b_rg, w_re, b_re, tm=512):
    t, d = x2d.shape
    pad = ROUTE_LANES - N_ROUTER
    wr = jnp.pad(jnp.concatenate([w_rg, w_re], axis=1), ((0, 0), (0, pad))).astype(BF16)
    br = jnp.pad(jnp.concatenate([b_rg, b_re]), (0, pad)).reshape(1, ROUTE_LANES)
    tri = (jnp.arange(tm)[:, None] > jnp.arange(tm)[None, :]).astype(BF16)
    full = lambda r, c: pl.BlockSpec((r, c), lambda i: (0, 0))
    return pl.pallas_call(
        _moe_route_kernel,
        grid=(t // tm,),
        in_specs=[pl.BlockSpec((tm, d), lambda i: (i, 0)), full(1, d), full(d, ROUTE_LANES), full(1, ROUTE_LANES), full(tm, tm)],
        out_specs=[pl.BlockSpec((tm, d), lambda i: (i, 0)), pl.BlockSpec((tm, ROUTE_LANES), lambda i: (i, 0)), full(8, ROUTE_LANES)],
        out_shape=[jax.ShapeDtypeStruct((t, d), F32), jax.ShapeDtypeStruct((t, ROUTE_LANES), F32), jax.ShapeDtypeStruct((8, ROUTE_LANES), F32)],
        scratch_shapes=[pltpu.VMEM((1, ROUTE_LANES), F32)],
        compiler_params=pltpu.CompilerParams(dimension_semantics=("arbitrary",)),
        name="moe_route",
    )(x2d, gain.reshape(1, d), wr, br, tri)


def _row_copy(src_ref, src_row, dst_ref, dst_row, sem):
    return pltpu.make_async_copy(src_ref.at[src_row], dst_ref.at[dst_row], sem)


def _moe_dispatch_kernel(dest_ref, h_ref, buf_in_ref, buf_ref, sem):
    del buf_in_ref
    tm = h_ref.shape[0]
    base = pl.program_id(0) * tm

    def send(r, carry):
        for k in range(TOPK_IN_GROUP):
            _row_copy(h_ref, r, buf_ref, dest_ref[(base + r) * TOPK_IN_GROUP + k], sem).start()
        return carry

    def drain(r, carry):
        for k in range(TOPK_IN_GROUP):
            _row_copy(h_ref, r, buf_ref, 0, sem).wait()
        return carry

    lax.fori_loop(0, tm, send, 0)
    lax.fori_loop(0, tm, drain, 0)


def moe_dispatch(dest, h3, n_rows, tm=512):
    t, _, d = h3.shape
    grid_spec = pltpu.PrefetchScalarGridSpec(
        num_scalar_prefetch=1, grid=(t // tm,),
        in_specs=[pl.BlockSpec((tm, 1, d), lambda i, dest: (i, 0, 0)), pl.BlockSpec(memory_space=pl.ANY)],
        out_specs=pl.BlockSpec(memory_space=pl.ANY),
        scratch_shapes=[pltpu.SemaphoreType.DMA(())],
    )
    return pl.pallas_call(
        _moe_dispatch_kernel, grid_spec=grid_spec,
        out_shape=jax.ShapeDtypeStruct((n_rows, 1, d), F32),
        input_output_aliases={2: 0},
        compiler_params=pltpu.CompilerParams(dimension_semantics=("arbitrary",), has_side_effects=True),
        name="moe_dispatch",
    )(dest, h3, jnp.zeros((n_rows, 1, d), F32))


def _moe_ffn_kernel(blk_e_ref, n_used_ref, x_ref, wg_ref, wu_ref, wd_ref, y_ref):
    del blk_e_ref
    used = pl.program_id(0) < n_used_ref[0]

    @pl.when(used)
    def _():
        xb = x_ref[...].astype(BF16)
        a = _dot(xb, wg_ref[0])
        mid = a * jax.nn.sigmoid(a) * _dot(xb, wu_ref[0])
        y_ref[...] = _dot(mid.astype(BF16), wd_ref[0])

    @pl.when(jnp.logical_not(used))
    def _():
        y_ref[...] = jnp.zeros_like(y_ref)


def moe_ffn(blk_expert, n_used, buf2d, w_eg, w_eu, w_ed):
    n_rows, d = buf2d.shape
    ff = w_eg.shape[2]
    w_in_spec = pl.BlockSpec((1, d, ff), lambda b, blk_e, n_used: (blk_e[b], 0, 0))
    grid_spec = pltpu.PrefetchScalarGridSpec(
        num_scalar_prefetch=2, grid=(n_rows // MOE_ROWS,),
        in_specs=[pl.BlockSpec((MOE_ROWS, d), lambda b, blk_e, n_used: (b, 0)), w_in_spec, w_in_spec,
                  pl.BlockSpec((1, ff, d), lambda b, blk_e, n_used: (blk_e[b], 0, 0))],
        out_specs=pl.BlockSpec((MOE_ROWS, d), lambda b, blk_e, n_used: (b, 0)),
    )
    return pl.pallas_call(
        _moe_ffn_kernel, grid_spec=grid_spec,
        out_shape=jax.ShapeDtypeStruct((n_rows, d), F32),
        compiler_params=pltpu.CompilerParams(dimension_semantics=("arbitrary",)),
        name="moe_ffn",
    )(blk_expert, n_used, buf2d, w_eg.astype(BF16), w_eu.astype(BF16), w_ed.astype(BF16))


def _moe_combine_kernel(dest_ref, y_ref, gate_ref, out_ref, ya_ref, yb_ref, sem):
    tm = out_ref.shape[0]
    base = pl.program_id(0) * tm

    def fetch(r, carry):
        _row_copy(y_ref, dest_ref[(base + r) * TOPK_IN_GROUP], ya_ref, r, sem).start()
        _row_copy(y_ref, dest_ref[(base + r) * TOPK_IN_GROUP + 1], yb_ref, r, sem).start()
        return carry

    def drain(r, carry):
        _row_copy(y_ref, 0, ya_ref, r, sem).wait()
        _row_copy(y_ref, 0, yb_ref, r, sem).wait()
        return carry

    lax.fori_loop(0, tm, fetch, 0)
    lax.fori_loop(0, tm, drain, 0)
    gates = gate_ref[...]
    out_ref[...] = gates[:, :, 0:1] * ya_ref[...] + gates[:, :, 1:2] * yb_ref[...]


def moe_combine(dest, y3, gates3, tm=512):
    t = gates3.shape[0]
    d = y3.shape[2]
    grid_spec = pltpu.PrefetchScalarGridSpec(
        num_scalar_prefetch=1, grid=(t // tm,),
        in_specs=[pl.BlockSpec(memory_space=pl.ANY), pl.BlockSpec((tm, 1, TOPK_IN_GROUP), lambda i, dest: (i, 0, 0))],
        out_specs=pl.BlockSpec((tm, 1, d), lambda i, dest: (i, 0, 0)),
        scratch_shapes=[pltpu.VMEM((tm, 1, d), F32), pltpu.VMEM((tm, 1, d), F32), pltpu.SemaphoreType.DMA(())],
    )
    return pl.pallas_call(
        _moe_combine_kernel, grid_spec=grid_spec,
        out_shape=jax.ShapeDtypeStruct((t, 1, d), F32),
        compiler_params=pltpu.CompilerParams(dimension_semantics=("arbitrary",)),
        name="moe_combine",
    )(dest, y3, gates3)


def hier_moe_pallas(x2d, gain, w_rg, b_rg, w_re, b_re, w_eg, w_eu, w_ed):
    t, d = x2d.shape
    h, rec, cnt = moe_route(x2d, gain, w_rg, b_rg, w_re, b_re)
    counts = cnt[0, :N_EXPERTS].astype(jnp.int32)
    n_blk = (counts + MOE_ROWS - 1) // MOE_ROWS
    blk_end = jnp.cumsum(n_blk)
    pad_start = (blk_end - n_blk) * MOE_ROWS
    experts = rec[:, 0:2].astype(jnp.int32)
    dest = (pad_start[experts] + rec[:, 2:4].astype(jnp.int32)).reshape(-1)
    total_blk = t * TOPK_IN_GROUP // MOE_ROWS + N_EXPERTS
    blk_expert = jnp.minimum(jnp.searchsorted(blk_end, jnp.arange(total_blk), side='right'), N_EXPERTS - 1).astype(jnp.int32)
    buf = moe_dispatch(dest, h.reshape(t, 1, d), total_blk * MOE_ROWS)
    y = moe_ffn(blk_expert, blk_end[-1:].astype(jnp.int32), buf.reshape(total_blk * MOE_ROWS, d), w_eg, w_eu, w_ed)
    out = moe_combine(dest, y.reshape(total_blk * MOE_ROWS, 1, d), rec[:, 4:6].reshape(t, 1, TOPK_IN_GROUP))
    return out.reshape(t, d)


def kernel(x, norm_mix, w_in, conv_a_w, nsa_qk_gain, cmp_pe, cmp_w1, cmp_w2, gdn_conv_w, gdn_a_log, gdn_dt_bias, gdn_out_gain, w_branch, w_gate, b_gate, w_out, norm_ffn, w_router_group, b_router_group, w_router_expert, b_router_expert, w_expert_gate, w_expert_up, w_expert_down):
    for l in range(DEPTH):
        x = x + hybrid_mixer(x, norm_mix[l], w_in[l], conv_a_w[l], nsa_qk_gain[l], cmp_pe[l], cmp_w1[l], cmp_w2[l], gdn_conv_w[l], gdn_a_log[l], gdn_dt_bias[l], gdn_out_gain[l], w_branch[l], w_gate[l], b_gate[l], w_out[l])
        b, s, dm = x.shape
        x = x + hier_moe_pallas(x.reshape(b * s, dm), norm_ffn[l], w_router_group[l], b_router_group[l], w_router_expert[l], b_router_expert[l], w_expert_gate[l], w_expert_up[l], w_expert_down[l]).reshape(b, s, dm)
    return x
```

```python
import functools
import math

import jax
import jax.numpy as jnp
from jax import lax
from jax.experimental import pallas as pl
from jax.experimental.pallas import tpu as pltpu

D_MODEL = 1024
DEPTH = 4
CONV_DIM = 512
CONV_WIDTH = 3
NSA_HEADS = 8
NSA_KV_GROUPS = 2
NSA_HPG = NSA_HEADS // NSA_KV_GROUPS
NSA_HEAD_DIM = 64
NSA_DIM = NSA_HEADS * NSA_HEAD_DIM
CMP_LEN = 32
CMP_STRIDE = 16
CMP_HIDDEN = 256
SLC_LEN = 64
SLC_TOPN = 8
SLC_FORCE_BONUS = 1e6
WINDOW = 512
NSA_QBLOCK = 64
GDN_HEADS = 4
GDN_HEAD_DIM = 128
GDN_DIM = GDN_HEADS * GDN_HEAD_DIM
GDN_CONV = 4
GDN_CHUNK = 64
N_BRANCH = 3
BRANCH_DIM = 512
IN_SPLITS = (3 * CONV_DIM, NSA_DIM, 6 * NSA_KV_GROUPS * NSA_HEAD_DIM, 3 * NSA_HEADS, 3 * GDN_DIM, GDN_DIM, GDN_HEADS, GDN_HEADS)
D_IN = sum(IN_SPLITS)
COL_CONV = 0
COL_GDN_QKV = 3 * CONV_DIM
COL_NSA_Q = COL_GDN_QKV + 3 * GDN_DIM
COL_GDN_Z = COL_NSA_Q + NSA_DIM
COL_NSA_KV = COL_GDN_Z + GDN_DIM
COL_SMALL = COL_NSA_KV + 6 * NSA_KV_GROUPS * NSA_HEAD_DIM
SMALL_LANES = 128
D_U = COL_SMALL + SMALL_LANES
LANE_BETA = 3 * NSA_HEADS
LANE_DECAY = LANE_BETA + GDN_HEADS


def permute_in_proj(w_in):
    conv, nq, nkv, ng, gqkv, gz, gb, ga = split_last(w_in, IN_SPLITS)
    pad = jnp.zeros((w_in.shape[0], SMALL_LANES - LANE_DECAY - GDN_HEADS), w_in.dtype)
    return jnp.concatenate([conv, gqkv, nq, gz, nkv, ng, gb, ga, pad], axis=1)


def split_last(u, sizes):
    out, start = [], 0
    for n in sizes:
        out.append(u[..., start:start + n])
        start += n
    return out
MOE_GROUPS = 4
EXPERTS_PER_GROUP = 8
N_EXPERTS = MOE_GROUPS * EXPERTS_PER_GROUP
TOPK_IN_GROUP = 2
EXPERT_FF = 512
MOE_BLOCK = 256
RMS_EPS = 1e-6
NEG_INF = -1e30

F32 = jnp.float32
BF16 = jnp.bfloat16


def _norm_proj_kernel(x_ref, g_ref, w_ref, o_ref, h_ref):
    @pl.when(pl.program_id(1) == 0)
    def _():
        x = x_ref[...]
        y = x * lax.rsqrt(jnp.mean(x * x, axis=-1, keepdims=True) + RMS_EPS)
        h_ref[...] = (y * g_ref[...]).astype(BF16)

    o_ref[...] = jnp.dot(h_ref[...], w_ref[...], preferred_element_type=F32)


def norm_proj(x2d, gain, w, tm=512, tn=D_U // 3):
    t, d = x2d.shape
    n = w.shape[1]
    n_blk = pl.cdiv(n, tn)
    wb = jnp.pad(w.astype(BF16), ((0, 0), (0, n_blk * tn - n)))
    return pl.pallas_call(
        _norm_proj_kernel,
        grid=(t // tm, n_blk),
        in_specs=[
            pl.BlockSpec((tm, d), lambda i, j: (i, 0)),
            pl.BlockSpec((1, d), lambda i, j: (0, 0)),
            pl.BlockSpec((d, tn), lambda i, j: (0, j)),
        ],
        out_specs=pl.BlockSpec((tm, tn), lambda i, j: (i, j)),
        out_shape=jax.ShapeDtypeStruct((t, n), F32),
        scratch_shapes=[pltpu.VMEM((tm, d), BF16)],
        compiler_params=pltpu.CompilerParams(dimension_semantics=("arbitrary", "arbitrary")),
        name="norm_proj",
    )(x2d, gain.reshape(1, d), wb)


KV_LANES = NSA_KV_GROUPS * NSA_HEAD_DIM
N_WIN_KEYS = WINDOW + NSA_QBLOCK
SEL_CHUNK = 8
M_INIT = -1e29


def _dot(a, b):
    return jnp.dot(a, b, preferred_element_type=F32)


def _dot_nt(a, b):
    return lax.dot_general(a, b, (((1,), (1,)), ((), ())), preferred_element_type=F32)


def _split_dot(x, m):
    hi = x.astype(BF16)
    lo = (x - hi.astype(F32)).astype(BF16)
    return _dot(hi, m) + _dot(lo, m)


def _head_slope(g, h):
    return 2.0 ** (-8.0 * (g * NSA_HPG + h + 1) / NSA_HEADS)


def _group_rms(x, bd, gain):
    ms = _split_dot(x * x, bd)
    return x * lax.rsqrt(ms + RMS_EPS) * gain


def _nsa_prep_kernel(uq_ref, uks_ref, uvs_ref, ukw_ref, uvw_ref, gq_ref, gks_ref, gkw_ref, bdq_ref, bdk_ref,
                     q_ref, ks_ref, vs_ref, kw_ref, vw_ref):
    scale = NSA_HEAD_DIM ** -0.5
    q_ref[...] = (_group_rms(uq_ref[...], bdq_ref[...], gq_ref[...]) * scale).astype(BF16)
    ks_ref[...] = _group_rms(uks_ref[...], bdk_ref[...], gks_ref[...]).astype(BF16)
    kw_ref[...] = _group_rms(ukw_ref[...], bdk_ref[...], gkw_ref[...]).astype(BF16)
    vs_ref[...] = uvs_ref[...].astype(BF16)
    vw_ref[...] = uvw_ref[...].astype(BF16)


def _block_diag_mean(n):
    i = jnp.arange(n) // NSA_HEAD_DIM
    return ((i[:, None] == i[None, :]).astype(F32) / NSA_HEAD_DIM).astype(BF16)


def nsa_prep(u, qk_gain, col_q, col_kv, tm=512):
    t = u.shape[0]
    assert t % tm == 0 and col_q % NSA_DIM == 0 and col_kv % KV_LANES == 0
    qb = col_q // NSA_DIM
    kb = col_kv // KV_LANES
    kv_spec = lambda j: pl.BlockSpec((tm, KV_LANES), lambda i, j=j: (i, kb + j))
    full = lambda r, c: pl.BlockSpec((r, c), lambda i: (0, 0))
    row = lambda c: pl.BlockSpec((tm, c), lambda i: (i, 0))
    gq = jnp.tile(qk_gain[0], NSA_HEADS).reshape(1, NSA_DIM)
    gks = jnp.tile(qk_gain[2], NSA_KV_GROUPS).reshape(1, KV_LANES)
    gkw = jnp.tile(qk_gain[3], NSA_KV_GROUPS).reshape(1, KV_LANES)
    return pl.pallas_call(
        _nsa_prep_kernel,
        grid=(t // tm,),
        in_specs=[pl.BlockSpec((tm, NSA_DIM), lambda i: (i, qb)), kv_spec(2), kv_spec(3), kv_spec(4), kv_spec(5),
                  full(1, NSA_DIM), full(1, KV_LANES), full(1, KV_LANES), full(NSA_DIM, NSA_DIM), full(KV_LANES, KV_LANES)],
        out_specs=[row(NSA_DIM), row(KV_LANES), row(KV_LANES), row(KV_LANES), row(KV_LANES)],
        out_shape=[jax.ShapeDtypeStruct((t, NSA_DIM), BF16)] + [jax.ShapeDtypeStruct((t, KV_LANES), BF16)] * 4,
        compiler_params=pltpu.CompilerParams(dimension_semantics=("arbitrary",)),
        name="nsa_prep",
    )(u, u, u, u, u, gq, gks, gkw, _block_diag_mean(NSA_DIM), _block_diag_mean(KV_LANES))


def _gelu_tanh(x):
    return 0.5 * x * (1.0 + jnp.tanh(0.7978845608028654 * (x + 0.044715 * x * x * x)))


def _nsa_compress_kernel(uk_ref, uv_ref, pe_ref, w1_ref, w2_ref, gk_ref, bd_ref, kc_ref, vc_ref):
    n_row = uk_ref.shape[1] // CMP_STRIDE
    d = NSA_HEAD_DIM
    for kv, (src, dst) in enumerate(((uk_ref, kc_ref), (uv_ref, vc_ref))):
        top = [jnp.zeros((n_row, CMP_HIDDEN), F32) for _ in range(NSA_KV_GROUPS)]
        bot = [jnp.zeros((n_row, CMP_HIDDEN), F32) for _ in range(NSA_KV_GROUPS)]
        for l in range(CMP_STRIDE):
            x2 = src[0, pl.ds(l, n_row, stride=CMP_STRIDE), :]
            l2 = l + CMP_STRIDE
            for g in range(NSA_KV_GROUPS):
                x = x2[:, g * d:(g + 1) * d]
                top[g] += _dot((x + pe_ref[kv, l:l + 1, :]).astype(BF16), w1_ref[kv, l * d:(l + 1) * d, :])
                bot[g] += _dot((x + pe_ref[kv, l2:l2 + 1, :]).astype(BF16), w1_ref[kv, l2 * d:(l2 + 1) * d, :])
        outs = []
        for g in range(NSA_KV_GROUPS):
            hid = top[g] + pltpu.roll(bot[g], n_row - 1, 0)
            outs.append(_dot(_gelu_tanh(hid).astype(BF16), w2_ref[kv]))
        y = jnp.concatenate(outs, axis=1)
        if kv == 0:
            y = _group_rms(y, bd_ref[...], gk_ref[...])
        dst[0] = y.astype(BF16)


def nsa_compress(u3, qk_gain, cmp_pe, cmp_w1, cmp_w2, col_kv):
    b, s, _ = u3.shape
    kb = col_kv // KV_LANES
    n_row = s // CMP_STRIDE
    full = lambda *shape: pl.BlockSpec(shape, lambda i: (0,) * len(shape))
    gk = jnp.tile(qk_gain[1], NSA_KV_GROUPS).reshape(1, KV_LANES)
    out_spec = pl.BlockSpec((1, n_row, KV_LANES), lambda i: (i, 0, 0))
    return pl.pallas_call(
        _nsa_compress_kernel,
        grid=(b,),
        in_specs=[pl.BlockSpec((1, s, KV_LANES), lambda i: (i, 0, kb)),
                  pl.BlockSpec((1, s, KV_LANES), lambda i: (i, 0, kb + 1)),
                  full(2, CMP_LEN, NSA_HEAD_DIM), full(2, CMP_LEN * NSA_HEAD_DIM, CMP_HIDDEN),
                  full(2, CMP_HIDDEN, NSA_HEAD_DIM), full(1, KV_LANES), full(KV_LANES, KV_LANES)],
        out_specs=[out_spec, out_spec],
        out_shape=[jax.ShapeDtypeStruct((b, n_row, KV_LANES), BF16)] * 2,
        compiler_params=pltpu.CompilerParams(dimension_semantics=("arbitrary",)),
        name="nsa_compress",
    )(u3, u3, cmp_pe, cmp_w1.astype(BF16), cmp_w2.astype(BF16), gk, _block_diag_mean(KV_LANES))


def _stack_heads(q, g):
    d = NSA_HEAD_DIM
    base = g * NSA_HPG * d
    return jnp.concatenate([q[:, base + h * d: base + (h + 1) * d] for h in range(NSA_HPG)], axis=0)


def _nsa_select_kernel(q_ref, kc_ref, vc_ref, ov_ref, oc_ref, sel_ref, flag_ref):
    qt = pl.program_id(1)
    qbl, d = NSA_QBLOCK, NSA_HEAD_DIM
    n_key = kc_ref.shape[1]
    n_slc = ov_ref.shape[1]
    t = qt * qbl + lax.broadcasted_iota(jnp.int32, (qbl, n_key), 0)
    c = lax.broadcasted_iota(jnp.int32, (qbl, n_key), 1)
    dist = (t - (c * CMP_STRIDE + CMP_LEN - 1)).astype(F32)
    ok = dist >= 0
    j = lax.broadcasted_iota(jnp.int32, (qbl, n_slc), 1)
    forced = (j == 0) | (j == qt) | (j == qt - 1)
    q = q_ref[0]
    flags = []
    for g in range(NSA_KV_GROUPS):
        kc = kc_ref[0, :, g * d:(g + 1) * d]
        vc = vc_ref[0, :, g * d:(g + 1) * d]
        s_all = _dot_nt(_stack_heads(q, g), kc)
        p_sum = jnp.zeros((qbl, n_key), F32)
        for h in range(NSA_HPG):
            s = jnp.where(ok, s_all[h * qbl:(h + 1) * qbl] - _head_slope(g, h) * dist, NEG_INF)
            m = jnp.max(s, axis=-1, keepdims=True)
            e = jnp.where(ok, jnp.exp(s - m), 0.0)
            l = jnp.sum(e, axis=-1, keepdims=True)
            p = e * jnp.where(l > 0, 1.0 / l, 0.0)
            col = (g * NSA_HPG + h) * d
            oc_ref[0, :, col:col + d] = _dot(p.astype(BF16), vc)
            p_sum += p
        imp = _split_dot(p_sum, ov_ref[...])
        score = jnp.where(j <= qt, imp + jnp.where(forced, SLC_FORCE_BONUS, 0.0), NEG_INF)
        sel = jnp.zeros((qbl, n_slc), F32)
        for _ in range(min(SLC_TOPN, n_slc)):
            m = jnp.max(score, axis=-1, keepdims=True)
            first = jnp.min(jnp.where(score == m, j, n_slc), axis=-1, keepdims=True)
            pick = j == first
            sel = jnp.where(pick, 1.0, sel)
            score = jnp.where(pick, -3e38, score)
        sel_ref[0, :, g * n_slc:(g + 1) * n_slc] = sel.astype(BF16)
        flags.append(jnp.max(sel, axis=0, keepdims=True))
    flag_ref[0, 0] = jnp.broadcast_to(jnp.concatenate(flags, axis=1), flag_ref.shape[2:])


def nsa_select(q3, k_cmp, v_cmp):
    b, s, _ = q3.shape
    n_qt = s // NSA_QBLOCK
    n_slc = s // SLC_LEN
    n_key = k_cmp.shape[1]
    c_lo = jnp.arange(n_key) * CMP_STRIDE
    j_lo = jnp.arange(n_slc) * SLC_LEN
    overlap = ((c_lo[:, None] < j_lo[None, :] + SLC_LEN) & (c_lo[:, None] + CMP_LEN > j_lo[None, :])).astype(BF16)
    tile = lambda c: pl.BlockSpec((1, NSA_QBLOCK, c), lambda i, t: (i, t, 0))
    per_b = pl.BlockSpec((1, n_key, KV_LANES), lambda i, t: (i, 0, 0))
    return pl.pallas_call(
        _nsa_select_kernel,
        grid=(b, n_qt),
        in_specs=[tile(NSA_DIM), per_b, per_b, pl.BlockSpec((n_key, n_slc), lambda i, t: (0, 0))],
        out_specs=[tile(NSA_DIM), tile(2 * n_slc), pl.BlockSpec((1, 1, 8, 2 * n_slc), lambda i, t: (i, t, 0, 0))],
        out_shape=[jax.ShapeDtypeStruct((b, s, NSA_DIM), F32), jax.ShapeDtypeStruct((b, s, 2 * n_slc), BF16),
                   jax.ShapeDtypeStruct((b, n_qt, 8, 2 * n_slc), F32)],
        compiler_params=pltpu.CompilerParams(dimension_semantics=("arbitrary", "arbitrary")),
        name="nsa_select",
    )(q3, k_cmp, v_cmp, overlap)


def _nsa_attend_kernel(bits_ref, q_ref, ks_ref, vs_ref, kw_ref, vw_ref, sel_ref, oc_ref, gate_ref, out_ref,
                       list_ref, ksel_ref, vsel_ref):
    bi, qt = pl.program_id(0), pl.program_id(1)
    n_qt = pl.num_programs(1)
    qbl, d, ch = NSA_QBLOCK, NSA_HEAD_DIM, SEL_CHUNK
    n_slc = sel_ref.shape[2] // NSA_KV_GROUPS
    n_word = (n_slc + 31) // 32
    q = q_ref[0]
    gates = jax.nn.sigmoid(gate_ref[0])
    lane = lax.broadcasted_iota(jnp.int32, (1, ch * SLC_LEN), 1)
    slot_of_lane = lane // SLC_LEN
    t_sel = qt * qbl + lax.broadcasted_iota(jnp.int32, (qbl, ch * SLC_LEN), 0)
    j_iota = lax.broadcasted_iota(jnp.int32, (n_slc, ch * SLC_LEN), 0)
    win_start = jnp.maximum(qt - WINDOW // qbl, 0) * qbl
    t_win = qt * qbl + lax.broadcasted_iota(jnp.int32, (qbl, N_WIN_KEYS), 0)
    dist_win = t_win - (win_start + lax.broadcasted_iota(jnp.int32, (qbl, N_WIN_KEYS), 1))
    ok_win = (dist_win >= 0) & (dist_win < WINDOW)
    dist_win = dist_win.astype(F32)

    for g in range(NSA_KV_GROUPS):
        lanes = slice(g * d, (g + 1) * d)
        qs = _stack_heads(q, g)

        kw = kw_ref[0, pl.ds(pl.multiple_of(win_start, qbl), N_WIN_KEYS), lanes]
        vw = vw_ref[0, pl.ds(pl.multiple_of(win_start, qbl), N_WIN_KEYS), lanes]
        s_all = _dot_nt(qs, kw)
        o_win = []
        for h in range(NSA_HPG):
            s = jnp.where(ok_win, s_all[h * qbl:(h + 1) * qbl] - _head_slope(g, h) * dist_win, NEG_INF)
            e = jnp.exp(s - jnp.max(s, axis=-1, keepdims=True))
            p = e * (1.0 / jnp.sum(e, axis=-1, keepdims=True))
            o_win.append(_dot(p.astype(BF16), vw))

        word0 = ((bi * n_qt + qt) * NSA_KV_GROUPS + g) * n_word

        def decode(jj, n, word0=word0):
            bit = (bits_ref[word0 + jj // 32] >> (jj % 32)) & 1
            list_ref[n] = jj
            return n + bit

        n_sel = lax.fori_loop(0, n_slc, decode, 0)
        sel_g = sel_ref[0, :, g * n_slc:(g + 1) * n_slc]

        def chunk(c, carry, lanes=lanes, qs=qs, sel_g=sel_g, g=g, n_sel=n_sel):
            m_old, l_old, acc = carry
            j_row = jnp.full((1, ch * SLC_LEN), -1, jnp.int32)
            for slot in range(ch):
                idx = c * ch + slot
                valid = idx < n_sel
                jb = jnp.where(valid, list_ref[jnp.minimum(idx, n_slc - 1)], 0)
                rows = pl.ds(pl.multiple_of(jb * SLC_LEN, SLC_LEN), SLC_LEN)
                ksel_ref[slot * SLC_LEN:(slot + 1) * SLC_LEN, :] = ks_ref[0, rows, lanes]
                vsel_ref[slot * SLC_LEN:(slot + 1) * SLC_LEN, :] = vs_ref[0, rows, lanes]
                j_row = jnp.where(slot_of_lane == slot, jnp.where(valid, jb, -1), j_row)
            s_all = _dot_nt(qs, ksel_ref[...])
            chosen = _dot(sel_g, (j_iota == j_row).astype(BF16))
            dist = t_sel - (j_row * SLC_LEN + lane % SLC_LEN)
            ok = (chosen > 0.5) & (dist >= 0)
            dist = dist.astype(F32)
            ps, ms, ls = [], [], []
            for h in range(NSA_HPG):
                rows_h = slice(h * qbl, (h + 1) * qbl)
                s = jnp.where(ok, s_all[rows_h] - _head_slope(g, h) * dist, NEG_INF)
                m_new = jnp.maximum(m_old[rows_h], jnp.max(s, axis=-1, keepdims=True))
                p = jnp.exp(s - m_new)
                alpha = jnp.exp(m_old[rows_h] - m_new)
                ls.append(alpha * l_old[rows_h] + jnp.sum(p, axis=-1, keepdims=True))
                ms.append(m_new)
                ps.append(p.astype(BF16))
            m_new = jnp.concatenate(ms, axis=0)
            alpha = jnp.exp(m_old - m_new)
            acc = alpha * acc + _dot(jnp.concatenate(ps, axis=0), vsel_ref[...])
            return m_new, jnp.concatenate(ls, axis=0), acc

        init = (jnp.full((NSA_HPG * qbl, 1), M_INIT, F32), jnp.zeros((NSA_HPG * qbl, 1), F32),
                jnp.zeros((NSA_HPG * qbl, d), F32))
        _, l_fin, acc = lax.fori_loop(0, (n_sel + ch - 1) // ch, chunk, init)
        o_sel = acc * (1.0 / l_fin)

        for h in range(NSA_HPG):
            hh = g * NSA_HPG + h
            col = hh * d
            out_ref[0, :, col:col + d] = (gates[:, hh:hh + 1] * oc_ref[0, :, col:col + d]
                                          + gates[:, NSA_HEADS + hh:NSA_HEADS + hh + 1] * o_sel[h * qbl:(h + 1) * qbl]
                                          + gates[:, 2 * NSA_HEADS + hh:2 * NSA_HEADS + hh + 1] * o_win[h])


def nsa_attend(bits, q3, ks, vs, kw, vw, sel, o_cmp, u3, col_gate):
    b, s, _ = q3.shape
    n_qt = s // NSA_QBLOCK
    assert col_gate % 128 == 0 and s >= N_WIN_KEYS
    gb = col_gate // 128
    tile = lambda c: pl.BlockSpec((1, NSA_QBLOCK, c), lambda i, t, bits: (i, t, 0))
    per_b = pl.BlockSpec((1, s, KV_LANES), lambda i, t, bits: (i, 0, 0))
    grid_spec = pltpu.PrefetchScalarGridSpec(
        num_scalar_prefetch=1,
        grid=(b, n_qt),
        in_specs=[tile(NSA_DIM), per_b, per_b, per_b, per_b, tile(sel.shape[2]), tile(NSA_DIM),
                  pl.BlockSpec((1, NSA_QBLOCK, 128), lambda i, t, bits: (i, t, gb))],
        out_specs=tile(NSA_DIM),
        scratch_shapes=[pltpu.SMEM((s // SLC_LEN,), jnp.int32),
                        pltpu.VMEM((SEL_CHUNK * SLC_LEN, NSA_HEAD_DIM), BF16),
                        pltpu.VMEM((SEL_CHUNK * SLC_LEN, NSA_HEAD_DIM), BF16)],
    )
    return pl.pallas_call(
        _nsa_attend_kernel,
        grid_spec=grid_spec,
        out_shape=jax.ShapeDtypeStruct((b, s, NSA_DIM), F32),
        compiler_params=pltpu.CompilerParams(dimension_semantics=("arbitrary", "arbitrary")),
        name="nsa_attend",
    )(bits, q3, ks, vs, kw, vw, sel, o_cmp, u3)


def _pack_union_bits(flags, n_slc):
    b, n_qt = flags.shape[:2]
    n_word = (n_slc + 31) // 32
    f = flags[:, :, 0, :].reshape(b, n_qt, NSA_KV_GROUPS, n_slc) > 0.5
    f = jnp.pad(f, ((0, 0), (0, 0), (0, 0), (0, n_word * 32 - n_slc))).reshape(b, n_qt, NSA_KV_GROUPS, n_word, 32)
    words = jnp.sum(f.astype(jnp.uint32) << jnp.arange(32, dtype=jnp.uint32), axis=-1, dtype=jnp.uint32)
    return lax.bitcast_convert_type(words, jnp.int32).reshape(-1)


def nsa_mixer_pallas(u3, qk_gain, cmp_pe, cmp_w1, cmp_w2, col_q, col_kv, col_gate):
    b, s, d_in = u3.shape
    q, ks, vs, kw, vw = nsa_prep(u3.reshape(b * s, d_in), qk_gain, col_q, col_kv)
    k_cmp, v_cmp = nsa_compress(u3, qk_gain, cmp_pe, cmp_w1, cmp_w2, col_kv)
    q3 = q.reshape(b, s, NSA_DIM)
    r3 = lambda a: a.reshape(b, s, KV_LANES)
    o_cmp, sel, flags = nsa_select(q3, k_cmp, v_cmp)
    bits = _pack_union_bits(flags, s // SLC_LEN)
    return nsa_attend(bits, q3, r3(ks), r3(vs), r3(kw), r3(vw), sel, o_cmp, u3, col_gate)


GDN_TILE = 256
GDN_HALO = 8


def _dot3(a, b):
    ah = a.astype(BF16)
    bh = b.astype(BF16)
    al = (a - ah.astype(F32)).astype(BF16)
    bl = (b - bh.astype(F32)).astype(BF16)
    return _dot(ah, bh) + _dot(ah, bl) + _dot(al, bh)


def _dot_tn(a, b):
    return lax.dot_general(a, b, (((0,), (0,)), ((), ())), preferred_element_type=F32)


def _softplus(x):
    return jnp.maximum(x, 0.0) + jnp.log(1.0 + jnp.exp(-jnp.abs(x)))


def _l2_norm(x):
    return x * lax.rsqrt(jnp.sum(x * x, axis=-1, keepdims=True) + RMS_EPS)


def _gdn_kernel(qkv_ref, z_ref, small_ref, cw_ref, coef_ref, gain_ref, y_ref, xe_ref, state_ref):
    tt = pl.program_id(1)
    tile, c, hd = GDN_TILE, GDN_CHUNK, GDN_HEAD_DIM

    @pl.when(tt == 0)
    def _():
        xe_ref[0:GDN_HALO, :] = jnp.zeros((GDN_HALO, xe_ref.shape[1]), F32)
        state_ref[...] = jnp.zeros_like(state_ref)

    @pl.when(tt > 0)
    def _():
        xe_ref[0:GDN_HALO, :] = xe_ref[tile:tile + GDN_HALO, :]

    xe_ref[GDN_HALO:, :] = qkv_ref[0]
    conv = jnp.zeros((tile, xe_ref.shape[1]), F32)
    for j in range(GDN_CONV):
        conv += cw_ref[j:j + 1, :] * xe_ref[pl.ds(GDN_HALO - (GDN_CONV - 1) + j, tile), :]
    act = conv * jax.nn.sigmoid(conv)

    small = small_ref[0]
    beta_all = jax.nn.sigmoid(small)
    g_all = coef_ref[0:1, :] * _softplus(small + coef_ref[1:2, :])
    row = lax.broadcasted_iota(jnp.int32, (c, SMALL_LANES), 0)
    ri = lax.broadcasted_iota(jnp.int32, (c, c), 0)
    ci = lax.broadcasted_iota(jnp.int32, (c, c), 1)
    lower = ri >= ci
    strict = ri > ci

    for n in range(tile // c):
        rows = slice(n * c, (n + 1) * c)
        gc = g_all[rows]
        shift = 1
        while shift < c:
            gc = gc + jnp.where(row >= shift, pltpu.roll(gc, shift, 0), 0.0)
            shift *= 2
        gc_t = gc.T
        for h in range(GDN_HEADS):
            q = _l2_norm(act[rows, h * hd:(h + 1) * hd]) * hd ** -0.5
            k = _l2_norm(act[rows, GDN_DIM + h * hd:GDN_DIM + (h + 1) * hd])
            v = act[rows, 2 * GDN_DIM + h * hd:2 * GDN_DIM + (h + 1) * hd]
            beta = beta_all[rows, LANE_BETA + h:LANE_BETA + h + 1]
            gcol = gc[:, LANE_DECAY + h:LANE_DECAY + h + 1]
            grow = gc_t[LANE_DECAY + h:LANE_DECAY + h + 1, :]
            g_last = gcol[c - 1:c, :]
            decay = jnp.where(lower, jnp.exp(jnp.where(lower, gcol - grow, 0.0)), 0.0)
            kb = (k * beta).astype(BF16)
            kh = k.astype(BF16)
            lmat = jnp.where(strict, _dot_nt(kb, kh) * decay, 0.0)
            powers = [lmat]
            while len(powers) < 6:
                powers.append(_dot3(powers[-1], powers[-1]))
            rhs = jnp.concatenate([v * beta, (k * beta) * jnp.exp(gcol)], axis=1)
            for p in reversed(powers[1:]):
                rhs = rhs + _dot3(p, rhs)
            rhs = rhs - _dot3(lmat, rhs)
            u_i, w_i = rhs[:, :hd], rhs[:, hd:]
            attn = _dot_nt(q.astype(BF16), kh) * decay
            q_dec = q * jnp.exp(gcol)
            k_dec = k * jnp.exp(g_last - gcol)
            state = state_ref[h]
            sb = state.astype(BF16)
            v_new = u_i - _dot(w_i.astype(BF16), sb)
            o = _dot(q_dec.astype(BF16), sb) + _dot(attn.astype(BF16), v_new.astype(BF16))
            state_ref[h] = state * jnp.exp(g_last) + _dot_tn(k_dec.astype(BF16), v_new.astype(BF16))
            o = o * lax.rsqrt(jnp.mean(o * o, axis=-1, keepdims=True) + RMS_EPS) * gain_ref[...]
            zz = z_ref[0, rows, h * hd:(h + 1) * hd]
            y_ref[0, rows, h * hd:(h + 1) * hd] = o * (zz * jax.nn.sigmoid(zz))


def gdn_mixer(u3, col_qkv, col_z, col_small, conv_w, a_log, dt_bias, out_gain):
    b, s, _ = u3.shape
    lane = jnp.arange(SMALL_LANES)
    in_decay = (lane >= LANE_DECAY) & (lane < LANE_DECAY + GDN_HEADS)
    idx = jnp.clip(lane - LANE_DECAY, 0, GDN_HEADS - 1)
    coef = jnp.stack([jnp.where(in_decay, -jnp.exp(a_log)[idx], 0.0), jnp.where(in_decay, dt_bias[idx], 0.0)])
    qkv_w = 3 * GDN_DIM
    assert col_qkv % qkv_w == 0 and col_z % GDN_DIM == 0 and col_small % SMALL_LANES == 0
    full = lambda r, cc: pl.BlockSpec((r, cc), lambda i, t: (0, 0))
    return pl.pallas_call(
        _gdn_kernel,
        grid=(b, s // GDN_TILE),
        in_specs=[pl.BlockSpec((1, GDN_TILE, qkv_w), lambda i, t: (i, t, col_qkv // qkv_w)),
                  pl.BlockSpec((1, GDN_TILE, GDN_DIM), lambda i, t: (i, t, col_z // GDN_DIM)),
                  pl.BlockSpec((1, GDN_TILE, SMALL_LANES), lambda i, t: (i, t, col_small // SMALL_LANES)),
                  full(GDN_CONV, qkv_w), full(2, SMALL_LANES), full(1, GDN_HEAD_DIM)],
        out_specs=pl.BlockSpec((1, GDN_TILE, GDN_DIM), lambda i, t: (i, t, 0)),
        out_shape=jax.ShapeDtypeStruct((b, s, GDN_DIM), F32),
        scratch_shapes=[pltpu.VMEM((GDN_TILE + GDN_HALO, qkv_w), F32), pltpu.VMEM((GDN_HEADS, GDN_HEAD_DIM, GDN_HEAD_DIM), F32)],
        compiler_params=pltpu.CompilerParams(dimension_semantics=("arbitrary", "arbitrary")),
        name="gdn_mixer",
    )(u3, u3, u3, conv_w, coef, out_gain.reshape(1, GDN_HEAD_DIM))


CONV_TILE = 512


def _short_conv_kernel(u_ref, cw_ref, y_ref, xe_ref):
    tile, cd = CONV_TILE, CONV_DIM

    @pl.when(pl.program_id(1) == 0)
    def _():
        xe_ref[0:GDN_HALO, :] = jnp.zeros((GDN_HALO, cd), F32)

    @pl.when(pl.program_id(1) > 0)
    def _():
        xe_ref[0:GDN_HALO, :] = xe_ref[tile:tile + GDN_HALO, :]

    xe_ref[GDN_HALO:, :] = u_ref[0, :, cd:2 * cd] * u_ref[0, :, 2 * cd:3 * cd]
    conv = jnp.zeros((tile, cd), F32)
    for j in range(CONV_WIDTH):
        conv += cw_ref[j:j + 1, :] * xe_ref[pl.ds(GDN_HALO - (CONV_WIDTH - 1) + j, tile), :]
    y_ref[0] = u_ref[0, :, 0:cd] * conv


def short_conv(u3, conv_w):
    b, s, _ = u3.shape
    return pl.pallas_call(
        _short_conv_kernel,
        grid=(b, s // CONV_TILE),
        in_specs=[pl.BlockSpec((1, CONV_TILE, 3 * CONV_DIM), lambda i, t: (i, t, 0)),
                  pl.BlockSpec((CONV_WIDTH, CONV_DIM), lambda i, t: (0, 0))],
        out_specs=pl.BlockSpec((1, CONV_TILE, CONV_DIM), lambda i, t: (i, t, 0)),
        out_shape=jax.ShapeDtypeStruct((b, s, CONV_DIM), F32),
        scratch_shapes=[pltpu.VMEM((CONV_TILE + GDN_HALO, CONV_DIM), F32)],
        compiler_params=pltpu.CompilerParams(dimension_semantics=("arbitrary", "arbitrary")),
        name="short_conv",
    )(u3, conv_w)


def _merge_kernel(x_ref, g_ref, ya_ref, yn_ref, yg_ref, wb_ref, wg_ref, bg_ref, wo_ref, o_ref):
    x = x_ref[...]
    d = x.shape[1]
    h = (x * lax.rsqrt(jnp.mean(x * x, axis=-1, keepdims=True) + RMS_EPS) * g_ref[...]).astype(BF16)
    merged = jnp.zeros(x.shape, F32)
    for r, y_ref in enumerate((ya_ref, yn_ref, yg_ref)):
        gate = jax.nn.sigmoid(_dot(h, wg_ref[:, r * d:(r + 1) * d]) + bg_ref[:, r * d:(r + 1) * d])
        merged += gate * _dot(y_ref[...].astype(BF16), wb_ref[r])
    o_ref[...] = x + _dot(merged.astype(BF16), wo_ref[...])


def merge_branches(x2d, gain, y_a, y_n, y_g, w_branch, w_gate, b_gate, w_out, tm=512):
    t, d = x2d.shape
    row = lambda c: pl.BlockSpec((tm, c), lambda i: (i, 0))
    full = lambda *shape: pl.BlockSpec(shape, lambda i: (0,) * len(shape))
    return pl.pallas_call(
        _merge_kernel,
        grid=(t // tm,),
        in_specs=[row(d), full(1, d), row(BRANCH_DIM), row(BRANCH_DIM), row(BRANCH_DIM),
                  full(N_BRANCH, BRANCH_DIM, d), full(d, N_BRANCH * d), full(1, N_BRANCH * d), full(d, d)],
        out_specs=row(d),
        out_shape=jax.ShapeDtypeStruct((t, d), F32),
        compiler_params=pltpu.CompilerParams(dimension_semantics=("arbitrary",), vmem_limit_bytes=56 * 1024 * 1024),
        name="merge_branches",
    )(x2d, gain.reshape(1, d), y_a, y_n, y_g, w_branch.astype(BF16), w_gate.astype(BF16), b_gate.reshape(1, -1), w_out.astype(BF16))


def hybrid_mixer(x2d, b, s, norm_gain, w_in, conv_a_w, nsa_qk_gain, cmp_pe, cmp_w1, cmp_w2, gdn_conv_w, gdn_a_log, gdn_dt_bias, gdn_out_gain, w_branch, w_gate, b_gate, w_out):
    t = b * s
    u3 = norm_proj(x2d, norm_gain, permute_in_proj(w_in)).reshape(b, s, D_U)
    y_a = short_conv(u3, conv_a_w)
    y_n = nsa_mixer_pallas(u3, nsa_qk_gain, cmp_pe, cmp_w1, cmp_w2, COL_NSA_Q, COL_NSA_KV, COL_SMALL)
    y_g = gdn_mixer(u3, COL_GDN_QKV, COL_GDN_Z, COL_SMALL, gdn_conv_w, gdn_a_log, gdn_dt_bias, gdn_out_gain)
    return merge_branches(x2d, norm_gain, y_a.reshape(t, BRANCH_DIM), y_n.reshape(t, BRANCH_DIM), y_g.reshape(t, BRANCH_DIM), w_branch, w_gate, b_gate, w_out)


MOE_ROWS = 512
ROUTE_LANES = 128
N_ROUTER = MOE_GROUPS + N_EXPERTS


def _moe_route_kernel(x_ref, g_ref, wr_ref, br_ref, tri_ref, h_ref, route_ref, cnt_ref, run_ref):
    @pl.when(pl.program_id(0) == 0)
    def _():
        run_ref[...] = jnp.zeros_like(run_ref)

    x = x_ref[...]
    h = x * lax.rsqrt(jnp.mean(x * x, axis=-1, keepdims=True) + RMS_EPS) * g_ref[...]
    h_ref[...] = h
    logits = _dot(h.astype(BF16), wr_ref[...]) + br_ref[...]
    lane = lax.broadcasted_iota(jnp.int32, logits.shape, 1)
    first_of = lambda hit: jnp.min(jnp.where(hit, lane, ROUTE_LANES), axis=-1, keepdims=True)
    is_grp = lane < MOE_GROUPS
    lg = jnp.where(is_grp, logits, NEG_INF)
    m_g = jnp.max(lg, axis=-1, keepdims=True)
    grp = first_of(lg == m_g)
    p_grp = 1.0 / jnp.sum(jnp.where(is_grp, jnp.exp(lg - m_g), 0.0), axis=-1, keepdims=True)
    lo = MOE_GROUPS + grp * EXPERTS_PER_GROUP
    le = jnp.where((lane >= lo) & (lane < lo + EXPERTS_PER_GROUP), logits, NEG_INF)
    m1 = jnp.max(le, axis=-1, keepdims=True)
    i1 = first_of(le == m1)
    le2 = jnp.where(lane == i1, NEG_INF, le)
    m2 = jnp.max(le2, axis=-1, keepdims=True)
    i2 = first_of(le2 == m2)
    r = jnp.exp(m2 - m1)
    g1 = p_grp / (1.0 + r)
    g2 = p_grp * r / (1.0 + r)
    e1 = i1 - MOE_GROUPS
    e2 = i2 - MOE_GROUPS
    hit1 = lane == e1
    hit2 = lane == e2
    onehot = (hit1 | hit2).astype(BF16)
    before = _dot(tri_ref[...], onehot) + run_ref[...]
    r1 = jnp.sum(jnp.where(hit1, before, 0.0), axis=-1, keepdims=True)
    r2 = jnp.sum(jnp.where(hit2, before, 0.0), axis=-1, keepdims=True)
    run_ref[...] += jnp.sum(onehot.astype(F32), axis=0, keepdims=True)
    rec = jnp.zeros(logits.shape, F32)
    for k, v in enumerate((e1.astype(F32), e2.astype(F32), r1, r2, g1, g2)):
        rec = jnp.where(lane == k, v, rec)
    route_ref[...] = rec
    cnt_ref[...] = jnp.broadcast_to(run_ref[...], cnt_ref.shape)


def moe_route(x2d, gain, w_rg, b_rg, w_re, b_re, tm=512):
    t, d = x2d.shape
    pad = ROUTE_LANES - N_ROUTER
    wr = jnp.pad(jnp.concatenate([w_rg, w_re], axis=1), ((0, 0), (0, pad))).astype(BF16)
    br = jnp.pad(jnp.concatenate([b_rg, b_re]), (0, pad)).reshape(1, ROUTE_LANES)
    tri = (jnp.arange(tm)[:, None] > jnp.arange(tm)[None, :]).astype(BF16)
    full = lambda r, c: pl.BlockSpec((r, c), lambda i: (0, 0))
    return pl.pallas_call(
        _moe_route_kernel,
        grid=(t // tm,),
        in_specs=[pl.BlockSpec((tm, d), lambda i: (i, 0)), full(1, d), full(d, ROUTE_LANES), full(1, ROUTE_LANES), full(tm, tm)],
        out_specs=[pl.BlockSpec((tm, d), lambda i: (i, 0)), pl.BlockSpec((tm, ROUTE_LANES), lambda i: (i, 0)), full(8, ROUTE_LANES)],
        out_shape=[jax.ShapeDtypeStruct((t, d), F32), jax.ShapeDtypeStruct((t, ROUTE_LANES), F32), jax.ShapeDtypeStruct((8, ROUTE_LANES), F32)],
        scratch_shapes=[pltpu.VMEM((1, ROUTE_LANES), F32)],
        compiler_params=pltpu.CompilerParams(dimension_semantics=("arbitrary",)),
        name="moe_route",
    )(x2d, gain.reshape(1, d), wr, br, tri)


def _row_copy(src_ref, src_row, dst_ref, dst_row, sem):
    return pltpu.make_async_copy(src_ref.at[src_row], dst_ref.at[dst_row], sem)


def _moe_dispatch_kernel(dest_ref, h_ref, buf_in_ref, buf_ref, sem):
    del buf_in_ref
    tm = h_ref.shape[0]
    base = pl.program_id(0) * tm

    def send(r, carry):
        for k in range(TOPK_IN_GROUP):
            _row_copy(h_ref, r, buf_ref, dest_ref[(base + r) * TOPK_IN_GROUP + k], sem).start()
        return carry

    def drain(r, carry):
        for k in range(TOPK_IN_GROUP):
            _row_copy(h_ref, r, buf_ref, 0, sem).wait()
        return carry

    lax.fori_loop(0, tm, send, 0)
    lax.fori_loop(0, tm, drain, 0)


def moe_dispatch(dest, h3, n_rows, tm=512):
    t, _, d = h3.shape
    grid_spec = pltpu.PrefetchScalarGridSpec(
        num_scalar_prefetch=1, grid=(t // tm,),
        in_specs=[pl.BlockSpec((tm, 1, d), lambda i, dest: (i, 0, 0)), pl.BlockSpec(memory_space=pl.ANY)],
        out_specs=pl.BlockSpec(memory_space=pl.ANY),
        scratch_shapes=[pltpu.SemaphoreType.DMA(())],
    )
    return pl.pallas_call(
        _moe_dispatch_kernel, grid_spec=grid_spec,
        out_shape=jax.ShapeDtypeStruct((n_rows, 1, d), F32),
        input_output_aliases={2: 0},
        compiler_params=pltpu.CompilerParams(dimension_semantics=("arbitrary",), has_side_effects=True),
        name="moe_dispatch",
    )(dest, h3, jnp.zeros((n_rows, 1, d), F32))


def _moe_ffn_kernel(blk_e_ref, n_used_ref, x_ref, wg_ref, wu_ref, wd_ref, y_ref):
    del blk_e_ref
    used = pl.program_id(0) < n_used_ref[0]

    @pl.when(used)
    def _():
        xb = x_ref[...].astype(BF16)
        a = _dot(xb, wg_ref[0])
        mid = a * jax.nn.sigmoid(a) * _dot(xb, wu_ref[0])
        y_ref[...] = _dot(mid.astype(BF16), wd_ref[0])

    @pl.when(jnp.logical_not(used))
    def _():
        y_ref[...] = jnp.zeros_like(y_ref)


def moe_ffn(blk_expert, n_used, buf2d, w_eg, w_eu, w_ed):
    n_rows, d = buf2d.shape
    ff = w_eg.shape[2]
    w_in_spec = pl.BlockSpec((1, d, ff), lambda b, blk_e, n_used: (blk_e[b], 0, 0))
    grid_spec = pltpu.PrefetchScalarGridSpec(
        num_scalar_prefetch=2, grid=(n_rows // MOE_ROWS,),
        in_specs=[pl.BlockSpec((MOE_ROWS, d), lambda b, blk_e, n_used: (b, 0)), w_in_spec, w_in_spec,
                  pl.BlockSpec((1, ff, d), lambda b, blk_e, n_used: (blk_e[b], 0, 0))],
        out_specs=pl.BlockSpec((MOE_ROWS, d), lambda b, blk_e, n_used: (b, 0)),
    )
    return pl.pallas_call(
        _moe_ffn_kernel, grid_spec=grid_spec,
        out_shape=jax.ShapeDtypeStruct((n_rows, d), F32),
        compiler_params=pltpu.CompilerParams(dimension_semantics=("arbitrary",)),
        name="moe_ffn",
    )(blk_expert, n_used, buf2d, w_eg.astype(BF16), w_eu.astype(BF16), w_ed.astype(BF16))


def _moe_combine_kernel(dest_ref, y_ref, gate_ref, out_ref, ya_ref, yb_ref, sem):
    tm = out_ref.shape[0]
    base = pl.program_id(0) * tm

    def fetch(r, carry):
        _row_copy(y_ref, dest_ref[(base + r) * TOPK_IN_GROUP], ya_ref, r, sem).start()
        _row_copy(y_ref, dest_ref[(base + r) * TOPK_IN_GROUP + 1], yb_ref, r, sem).start()
        return carry

    def drain(r, carry):
        _row_copy(y_ref, 0, ya_ref, r, sem).wait()
        _row_copy(y_ref, 0, yb_ref, r, sem).wait()
        return carry

    lax.fori_loop(0, tm, fetch, 0)
    lax.fori_loop(0, tm, drain, 0)
    gates = gate_ref[...]
    out_ref[...] = gates[:, :, 0:1] * ya_ref[...] + gates[:, :, 1:2] * yb_ref[...]


def moe_combine(dest, y3, gates3, tm=512):
    t = gates3.shape[0]
    d = y3.shape[2]
    grid_spec = pltpu.PrefetchScalarGridSpec(
        num_scalar_prefetch=1, grid=(t // tm,),
        in_specs=[pl.BlockSpec(memory_space=pl.ANY), pl.BlockSpec((tm, 1, TOPK_IN_GROUP), lambda i, dest: (i, 0, 0))],
        out_specs=pl.BlockSpec((tm, 1, d), lambda i, dest: (i, 0, 0)),
        scratch_shapes=[pltpu.VMEM((tm, 1, d), F32), pltpu.VMEM((tm, 1, d), F32), pltpu.SemaphoreType.DMA(())],
    )
    return pl.pallas_call(
        _moe_combine_kernel, grid_spec=grid_spec,
        out_shape=jax.ShapeDtypeStruct((t, 1, d), F32),
        compiler_params=pltpu.CompilerParams(dimension_semantics=("arbitrary",)),
        name="moe_combine",
    )(dest, y3, gates3)


def hier_moe_pallas(x2d, gain, w_rg, b_rg, w_re, b_re, w_eg, w_eu, w_ed):
    t, d = x2d.shape
    h, rec, cnt = moe_route(x2d, gain, w_rg, b_rg, w_re, b_re)
    counts = cnt[0, :N_EXPERTS].astype(jnp.int32)
    n_blk = (counts + MOE_ROWS - 1) // MOE_ROWS
    blk_end = jnp.cumsum(n_blk)
    pad_start = (blk_end - n_blk) * MOE_ROWS
    experts = rec[:, 0:2].astype(jnp.int32)
    dest = (pad_start[experts] + rec[:, 2:4].astype(jnp.int32)).reshape(-1)
    total_blk = t * TOPK_IN_GROUP // MOE_ROWS + N_EXPERTS
    blk_expert = jnp.minimum(jnp.searchsorted(blk_end, jnp.arange(total_blk), side='right'), N_EXPERTS - 1).astype(jnp.int32)
    buf = moe_dispatch(dest, h.reshape(t, 1, d), total_blk * MOE_ROWS)
    y = moe_ffn(blk_expert, blk_end[-1:].astype(jnp.int32), buf.reshape(total_blk * MOE_ROWS, d), w_eg, w_eu, w_ed)
    out = moe_combine(dest, y.reshape(total_blk * MOE_ROWS, 1, d), rec[:, 4:6].reshape(t, 1, TOPK_IN_GROUP))
    return out.reshape(t, d)


def kernel(x, norm_mix, w_in, conv_a_w, nsa_qk_gain, cmp_pe, cmp_w1, cmp_w2, gdn_conv_w, gdn_a_log, gdn_dt_bias, gdn_out_gain, w_branch, w_gate, b_gate, w_out, norm_ffn, w_router_group, b_router_group, w_router_expert, b_router_expert, w_expert_gate, w_expert_up, w_expert_down):
    b, s, dm = x.shape
    x = x.reshape(b * s, dm)
    for l in range(DEPTH):
        x = hybrid_mixer(x, b, s, norm_mix[l], w_in[l], conv_a_w[l], nsa_qk_gain[l], cmp_pe[l], cmp_w1[l], cmp_w2[l], gdn_conv_w[l], gdn_a_log[l], gdn_dt_bias[l], gdn_out_gain[l], w_branch[l], w_gate[l], b_gate[l], w_out[l])
        x = x + hier_moe_pallas(x, norm_ffn[l], w_router_group[l], b_router_group[l], w_router_expert[l], b_router_expert[l], w_expert_gate[l], w_expert_up[l], w_expert_down[l])
    return x.reshape(b, s, dm)
```

```python
import functools
import math

import jax
import jax.numpy as jnp
from jax import lax
from jax.experimental import pallas as pl
from jax.experimental.pallas import tpu as pltpu

D_MODEL = 1024
DEPTH = 4
CONV_DIM = 512
CONV_WIDTH = 3
NSA_HEADS = 8
NSA_KV_GROUPS = 2
NSA_HPG = NSA_HEADS // NSA_KV_GROUPS
NSA_HEAD_DIM = 64
NSA_DIM = NSA_HEADS * NSA_HEAD_DIM
CMP_LEN = 32
CMP_STRIDE = 16
CMP_HIDDEN = 256
SLC_LEN = 64
SLC_TOPN = 8
SLC_FORCE_BONUS = 1e6
WINDOW = 512
NSA_QBLOCK = 64
GDN_HEADS = 4
GDN_HEAD_DIM = 128
GDN_DIM = GDN_HEADS * GDN_HEAD_DIM
GDN_CONV = 4
GDN_CHUNK = 64
N_BRANCH = 3
BRANCH_DIM = 512
IN_SPLITS = (3 * CONV_DIM, NSA_DIM, 6 * NSA_KV_GROUPS * NSA_HEAD_DIM, 3 * NSA_HEADS, 3 * GDN_DIM, GDN_DIM, GDN_HEADS, GDN_HEADS)
D_IN = sum(IN_SPLITS)
COL_CONV = 0
COL_GDN_QKV = 3 * CONV_DIM
COL_NSA_Q = COL_GDN_QKV + 3 * GDN_DIM
COL_GDN_Z = COL_NSA_Q + NSA_DIM
COL_NSA_KV = COL_GDN_Z + GDN_DIM
COL_SMALL = COL_NSA_KV + 6 * NSA_KV_GROUPS * NSA_HEAD_DIM
SMALL_LANES = 128
D_U = COL_SMALL + SMALL_LANES
LANE_BETA = 3 * NSA_HEADS
LANE_DECAY = LANE_BETA + GDN_HEADS


def permute_in_proj(w_in):
    conv, nq, nkv, ng, gqkv, gz, gb, ga = split_last(w_in, IN_SPLITS)
    pad = jnp.zeros((w_in.shape[0], SMALL_LANES - LANE_DECAY - GDN_HEADS), w_in.dtype)
    return jnp.concatenate([conv, gqkv, nq, gz, nkv, ng, gb, ga, pad], axis=1)


def split_last(u, sizes):
    out, start = [], 0
    for n in sizes:
        out.append(u[..., start:start + n])
        start += n
    return out
MOE_GROUPS = 4
EXPERTS_PER_GROUP = 8
N_EXPERTS = MOE_GROUPS * EXPERTS_PER_GROUP
TOPK_IN_GROUP = 2
EXPERT_FF = 512
MOE_BLOCK = 256
RMS_EPS = 1e-6
NEG_INF = -1e30

F32 = jnp.float32
BF16 = jnp.bfloat16


def _norm_proj_kernel(x_ref, g_ref, w_ref, o_ref, h_ref):
    @pl.when(pl.program_id(1) == 0)
    def _():
        x = x_ref[...]
        y = x * lax.rsqrt(jnp.mean(x * x, axis=-1, keepdims=True) + RMS_EPS)
        h_ref[...] = (y * g_ref[...]).astype(BF16)

    o_ref[...] = jnp.dot(h_ref[...], w_ref[...], preferred_element_type=F32)


def norm_proj(x2d, gain, w, tm=512, tn=D_U // 3):
    t, d = x2d.shape
    n = w.shape[1]
    n_blk = pl.cdiv(n, tn)
    wb = jnp.pad(w.astype(BF16), ((0, 0), (0, n_blk * tn - n)))
    return pl.pallas_call(
        _norm_proj_kernel,
        grid=(t // tm, n_blk),
        in_specs=[
            pl.BlockSpec((tm, d), lambda i, j: (i, 0)),
            pl.BlockSpec((1, d), lambda i, j: (0, 0)),
            pl.BlockSpec((d, tn), lambda i, j: (0, j)),
        ],
        out_specs=pl.BlockSpec((tm, tn), lambda i, j: (i, j)),
        out_shape=jax.ShapeDtypeStruct((t, n), F32),
        scratch_shapes=[pltpu.VMEM((tm, d), BF16)],
        compiler_params=pltpu.CompilerParams(dimension_semantics=("arbitrary", "arbitrary")),
        name="norm_proj",
    )(x2d, gain.reshape(1, d), wb)


KV_LANES = NSA_KV_GROUPS * NSA_HEAD_DIM
N_WIN_KEYS = WINDOW + NSA_QBLOCK
SEL_CHUNK = 8
IDS_PER_WORD = 4
M_INIT = -1e29


def _dot(a, b):
    return jnp.dot(a, b, preferred_element_type=F32)


def _dot_nt(a, b):
    return lax.dot_general(a, b, (((1,), (1,)), ((), ())), preferred_element_type=F32)


def _split_dot(x, m):
    hi = x.astype(BF16)
    lo = (x - hi.astype(F32)).astype(BF16)
    return _dot(hi, m) + _dot(lo, m)


def _head_slope(g, h):
    return 2.0 ** (-8.0 * (g * NSA_HPG + h + 1) / NSA_HEADS)


def _group_rms(x, bd, gain):
    ms = _split_dot(x * x, bd)
    return x * lax.rsqrt(ms + RMS_EPS) * gain


def _nsa_prep_kernel(uq_ref, uks_ref, uvs_ref, ukw_ref, uvw_ref, gq_ref, gks_ref, gkw_ref, bdq_ref, bdk_ref,
                     q_ref, ks_ref, vs_ref, kw_ref, vw_ref):
    scale = NSA_HEAD_DIM ** -0.5
    q_ref[...] = (_group_rms(uq_ref[...], bdq_ref[...], gq_ref[...]) * scale).astype(BF16)
    ks_ref[...] = _group_rms(uks_ref[...], bdk_ref[...], gks_ref[...]).astype(BF16)
    kw_ref[...] = _group_rms(ukw_ref[...], bdk_ref[...], gkw_ref[...]).astype(BF16)
    vs_ref[...] = uvs_ref[...].astype(BF16)
    vw_ref[...] = uvw_ref[...].astype(BF16)


def _block_diag_mean(n):
    i = jnp.arange(n) // NSA_HEAD_DIM
    return ((i[:, None] == i[None, :]).astype(F32) / NSA_HEAD_DIM).astype(BF16)


def nsa_prep(u, qk_gain, col_q, col_kv, tm=512):
    t = u.shape[0]
    assert t % tm == 0 and col_q % NSA_DIM == 0 and col_kv % KV_LANES == 0
    qb = col_q // NSA_DIM
    kb = col_kv // KV_LANES
    kv_spec = lambda j: pl.BlockSpec((tm, KV_LANES), lambda i, j=j: (i, kb + j))
    full = lambda r, c: pl.BlockSpec((r, c), lambda i: (0, 0))
    row = lambda c: pl.BlockSpec((tm, c), lambda i: (i, 0))
    gq = jnp.tile(qk_gain[0], NSA_HEADS).reshape(1, NSA_DIM)
    gks = jnp.tile(qk_gain[2], NSA_KV_GROUPS).reshape(1, KV_LANES)
    gkw = jnp.tile(qk_gain[3], NSA_KV_GROUPS).reshape(1, KV_LANES)
    return pl.pallas_call(
        _nsa_prep_kernel,
        grid=(t // tm,),
        in_specs=[pl.BlockSpec((tm, NSA_DIM), lambda i: (i, qb)), kv_spec(2), kv_spec(3), kv_spec(4), kv_spec(5),
                  full(1, NSA_DIM), full(1, KV_LANES), full(1, KV_LANES), full(NSA_DIM, NSA_DIM), full(KV_LANES, KV_LANES)],
        out_specs=[row(NSA_DIM), row(KV_LANES), row(KV_LANES), row(KV_LANES), row(KV_LANES)],
        out_shape=[jax.ShapeDtypeStruct((t, NSA_DIM), BF16)] + [jax.ShapeDtypeStruct((t, KV_LANES), BF16)] * 4,
        compiler_params=pltpu.CompilerParams(dimension_semantics=("arbitrary",)),
        name="nsa_prep",
    )(u, u, u, u, u, gq, gks, gkw, _block_diag_mean(NSA_DIM), _block_diag_mean(KV_LANES))


def _gelu_tanh(x):
    return 0.5 * x * (1.0 + jnp.tanh(0.7978845608028654 * (x + 0.044715 * x * x * x)))


def _nsa_compress_kernel(uk_ref, uv_ref, pe_ref, w1_ref, w2_ref, gk_ref, bd_ref, kc_ref, vc_ref):
    n_row = uk_ref.shape[1] // CMP_STRIDE
    d = NSA_HEAD_DIM
    for kv, (src, dst) in enumerate(((uk_ref, kc_ref), (uv_ref, vc_ref))):
        top = [jnp.zeros((n_row, CMP_HIDDEN), F32) for _ in range(NSA_KV_GROUPS)]
        bot = [jnp.zeros((n_row, CMP_HIDDEN), F32) for _ in range(NSA_KV_GROUPS)]
        for l in range(CMP_STRIDE):
            x2 = src[0, pl.ds(l, n_row, stride=CMP_STRIDE), :]
            l2 = l + CMP_STRIDE
            for g in range(NSA_KV_GROUPS):
                x = x2[:, g * d:(g + 1) * d]
                top[g] += _dot((x + pe_ref[kv, l:l + 1, :]).astype(BF16), w1_ref[kv, l * d:(l + 1) * d, :])
                bot[g] += _dot((x + pe_ref[kv, l2:l2 + 1, :]).astype(BF16), w1_ref[kv, l2 * d:(l2 + 1) * d, :])
        outs = []
        for g in range(NSA_KV_GROUPS):
            hid = top[g] + pltpu.roll(bot[g], n_row - 1, 0)
            outs.append(_dot(_gelu_tanh(hid).astype(BF16), w2_ref[kv]))
        y = jnp.concatenate(outs, axis=1)
        if kv == 0:
            y = _group_rms(y, bd_ref[...], gk_ref[...])
        dst[0] = y.astype(BF16)


def nsa_compress(u3, qk_gain, cmp_pe, cmp_w1, cmp_w2, col_kv):
    b, s, _ = u3.shape
    kb = col_kv // KV_LANES
    n_row = s // CMP_STRIDE
    full = lambda *shape: pl.BlockSpec(shape, lambda i: (0,) * len(shape))
    gk = jnp.tile(qk_gain[1], NSA_KV_GROUPS).reshape(1, KV_LANES)
    out_spec = pl.BlockSpec((1, n_row, KV_LANES), lambda i: (i, 0, 0))
    return pl.pallas_call(
        _nsa_compress_kernel,
        grid=(b,),
        in_specs=[pl.BlockSpec((1, s, KV_LANES), lambda i: (i, 0, kb)),
                  pl.BlockSpec((1, s, KV_LANES), lambda i: (i, 0, kb + 1)),
                  full(2, CMP_LEN, NSA_HEAD_DIM), full(2, CMP_LEN * NSA_HEAD_DIM, CMP_HIDDEN),
                  full(2, CMP_HIDDEN, NSA_HEAD_DIM), full(1, KV_LANES), full(KV_LANES, KV_LANES)],
        out_specs=[out_spec, out_spec],
        out_shape=[jax.ShapeDtypeStruct((b, n_row, KV_LANES), BF16)] * 2,
        compiler_params=pltpu.CompilerParams(dimension_semantics=("arbitrary",)),
        name="nsa_compress",
    )(u3, u3, cmp_pe, cmp_w1.astype(BF16), cmp_w2.astype(BF16), gk, _block_diag_mean(KV_LANES))


def _stack_heads(q, g):
    d = NSA_HEAD_DIM
    base = g * NSA_HPG * d
    return jnp.concatenate([q[:, base + h * d: base + (h + 1) * d] for h in range(NSA_HPG)], axis=0)


def _nsa_select_kernel(q_ref, kc_ref, vc_ref, ov_ref, oc_ref, sel_ref, flag_ref):
    qt = pl.program_id(1)
    qbl, d = NSA_QBLOCK, NSA_HEAD_DIM
    n_key = kc_ref.shape[1]
    n_slc = ov_ref.shape[1]
    t = qt * qbl + lax.broadcasted_iota(jnp.int32, (qbl, n_key), 0)
    c = lax.broadcasted_iota(jnp.int32, (qbl, n_key), 1)
    dist = (t - (c * CMP_STRIDE + CMP_LEN - 1)).astype(F32)
    ok = dist >= 0
    q = q_ref[0]
    imps = []
    for g in range(NSA_KV_GROUPS):
        kc = kc_ref[0, :, g * d:(g + 1) * d]
        vc = vc_ref[0, :, g * d:(g + 1) * d]
        s_all = _dot_nt(_stack_heads(q, g), kc)
        p_sum = jnp.zeros((qbl, n_key), F32)
        for h in range(NSA_HPG):
            s = jnp.where(ok, s_all[h * qbl:(h + 1) * qbl] - _head_slope(g, h) * dist, NEG_INF)
            m = jnp.max(s, axis=-1, keepdims=True)
            e = jnp.where(ok, jnp.exp(s - m), 0.0)
            l = jnp.sum(e, axis=-1, keepdims=True)
            p = e * jnp.where(l > 0, 1.0 / l, 0.0)
            col = (g * NSA_HPG + h) * d
            oc_ref[0, :, col:col + d] = _dot(p.astype(BF16), vc)
            p_sum += p
        imps.append(_split_dot(p_sum, ov_ref[...]))
    imp_t = jnp.concatenate(imps, axis=1).T
    j = lax.broadcasted_iota(jnp.int32, (n_slc, qbl), 0)
    forced = (j == 0) | (j == qt) | (j == qt - 1)
    visible = j <= qt
    sels = []
    for g in range(NSA_KV_GROUPS):
        score = jnp.where(visible, imp_t[g * n_slc:(g + 1) * n_slc] + jnp.where(forced, SLC_FORCE_BONUS, 0.0), NEG_INF)
        sel = jnp.zeros((n_slc, qbl), F32)
        for _ in range(min(SLC_TOPN, n_slc)):
            m = jnp.max(score, axis=0, keepdims=True)
            first = jnp.min(jnp.where(score == m, j, n_slc), axis=0, keepdims=True)
            pick = j == first
            sel = jnp.where(pick, 1.0, sel)
            score = jnp.where(pick, -3e38, score)
        sels.append(jnp.where(visible, sel, 0.0))
    sel_all = jnp.concatenate(sels, axis=0).T
    sel_ref[0] = sel_all.astype(BF16)
    flag_ref[0, 0] = jnp.broadcast_to(jnp.max(sel_all, axis=0, keepdims=True), flag_ref.shape[2:])


def nsa_select(q3, k_cmp, v_cmp):
    b, s, _ = q3.shape
    n_qt = s // NSA_QBLOCK
    n_slc = s // SLC_LEN
    n_key = k_cmp.shape[1]
    c_lo = jnp.arange(n_key) * CMP_STRIDE
    j_lo = jnp.arange(n_slc) * SLC_LEN
    overlap = ((c_lo[:, None] < j_lo[None, :] + SLC_LEN) & (c_lo[:, None] + CMP_LEN > j_lo[None, :])).astype(BF16)
    tile = lambda c: pl.BlockSpec((1, NSA_QBLOCK, c), lambda i, t: (i, t, 0))
    per_b = pl.BlockSpec((1, n_key, KV_LANES), lambda i, t: (i, 0, 0))
    return pl.pallas_call(
        _nsa_select_kernel,
        grid=(b, n_qt),
        in_specs=[tile(NSA_DIM), per_b, per_b, pl.BlockSpec((n_key, n_slc), lambda i, t: (0, 0))],
        out_specs=[tile(NSA_DIM), tile(2 * n_slc), pl.BlockSpec((1, 1, 8, 2 * n_slc), lambda i, t: (i, t, 0, 0))],
        out_shape=[jax.ShapeDtypeStruct((b, s, NSA_DIM), F32), jax.ShapeDtypeStruct((b, s, 2 * n_slc), BF16),
                   jax.ShapeDtypeStruct((b, n_qt, 8, 2 * n_slc), F32)],
        compiler_params=pltpu.CompilerParams(dimension_semantics=("arbitrary", "arbitrary")),
        name="nsa_select",
    )(q3, k_cmp, v_cmp, overlap)


def _nsa_attend_kernel(count_ref, list_ref, q_ref, ks_ref, vs_ref, kw_ref, vw_ref, sel_ref, oc_ref, gate_ref, out_ref,
                       ksel_ref, vsel_ref):
    bi, qt = pl.program_id(0), pl.program_id(1)
    n_qt = pl.num_programs(1)
    qbl, d, ch = NSA_QBLOCK, NSA_HEAD_DIM, SEL_CHUNK
    n_slc = sel_ref.shape[2] // NSA_KV_GROUPS
    n_word = n_slc // IDS_PER_WORD
    q = q_ref[0]
    gates = jax.nn.sigmoid(gate_ref[0])
    lane = lax.broadcasted_iota(jnp.int32, (1, ch * SLC_LEN), 1)
    slot_of_lane = lane // SLC_LEN
    t_sel = qt * qbl + lax.broadcasted_iota(jnp.int32, (qbl, ch * SLC_LEN), 0)
    j_iota = lax.broadcasted_iota(jnp.int32, (n_slc, ch * SLC_LEN), 0)
    win_start = jnp.maximum(qt - WINDOW // qbl, 0) * qbl
    t_win = qt * qbl + lax.broadcasted_iota(jnp.int32, (qbl, N_WIN_KEYS), 0)
    dist_win = t_win - (win_start + lax.broadcasted_iota(jnp.int32, (qbl, N_WIN_KEYS), 1))
    ok_win = (dist_win >= 0) & (dist_win < WINDOW)
    dist_win = dist_win.astype(F32)

    for g in range(NSA_KV_GROUPS):
        lanes = slice(g * d, (g + 1) * d)
        qs = _stack_heads(q, g)

        kw = kw_ref[0, pl.ds(pl.multiple_of(win_start, qbl), N_WIN_KEYS), lanes]
        vw = vw_ref[0, pl.ds(pl.multiple_of(win_start, qbl), N_WIN_KEYS), lanes]
        s_all = _dot_nt(qs, kw)
        o_win = []
        for h in range(NSA_HPG):
            s = jnp.where(ok_win, s_all[h * qbl:(h + 1) * qbl] - _head_slope(g, h) * dist_win, NEG_INF)
            e = jnp.exp(s - jnp.max(s, axis=-1, keepdims=True))
            p = e * (1.0 / jnp.sum(e, axis=-1, keepdims=True))
            o_win.append(_dot(p.astype(BF16), vw))

        tile_g = (bi * n_qt + qt) * NSA_KV_GROUPS + g
        n_sel = count_ref[tile_g]
        word0 = tile_g * n_word
        sel_g = sel_ref[0, :, g * n_slc:(g + 1) * n_slc]

        def chunk(c, carry, lanes=lanes, qs=qs, sel_g=sel_g, g=g, n_sel=n_sel, word0=word0):
            m_old, l_old, acc = carry
            j_row = jnp.full((1, ch * SLC_LEN), -1, jnp.int32)
            for slot in range(ch):
                idx = c * ch + slot
                valid = idx < n_sel
                word = list_ref[word0 + c * (ch // IDS_PER_WORD) + slot // IDS_PER_WORD]
                jb = (word >> (8 * (slot % IDS_PER_WORD))) & 0xFF
                rows = pl.ds(pl.multiple_of(jb * SLC_LEN, SLC_LEN), SLC_LEN)
                ksel_ref[slot * SLC_LEN:(slot + 1) * SLC_LEN, :] = ks_ref[0, rows, lanes]
                vsel_ref[slot * SLC_LEN:(slot + 1) * SLC_LEN, :] = vs_ref[0, rows, lanes]
                j_row = jnp.where(slot_of_lane == slot, jnp.where(valid, jb, -1), j_row)
            s_all = _dot_nt(qs, ksel_ref[...])
            chosen = _dot(sel_g, (j_iota == j_row).astype(BF16))
            dist = t_sel - (j_row * SLC_LEN + lane % SLC_LEN)
            ok = (chosen > 0.5) & (dist >= 0)
            dist = dist.astype(F32)
            ps, ms, ls = [], [], []
            for h in range(NSA_HPG):
                rows_h = slice(h * qbl, (h + 1) * qbl)
                s = jnp.where(ok, s_all[rows_h] - _head_slope(g, h) * dist, NEG_INF)
                m_new = jnp.maximum(m_old[rows_h], jnp.max(s, axis=-1, keepdims=True))
                p = jnp.exp(s - m_new)
                alpha = jnp.exp(m_old[rows_h] - m_new)
                ls.append(alpha * l_old[rows_h] + jnp.sum(p, axis=-1, keepdims=True))
                ms.append(m_new)
                ps.append(p.astype(BF16))
            m_new = jnp.concatenate(ms, axis=0)
            alpha = jnp.exp(m_old - m_new)
            acc = alpha * acc + _dot(jnp.concatenate(ps, axis=0), vsel_ref[...])
            return m_new, jnp.concatenate(ls, axis=0), acc

        init = (jnp.full((NSA_HPG * qbl, 1), M_INIT, F32), jnp.zeros((NSA_HPG * qbl, 1), F32),
                jnp.zeros((NSA_HPG * qbl, d), F32))
        _, l_fin, acc = lax.fori_loop(0, (n_sel + ch - 1) // ch, chunk, init)
        o_sel = acc * (1.0 / l_fin)

        for h in range(NSA_HPG):
            hh = g * NSA_HPG + h
            col = hh * d
            out_ref[0, :, col:col + d] = (gates[:, hh:hh + 1] * oc_ref[0, :, col:col + d]
                                          + gates[:, NSA_HEADS + hh:NSA_HEADS + hh + 1] * o_sel[h * qbl:(h + 1) * qbl]
                                          + gates[:, 2 * NSA_HEADS + hh:2 * NSA_HEADS + hh + 1] * o_win[h])


def nsa_attend(counts, lists, q3, ks, vs, kw, vw, sel, o_cmp, u3, col_gate):
    b, s, _ = q3.shape
    n_qt = s // NSA_QBLOCK
    assert col_gate % 128 == 0 and s >= N_WIN_KEYS and SEL_CHUNK % IDS_PER_WORD == 0
    gb = col_gate // 128
    tile = lambda c: pl.BlockSpec((1, NSA_QBLOCK, c), lambda i, t, counts, lists: (i, t, 0))
    per_b = pl.BlockSpec((1, s, KV_LANES), lambda i, t, counts, lists: (i, 0, 0))
    grid_spec = pltpu.PrefetchScalarGridSpec(
        num_scalar_prefetch=2,
        grid=(b, n_qt),
        in_specs=[tile(NSA_DIM), per_b, per_b, per_b, per_b, tile(sel.shape[2]), tile(NSA_DIM),
                  pl.BlockSpec((1, NSA_QBLOCK, 128), lambda i, t, counts, lists: (i, t, gb))],
        out_specs=tile(NSA_DIM),
        scratch_shapes=[pltpu.VMEM((SEL_CHUNK * SLC_LEN, NSA_HEAD_DIM), BF16),
                        pltpu.VMEM((SEL_CHUNK * SLC_LEN, NSA_HEAD_DIM), BF16)],
    )
    return pl.pallas_call(
        _nsa_attend_kernel,
        grid_spec=grid_spec,
        out_shape=jax.ShapeDtypeStruct((b, s, NSA_DIM), F32),
        compiler_params=pltpu.CompilerParams(dimension_semantics=("arbitrary", "arbitrary")),
        name="nsa_attend",
    )(counts, lists, q3, ks, vs, kw, vw, sel, o_cmp, u3)


def _pack_union_lists(flags, n_slc):
    assert n_slc <= 256 and n_slc % IDS_PER_WORD == 0
    b, n_qt = flags.shape[:2]
    f = flags[:, :, 0, :].reshape(b, n_qt, NSA_KV_GROUPS, n_slc) > 0.5
    counts = jnp.sum(f, axis=-1, dtype=jnp.int32)
    order = jnp.argsort(jnp.logical_not(f), axis=-1, stable=True).astype(jnp.int32)
    order = order.reshape(b, n_qt, NSA_KV_GROUPS, n_slc // IDS_PER_WORD, IDS_PER_WORD)
    words = jnp.sum(order << (8 * jnp.arange(IDS_PER_WORD, dtype=jnp.int32)), axis=-1, dtype=jnp.int32)
    return counts.reshape(-1), words.reshape(-1)


def nsa_mixer_pallas(u3, qk_gain, cmp_pe, cmp_w1, cmp_w2, col_q, col_kv, col_gate):
    b, s, d_in = u3.shape
    q, ks, vs, kw, vw = nsa_prep(u3.reshape(b * s, d_in), qk_gain, col_q, col_kv)
    k_cmp, v_cmp = nsa_compress(u3, qk_gain, cmp_pe, cmp_w1, cmp_w2, col_kv)
    q3 = q.reshape(b, s, NSA_DIM)
    r3 = lambda a: a.reshape(b, s, KV_LANES)
    o_cmp, sel, flags = nsa_select(q3, k_cmp, v_cmp)
    counts, lists = _pack_union_lists(flags, s // SLC_LEN)
    return nsa_attend(counts, lists, q3, r3(ks), r3(vs), r3(kw), r3(vw), sel, o_cmp, u3, col_gate)


GDN_TILE = 256
GDN_HALO = 8


def _dot3(a, b):
    ah = a.astype(BF16)
    bh = b.astype(BF16)
    al = (a - ah.astype(F32)).astype(BF16)
    bl = (b - bh.astype(F32)).astype(BF16)
    return _dot(ah, bh) + _dot(ah, bl) + _dot(al, bh)


def _dot_tn(a, b):
    return lax.dot_general(a, b, (((0,), (0,)), ((), ())), preferred_element_type=F32)


def _softplus(x):
    return jnp.maximum(x, 0.0) + jnp.log(1.0 + jnp.exp(-jnp.abs(x)))


def _l2_norm(x):
    return x * lax.rsqrt(jnp.sum(x * x, axis=-1, keepdims=True) + RMS_EPS)


def _gdn_kernel(qkv_ref, z_ref, small_ref, cw_ref, coef_ref, gain_ref, y_ref, xe_ref, state_ref):
    tt = pl.program_id(1)
    tile, c, hd = GDN_TILE, GDN_CHUNK, GDN_HEAD_DIM

    @pl.when(tt == 0)
    def _():
        xe_ref[0:GDN_HALO, :] = jnp.zeros((GDN_HALO, xe_ref.shape[1]), F32)
        state_ref[...] = jnp.zeros_like(state_ref)

    @pl.when(tt > 0)
    def _():
        xe_ref[0:GDN_HALO, :] = xe_ref[tile:tile + GDN_HALO, :]

    xe_ref[GDN_HALO:, :] = qkv_ref[0]
    conv = jnp.zeros((tile, xe_ref.shape[1]), F32)
    for j in range(GDN_CONV):
        conv += cw_ref[j:j + 1, :] * xe_ref[pl.ds(GDN_HALO - (GDN_CONV - 1) + j, tile), :]
    act = conv * jax.nn.sigmoid(conv)

    small = small_ref[0]
    beta_all = jax.nn.sigmoid(small)
    g_all = coef_ref[0:1, :] * _softplus(small + coef_ref[1:2, :])
    row = lax.broadcasted_iota(jnp.int32, (c, SMALL_LANES), 0)
    ri = lax.broadcasted_iota(jnp.int32, (c, c), 0)
    ci = lax.broadcasted_iota(jnp.int32, (c, c), 1)
    lower = ri >= ci
    strict = ri > ci

    n_chunk = tile // c
    pairs = [(n, h) for n in range(n_chunk) for h in range(GDN_HEADS)]
    gcs, gc_ts = [], []
    for n in range(n_chunk):
        gc = g_all[n * c:(n + 1) * c]
        shift = 1
        while shift < c:
            gc = gc + jnp.where(row >= shift, pltpu.roll(gc, shift, 0), 0.0)
            shift *= 2
        gcs.append(gc)
        gc_ts.append(gc.T)
    pre = []
    for n, h in pairs:
        rows = slice(n * c, (n + 1) * c)
        q = _l2_norm(act[rows, h * hd:(h + 1) * hd]) * hd ** -0.5
        k = _l2_norm(act[rows, GDN_DIM + h * hd:GDN_DIM + (h + 1) * hd])
        v = act[rows, 2 * GDN_DIM + h * hd:2 * GDN_DIM + (h + 1) * hd]
        beta = beta_all[rows, LANE_BETA + h:LANE_BETA + h + 1]
        gcol = gcs[n][:, LANE_DECAY + h:LANE_DECAY + h + 1]
        grow = gc_ts[n][LANE_DECAY + h:LANE_DECAY + h + 1, :]
        g_last = gcol[c - 1:c, :]
        decay = jnp.where(lower, jnp.exp(jnp.where(lower, gcol - grow, 0.0)), 0.0)
        kb = k * beta
        kh = k.astype(BF16)
        pre.append(dict(
            lmat=jnp.where(strict, _dot_nt(kb.astype(BF16), kh) * decay, 0.0),
            rhs=jnp.concatenate([v * beta, kb * jnp.exp(gcol)], axis=1),
            attn=(_dot_nt(q.astype(BF16), kh) * decay).astype(BF16),
            q_dec=(q * jnp.exp(gcol)).astype(BF16),
            k_dec=(k * jnp.exp(g_last - gcol)).astype(BF16),
            d_last=jnp.exp(g_last)))
    power = [p['lmat'].astype(BF16) for p in pre]
    rhs = [p['rhs'] - _dot(lm, p['rhs'].astype(BF16)) for lm, p in zip(power, pre)]
    for _ in range(5):
        power = [_dot(lm, lm).astype(BF16) for lm in power]
        rhs = [r + _dot(lm, r.astype(BF16)) for lm, r in zip(power, rhs)]

    state = [state_ref[h] for h in range(GDN_HEADS)]
    for n in range(n_chunk):
        rows = slice(n * c, (n + 1) * c)
        sb = [s.astype(BF16) for s in state]
        ps = [pre[n * GDN_HEADS + h] for h in range(GDN_HEADS)]
        rs = [rhs[n * GDN_HEADS + h] for h in range(GDN_HEADS)]
        v_new = [(r[:, :hd] - _dot(r[:, hd:].astype(BF16), s)).astype(BF16) for r, s in zip(rs, sb)]
        outs = [_dot(p['q_dec'], s) + _dot(p['attn'], vn) for p, s, vn in zip(ps, sb, v_new)]
        state = [s * p['d_last'] + _dot_tn(p['k_dec'], vn) for p, s, vn in zip(ps, state, v_new)]
        for h, o in enumerate(outs):
            o = o * lax.rsqrt(jnp.mean(o * o, axis=-1, keepdims=True) + RMS_EPS) * gain_ref[...]
            zz = z_ref[0, rows, h * hd:(h + 1) * hd]
            y_ref[0, rows, h * hd:(h + 1) * hd] = o * (zz * jax.nn.sigmoid(zz))
    for h in range(GDN_HEADS):
        state_ref[h] = state[h]


def gdn_mixer(u3, col_qkv, col_z, col_small, conv_w, a_log, dt_bias, out_gain):
    b, s, _ = u3.shape
    lane = jnp.arange(SMALL_LANES)
    in_decay = (lane >= LANE_DECAY) & (lane < LANE_DECAY + GDN_HEADS)
    idx = jnp.clip(lane - LANE_DECAY, 0, GDN_HEADS - 1)
    coef = jnp.stack([jnp.where(in_decay, -jnp.exp(a_log)[idx], 0.0), jnp.where(in_decay, dt_bias[idx], 0.0)])
    qkv_w = 3 * GDN_DIM
    assert col_qkv % qkv_w == 0 and col_z % GDN_DIM == 0 and col_small % SMALL_LANES == 0
    full = lambda r, cc: pl.BlockSpec((r, cc), lambda i, t: (0, 0))
    return pl.pallas_call(
        _gdn_kernel,
        grid=(b, s // GDN_TILE),
        in_specs=[pl.BlockSpec((1, GDN_TILE, qkv_w), lambda i, t: (i, t, col_qkv // qkv_w)),
                  pl.BlockSpec((1, GDN_TILE, GDN_DIM), lambda i, t: (i, t, col_z // GDN_DIM)),
                  pl.BlockSpec((1, GDN_TILE, SMALL_LANES), lambda i, t: (i, t, col_small // SMALL_LANES)),
                  full(GDN_CONV, qkv_w), full(2, SMALL_LANES), full(1, GDN_HEAD_DIM)],
        out_specs=pl.BlockSpec((1, GDN_TILE, GDN_DIM), lambda i, t: (i, t, 0)),
        out_shape=jax.ShapeDtypeStruct((b, s, GDN_DIM), F32),
        scratch_shapes=[pltpu.VMEM((GDN_TILE + GDN_HALO, qkv_w), F32), pltpu.VMEM((GDN_HEADS, GDN_HEAD_DIM, GDN_HEAD_DIM), F32)],
        compiler_params=pltpu.CompilerParams(dimension_semantics=("arbitrary", "arbitrary")),
        name="gdn_mixer",
    )(u3, u3, u3, conv_w, coef, out_gain.reshape(1, GDN_HEAD_DIM))


CONV_TILE = 512


def _short_conv_kernel(u_ref, cw_ref, y_ref, xe_ref):
    tile, cd = CONV_TILE, CONV_DIM

    @pl.when(pl.program_id(1) == 0)
    def _():
        xe_ref[0:GDN_HALO, :] = jnp.zeros((GDN_HALO, cd), F32)

    @pl.when(pl.program_id(1) > 0)
    def _():
        xe_ref[0:GDN_HALO, :] = xe_ref[tile:tile + GDN_HALO, :]

    xe_ref[GDN_HALO:, :] = u_ref[0, :, cd:2 * cd] * u_ref[0, :, 2 * cd:3 * cd]
    conv = jnp.zeros((tile, cd), F32)
    for j in range(CONV_WIDTH):
        conv += cw_ref[j:j + 1, :] * xe_ref[pl.ds(GDN_HALO - (CONV_WIDTH - 1) + j, tile), :]
    y_ref[0] = u_ref[0, :, 0:cd] * conv


def short_conv(u3, conv_w):
    b, s, _ = u3.shape
    return pl.pallas_call(
        _short_conv_kernel,
        grid=(b, s // CONV_TILE),
        in_specs=[pl.BlockSpec((1, CONV_TILE, 3 * CONV_DIM), lambda i, t: (i, t, 0)),
                  pl.BlockSpec((CONV_WIDTH, CONV_DIM), lambda i, t: (0, 0))],
        out_specs=pl.BlockSpec((1, CONV_TILE, CONV_DIM), lambda i, t: (i, t, 0)),
        out_shape=jax.ShapeDtypeStruct((b, s, CONV_DIM), F32),
        scratch_shapes=[pltpu.VMEM((CONV_TILE + GDN_HALO, CONV_DIM), F32)],
        compiler_params=pltpu.CompilerParams(dimension_semantics=("arbitrary", "arbitrary")),
        name="short_conv",
    )(u3, conv_w)


def _merge_kernel(x_ref, g_ref, ya_ref, yn_ref, yg_ref, wb_ref, wg_ref, bg_ref, wo_ref, o_ref):
    x = x_ref[...]
    d = x.shape[1]
    h = (x * lax.rsqrt(jnp.mean(x * x, axis=-1, keepdims=True) + RMS_EPS) * g_ref[...]).astype(BF16)
    merged = jnp.zeros(x.shape, F32)
    for r, y_ref in enumerate((ya_ref, yn_ref, yg_ref)):
        gate = jax.nn.sigmoid(_dot(h, wg_ref[:, r * d:(r + 1) * d]) + bg_ref[:, r * d:(r + 1) * d])
        merged += gate * _dot(y_ref[...].astype(BF16), wb_ref[r])
    o_ref[...] = x + _dot(merged.astype(BF16), wo_ref[...])


def merge_branches(x2d, gain, y_a, y_n, y_g, w_branch, w_gate, b_gate, w_out, tm=512):
    t, d = x2d.shape
    row = lambda c: pl.BlockSpec((tm, c), lambda i: (i, 0))
    full = lambda *shape: pl.BlockSpec(shape, lambda i: (0,) * len(shape))
    return pl.pallas_call(
        _merge_kernel,
        grid=(t // tm,),
        in_specs=[row(d), full(1, d), row(BRANCH_DIM), row(BRANCH_DIM), row(BRANCH_DIM),
                  full(N_BRANCH, BRANCH_DIM, d), full(d, N_BRANCH * d), full(1, N_BRANCH * d), full(d, d)],
        out_specs=row(d),
        out_shape=jax.ShapeDtypeStruct((t, d), F32),
        compiler_params=pltpu.CompilerParams(dimension_semantics=("arbitrary",), vmem_limit_bytes=56 * 1024 * 1024),
        name="merge_branches",
    )(x2d, gain.reshape(1, d), y_a, y_n, y_g, w_branch.astype(BF16), w_gate.astype(BF16), b_gate.reshape(1, -1), w_out.astype(BF16))


def hybrid_mixer(x2d, b, s, norm_gain, w_in, conv_a_w, nsa_qk_gain, cmp_pe, cmp_w1, cmp_w2, gdn_conv_w, gdn_a_log, gdn_dt_bias, gdn_out_gain, w_branch, w_gate, b_gate, w_out):
    t = b * s
    u3 = norm_proj(x2d, norm_gain, permute_in_proj(w_in)).reshape(b, s, D_U)
    y_a = short_conv(u3, conv_a_w)
    y_n = nsa_mixer_pallas(u3, nsa_qk_gain, cmp_pe, cmp_w1, cmp_w2, COL_NSA_Q, COL_NSA_KV, COL_SMALL)
    y_g = gdn_mixer(u3, COL_GDN_QKV, COL_GDN_Z, COL_SMALL, gdn_conv_w, gdn_a_log, gdn_dt_bias, gdn_out_gain)
    return merge_branches(x2d, norm_gain, y_a.reshape(t, BRANCH_DIM), y_n.reshape(t, BRANCH_DIM), y_g.reshape(t, BRANCH_DIM), w_branch, w_gate, b_gate, w_out)


MOE_ROWS = 512
ROUTE_LANES = 128
N_ROUTER = MOE_GROUPS + N_EXPERTS


def _moe_route_kernel(x_ref, g_ref, wr_ref, br_ref, tri_ref, h_ref, route_ref, cnt_ref, run_ref):
    @pl.when(pl.program_id(0) == 0)
    def _():
        run_ref[...] = jnp.zeros_like(run_ref)

    x = x_ref[...]
    h = x * lax.rsqrt(jnp.mean(x * x, axis=-1, keepdims=True) + RMS_EPS) * g_ref[...]
    h_ref[...] = h
    logits = _dot(h.astype(BF16), wr_ref[...]) + br_ref[...]
    lane = lax.broadcasted_iota(jnp.int32, logits.shape, 1)
    first_of = lambda hit: jnp.min(jnp.where(hit, lane, ROUTE_LANES), axis=-1, keepdims=True)
    is_grp = lane < MOE_GROUPS
    lg = jnp.where(is_grp, logits, NEG_INF)
    m_g = jnp.max(lg, axis=-1, keepdims=True)
    grp = first_of(lg == m_g)
    p_grp = 1.0 / jnp.sum(jnp.where(is_grp, jnp.exp(lg - m_g), 0.0), axis=-1, keepdims=True)
    lo = MOE_GROUPS + grp * EXPERTS_PER_GROUP
    le = jnp.where((lane >= lo) & (lane < lo + EXPERTS_PER_GROUP), logits, NEG_INF)
    m1 = jnp.max(le, axis=-1, keepdims=True)
    i1 = first_of(le == m1)
    le2 = jnp.where(lane == i1, NEG_INF, le)
    m2 = jnp.max(le2, axis=-1, keepdims=True)
    i2 = first_of(le2 == m2)
    r = jnp.exp(m2 - m1)
    g1 = p_grp / (1.0 + r)
    g2 = p_grp * r / (1.0 + r)
    e1 = i1 - MOE_GROUPS
    e2 = i2 - MOE_GROUPS
    hit1 = lane == e1
    hit2 = lane == e2
    onehot = (hit1 | hit2).astype(BF16)
    before = _dot(tri_ref[...], onehot) + run_ref[...]
    r1 = jnp.sum(jnp.where(hit1, before, 0.0), axis=-1, keepdims=True)
    r2 = jnp.sum(jnp.where(hit2, before, 0.0), axis=-1, keepdims=True)
    run_ref[...] += jnp.sum(onehot.astype(F32), axis=0, keepdims=True)
    rec = jnp.zeros(logits.shape, F32)
    for k, v in enumerate((e1.astype(F32), e2.astype(F32), r1, r2, g1, g2)):
        rec = jnp.where(lane == k, v, rec)
    route_ref[...] = rec
    cnt_ref[...] = jnp.broadcast_to(run_ref[...], cnt_ref.shape)


def moe_route(x2d, gain, w_rg, b_rg, w_re, b_re, tm=512):
    t, d = x2d.shape
    pad = ROUTE_LANES - N_ROUTER
    wr = jnp.pad(jnp.concatenate([w_rg, w_re], axis=1), ((0, 0), (0, pad))).astype(BF16)
    br = jnp.pad(jnp.concatenate([b_rg, b_re]), (0, pad)).reshape(1, ROUTE_LANES)
    tri = (jnp.arange(tm)[:, None] > jnp.arange(tm)[None, :]).astype(BF16)
    full = lambda r, c: pl.BlockSpec((r, c), lambda i: (0, 0))
    return pl.pallas_call(
        _moe_route_kernel,
        grid=(t // tm,),
        in_specs=[pl.BlockSpec((tm, d), lambda i: (i, 0)), full(1, d), full(d, ROUTE_LANES), full(1, ROUTE_LANES), full(tm, tm)],
        out_specs=[pl.BlockSpec((tm, d), lambda i: (i, 0)), pl.BlockSpec((tm, ROUTE_LANES), lambda i: (i, 0)), full(8, ROUTE_LANES)],
        out_shape=[jax.ShapeDtypeStruct((t, d), F32), jax.ShapeDtypeStruct((t, ROUTE_LANES), F32), jax.ShapeDtypeStruct((8, ROUTE_LANES), F32)],
        scratch_shapes=[pltpu.VMEM((1, ROUTE_LANES), F32)],
        compiler_params=pltpu.CompilerParams(dimension_semantics=("arbitrary",)),
        name="moe_route",
    )(x2d, gain.reshape(1, d), wr, br, tri)


def _row_copy(src_ref, src_row, dst_ref, dst_row, sem):
    return pltpu.make_async_copy(src_ref.at[src_row], dst_ref.at[dst_row], sem)


def _moe_dispatch_kernel(dest_ref, h_ref, buf_in_ref, buf_ref, sem):
    del buf_in_ref
    tm = h_ref.shape[0]
    base = pl.program_id(0) * tm

    def send(r, carry):
        for k in range(TOPK_IN_GROUP):
            _row_copy(h_ref, r, buf_ref, dest_ref[(base + r) * TOPK_IN_GROUP + k], sem).start()
        return carry

    def drain(r, carry):
        for k in range(TOPK_IN_GROUP):
            _row_copy(h_ref, r, buf_ref, 0, sem).wait()
        return carry

    lax.fori_loop(0, tm, send, 0)
    lax.fori_loop(0, tm, drain, 0)


def moe_dispatch(dest, h3, n_rows, tm=512):
    t, _, d = h3.shape
    grid_spec = pltpu.PrefetchScalarGridSpec(
        num_scalar_prefetch=1, grid=(t // tm,),
        in_specs=[pl.BlockSpec((tm, 1, d), lambda i, dest: (i, 0, 0)), pl.BlockSpec(memory_space=pl.ANY)],
        out_specs=pl.BlockSpec(memory_space=pl.ANY),
        scratch_shapes=[pltpu.SemaphoreType.DMA(())],
    )
    return pl.pallas_call(
        _moe_dispatch_kernel, grid_spec=grid_spec,
        out_shape=jax.ShapeDtypeStruct((n_rows, 1, d), F32),
        input_output_aliases={2: 0},
        compiler_params=pltpu.CompilerParams(dimension_semantics=("arbitrary",), has_side_effects=True),
        name="moe_dispatch",
    )(dest, h3, jnp.zeros((n_rows, 1, d), F32))


def _moe_ffn_kernel(blk_e_ref, n_used_ref, x_ref, wg_ref, wu_ref, wd_ref, y_ref):
    del blk_e_ref
    used = pl.program_id(0) < n_used_ref[0]

    @pl.when(used)
    def _():
        xb = x_ref[...].astype(BF16)
        a = _dot(xb, wg_ref[0])
        mid = a * jax.nn.sigmoid(a) * _dot(xb, wu_ref[0])
        y_ref[...] = _dot(mid.astype(BF16), wd_ref[0])

    @pl.when(jnp.logical_not(used))
    def _():
        y_ref[...] = jnp.zeros_like(y_ref)


def moe_ffn(blk_expert, n_used, buf2d, w_eg, w_eu, w_ed):
    n_rows, d = buf2d.shape
    ff = w_eg.shape[2]
    w_in_spec = pl.BlockSpec((1, d, ff), lambda b, blk_e, n_used: (blk_e[b], 0, 0))
    grid_spec = pltpu.PrefetchScalarGridSpec(
        num_scalar_prefetch=2, grid=(n_rows // MOE_ROWS,),
        in_specs=[pl.BlockSpec((MOE_ROWS, d), lambda b, blk_e, n_used: (b, 0)), w_in_spec, w_in_spec,
                  pl.BlockSpec((1, ff, d), lambda b, blk_e, n_used: (blk_e[b], 0, 0))],
        out_specs=pl.BlockSpec((MOE_ROWS, d), lambda b, blk_e, n_used: (b, 0)),
    )
    return pl.pallas_call(
        _moe_ffn_kernel, grid_spec=grid_spec,
        out_shape=jax.ShapeDtypeStruct((n_rows, d), F32),
        compiler_params=pltpu.CompilerParams(dimension_semantics=("arbitrary",)),
        name="moe_ffn",
    )(blk_expert, n_used, buf2d, w_eg.astype(BF16), w_eu.astype(BF16), w_ed.astype(BF16))


def _moe_combine_kernel(dest_ref, y_ref, gate_ref, out_ref, ya_ref, yb_ref, sem):
    tm = out_ref.shape[0]
    base = pl.program_id(0) * tm

    def fetch(r, carry):
        _row_copy(y_ref, dest_ref[(base + r) * TOPK_IN_GROUP], ya_ref, r, sem).start()
        _row_copy(y_ref, dest_ref[(base + r) * TOPK_IN_GROUP + 1], yb_ref, r, sem).start()
        return carry

    def drain(r, carry):
        _row_copy(y_ref, 0, ya_ref, r, sem).wait()
        _row_copy(y_ref, 0, yb_ref, r, sem).wait()
        return carry

    lax.fori_loop(0, tm, fetch, 0)
    lax.fori_loop(0, tm, drain, 0)
    gates = gate_ref[...]
    out_ref[...] = gates[:, :, 0:1] * ya_ref[...] + gates[:, :, 1:2] * yb_ref[...]


def moe_combine(dest, y3, gates3, tm=512):
    t = gates3.shape[0]
    d = y3.shape[2]
    grid_spec = pltpu.PrefetchScalarGridSpec(
        num_scalar_prefetch=1, grid=(t // tm,),
        in_specs=[pl.BlockSpec(memory_space=pl.ANY), pl.BlockSpec((tm, 1, TOPK_IN_GROUP), lambda i, dest: (i, 0, 0))],
        out_specs=pl.BlockSpec((tm, 1, d), lambda i, dest: (i, 0, 0)),
        scratch_shapes=[pltpu.VMEM((tm, 1, d), F32), pltpu.VMEM((tm, 1, d), F32), pltpu.SemaphoreType.DMA(())],
    )
    return pl.pallas_call(
        _moe_combine_kernel, grid_spec=grid_spec,
        out_shape=jax.ShapeDtypeStruct((t, 1, d), F32),
        compiler_params=pltpu.CompilerParams(dimension_semantics=("arbitrary",)),
        name="moe_combine",
    )(dest, y3, gates3)


def hier_moe_pallas(x2d, gain, w_rg, b_rg, w_re, b_re, w_eg, w_eu, w_ed):
    t, d = x2d.shape
    h, rec, cnt = moe_route(x2d, gain, w_rg, b_rg, w_re, b_re)
    counts = cnt[0, :N_EXPERTS].astype(jnp.int32)
    n_blk = (counts + MOE_ROWS - 1) // MOE_ROWS
    blk_end = jnp.cumsum(n_blk)
    pad_start = (blk_end - n_blk) * MOE_ROWS
    experts = rec[:, 0:2].astype(jnp.int32)
    dest = (pad_start[experts] + rec[:, 2:4].astype(jnp.int32)).reshape(-1)
    total_blk = t * TOPK_IN_GROUP // MOE_ROWS + N_EXPERTS
    blk_expert = jnp.minimum(jnp.searchsorted(blk_end, jnp.arange(total_blk), side='right'), N_EXPERTS - 1).astype(jnp.int32)
    buf = moe_dispatch(dest, h.reshape(t, 1, d), total_blk * MOE_ROWS)
    y = moe_ffn(blk_expert, blk_end[-1:].astype(jnp.int32), buf.reshape(total_blk * MOE_ROWS, d), w_eg, w_eu, w_ed)
    out = moe_combine(dest, y.reshape(total_blk * MOE_ROWS, 1, d), rec[:, 4:6].reshape(t, 1, TOPK_IN_GROUP))
    return out.reshape(t, d)


def kernel(x, norm_mix, w_in, conv_a_w, nsa_qk_gain, cmp_pe, cmp_w1, cmp_w2, gdn_conv_w, gdn_a_log, gdn_dt_bias, gdn_out_gain, w_branch, w_gate, b_gate, w_out, norm_ffn, w_router_group, b_router_group, w_router_expert, b_router_expert, w_expert_gate, w_expert_up, w_expert_down):
    b, s, dm = x.shape
    x = x.reshape(b * s, dm)
    for l in range(DEPTH):
        x = hybrid_mixer(x, b, s, norm_mix[l], w_in[l], conv_a_w[l], nsa_qk_gain[l], cmp_pe[l], cmp_w1[l], cmp_w2[l], gdn_conv_w[l], gdn_a_log[l], gdn_dt_bias[l], gdn_out_gain[l], w_branch[l], w_gate[l], b_gate[l], w_out[l])
        x = x + hier_moe_pallas(x, norm_ffn[l], w_router_group[l], b_router_group[l], w_router_expert[l], b_router_expert[l], w_expert_gate[l], w_expert_up[l], w_expert_down[l])
    return x.reshape(b, s, dm)
```

```python
import functools
import math

import jax
import jax.numpy as jnp
from jax import lax
from jax.experimental import pallas as pl
from jax.experimental.pallas import tpu as pltpu

D_MODEL = 1024
DEPTH = 4
CONV_DIM = 512
CONV_WIDTH = 3
NSA_HEADS = 8
NSA_KV_GROUPS = 2
NSA_HPG = NSA_HEADS // NSA_KV_GROUPS
NSA_HEAD_DIM = 64
NSA_DIM = NSA_HEADS * NSA_HEAD_DIM
CMP_LEN = 32
CMP_STRIDE = 16
CMP_HIDDEN = 256
SLC_LEN = 64
SLC_TOPN = 8
SLC_FORCE_BONUS = 1e6
WINDOW = 512
NSA_QBLOCK = 64
GDN_HEADS = 4
GDN_HEAD_DIM = 128
GDN_DIM = GDN_HEADS * GDN_HEAD_DIM
GDN_CONV = 4
GDN_CHUNK = 64
N_BRANCH = 3
BRANCH_DIM = 512
IN_SPLITS = (3 * CONV_DIM, NSA_DIM, 6 * NSA_KV_GROUPS * NSA_HEAD_DIM, 3 * NSA_HEADS, 3 * GDN_DIM, GDN_DIM, GDN_HEADS, GDN_HEADS)
D_IN = sum(IN_SPLITS)
COL_CONV = 0
COL_GDN_QKV = 3 * CONV_DIM
COL_NSA_Q = COL_GDN_QKV + 3 * GDN_DIM
COL_GDN_Z = COL_NSA_Q + NSA_DIM
COL_NSA_KV = COL_GDN_Z + GDN_DIM
COL_SMALL = COL_NSA_KV + 6 * NSA_KV_GROUPS * NSA_HEAD_DIM
SMALL_LANES = 128
D_U = COL_SMALL + SMALL_LANES
LANE_BETA = 3 * NSA_HEADS
LANE_DECAY = LANE_BETA + GDN_HEADS


def permute_in_proj(w_in):
    conv, nq, nkv, ng, gqkv, gz, gb, ga = split_last(w_in, IN_SPLITS)
    pad = jnp.zeros((w_in.shape[0], SMALL_LANES - LANE_DECAY - GDN_HEADS), w_in.dtype)
    return jnp.concatenate([conv, gqkv, nq, gz, nkv, ng, gb, ga, pad], axis=1)


def split_last(u, sizes):
    out, start = [], 0
    for n in sizes:
        out.append(u[..., start:start + n])
        start += n
    return out
MOE_GROUPS = 4
EXPERTS_PER_GROUP = 8
N_EXPERTS = MOE_GROUPS * EXPERTS_PER_GROUP
TOPK_IN_GROUP = 2
EXPERT_FF = 512
MOE_BLOCK = 256
RMS_EPS = 1e-6
NEG_INF = -1e30

F32 = jnp.float32
BF16 = jnp.bfloat16


def _norm_proj_kernel(x_ref, g_ref, w_ref, o_ref, h_ref):
    @pl.when(pl.program_id(1) == 0)
    def _():
        x = x_ref[...]
        y = x * lax.rsqrt(jnp.mean(x * x, axis=-1, keepdims=True) + RMS_EPS)
        h_ref[...] = (y * g_ref[...]).astype(BF16)

    o_ref[...] = jnp.dot(h_ref[...], w_ref[...], preferred_element_type=F32)


def norm_proj(x2d, gain, w, tm=512, tn=D_U // 3):
    t, d = x2d.shape
    n = w.shape[1]
    n_blk = pl.cdiv(n, tn)
    wb = jnp.pad(w.astype(BF16), ((0, 0), (0, n_blk * tn - n)))
    return pl.pallas_call(
        _norm_proj_kernel,
        grid=(t // tm, n_blk),
        in_specs=[
            pl.BlockSpec((tm, d), lambda i, j: (i, 0)),
            pl.BlockSpec((1, d), lambda i, j: (0, 0)),
            pl.BlockSpec((d, tn), lambda i, j: (0, j)),
        ],
        out_specs=pl.BlockSpec((tm, tn), lambda i, j: (i, j)),
        out_shape=jax.ShapeDtypeStruct((t, n), F32),
        scratch_shapes=[pltpu.VMEM((tm, d), BF16)],
        compiler_params=pltpu.CompilerParams(dimension_semantics=("arbitrary", "arbitrary")),
        name="norm_proj",
    )(x2d, gain.reshape(1, d), wb)


KV_LANES = NSA_KV_GROUPS * NSA_HEAD_DIM
N_WIN_KEYS = WINDOW + NSA_QBLOCK
SEL_CHUNK = 8
IDS_PER_WORD = 4
M_INIT = -1e29


def _dot(a, b):
    return jnp.dot(a, b, preferred_element_type=F32)


def _dot_nt(a, b):
    return lax.dot_general(a, b, (((1,), (1,)), ((), ())), preferred_element_type=F32)


def _split_dot(x, m):
    hi = x.astype(BF16)
    lo = (x - hi.astype(F32)).astype(BF16)
    return _dot(hi, m) + _dot(lo, m)


def _head_slope(g, h):
    return 2.0 ** (-8.0 * (g * NSA_HPG + h + 1) / NSA_HEADS)


def _group_rms(x, bd, gain):
    ms = _split_dot(x * x, bd)
    return x * lax.rsqrt(ms + RMS_EPS) * gain


def _nsa_prep_kernel(uq_ref, uks_ref, uvs_ref, ukw_ref, uvw_ref, gq_ref, gks_ref, gkw_ref, bdq_ref, bdk_ref,
                     q_ref, ks_ref, vs_ref, kw_ref, vw_ref):
    scale = NSA_HEAD_DIM ** -0.5
    q_ref[...] = (_group_rms(uq_ref[...], bdq_ref[...], gq_ref[...]) * scale).astype(BF16)
    ks_ref[...] = _group_rms(uks_ref[...], bdk_ref[...], gks_ref[...]).astype(BF16)
    kw_ref[...] = _group_rms(ukw_ref[...], bdk_ref[...], gkw_ref[...]).astype(BF16)
    vs_ref[...] = uvs_ref[...].astype(BF16)
    vw_ref[...] = uvw_ref[...].astype(BF16)


def _block_diag_mean(n):
    i = jnp.arange(n) // NSA_HEAD_DIM
    return ((i[:, None] == i[None, :]).astype(F32) / NSA_HEAD_DIM).astype(BF16)


def nsa_prep(u, qk_gain, col_q, col_kv, tm=512):
    t = u.shape[0]
    assert t % tm == 0 and col_q % NSA_DIM == 0 and col_kv % KV_LANES == 0
    qb = col_q // NSA_DIM
    kb = col_kv // KV_LANES
    kv_spec = lambda j: pl.BlockSpec((tm, KV_LANES), lambda i, j=j: (i, kb + j))
    full = lambda r, c: pl.BlockSpec((r, c), lambda i: (0, 0))
    row = lambda c: pl.BlockSpec((tm, c), lambda i: (i, 0))
    gq = jnp.tile(qk_gain[0], NSA_HEADS).reshape(1, NSA_DIM)
    gks = jnp.tile(qk_gain[2], NSA_KV_GROUPS).reshape(1, KV_LANES)
    gkw = jnp.tile(qk_gain[3], NSA_KV_GROUPS).reshape(1, KV_LANES)
    return pl.pallas_call(
        _nsa_prep_kernel,
        grid=(t // tm,),
        in_specs=[pl.BlockSpec((tm, NSA_DIM), lambda i: (i, qb)), kv_spec(2), kv_spec(3), kv_spec(4), kv_spec(5),
                  full(1, NSA_DIM), full(1, KV_LANES), full(1, KV_LANES), full(NSA_DIM, NSA_DIM), full(KV_LANES, KV_LANES)],
        out_specs=[row(NSA_DIM), row(KV_LANES), row(KV_LANES), row(KV_LANES), row(KV_LANES)],
        out_shape=[jax.ShapeDtypeStruct((t, NSA_DIM), BF16)] + [jax.ShapeDtypeStruct((t, KV_LANES), BF16)] * 4,
        compiler_params=pltpu.CompilerParams(dimension_semantics=("arbitrary",)),
        name="nsa_prep",
    )(u, u, u, u, u, gq, gks, gkw, _block_diag_mean(NSA_DIM), _block_diag_mean(KV_LANES))


def _gelu_tanh(x):
    return 0.5 * x * (1.0 + jnp.tanh(0.7978845608028654 * (x + 0.044715 * x * x * x)))


def _nsa_compress_kernel(uk_ref, uv_ref, pe_ref, w1_ref, w2_ref, gk_ref, bd_ref, kc_ref, vc_ref):
    n_row = uk_ref.shape[1] // CMP_STRIDE
    d = NSA_HEAD_DIM
    for kv, (src, dst) in enumerate(((uk_ref, kc_ref), (uv_ref, vc_ref))):
        top = [jnp.zeros((n_row, CMP_HIDDEN), F32) for _ in range(NSA_KV_GROUPS)]
        bot = [jnp.zeros((n_row, CMP_HIDDEN), F32) for _ in range(NSA_KV_GROUPS)]
        for l in range(CMP_STRIDE):
            x2 = src[0, pl.ds(l, n_row, stride=CMP_STRIDE), :]
            l2 = l + CMP_STRIDE
            for g in range(NSA_KV_GROUPS):
                x = x2[:, g * d:(g + 1) * d]
                top[g] += _dot((x + pe_ref[kv, l:l + 1, :]).astype(BF16), w1_ref[kv, l * d:(l + 1) * d, :])
                bot[g] += _dot((x + pe_ref[kv, l2:l2 + 1, :]).astype(BF16), w1_ref[kv, l2 * d:(l2 + 1) * d, :])
        outs = []
        for g in range(NSA_KV_GROUPS):
            hid = top[g] + pltpu.roll(bot[g], n_row - 1, 0)
            outs.append(_dot(_gelu_tanh(hid).astype(BF16), w2_ref[kv]))
        y = jnp.concatenate(outs, axis=1)
        if kv == 0:
            y = _group_rms(y, bd_ref[...], gk_ref[...])
        dst[0] = y.astype(BF16)


def nsa_compress(u3, qk_gain, cmp_pe, cmp_w1, cmp_w2, col_kv):
    b, s, _ = u3.shape
    kb = col_kv // KV_LANES
    n_row = s // CMP_STRIDE
    full = lambda *shape: pl.BlockSpec(shape, lambda i: (0,) * len(shape))
    gk = jnp.tile(qk_gain[1], NSA_KV_GROUPS).reshape(1, KV_LANES)
    out_spec = pl.BlockSpec((1, n_row, KV_LANES), lambda i: (i, 0, 0))
    return pl.pallas_call(
        _nsa_compress_kernel,
        grid=(b,),
        in_specs=[pl.BlockSpec((1, s, KV_LANES), lambda i: (i, 0, kb)),
                  pl.BlockSpec((1, s, KV_LANES), lambda i: (i, 0, kb + 1)),
                  full(2, CMP_LEN, NSA_HEAD_DIM), full(2, CMP_LEN * NSA_HEAD_DIM, CMP_HIDDEN),
                  full(2, CMP_HIDDEN, NSA_HEAD_DIM), full(1, KV_LANES), full(KV_LANES, KV_LANES)],
        out_specs=[out_spec, out_spec],
        out_shape=[jax.ShapeDtypeStruct((b, n_row, KV_LANES), BF16)] * 2,
        compiler_params=pltpu.CompilerParams(dimension_semantics=("arbitrary",)),
        name="nsa_compress",
    )(u3, u3, cmp_pe, cmp_w1.astype(BF16), cmp_w2.astype(BF16), gk, _block_diag_mean(KV_LANES))


def _stack_heads(q, g):
    d = NSA_HEAD_DIM
    base = g * NSA_HPG * d
    return jnp.concatenate([q[:, base + h * d: base + (h + 1) * d] for h in range(NSA_HPG)], axis=0)


def _nsa_select_kernel(q_ref, kc_ref, vc_ref, ov_ref, oc_ref, sel_ref, flag_ref):
    qt = pl.program_id(1)
    qbl, d = NSA_QBLOCK, NSA_HEAD_DIM
    n_key = kc_ref.shape[1]
    n_slc = ov_ref.shape[1]
    t = qt * qbl + lax.broadcasted_iota(jnp.int32, (qbl, n_key), 0)
    c = lax.broadcasted_iota(jnp.int32, (qbl, n_key), 1)
    dist = (t - (c * CMP_STRIDE + CMP_LEN - 1)).astype(F32)
    ok = dist >= 0
    q = q_ref[0]
    imps = []
    for g in range(NSA_KV_GROUPS):
        kc = kc_ref[0, :, g * d:(g + 1) * d]
        vc = vc_ref[0, :, g * d:(g + 1) * d]
        s_all = _dot_nt(_stack_heads(q, g), kc)
        p_sum = jnp.zeros((qbl, n_key), F32)
        for h in range(NSA_HPG):
            s = jnp.where(ok, s_all[h * qbl:(h + 1) * qbl] - _head_slope(g, h) * dist, NEG_INF)
            m = jnp.max(s, axis=-1, keepdims=True)
            e = jnp.where(ok, jnp.exp(s - m), 0.0)
            l = jnp.sum(e, axis=-1, keepdims=True)
            p = e * jnp.where(l > 0, 1.0 / l, 0.0)
            col = (g * NSA_HPG + h) * d
            oc_ref[0, :, col:col + d] = _dot(p.astype(BF16), vc)
            p_sum += p
        imps.append(_split_dot(p_sum, ov_ref[...]))
    imp_t = jnp.concatenate(imps, axis=1).T
    j = lax.broadcasted_iota(jnp.int32, (n_slc, qbl), 0)
    forced = (j == 0) | (j == qt) | (j == qt - 1)
    visible = j <= qt
    sels = []
    for g in range(NSA_KV_GROUPS):
        score = jnp.where(visible, imp_t[g * n_slc:(g + 1) * n_slc] + jnp.where(forced, SLC_FORCE_BONUS, 0.0), NEG_INF)
        sel = jnp.zeros((n_slc, qbl), F32)
        for _ in range(min(SLC_TOPN, n_slc)):
            m = jnp.max(score, axis=0, keepdims=True)
            first = jnp.min(jnp.where(score == m, j, n_slc), axis=0, keepdims=True)
            pick = j == first
            sel = jnp.where(pick, 1.0, sel)
            score = jnp.where(pick, -3e38, score)
        sels.append(jnp.where(visible, sel, 0.0))
    sel_all = jnp.concatenate(sels, axis=0).T
    sel_ref[0] = sel_all.astype(BF16)
    flag_ref[0, 0] = jnp.broadcast_to(jnp.max(sel_all, axis=0, keepdims=True), flag_ref.shape[2:])


def nsa_select(q3, k_cmp, v_cmp):
    b, s, _ = q3.shape
    n_qt = s // NSA_QBLOCK
    n_slc = s // SLC_LEN
    n_key = k_cmp.shape[1]
    c_lo = jnp.arange(n_key) * CMP_STRIDE
    j_lo = jnp.arange(n_slc) * SLC_LEN
    overlap = ((c_lo[:, None] < j_lo[None, :] + SLC_LEN) & (c_lo[:, None] + CMP_LEN > j_lo[None, :])).astype(BF16)
    tile = lambda c: pl.BlockSpec((1, NSA_QBLOCK, c), lambda i, t: (i, t, 0))
    per_b = pl.BlockSpec((1, n_key, KV_LANES), lambda i, t: (i, 0, 0))
    return pl.pallas_call(
        _nsa_select_kernel,
        grid=(b, n_qt),
        in_specs=[tile(NSA_DIM), per_b, per_b, pl.BlockSpec((n_key, n_slc), lambda i, t: (0, 0))],
        out_specs=[tile(NSA_DIM), tile(2 * n_slc), pl.BlockSpec((1, 1, 8, 2 * n_slc), lambda i, t: (i, t, 0, 0))],
        out_shape=[jax.ShapeDtypeStruct((b, s, NSA_DIM), F32), jax.ShapeDtypeStruct((b, s, 2 * n_slc), BF16),
                   jax.ShapeDtypeStruct((b, n_qt, 8, 2 * n_slc), F32)],
        compiler_params=pltpu.CompilerParams(dimension_semantics=("arbitrary", "arbitrary")),
        name="nsa_select",
    )(q3, k_cmp, v_cmp, overlap)


def _nsa_attend_kernel(count_ref, list_ref, q_ref, ks_ref, vs_ref, kw_ref, vw_ref, sel_ref, oc_ref, gate_ref, out_ref,
                       ksel_ref, vsel_ref):
    bi, qt = pl.program_id(0), pl.program_id(1)
    n_qt = pl.num_programs(1)
    qbl, d, ch = NSA_QBLOCK, NSA_HEAD_DIM, SEL_CHUNK
    n_slc = sel_ref.shape[2] // NSA_KV_GROUPS
    n_word = n_slc // IDS_PER_WORD
    q = q_ref[0]
    gates = jax.nn.sigmoid(gate_ref[0])
    lane = lax.broadcasted_iota(jnp.int32, (1, ch * SLC_LEN), 1)
    slot_of_lane = lane // SLC_LEN
    t_sel = qt * qbl + lax.broadcasted_iota(jnp.int32, (qbl, ch * SLC_LEN), 0)
    j_iota = lax.broadcasted_iota(jnp.int32, (n_slc, ch * SLC_LEN), 0)
    win_start = jnp.maximum(qt - WINDOW // qbl, 0) * qbl
    t_win = qt * qbl + lax.broadcasted_iota(jnp.int32, (qbl, N_WIN_KEYS), 0)
    dist_win = t_win - (win_start + lax.broadcasted_iota(jnp.int32, (qbl, N_WIN_KEYS), 1))
    ok_win = (dist_win >= 0) & (dist_win < WINDOW)
    dist_win = dist_win.astype(F32)

    groups = range(NSA_KV_GROUPS)
    lanes = [slice(g * d, (g + 1) * d) for g in groups]
    qs = [_stack_heads(q, g) for g in groups]
    tile_g = [(bi * n_qt + qt) * NSA_KV_GROUPS + g for g in groups]
    n_sel = [count_ref[tg] for tg in tile_g]
    sel_g = [sel_ref[0, :, g * n_slc:(g + 1) * n_slc] for g in groups]

    def sel_scores(c, g):
        word0 = tile_g[g] * n_word + c * (ch // IDS_PER_WORD)
        j_row = jnp.full((1, ch * SLC_LEN), -1, jnp.int32)
        for slot in range(ch):
            valid = c * ch + slot < n_sel[g]
            jb = (list_ref[word0 + slot // IDS_PER_WORD] >> (8 * (slot % IDS_PER_WORD))) & 0xFF
            rows = pl.ds(pl.multiple_of(jb * SLC_LEN, SLC_LEN), SLC_LEN)
            ksel_ref[g, slot * SLC_LEN:(slot + 1) * SLC_LEN, :] = ks_ref[0, rows, lanes[g]]
            vsel_ref[g, slot * SLC_LEN:(slot + 1) * SLC_LEN, :] = vs_ref[0, rows, lanes[g]]
            j_row = jnp.where(slot_of_lane == slot, jnp.where(valid, jb, -1), j_row)
        s_all = _dot_nt(qs[g], ksel_ref[g])
        chosen = _dot(sel_g[g], (j_iota == j_row).astype(BF16))
        dist = t_sel - (j_row * SLC_LEN + lane % SLC_LEN)
        ok = (chosen > 0.5) & (dist >= 0)
        return s_all, ok, dist.astype(F32)

    def sel_softmax(scores, carry, g):
        s_all, ok, dist = scores
        m_old, l_old, acc = carry
        ps, ms, ls = [], [], []
        for h in range(NSA_HPG):
            rows_h = slice(h * qbl, (h + 1) * qbl)
            s = jnp.where(ok, s_all[rows_h] - _head_slope(g, h) * dist, NEG_INF)
            m_new = jnp.maximum(m_old[rows_h], jnp.max(s, axis=-1, keepdims=True))
            p = jnp.exp(s - m_new)
            alpha = jnp.exp(m_old[rows_h] - m_new)
            ls.append(alpha * l_old[rows_h] + jnp.sum(p, axis=-1, keepdims=True))
            ms.append(m_new)
            ps.append(p.astype(BF16))
        m_new = jnp.concatenate(ms, axis=0)
        alpha = jnp.exp(m_old - m_new)
        acc = alpha * acc + _dot(jnp.concatenate(ps, axis=0), vsel_ref[g])
        return m_new, jnp.concatenate(ls, axis=0), acc

    win_rows = pl.ds(pl.multiple_of(win_start, qbl), N_WIN_KEYS)
    s_win = [_dot_nt(qs[g], kw_ref[0, win_rows, lanes[g]]) for g in groups]
    sc0 = [sel_scores(0, g) for g in groups]
    p_win, inv_win = [], []
    for g in groups:
        ps, inv = [], []
        for h in range(NSA_HPG):
            s = jnp.where(ok_win, s_win[g][h * qbl:(h + 1) * qbl] - _head_slope(g, h) * dist_win, NEG_INF)
            e = jnp.exp(s - jnp.max(s, axis=-1, keepdims=True))
            inv.append(1.0 / jnp.sum(e, axis=-1, keepdims=True))
            ps.append(e.astype(BF16))
        p_win.append(jnp.concatenate(ps, axis=0))
        inv_win.append(jnp.concatenate(inv, axis=0))
    init = (jnp.full((NSA_HPG * qbl, 1), M_INIT, F32), jnp.zeros((NSA_HPG * qbl, 1), F32),
            jnp.zeros((NSA_HPG * qbl, d), F32))
    carry = [sel_softmax(sc0[g], init, g) for g in groups]
    o_win = [_dot(p_win[g], vw_ref[0, win_rows, lanes[g]]) * inv_win[g] for g in groups]

    for g in groups:
        rest = lax.fori_loop(1, (n_sel[g] + ch - 1) // ch,
                             lambda c, cr, g=g: sel_softmax(sel_scores(c, g), cr, g), carry[g])
        o_sel = rest[2] * (1.0 / rest[1])
        for h in range(NSA_HPG):
            hh = g * NSA_HPG + h
            col = hh * d
            rows_h = slice(h * qbl, (h + 1) * qbl)
            out_ref[0, :, col:col + d] = (gates[:, hh:hh + 1] * oc_ref[0, :, col:col + d]
                                          + gates[:, NSA_HEADS + hh:NSA_HEADS + hh + 1] * o_sel[rows_h]
                                          + gates[:, 2 * NSA_HEADS + hh:2 * NSA_HEADS + hh + 1] * o_win[g][rows_h])


def nsa_attend(counts, lists, q3, ks, vs, kw, vw, sel, o_cmp, u3, col_gate):
    b, s, _ = q3.shape
    n_qt = s // NSA_QBLOCK
    assert col_gate % 128 == 0 and s >= N_WIN_KEYS and SEL_CHUNK % IDS_PER_WORD == 0
    gb = col_gate // 128
    tile = lambda c: pl.BlockSpec((1, NSA_QBLOCK, c), lambda i, t, counts, lists: (i, t, 0))
    per_b = pl.BlockSpec((1, s, KV_LANES), lambda i, t, counts, lists: (i, 0, 0))
    grid_spec = pltpu.PrefetchScalarGridSpec(
        num_scalar_prefetch=2,
        grid=(b, n_qt),
        in_specs=[tile(NSA_DIM), per_b, per_b, per_b, per_b, tile(sel.shape[2]), tile(NSA_DIM),
                  pl.BlockSpec((1, NSA_QBLOCK, 128), lambda i, t, counts, lists: (i, t, gb))],
        out_specs=tile(NSA_DIM),
        scratch_shapes=[pltpu.VMEM((NSA_KV_GROUPS, SEL_CHUNK * SLC_LEN, NSA_HEAD_DIM), BF16),
                        pltpu.VMEM((NSA_KV_GROUPS, SEL_CHUNK * SLC_LEN, NSA_HEAD_DIM), BF16)],
    )
    return pl.pallas_call(
        _nsa_attend_kernel,
        grid_spec=grid_spec,
        out_shape=jax.ShapeDtypeStruct((b, s, NSA_DIM), F32),
        compiler_params=pltpu.CompilerParams(dimension_semantics=("arbitrary", "arbitrary")),
        name="nsa_attend",
    )(counts, lists, q3, ks, vs, kw, vw, sel, o_cmp, u3)


def _pack_union_lists(flags, n_slc):
    assert n_slc <= 256 and n_slc % IDS_PER_WORD == 0
    b, n_qt = flags.shape[:2]
    f = flags[:, :, 0, :].reshape(b, n_qt, NSA_KV_GROUPS, n_slc) > 0.5
    fi = f.astype(jnp.int32)
    counts = jnp.sum(fi, axis=-1)
    ids = jnp.arange(n_slc, dtype=jnp.int32)
    pos = jnp.where(f, jnp.cumsum(fi, axis=-1) - 1, counts[..., None] + jnp.cumsum(1 - fi, axis=-1) - 1)
    order = jnp.sum(jnp.where(pos[..., :, None] == ids, ids[:, None], 0), axis=-2)
    order = order.reshape(b, n_qt, NSA_KV_GROUPS, n_slc // IDS_PER_WORD, IDS_PER_WORD)
    words = jnp.sum(order << (8 * jnp.arange(IDS_PER_WORD, dtype=jnp.int32)), axis=-1, dtype=jnp.int32)
    return counts.reshape(-1), words.reshape(-1)


def nsa_mixer_pallas(u3, qk_gain, cmp_pe, cmp_w1, cmp_w2, col_q, col_kv, col_gate):
    b, s, d_in = u3.shape
    q, ks, vs, kw, vw = nsa_prep(u3.reshape(b * s, d_in), qk_gain, col_q, col_kv)
    k_cmp, v_cmp = nsa_compress(u3, qk_gain, cmp_pe, cmp_w1, cmp_w2, col_kv)
    q3 = q.reshape(b, s, NSA_DIM)
    r3 = lambda a: a.reshape(b, s, KV_LANES)
    o_cmp, sel, flags = nsa_select(q3, k_cmp, v_cmp)
    counts, lists = _pack_union_lists(flags, s // SLC_LEN)
    return nsa_attend(counts, lists, q3, r3(ks), r3(vs), r3(kw), r3(vw), sel, o_cmp, u3, col_gate)


GDN_TILE = 256
GDN_HALO = 8


def _dot3(a, b):
    ah = a.astype(BF16)
    bh = b.astype(BF16)
    al = (a - ah.astype(F32)).astype(BF16)
    bl = (b - bh.astype(F32)).astype(BF16)
    return _dot(ah, bh) + _dot(ah, bl) + _dot(al, bh)


def _dot_tn(a, b):
    return lax.dot_general(a, b, (((0,), (0,)), ((), ())), preferred_element_type=F32)


def _softplus(x):
    return jnp.maximum(x, 0.0) + jnp.log(1.0 + jnp.exp(-jnp.abs(x)))


def _l2_norm(x):
    return x * lax.rsqrt(jnp.sum(x * x, axis=-1, keepdims=True) + RMS_EPS)


def _gdn_kernel(qkv_ref, z_ref, small_ref, cw_ref, coef_ref, gain_ref, y_ref, xe_ref, state_ref):
    tt = pl.program_id(1)
    tile, c, hd = GDN_TILE, GDN_CHUNK, GDN_HEAD_DIM

    @pl.when(tt == 0)
    def _():
        xe_ref[0:GDN_HALO, :] = jnp.zeros((GDN_HALO, xe_ref.shape[1]), F32)
        state_ref[...] = jnp.zeros_like(state_ref)

    @pl.when(tt > 0)
    def _():
        xe_ref[0:GDN_HALO, :] = xe_ref[tile:tile + GDN_HALO, :]

    xe_ref[GDN_HALO:, :] = qkv_ref[0]
    conv = jnp.zeros((tile, xe_ref.shape[1]), F32)
    for j in range(GDN_CONV):
        conv += cw_ref[j:j + 1, :] * xe_ref[pl.ds(GDN_HALO - (GDN_CONV - 1) + j, tile), :]
    act = conv * jax.nn.sigmoid(conv)

    small = small_ref[0]
    beta_all = jax.nn.sigmoid(small)
    g_all = coef_ref[0:1, :] * _softplus(small + coef_ref[1:2, :])
    row = lax.broadcasted_iota(jnp.int32, (c, SMALL_LANES), 0)
    ri = lax.broadcasted_iota(jnp.int32, (c, c), 0)
    ci = lax.broadcasted_iota(jnp.int32, (c, c), 1)
    lower = ri >= ci
    strict = ri > ci

    n_chunk = tile // c
    pairs = [(n, h) for n in range(n_chunk) for h in range(GDN_HEADS)]
    gcs, gc_ts = [], []
    for n in range(n_chunk):
        gc = g_all[n * c:(n + 1) * c]
        shift = 1
        while shift < c:
            gc = gc + jnp.where(row >= shift, pltpu.roll(gc, shift, 0), 0.0)
            shift *= 2
        gcs.append(gc)
        gc_ts.append(gc.T)
    pre = []
    for n, h in pairs:
        rows = slice(n * c, (n + 1) * c)
        q = _l2_norm(act[rows, h * hd:(h + 1) * hd]) * hd ** -0.5
        k = _l2_norm(act[rows, GDN_DIM + h * hd:GDN_DIM + (h + 1) * hd])
        v = act[rows, 2 * GDN_DIM + h * hd:2 * GDN_DIM + (h + 1) * hd]
        beta = beta_all[rows, LANE_BETA + h:LANE_BETA + h + 1]
        gcol = gcs[n][:, LANE_DECAY + h:LANE_DECAY + h + 1]
        grow = gc_ts[n][LANE_DECAY + h:LANE_DECAY + h + 1, :]
        g_last = gcol[c - 1:c, :]
        decay = jnp.where(lower, jnp.exp(jnp.where(lower, gcol - grow, 0.0)), 0.0)
        kb = k * beta
        kh = k.astype(BF16)
        pre.append(dict(
            lmat=jnp.where(strict, _dot_nt(kb.astype(BF16), kh) * decay, 0.0),
            rhs=jnp.concatenate([v * beta, kb * jnp.exp(gcol)], axis=1),
            attn=(_dot_nt(q.astype(BF16), kh) * decay).astype(BF16),
            q_dec=(q * jnp.exp(gcol)).astype(BF16),
            k_dec=(k * jnp.exp(g_last - gcol)).astype(BF16),
            d_last=jnp.exp(g_last)))
    power = [p['lmat'].astype(BF16) for p in pre]
    rhs = [p['rhs'] - _dot(lm, p['rhs'].astype(BF16)) for lm, p in zip(power, pre)]
    for _ in range(5):
        power = [_dot(lm, lm).astype(BF16) for lm in power]
        rhs = [r + _dot(lm, r.astype(BF16)) for lm, r in zip(power, rhs)]

    state = [state_ref[h] for h in range(GDN_HEADS)]
    for n in range(n_chunk):
        rows = slice(n * c, (n + 1) * c)
        sb = [s.astype(BF16) for s in state]
        ps = [pre[n * GDN_HEADS + h] for h in range(GDN_HEADS)]
        rs = [rhs[n * GDN_HEADS + h] for h in range(GDN_HEADS)]
        v_new = [(r[:, :hd] - _dot(r[:, hd:].astype(BF16), s)).astype(BF16) for r, s in zip(rs, sb)]
        outs = [_dot(p['q_dec'], s) + _dot(p['attn'], vn) for p, s, vn in zip(ps, sb, v_new)]
        state = [s * p['d_last'] + _dot_tn(p['k_dec'], vn) for p, s, vn in zip(ps, state, v_new)]
        for h, o in enumerate(outs):
            o = o * lax.rsqrt(jnp.mean(o * o, axis=-1, keepdims=True) + RMS_EPS) * gain_ref[...]
            zz = z_ref[0, rows, h * hd:(h + 1) * hd]
            y_ref[0, rows, h * hd:(h + 1) * hd] = o * (zz * jax.nn.sigmoid(zz))
    for h in range(GDN_HEADS):
        state_ref[h] = state[h]


def gdn_mixer(u3, col_qkv, col_z, col_small, conv_w, a_log, dt_bias, out_gain):
    b, s, _ = u3.shape
    lane = jnp.arange(SMALL_LANES)
    in_decay = (lane >= LANE_DECAY) & (lane < LANE_DECAY + GDN_HEADS)
    idx = jnp.clip(lane - LANE_DECAY, 0, GDN_HEADS - 1)
    coef = jnp.stack([jnp.where(in_decay, -jnp.exp(a_log)[idx], 0.0), jnp.where(in_decay, dt_bias[idx], 0.0)])
    qkv_w = 3 * GDN_DIM
    assert col_qkv % qkv_w == 0 and col_z % GDN_DIM == 0 and col_small % SMALL_LANES == 0
    full = lambda r, cc: pl.BlockSpec((r, cc), lambda i, t: (0, 0))
    return pl.pallas_call(
        _gdn_kernel,
        grid=(b, s // GDN_TILE),
        in_specs=[pl.BlockSpec((1, GDN_TILE, qkv_w), lambda i, t: (i, t, col_qkv // qkv_w)),
                  pl.BlockSpec((1, GDN_TILE, GDN_DIM), lambda i, t: (i, t, col_z // GDN_DIM)),
                  pl.BlockSpec((1, GDN_TILE, SMALL_LANES), lambda i, t: (i, t, col_small // SMALL_LANES)),
                  full(GDN_CONV, qkv_w), full(2, SMALL_LANES), full(1, GDN_HEAD_DIM)],
        out_specs=pl.BlockSpec((1, GDN_TILE, GDN_DIM), lambda i, t: (i, t, 0)),
        out_shape=jax.ShapeDtypeStruct((b, s, GDN_DIM), F32),
        scratch_shapes=[pltpu.VMEM((GDN_TILE + GDN_HALO, qkv_w), F32), pltpu.VMEM((GDN_HEADS, GDN_HEAD_DIM, GDN_HEAD_DIM), F32)],
        compiler_params=pltpu.CompilerParams(dimension_semantics=("arbitrary", "arbitrary")),
        name="gdn_mixer",
    )(u3, u3, u3, conv_w, coef, out_gain.reshape(1, GDN_HEAD_DIM))


CONV_TILE = 512


def _short_conv_kernel(u_ref, cw_ref, y_ref, xe_ref):
    tile, cd = CONV_TILE, CONV_DIM

    @pl.when(pl.program_id(1) == 0)
    def _():
        xe_ref[0:GDN_HALO, :] = jnp.zeros((GDN_HALO, cd), F32)

    @pl.when(pl.program_id(1) > 0)
    def _():
        xe_ref[0:GDN_HALO, :] = xe_ref[tile:tile + GDN_HALO, :]

    xe_ref[GDN_HALO:, :] = u_ref[0, :, cd:2 * cd] * u_ref[0, :, 2 * cd:3 * cd]
    conv = jnp.zeros((tile, cd), F32)
    for j in range(CONV_WIDTH):
        conv += cw_ref[j:j + 1, :] * xe_ref[pl.ds(GDN_HALO - (CONV_WIDTH - 1) + j, tile), :]
    y_ref[0] = u_ref[0, :, 0:cd] * conv


def short_conv(u3, conv_w):
    b, s, _ = u3.shape
    return pl.pallas_call(
        _short_conv_kernel,
        grid=(b, s // CONV_TILE),
        in_specs=[pl.BlockSpec((1, CONV_TILE, 3 * CONV_DIM), lambda i, t: (i, t, 0)),
                  pl.BlockSpec((CONV_WIDTH, CONV_DIM), lambda i, t: (0, 0))],
        out_specs=pl.BlockSpec((1, CONV_TILE, CONV_DIM), lambda i, t: (i, t, 0)),
        out_shape=jax.ShapeDtypeStruct((b, s, CONV_DIM), F32),
        scratch_shapes=[pltpu.VMEM((CONV_TILE + GDN_HALO, CONV_DIM), F32)],
        compiler_params=pltpu.CompilerParams(dimension_semantics=("arbitrary", "arbitrary")),
        name="short_conv",
    )(u3, conv_w)


def _merge_kernel(x_ref, g_ref, ya_ref, yn_ref, yg_ref, wb_ref, wg_ref, bg_ref, wo_ref, o_ref):
    x = x_ref[...]
    d = x.shape[1]
    h = (x * lax.rsqrt(jnp.mean(x * x, axis=-1, keepdims=True) + RMS_EPS) * g_ref[...]).astype(BF16)
    merged = jnp.zeros(x.shape, F32)
    for r, y_ref in enumerate((ya_ref, yn_ref, yg_ref)):
        gate = jax.nn.sigmoid(_dot(h, wg_ref[:, r * d:(r + 1) * d]) + bg_ref[:, r * d:(r + 1) * d])
        merged += gate * _dot(y_ref[...].astype(BF16), wb_ref[r])
    o_ref[...] = x + _dot(merged.astype(BF16), wo_ref[...])


def merge_branches(x2d, gain, y_a, y_n, y_g, w_branch, w_gate, b_gate, w_out, tm=512):
    t, d = x2d.shape
    row = lambda c: pl.BlockSpec((tm, c), lambda i: (i, 0))
    full = lambda *shape: pl.BlockSpec(shape, lambda i: (0,) * len(shape))
    return pl.pallas_call(
        _merge_kernel,
        grid=(t // tm,),
        in_specs=[row(d), full(1, d), row(BRANCH_DIM), row(BRANCH_DIM), row(BRANCH_DIM),
                  full(N_BRANCH, BRANCH_DIM, d), full(d, N_BRANCH * d), full(1, N_BRANCH * d), full(d, d)],
        out_specs=row(d),
        out_shape=jax.ShapeDtypeStruct((t, d), F32),
        compiler_params=pltpu.CompilerParams(dimension_semantics=("arbitrary",), vmem_limit_bytes=56 * 1024 * 1024),
        name="merge_branches",
    )(x2d, gain.reshape(1, d), y_a, y_n, y_g, w_branch.astype(BF16), w_gate.astype(BF16), b_gate.reshape(1, -1), w_out.astype(BF16))


def hybrid_mixer(x2d, b, s, norm_gain, w_in, conv_a_w, nsa_qk_gain, cmp_pe, cmp_w1, cmp_w2, gdn_conv_w, gdn_a_log, gdn_dt_bias, gdn_out_gain, w_branch, w_gate, b_gate, w_out):
    t = b * s
    u3 = norm_proj(x2d, norm_gain, permute_in_proj(w_in)).reshape(b, s, D_U)
    y_a = short_conv(u3, conv_a_w)
    y_n = nsa_mixer_pallas(u3, nsa_qk_gain, cmp_pe, cmp_w1, cmp_w2, COL_NSA_Q, COL_NSA_KV, COL_SMALL)
    y_g = gdn_mixer(u3, COL_GDN_QKV, COL_GDN_Z, COL_SMALL, gdn_conv_w, gdn_a_log, gdn_dt_bias, gdn_out_gain)
    return merge_branches(x2d, norm_gain, y_a.reshape(t, BRANCH_DIM), y_n.reshape(t, BRANCH_DIM), y_g.reshape(t, BRANCH_DIM), w_branch, w_gate, b_gate, w_out)


MOE_ROWS = 512
ROUTE_LANES = 128
N_ROUTER = MOE_GROUPS + N_EXPERTS


def _moe_route_kernel(x_ref, g_ref, wr_ref, br_ref, tri_ref, h_ref, route_ref, cnt_ref, run_ref):
    @pl.when(pl.program_id(0) == 0)
    def _():
        run_ref[...] = jnp.zeros_like(run_ref)

    x = x_ref[...]
    h = x * lax.rsqrt(jnp.mean(x * x, axis=-1, keepdims=True) + RMS_EPS) * g_ref[...]
    bits = lax.bitcast_convert_type(h.astype(BF16).astype(F32), jnp.uint32)
    half = h.shape[1] // 2
    h_ref[...] = (bits[:, half:] & jnp.uint32(0xFFFF0000)) | (bits[:, :half] >> 16)
    logits = _dot(h.astype(BF16), wr_ref[...]) + br_ref[...]
    lane = lax.broadcasted_iota(jnp.int32, logits.shape, 1)
    first_of = lambda hit: jnp.min(jnp.where(hit, lane, ROUTE_LANES), axis=-1, keepdims=True)
    is_grp = lane < MOE_GROUPS
    lg = jnp.where(is_grp, logits, NEG_INF)
    m_g = jnp.max(lg, axis=-1, keepdims=True)
    grp = first_of(lg == m_g)
    p_grp = 1.0 / jnp.sum(jnp.where(is_grp, jnp.exp(lg - m_g), 0.0), axis=-1, keepdims=True)
    lo = MOE_GROUPS + grp * EXPERTS_PER_GROUP
    le = jnp.where((lane >= lo) & (lane < lo + EXPERTS_PER_GROUP), logits, NEG_INF)
    m1 = jnp.max(le, axis=-1, keepdims=True)
    i1 = first_of(le == m1)
    le2 = jnp.where(lane == i1, NEG_INF, le)
    m2 = jnp.max(le2, axis=-1, keepdims=True)
    i2 = first_of(le2 == m2)
    r = jnp.exp(m2 - m1)
    g1 = p_grp / (1.0 + r)
    g2 = p_grp * r / (1.0 + r)
    e1 = i1 - MOE_GROUPS
    e2 = i2 - MOE_GROUPS
    hit1 = lane == e1
    hit2 = lane == e2
    onehot = (hit1 | hit2).astype(BF16)
    before = _dot(tri_ref[...], onehot) + run_ref[...]
    r1 = jnp.sum(jnp.where(hit1, before, 0.0), axis=-1, keepdims=True)
    r2 = jnp.sum(jnp.where(hit2, before, 0.0), axis=-1, keepdims=True)
    run_ref[...] += jnp.sum(onehot.astype(F32), axis=0, keepdims=True)
    rec = jnp.zeros(logits.shape, F32)
    for k, v in enumerate((e1.astype(F32), e2.astype(F32), r1, r2, g1, g2)):
        rec = jnp.where(lane == k, v, rec)
    route_ref[...] = rec
    cnt_ref[...] = jnp.broadcast_to(run_ref[...], cnt_ref.shape)


def moe_route(x2d, gain, w_rg, b_rg, w_re, b_re, tm=512):
    t, d = x2d.shape
    pad = ROUTE_LANES - N_ROUTER
    wr = jnp.pad(jnp.concatenate([w_rg, w_re], axis=1), ((0, 0), (0, pad))).astype(BF16)
    br = jnp.pad(jnp.concatenate([b_rg, b_re]), (0, pad)).reshape(1, ROUTE_LANES)
    tri = (jnp.arange(tm)[:, None] > jnp.arange(tm)[None, :]).astype(BF16)
    full = lambda r, c: pl.BlockSpec((r, c), lambda i: (0, 0))
    return pl.pallas_call(
        _moe_route_kernel,
        grid=(t // tm,),
        in_specs=[pl.BlockSpec((tm, d), lambda i: (i, 0)), full(1, d), full(d, ROUTE_LANES), full(1, ROUTE_LANES), full(tm, tm)],
        out_specs=[pl.BlockSpec((tm, d // 2), lambda i: (i, 0)), pl.BlockSpec((tm, ROUTE_LANES), lambda i: (i, 0)), full(8, ROUTE_LANES)],
        out_shape=[jax.ShapeDtypeStruct((t, d // 2), jnp.uint32), jax.ShapeDtypeStruct((t, ROUTE_LANES), F32), jax.ShapeDtypeStruct((8, ROUTE_LANES), F32)],
        scratch_shapes=[pltpu.VMEM((1, ROUTE_LANES), F32)],
        compiler_params=pltpu.CompilerParams(dimension_semantics=("arbitrary",)),
        name="moe_route",
    )(x2d, gain.reshape(1, d), wr, br, tri)


def _row_copy(src_ref, src_row, dst_ref, dst_row, sem):
    return pltpu.make_async_copy(src_ref.at[src_row], dst_ref.at[dst_row], sem)


def _moe_dispatch_kernel(dest_ref, h_ref, buf_in_ref, buf_ref, sem):
    del buf_in_ref
    tm = h_ref.shape[0]
    base = pl.program_id(0) * tm

    def send(r, carry):
        for k in range(TOPK_IN_GROUP):
            _row_copy(h_ref, r, buf_ref, dest_ref[(base + r) * TOPK_IN_GROUP + k], sem).start()
        return carry

    def drain(r, carry):
        for k in range(TOPK_IN_GROUP):
            _row_copy(h_ref, r, buf_ref, 0, sem).wait()
        return carry

    lax.fori_loop(0, tm, send, 0)
    lax.fori_loop(0, tm, drain, 0)


def moe_dispatch(dest, h3, n_rows, tm=512):
    t, _, d = h3.shape
    grid_spec = pltpu.PrefetchScalarGridSpec(
        num_scalar_prefetch=1, grid=(t // tm,),
        in_specs=[pl.BlockSpec((tm, 1, d), lambda i, dest: (i, 0, 0)), pl.BlockSpec(memory_space=pl.ANY)],
        out_specs=pl.BlockSpec(memory_space=pl.ANY),
        scratch_shapes=[pltpu.SemaphoreType.DMA(())],
    )
    return pl.pallas_call(
        _moe_dispatch_kernel, grid_spec=grid_spec,
        out_shape=jax.ShapeDtypeStruct((n_rows, 1, d), h3.dtype),
        input_output_aliases={2: 0},
        compiler_params=pltpu.CompilerParams(dimension_semantics=("arbitrary",), has_side_effects=True),
        name="moe_dispatch",
    )(dest, h3, jnp.zeros((n_rows, 1, d), h3.dtype))


def _moe_ffn_kernel(blk_e_ref, n_used_ref, x_ref, wg_ref, wu_ref, wd_ref, y_ref):
    del blk_e_ref
    used = pl.program_id(0) < n_used_ref[0]

    @pl.when(used)
    def _():
        w = x_ref[...]
        lo = lax.bitcast_convert_type(w << 16, F32)
        hi = lax.bitcast_convert_type(w & jnp.uint32(0xFFFF0000), F32)
        xb = jnp.concatenate([lo, hi], axis=1).astype(BF16)
        a = _dot(xb, wg_ref[0])
        mid = a * jax.nn.sigmoid(a) * _dot(xb, wu_ref[0])
        y_ref[...] = _dot(mid.astype(BF16), wd_ref[0])

    @pl.when(jnp.logical_not(used))
    def _():
        y_ref[...] = jnp.zeros_like(y_ref)


def moe_ffn(blk_expert, n_used, buf2d, w_eg, w_eu, w_ed):
    n_rows, packed_w = buf2d.shape
    d, ff = w_eg.shape[1:]
    w_in_spec = pl.BlockSpec((1, d, ff), lambda b, blk_e, n_used: (blk_e[b], 0, 0))
    grid_spec = pltpu.PrefetchScalarGridSpec(
        num_scalar_prefetch=2, grid=(n_rows // MOE_ROWS,),
        in_specs=[pl.BlockSpec((MOE_ROWS, packed_w), lambda b, blk_e, n_used: (b, 0)), w_in_spec, w_in_spec,
                  pl.BlockSpec((1, ff, d), lambda b, blk_e, n_used: (blk_e[b], 0, 0))],
        out_specs=pl.BlockSpec((MOE_ROWS, d), lambda b, blk_e, n_used: (b, 0)),
    )
    return pl.pallas_call(
        _moe_ffn_kernel, grid_spec=grid_spec,
        out_shape=jax.ShapeDtypeStruct((n_rows, d), F32),
        compiler_params=pltpu.CompilerParams(dimension_semantics=("arbitrary",)),
        name="moe_ffn",
    )(blk_expert, n_used, buf2d, w_eg.astype(BF16), w_eu.astype(BF16), w_ed.astype(BF16))


def _moe_combine_kernel(dest_ref, y_ref, gate_ref, out_ref, ya_ref, yb_ref, sem):
    tm = out_ref.shape[0]
    base = pl.program_id(0) * tm

    def fetch(r, carry):
        _row_copy(y_ref, dest_ref[(base + r) * TOPK_IN_GROUP], ya_ref, r, sem).start()
        _row_copy(y_ref, dest_ref[(base + r) * TOPK_IN_GROUP + 1], yb_ref, r, sem).start()
        return carry

    def drain(r, carry):
        _row_copy(y_ref, 0, ya_ref, r, sem).wait()
        _row_copy(y_ref, 0, yb_ref, r, sem).wait()
        return carry

    lax.fori_loop(0, tm, fetch, 0)
    lax.fori_loop(0, tm, drain, 0)
    gates = gate_ref[...]
    out_ref[...] = gates[:, :, 0:1] * ya_ref[...] + gates[:, :, 1:2] * yb_ref[...]


def moe_combine(dest, y3, gates3, tm=512):
    t = gates3.shape[0]
    d = y3.shape[2]
    grid_spec = pltpu.PrefetchScalarGridSpec(
        num_scalar_prefetch=1, grid=(t // tm,),
        in_specs=[pl.BlockSpec(memory_space=pl.ANY), pl.BlockSpec((tm, 1, TOPK_IN_GROUP), lambda i, dest: (i, 0, 0))],
        out_specs=pl.BlockSpec((tm, 1, d), lambda i, dest: (i, 0, 0)),
        scratch_shapes=[pltpu.VMEM((tm, 1, d), F32), pltpu.VMEM((tm, 1, d), F32), pltpu.SemaphoreType.DMA(())],
    )
    return pl.pallas_call(
        _moe_combine_kernel, grid_spec=grid_spec,
        out_shape=jax.ShapeDtypeStruct((t, 1, d), F32),
        compiler_params=pltpu.CompilerParams(dimension_semantics=("arbitrary",)),
        name="moe_combine",
    )(dest, y3, gates3)


def hier_moe_pallas(x2d, gain, w_rg, b_rg, w_re, b_re, w_eg, w_eu, w_ed):
    t, d = x2d.shape
    h, rec, cnt = moe_route(x2d, gain, w_rg, b_rg, w_re, b_re)
    counts = cnt[0, :N_EXPERTS].astype(jnp.int32)
    n_blk = (counts + MOE_ROWS - 1) // MOE_ROWS
    blk_end = jnp.cumsum(n_blk)
    pad_start = (blk_end - n_blk) * MOE_ROWS
    experts = rec[:, 0:2].astype(jnp.int32)
    dest = (pad_start[experts] + rec[:, 2:4].astype(jnp.int32)).reshape(-1)
    total_blk = t * TOPK_IN_GROUP // MOE_ROWS + N_EXPERTS
    blk_expert = jnp.minimum(jnp.searchsorted(blk_end, jnp.arange(total_blk), side='right'), N_EXPERTS - 1).astype(jnp.int32)
    buf = moe_dispatch(dest, h.reshape(t, 1, d // 2), total_blk * MOE_ROWS)
    y = moe_ffn(blk_expert, blk_end[-1:].astype(jnp.int32), buf.reshape(total_blk * MOE_ROWS, d // 2), w_eg, w_eu, w_ed)
    out = moe_combine(dest, y.reshape(total_blk * MOE_ROWS, 1, d), rec[:, 4:6].reshape(t, 1, TOPK_IN_GROUP))
    return out.reshape(t, d)


def kernel(x, norm_mix, w_in, conv_a_w, nsa_qk_gain, cmp_pe, cmp_w1, cmp_w2, gdn_conv_w, gdn_a_log, gdn_dt_bias, gdn_out_gain, w_branch, w_gate, b_gate, w_out, norm_ffn, w_router_group, b_router_group, w_router_expert, b_router_expert, w_expert_gate, w_expert_up, w_expert_down):
    b, s, dm = x.shape
    x = x.reshape(b * s, dm)
    for l in range(DEPTH):
        x = hybrid_mixer(x, b, s, norm_mix[l], w_in[l], conv_a_w[l], nsa_qk_gain[l], cmp_pe[l], cmp_w1[l], cmp_w2[l], gdn_conv_w[l], gdn_a_log[l], gdn_dt_bias[l], gdn_out_gain[l], w_branch[l], w_gate[l], b_gate[l], w_out[l])
        x = x + hier_moe_pallas(x, norm_ffn[l], w_router_group[l], b_router_group[l], w_router_expert[l], b_router_expert[l], w_expert_gate[l], w_expert_up[l], w_expert_down[l])
    return x.reshape(b, s, dm)
```

```python
import functools
import math

import jax
import jax.numpy as jnp
from jax import lax
from jax.experimental import pallas as pl
from jax.experimental.pallas import tpu as pltpu

D_MODEL = 1024
DEPTH = 4
CONV_DIM = 512
CONV_WIDTH = 3
NSA_HEADS = 8
NSA_KV_GROUPS = 2
NSA_HPG = NSA_HEADS // NSA_KV_GROUPS
NSA_HEAD_DIM = 64
NSA_DIM = NSA_HEADS * NSA_HEAD_DIM
CMP_LEN = 32
CMP_STRIDE = 16
CMP_HIDDEN = 256
SLC_LEN = 64
SLC_TOPN = 8
SLC_FORCE_BONUS = 1e6
WINDOW = 512
NSA_QBLOCK = 64
GDN_HEADS = 4
GDN_HEAD_DIM = 128
GDN_DIM = GDN_HEADS * GDN_HEAD_DIM
GDN_CONV = 4
GDN_CHUNK = 64
N_BRANCH = 3
BRANCH_DIM = 512
IN_SPLITS = (3 * CONV_DIM, NSA_DIM, 6 * NSA_KV_GROUPS * NSA_HEAD_DIM, 3 * NSA_HEADS, 3 * GDN_DIM, GDN_DIM, GDN_HEADS, GDN_HEADS)
D_IN = sum(IN_SPLITS)
COL_CONV = 0
COL_GDN_QKV = 3 * CONV_DIM
COL_NSA_Q = COL_GDN_QKV + 3 * GDN_DIM
COL_GDN_Z = COL_NSA_Q + NSA_DIM
COL_NSA_KV = COL_GDN_Z + GDN_DIM
COL_SMALL = COL_NSA_KV + 6 * NSA_KV_GROUPS * NSA_HEAD_DIM
SMALL_LANES = 128
D_U = COL_SMALL + SMALL_LANES
LANE_BETA = 3 * NSA_HEADS
LANE_DECAY = LANE_BETA + GDN_HEADS


def permute_in_proj(w_in):
    conv, nq, nkv, ng, gqkv, gz, gb, ga = split_last(w_in, IN_SPLITS)
    pad = jnp.zeros((w_in.shape[0], SMALL_LANES - LANE_DECAY - GDN_HEADS), w_in.dtype)
    return jnp.concatenate([conv, gqkv, nq, gz, nkv, ng, gb, ga, pad], axis=1)


def split_last(u, sizes):
    out, start = [], 0
    for n in sizes:
        out.append(u[..., start:start + n])
        start += n
    return out
MOE_GROUPS = 4
EXPERTS_PER_GROUP = 8
N_EXPERTS = MOE_GROUPS * EXPERTS_PER_GROUP
TOPK_IN_GROUP = 2
EXPERT_FF = 512
MOE_BLOCK = 256
RMS_EPS = 1e-6
NEG_INF = -1e30

F32 = jnp.float32
BF16 = jnp.bfloat16


def _norm_proj_kernel(x_ref, g_ref, w_ref, o_ref, h_ref):
    @pl.when(pl.program_id(1) == 0)
    def _():
        x = x_ref[...]
        y = x * lax.rsqrt(jnp.mean(x * x, axis=-1, keepdims=True) + RMS_EPS)
        h_ref[...] = (y * g_ref[...]).astype(BF16)

    o_ref[...] = jnp.dot(h_ref[...], w_ref[...], preferred_element_type=F32)


def norm_proj(x2d, gain, w, tm=1024, tn=D_U // 3):
    t, d = x2d.shape
    n = w.shape[1]
    n_blk = pl.cdiv(n, tn)
    wb = jnp.pad(w.astype(BF16), ((0, 0), (0, n_blk * tn - n)))
    return pl.pallas_call(
        _norm_proj_kernel,
        grid=(t // tm, n_blk),
        in_specs=[
            pl.BlockSpec((tm, d), lambda i, j: (i, 0)),
            pl.BlockSpec((1, d), lambda i, j: (0, 0)),
            pl.BlockSpec((d, tn), lambda i, j: (0, j)),
        ],
        out_specs=pl.BlockSpec((tm, tn), lambda i, j: (i, j)),
        out_shape=jax.ShapeDtypeStruct((t, n), F32),
        scratch_shapes=[pltpu.VMEM((tm, d), BF16)],
        compiler_params=pltpu.CompilerParams(dimension_semantics=("arbitrary", "arbitrary")),
        name="norm_proj",
    )(x2d, gain.reshape(1, d), wb)


KV_LANES = NSA_KV_GROUPS * NSA_HEAD_DIM
N_WIN_KEYS = WINDOW + NSA_QBLOCK
SEL_CHUNK = 8
IDS_PER_WORD = 4
M_INIT = -1e29


def _dot(a, b):
    return jnp.dot(a, b, preferred_element_type=F32)


def _dot_nt(a, b):
    return lax.dot_general(a, b, (((1,), (1,)), ((), ())), preferred_element_type=F32)


def _split_dot(x, m):
    hi = x.astype(BF16)
    lo = (x - hi.astype(F32)).astype(BF16)
    return _dot(hi, m) + _dot(lo, m)


def _head_slope(g, h):
    return 2.0 ** (-8.0 * (g * NSA_HPG + h + 1) / NSA_HEADS)


def _group_rms(x, bd, gain):
    ms = _split_dot(x * x, bd)
    return x * lax.rsqrt(ms + RMS_EPS) * gain


def _nsa_prep_kernel(uq_ref, uks_ref, uvs_ref, ukw_ref, uvw_ref, gq_ref, gks_ref, gkw_ref, bdq_ref, bdk_ref,
                     q_ref, ks_ref, vs_ref, kw_ref, vw_ref):
    scale = NSA_HEAD_DIM ** -0.5
    q_ref[...] = (_group_rms(uq_ref[...], bdq_ref[...], gq_ref[...]) * scale).astype(BF16)
    ks_ref[...] = _group_rms(uks_ref[...], bdk_ref[...], gks_ref[...]).astype(BF16)
    kw_ref[...] = _group_rms(ukw_ref[...], bdk_ref[...], gkw_ref[...]).astype(BF16)
    vs_ref[...] = uvs_ref[...].astype(BF16)
    vw_ref[...] = uvw_ref[...].astype(BF16)


def _block_diag_mean(n):
    i = jnp.arange(n) // NSA_HEAD_DIM
    return ((i[:, None] == i[None, :]).astype(F32) / NSA_HEAD_DIM).astype(BF16)


def nsa_prep(u, qk_gain, col_q, col_kv, tm=512):
    t = u.shape[0]
    assert t % tm == 0 and col_q % NSA_DIM == 0 and col_kv % KV_LANES == 0
    qb = col_q // NSA_DIM
    kb = col_kv // KV_LANES
    kv_spec = lambda j: pl.BlockSpec((tm, KV_LANES), lambda i, j=j: (i, kb + j))
    full = lambda r, c: pl.BlockSpec((r, c), lambda i: (0, 0))
    row = lambda c: pl.BlockSpec((tm, c), lambda i: (i, 0))
    gq = jnp.tile(qk_gain[0], NSA_HEADS).reshape(1, NSA_DIM)
    gks = jnp.tile(qk_gain[2], NSA_KV_GROUPS).reshape(1, KV_LANES)
    gkw = jnp.tile(qk_gain[3], NSA_KV_GROUPS).reshape(1, KV_LANES)
    return pl.pallas_call(
        _nsa_prep_kernel,
        grid=(t // tm,),
        in_specs=[pl.BlockSpec((tm, NSA_DIM), lambda i: (i, qb)), kv_spec(2), kv_spec(3), kv_spec(4), kv_spec(5),
                  full(1, NSA_DIM), full(1, KV_LANES), full(1, KV_LANES), full(NSA_DIM, NSA_DIM), full(KV_LANES, KV_LANES)],
        out_specs=[row(NSA_DIM), row(KV_LANES), row(KV_LANES), row(KV_LANES), row(KV_LANES)],
        out_shape=[jax.ShapeDtypeStruct((t, NSA_DIM), BF16)] + [jax.ShapeDtypeStruct((t, KV_LANES), BF16)] * 4,
        compiler_params=pltpu.CompilerParams(dimension_semantics=("arbitrary",)),
        name="nsa_prep",
    )(u, u, u, u, u, gq, gks, gkw, _block_diag_mean(NSA_DIM), _block_diag_mean(KV_LANES))


def _gelu_tanh(x):
    return 0.5 * x * (1.0 + jnp.tanh(0.7978845608028654 * (x + 0.044715 * x * x * x)))


def _nsa_compress_kernel(uk_ref, uv_ref, pe_ref, w1_ref, w2_ref, gk_ref, bd_ref, kc_ref, vc_ref):
    n_row = uk_ref.shape[1] // CMP_STRIDE
    d = NSA_HEAD_DIM
    for kv, (src, dst) in enumerate(((uk_ref, kc_ref), (uv_ref, vc_ref))):
        top = [jnp.zeros((n_row, CMP_HIDDEN), F32) for _ in range(NSA_KV_GROUPS)]
        bot = [jnp.zeros((n_row, CMP_HIDDEN), F32) for _ in range(NSA_KV_GROUPS)]
        for l in range(CMP_STRIDE):
            x2 = src[0, pl.ds(l, n_row, stride=CMP_STRIDE), :]
            l2 = l + CMP_STRIDE
            for g in range(NSA_KV_GROUPS):
                x = x2[:, g * d:(g + 1) * d]
                top[g] += _dot((x + pe_ref[kv, l:l + 1, :]).astype(BF16), w1_ref[kv, l * d:(l + 1) * d, :])
                bot[g] += _dot((x + pe_ref[kv, l2:l2 + 1, :]).astype(BF16), w1_ref[kv, l2 * d:(l2 + 1) * d, :])
        outs = []
        for g in range(NSA_KV_GROUPS):
            hid = top[g] + pltpu.roll(bot[g], n_row - 1, 0)
            outs.append(_dot(_gelu_tanh(hid).astype(BF16), w2_ref[kv]))
        y = jnp.concatenate(outs, axis=1)
        if kv == 0:
            y = _group_rms(y, bd_ref[...], gk_ref[...])
        dst[0] = y.astype(BF16)


def nsa_compress(u3, qk_gain, cmp_pe, cmp_w1, cmp_w2, col_kv):
    b, s, _ = u3.shape
    kb = col_kv // KV_LANES
    n_row = s // CMP_STRIDE
    full = lambda *shape: pl.BlockSpec(shape, lambda i: (0,) * len(shape))
    gk = jnp.tile(qk_gain[1], NSA_KV_GROUPS).reshape(1, KV_LANES)
    out_spec = pl.BlockSpec((1, n_row, KV_LANES), lambda i: (i, 0, 0))
    return pl.pallas_call(
        _nsa_compress_kernel,
        grid=(b,),
        in_specs=[pl.BlockSpec((1, s, KV_LANES), lambda i: (i, 0, kb)),
                  pl.BlockSpec((1, s, KV_LANES), lambda i: (i, 0, kb + 1)),
                  full(2, CMP_LEN, NSA_HEAD_DIM), full(2, CMP_LEN * NSA_HEAD_DIM, CMP_HIDDEN),
                  full(2, CMP_HIDDEN, NSA_HEAD_DIM), full(1, KV_LANES), full(KV_LANES, KV_LANES)],
        out_specs=[out_spec, out_spec],
        out_shape=[jax.ShapeDtypeStruct((b, n_row, KV_LANES), BF16)] * 2,
        compiler_params=pltpu.CompilerParams(dimension_semantics=("arbitrary",)),
        name="nsa_compress",
    )(u3, u3, cmp_pe, cmp_w1.astype(BF16), cmp_w2.astype(BF16), gk, _block_diag_mean(KV_LANES))


def _stack_heads(q, g):
    d = NSA_HEAD_DIM
    base = g * NSA_HPG * d
    return jnp.concatenate([q[:, base + h * d: base + (h + 1) * d] for h in range(NSA_HPG)], axis=0)


def _nsa_select_kernel(q_ref, kc_ref, vc_ref, ov_ref, oc_ref, sel_ref, flag_ref):
    qt = pl.program_id(1)
    qbl, d = NSA_QBLOCK, NSA_HEAD_DIM
    n_key = kc_ref.shape[1]
    n_slc = ov_ref.shape[1]
    t = qt * qbl + lax.broadcasted_iota(jnp.int32, (qbl, n_key), 0)
    c = lax.broadcasted_iota(jnp.int32, (qbl, n_key), 1)
    dist = (t - (c * CMP_STRIDE + CMP_LEN - 1)).astype(F32)
    ok = dist >= 0
    q = q_ref[0]
    imps = []
    for g in range(NSA_KV_GROUPS):
        kc = kc_ref[0, :, g * d:(g + 1) * d]
        vc = vc_ref[0, :, g * d:(g + 1) * d]
        s_all = _dot_nt(_stack_heads(q, g), kc)
        p_sum = jnp.zeros((qbl, n_key), F32)
        ps = []
        for h in range(NSA_HPG):
            s = jnp.where(ok, s_all[h * qbl:(h + 1) * qbl] - _head_slope(g, h) * dist, NEG_INF)
            m = jnp.max(s, axis=-1, keepdims=True)
            e = jnp.where(ok, jnp.exp(s - m), 0.0)
            l = jnp.sum(e, axis=-1, keepdims=True)
            p = e * jnp.where(l > 0, 1.0 / l, 0.0)
            ps.append(p.astype(BF16))
            p_sum += p
        o_all = _dot(jnp.concatenate(ps, axis=0), vc)
        for h in range(NSA_HPG):
            col = (g * NSA_HPG + h) * d
            oc_ref[0, :, col:col + d] = o_all[h * qbl:(h + 1) * qbl]
        imps.append(_split_dot(p_sum, ov_ref[...]))
    imp_t = jnp.concatenate(imps, axis=1).T
    j = lax.broadcasted_iota(jnp.int32, (n_slc, qbl), 0)
    forced = (j == 0) | (j == qt) | (j == qt - 1)
    visible = j <= qt
    sels = []
    for g in range(NSA_KV_GROUPS):
        score = jnp.where(visible, imp_t[g * n_slc:(g + 1) * n_slc] + jnp.where(forced, SLC_FORCE_BONUS, 0.0), NEG_INF)
        sel = jnp.zeros((n_slc, qbl), F32)
        for _ in range(min(SLC_TOPN, n_slc)):
            m = jnp.max(score, axis=0, keepdims=True)
            first = jnp.min(jnp.where(score == m, j, n_slc), axis=0, keepdims=True)
            pick = j == first
            sel = jnp.where(pick, 1.0, sel)
            score = jnp.where(pick, -3e38, score)
        sels.append(jnp.where(visible, sel, 0.0))
    sel_all = jnp.concatenate(sels, axis=0).T
    sel_ref[0] = sel_all.astype(BF16)
    flag_ref[0, 0] = jnp.broadcast_to(jnp.max(sel_all, axis=0, keepdims=True), flag_ref.shape[2:])


def nsa_select(q3, k_cmp, v_cmp):
    b, s, _ = q3.shape
    n_qt = s // NSA_QBLOCK
    n_slc = s // SLC_LEN
    n_key = k_cmp.shape[1]
    c_lo = jnp.arange(n_key) * CMP_STRIDE
    j_lo = jnp.arange(n_slc) * SLC_LEN
    overlap = ((c_lo[:, None] < j_lo[None, :] + SLC_LEN) & (c_lo[:, None] + CMP_LEN > j_lo[None, :])).astype(BF16)
    tile = lambda c: pl.BlockSpec((1, NSA_QBLOCK, c), lambda i, t: (i, t, 0))
    per_b = pl.BlockSpec((1, n_key, KV_LANES), lambda i, t: (i, 0, 0))
    return pl.pallas_call(
        _nsa_select_kernel,
        grid=(b, n_qt),
        in_specs=[tile(NSA_DIM), per_b, per_b, pl.BlockSpec((n_key, n_slc), lambda i, t: (0, 0))],
        out_specs=[tile(NSA_DIM), tile(2 * n_slc), pl.BlockSpec((1, 1, 8, 2 * n_slc), lambda i, t: (i, t, 0, 0))],
        out_shape=[jax.ShapeDtypeStruct((b, s, NSA_DIM), F32), jax.ShapeDtypeStruct((b, s, 2 * n_slc), BF16),
                   jax.ShapeDtypeStruct((b, n_qt, 8, 2 * n_slc), F32)],
        compiler_params=pltpu.CompilerParams(dimension_semantics=("arbitrary", "arbitrary")),
        name="nsa_select",
    )(q3, k_cmp, v_cmp, overlap)


def _nsa_attend_kernel(count_ref, list_ref, q_ref, ks_ref, vs_ref, kw_ref, vw_ref, sel_ref, oc_ref, gate_ref, out_ref,
                       ksel_ref, vsel_ref):
    bi, qt = pl.program_id(0), pl.program_id(1)
    n_qt = pl.num_programs(1)
    qbl, d, ch = NSA_QBLOCK, NSA_HEAD_DIM, SEL_CHUNK
    n_slc = sel_ref.shape[2] // NSA_KV_GROUPS
    n_word = n_slc // IDS_PER_WORD
    q = q_ref[0]
    gates = jax.nn.sigmoid(gate_ref[0])
    lane = lax.broadcasted_iota(jnp.int32, (1, ch * SLC_LEN), 1)
    slot_of_lane = lane // SLC_LEN
    t_sel = qt * qbl + lax.broadcasted_iota(jnp.int32, (qbl, ch * SLC_LEN), 0)
    j_iota = lax.broadcasted_iota(jnp.int32, (n_slc, ch * SLC_LEN), 0)
    win_start = jnp.maximum(qt - WINDOW // qbl, 0) * qbl
    t_win = qt * qbl + lax.broadcasted_iota(jnp.int32, (qbl, N_WIN_KEYS), 0)
    dist_win = t_win - (win_start + lax.broadcasted_iota(jnp.int32, (qbl, N_WIN_KEYS), 1))
    ok_win = (dist_win >= 0) & (dist_win < WINDOW)
    dist_win = dist_win.astype(F32)

    groups = range(NSA_KV_GROUPS)
    lanes = [slice(g * d, (g + 1) * d) for g in groups]
    qs = [_stack_heads(q, g) for g in groups]
    tile_g = [(bi * n_qt + qt) * NSA_KV_GROUPS + g for g in groups]
    n_sel = [count_ref[tg] for tg in tile_g]
    sel_g = [sel_ref[0, :, g * n_slc:(g + 1) * n_slc] for g in groups]

    def sel_scores(c, g):
        word0 = tile_g[g] * n_word + c * (ch // IDS_PER_WORD)
        j_row = jnp.full((1, ch * SLC_LEN), -1, jnp.int32)
        for slot in range(ch):
            valid = c * ch + slot < n_sel[g]
            jb = (list_ref[word0 + slot // IDS_PER_WORD] >> (8 * (slot % IDS_PER_WORD))) & 0xFF
            rows = pl.ds(pl.multiple_of(jb * SLC_LEN, SLC_LEN), SLC_LEN)
            ksel_ref[g, slot * SLC_LEN:(slot + 1) * SLC_LEN, :] = ks_ref[0, rows, lanes[g]]
            vsel_ref[g, slot * SLC_LEN:(slot + 1) * SLC_LEN, :] = vs_ref[0, rows, lanes[g]]
            j_row = jnp.where(slot_of_lane == slot, jnp.where(valid, jb, -1), j_row)
        s_all = _dot_nt(qs[g], ksel_ref[g])
        chosen = _dot(sel_g[g], (j_iota == j_row).astype(BF16))
        dist = t_sel - (j_row * SLC_LEN + lane % SLC_LEN)
        ok = (chosen > 0.5) & (dist >= 0)
        return s_all, ok, dist.astype(F32)

    def sel_softmax(scores, carry, g):
        s_all, ok, dist = scores
        m_old, l_old, acc = carry
        ps, ms, ls = [], [], []
        for h in range(NSA_HPG):
            rows_h = slice(h * qbl, (h + 1) * qbl)
            s = jnp.where(ok, s_all[rows_h] - _head_slope(g, h) * dist, NEG_INF)
            m_new = jnp.maximum(m_old[rows_h], jnp.max(s, axis=-1, keepdims=True))
            p = jnp.exp(s - m_new)
            alpha = jnp.exp(m_old[rows_h] - m_new)
            ls.append(alpha * l_old[rows_h] + jnp.sum(p, axis=-1, keepdims=True))
            ms.append(m_new)
            ps.append(p.astype(BF16))
        m_new = jnp.concatenate(ms, axis=0)
        alpha = jnp.exp(m_old - m_new)
        acc = alpha * acc + _dot(jnp.concatenate(ps, axis=0), vsel_ref[g])
        return m_new, jnp.concatenate(ls, axis=0), acc

    win_rows = pl.ds(pl.multiple_of(win_start, qbl), N_WIN_KEYS)
    s_win = [_dot_nt(qs[g], kw_ref[0, win_rows, lanes[g]]) for g in groups]
    sc0 = [sel_scores(0, g) for g in groups]
    p_win, inv_win = [], []
    for g in groups:
        ps, inv = [], []
        for h in range(NSA_HPG):
            s = jnp.where(ok_win, s_win[g][h * qbl:(h + 1) * qbl] - _head_slope(g, h) * dist_win, NEG_INF)
            e = jnp.exp(s - jnp.max(s, axis=-1, keepdims=True))
            inv.append(1.0 / jnp.sum(e, axis=-1, keepdims=True))
            ps.append(e.astype(BF16))
        p_win.append(jnp.concatenate(ps, axis=0))
        inv_win.append(jnp.concatenate(inv, axis=0))
    init = (jnp.full((NSA_HPG * qbl, 1), M_INIT, F32), jnp.zeros((NSA_HPG * qbl, 1), F32),
            jnp.zeros((NSA_HPG * qbl, d), F32))
    carry = [sel_softmax(sc0[g], init, g) for g in groups]
    o_win = [_dot(p_win[g], vw_ref[0, win_rows, lanes[g]]) * inv_win[g] for g in groups]

    for g in groups:
        rest = lax.fori_loop(1, (n_sel[g] + ch - 1) // ch,
                             lambda c, cr, g=g: sel_softmax(sel_scores(c, g), cr, g), carry[g])
        o_sel = rest[2] * (1.0 / rest[1])
        for h in range(NSA_HPG):
            hh = g * NSA_HPG + h
            col = hh * d
            rows_h = slice(h * qbl, (h + 1) * qbl)
            out_ref[0, :, col:col + d] = (gates[:, hh:hh + 1] * oc_ref[0, :, col:col + d]
                                          + gates[:, NSA_HEADS + hh:NSA_HEADS + hh + 1] * o_sel[rows_h]
                                          + gates[:, 2 * NSA_HEADS + hh:2 * NSA_HEADS + hh + 1] * o_win[g][rows_h])


def nsa_attend(counts, lists, q3, ks, vs, kw, vw, sel, o_cmp, u3, col_gate):
    b, s, _ = q3.shape
    n_qt = s // NSA_QBLOCK
    assert col_gate % 128 == 0 and s >= N_WIN_KEYS and SEL_CHUNK % IDS_PER_WORD == 0
    gb = col_gate // 128
    tile = lambda c: pl.BlockSpec((1, NSA_QBLOCK, c), lambda i, t, counts, lists: (i, t, 0))
    per_b = pl.BlockSpec((1, s, KV_LANES), lambda i, t, counts, lists: (i, 0, 0))
    grid_spec = pltpu.PrefetchScalarGridSpec(
        num_scalar_prefetch=2,
        grid=(b, n_qt),
        in_specs=[tile(NSA_DIM), per_b, per_b, per_b, per_b, tile(sel.shape[2]), tile(NSA_DIM),
                  pl.BlockSpec((1, NSA_QBLOCK, 128), lambda i, t, counts, lists: (i, t, gb))],
        out_specs=tile(NSA_DIM),
        scratch_shapes=[pltpu.VMEM((NSA_KV_GROUPS, SEL_CHUNK * SLC_LEN, NSA_HEAD_DIM), BF16),
                        pltpu.VMEM((NSA_KV_GROUPS, SEL_CHUNK * SLC_LEN, NSA_HEAD_DIM), BF16)],
    )
    return pl.pallas_call(
        _nsa_attend_kernel,
        grid_spec=grid_spec,
        out_shape=jax.ShapeDtypeStruct((b, s, NSA_DIM), F32),
        compiler_params=pltpu.CompilerParams(dimension_semantics=("arbitrary", "arbitrary")),
        name="nsa_attend",
    )(counts, lists, q3, ks, vs, kw, vw, sel, o_cmp, u3)


def _pack_union_lists(flags, n_slc):
    assert n_slc <= 256 and n_slc % IDS_PER_WORD == 0
    b, n_qt = flags.shape[:2]
    f = flags[:, :, 0, :].reshape(b, n_qt, NSA_KV_GROUPS, n_slc) > 0.5
    fi = f.astype(jnp.int32)
    counts = jnp.sum(fi, axis=-1)
    ids = jnp.arange(n_slc, dtype=jnp.int32)
    pos = jnp.where(f, jnp.cumsum(fi, axis=-1) - 1, counts[..., None] + jnp.cumsum(1 - fi, axis=-1) - 1)
    order = jnp.sum(jnp.where(pos[..., :, None] == ids, ids[:, None], 0), axis=-2)
    order = order.reshape(b, n_qt, NSA_KV_GROUPS, n_slc // IDS_PER_WORD, IDS_PER_WORD)
    words = jnp.sum(order << (8 * jnp.arange(IDS_PER_WORD, dtype=jnp.int32)), axis=-1, dtype=jnp.int32)
    return counts.reshape(-1), words.reshape(-1)


def nsa_mixer_pallas(u3, qk_gain, cmp_pe, cmp_w1, cmp_w2, col_q, col_kv, col_gate):
    b, s, d_in = u3.shape
    q, ks, vs, kw, vw = nsa_prep(u3.reshape(b * s, d_in), qk_gain, col_q, col_kv)
    k_cmp, v_cmp = nsa_compress(u3, qk_gain, cmp_pe, cmp_w1, cmp_w2, col_kv)
    q3 = q.reshape(b, s, NSA_DIM)
    r3 = lambda a: a.reshape(b, s, KV_LANES)
    o_cmp, sel, flags = nsa_select(q3, k_cmp, v_cmp)
    counts, lists = _pack_union_lists(flags, s // SLC_LEN)
    return nsa_attend(counts, lists, q3, r3(ks), r3(vs), r3(kw), r3(vw), sel, o_cmp, u3, col_gate)


GDN_TILE = 256
GDN_HALO = 8


def _dot3(a, b):
    ah = a.astype(BF16)
    bh = b.astype(BF16)
    al = (a - ah.astype(F32)).astype(BF16)
    bl = (b - bh.astype(F32)).astype(BF16)
    return _dot(ah, bh) + _dot(ah, bl) + _dot(al, bh)


def _dot_tn(a, b):
    return lax.dot_general(a, b, (((0,), (0,)), ((), ())), preferred_element_type=F32)


def _softplus(x):
    return jnp.maximum(x, 0.0) + jnp.log(1.0 + jnp.exp(-jnp.abs(x)))


def _l2_norm(x):
    return x * lax.rsqrt(jnp.sum(x * x, axis=-1, keepdims=True) + RMS_EPS)


def _gdn_kernel(qkv_ref, z_ref, small_ref, cw_ref, coef_ref, gain_ref, y_ref, xe_ref, state_ref):
    tt = pl.program_id(1)
    tile, c, hd = GDN_TILE, GDN_CHUNK, GDN_HEAD_DIM

    @pl.when(tt == 0)
    def _():
        xe_ref[0:GDN_HALO, :] = jnp.zeros((GDN_HALO, xe_ref.shape[1]), F32)
        state_ref[...] = jnp.zeros_like(state_ref)

    @pl.when(tt > 0)
    def _():
        xe_ref[0:GDN_HALO, :] = xe_ref[tile:tile + GDN_HALO, :]

    xe_ref[GDN_HALO:, :] = qkv_ref[0]
    conv = jnp.zeros((tile, xe_ref.shape[1]), F32)
    for j in range(GDN_CONV):
        conv += cw_ref[j:j + 1, :] * xe_ref[pl.ds(GDN_HALO - (GDN_CONV - 1) + j, tile), :]
    act = conv * jax.nn.sigmoid(conv)

    small = small_ref[0]
    beta_all = jax.nn.sigmoid(small)
    g_all = coef_ref[0:1, :] * _softplus(small + coef_ref[1:2, :])
    row = lax.broadcasted_iota(jnp.int32, (c, SMALL_LANES), 0)
    ri = lax.broadcasted_iota(jnp.int32, (c, c), 0)
    ci = lax.broadcasted_iota(jnp.int32, (c, c), 1)
    lower = ri >= ci
    strict = ri > ci

    n_chunk = tile // c
    pairs = [(n, h) for n in range(n_chunk) for h in range(GDN_HEADS)]
    gcs, gc_ts = [], []
    for n in range(n_chunk):
        gc = g_all[n * c:(n + 1) * c]
        shift = 1
        while shift < c:
            gc = gc + jnp.where(row >= shift, pltpu.roll(gc, shift, 0), 0.0)
            shift *= 2
        gcs.append(gc)
        gc_ts.append(gc.T)
    pre = []
    for n, h in pairs:
        rows = slice(n * c, (n + 1) * c)
        q = _l2_norm(act[rows, h * hd:(h + 1) * hd]) * hd ** -0.5
        k = _l2_norm(act[rows, GDN_DIM + h * hd:GDN_DIM + (h + 1) * hd])
        v = act[rows, 2 * GDN_DIM + h * hd:2 * GDN_DIM + (h + 1) * hd]
        beta = beta_all[rows, LANE_BETA + h:LANE_BETA + h + 1]
        gcol = gcs[n][:, LANE_DECAY + h:LANE_DECAY + h + 1]
        grow = gc_ts[n][LANE_DECAY + h:LANE_DECAY + h + 1, :]
        g_last = gcol[c - 1:c, :]
        decay = jnp.where(lower, jnp.exp(jnp.where(lower, gcol - grow, 0.0)), 0.0)
        kb = k * beta
        kh = k.astype(BF16)
        pre.append(dict(
            lmat=jnp.where(strict, _dot_nt(kb.astype(BF16), kh) * decay, 0.0),
            rhs=jnp.concatenate([v * beta, kb * jnp.exp(gcol)], axis=1),
            attn=(_dot_nt(q.astype(BF16), kh) * decay).astype(BF16),
            q_dec=(q * jnp.exp(gcol)).astype(BF16),
            k_dec=(k * jnp.exp(g_last - gcol)).astype(BF16),
            d_last=jnp.exp(g_last)))
    power = [p['lmat'].astype(BF16) for p in pre]
    rhs = [p['rhs'] - _dot(lm, p['rhs'].astype(BF16)) for lm, p in zip(power, pre)]
    for _ in range(5):
        power = [_dot(lm, lm).astype(BF16) for lm in power]
        rhs = [r + _dot(lm, r.astype(BF16)) for lm, r in zip(power, rhs)]

    state = [state_ref[h] for h in range(GDN_HEADS)]
    for n in range(n_chunk):
        rows = slice(n * c, (n + 1) * c)
        sb = [s.astype(BF16) for s in state]
        ps = [pre[n * GDN_HEADS + h] for h in range(GDN_HEADS)]
        rs = [rhs[n * GDN_HEADS + h] for h in range(GDN_HEADS)]
        v_new = [(r[:, :hd] - _dot(r[:, hd:].astype(BF16), s)).astype(BF16) for r, s in zip(rs, sb)]
        outs = [_dot(p['q_dec'], s) + _dot(p['attn'], vn) for p, s, vn in zip(ps, sb, v_new)]
        state = [s * p['d_last'] + _dot_tn(p['k_dec'], vn) for p, s, vn in zip(ps, state, v_new)]
        for h, o in enumerate(outs):
            o = o * lax.rsqrt(jnp.mean(o * o, axis=-1, keepdims=True) + RMS_EPS) * gain_ref[...]
            zz = z_ref[0, rows, h * hd:(h + 1) * hd]
            y_ref[0, rows, h * hd:(h + 1) * hd] = o * (zz * jax.nn.sigmoid(zz))
    for h in range(GDN_HEADS):
        state_ref[h] = state[h]


def gdn_mixer(u3, col_qkv, col_z, col_small, conv_w, a_log, dt_bias, out_gain):
    b, s, _ = u3.shape
    lane = jnp.arange(SMALL_LANES)
    in_decay = (lane >= LANE_DECAY) & (lane < LANE_DECAY + GDN_HEADS)
    idx = jnp.clip(lane - LANE_DECAY, 0, GDN_HEADS - 1)
    coef = jnp.stack([jnp.where(in_decay, -jnp.exp(a_log)[idx], 0.0), jnp.where(in_decay, dt_bias[idx], 0.0)])
    qkv_w = 3 * GDN_DIM
    assert col_qkv % qkv_w == 0 and col_z % GDN_DIM == 0 and col_small % SMALL_LANES == 0
    full = lambda r, cc: pl.BlockSpec((r, cc), lambda i, t: (0, 0))
    return pl.pallas_call(
        _gdn_kernel,
        grid=(b, s // GDN_TILE),
        in_specs=[pl.BlockSpec((1, GDN_TILE, qkv_w), lambda i, t: (i, t, col_qkv // qkv_w)),
                  pl.BlockSpec((1, GDN_TILE, GDN_DIM), lambda i, t: (i, t, col_z // GDN_DIM)),
                  pl.BlockSpec((1, GDN_TILE, SMALL_LANES), lambda i, t: (i, t, col_small // SMALL_LANES)),
                  full(GDN_CONV, qkv_w), full(2, SMALL_LANES), full(1, GDN_HEAD_DIM)],
        out_specs=pl.BlockSpec((1, GDN_TILE, GDN_DIM), lambda i, t: (i, t, 0)),
        out_shape=jax.ShapeDtypeStruct((b, s, GDN_DIM), F32),
        scratch_shapes=[pltpu.VMEM((GDN_TILE + GDN_HALO, qkv_w), F32), pltpu.VMEM((GDN_HEADS, GDN_HEAD_DIM, GDN_HEAD_DIM), F32)],
        compiler_params=pltpu.CompilerParams(dimension_semantics=("arbitrary", "arbitrary")),
        name="gdn_mixer",
    )(u3, u3, u3, conv_w, coef, out_gain.reshape(1, GDN_HEAD_DIM))


CONV_TILE = 512


def _short_conv_kernel(u_ref, cw_ref, y_ref, xe_ref):
    tile, cd = CONV_TILE, CONV_DIM

    @pl.when(pl.program_id(1) == 0)
    def _():
        xe_ref[0:GDN_HALO, :] = jnp.zeros((GDN_HALO, cd), F32)

    @pl.when(pl.program_id(1) > 0)
    def _():
        xe_ref[0:GDN_HALO, :] = xe_ref[tile:tile + GDN_HALO, :]

    xe_ref[GDN_HALO:, :] = u_ref[0, :, cd:2 * cd] * u_ref[0, :, 2 * cd:3 * cd]
    conv = jnp.zeros((tile, cd), F32)
    for j in range(CONV_WIDTH):
        conv += cw_ref[j:j + 1, :] * xe_ref[pl.ds(GDN_HALO - (CONV_WIDTH - 1) + j, tile), :]
    y_ref[0] = u_ref[0, :, 0:cd] * conv


def short_conv(u3, conv_w):
    b, s, _ = u3.shape
    return pl.pallas_call(
        _short_conv_kernel,
        grid=(b, s // CONV_TILE),
        in_specs=[pl.BlockSpec((1, CONV_TILE, 3 * CONV_DIM), lambda i, t: (i, t, 0)),
                  pl.BlockSpec((CONV_WIDTH, CONV_DIM), lambda i, t: (0, 0))],
        out_specs=pl.BlockSpec((1, CONV_TILE, CONV_DIM), lambda i, t: (i, t, 0)),
        out_shape=jax.ShapeDtypeStruct((b, s, CONV_DIM), F32),
        scratch_shapes=[pltpu.VMEM((CONV_TILE + GDN_HALO, CONV_DIM), F32)],
        compiler_params=pltpu.CompilerParams(dimension_semantics=("arbitrary", "arbitrary")),
        name="short_conv",
    )(u3, conv_w)


def _merge_kernel(x_ref, g_ref, ya_ref, yn_ref, yg_ref, wb_ref, wg_ref, bg_ref, wo_ref, o_ref):
    x = x_ref[...]
    d = x.shape[1]
    h = (x * lax.rsqrt(jnp.mean(x * x, axis=-1, keepdims=True) + RMS_EPS) * g_ref[...]).astype(BF16)
    merged = jnp.zeros(x.shape, F32)
    for r, y_ref in enumerate((ya_ref, yn_ref, yg_ref)):
        gate = jax.nn.sigmoid(_dot(h, wg_ref[:, r * d:(r + 1) * d]) + bg_ref[:, r * d:(r + 1) * d])
        merged += gate * _dot(y_ref[...].astype(BF16), wb_ref[r])
    o_ref[...] = x + _dot(merged.astype(BF16), wo_ref[...])


def merge_branches(x2d, gain, y_a, y_n, y_g, w_branch, w_gate, b_gate, w_out, tm=512):
    t, d = x2d.shape
    row = lambda c: pl.BlockSpec((tm, c), lambda i: (i, 0))
    full = lambda *shape: pl.BlockSpec(shape, lambda i: (0,) * len(shape))
    return pl.pallas_call(
        _merge_kernel,
        grid=(t // tm,),
        in_specs=[row(d), full(1, d), row(BRANCH_DIM), row(BRANCH_DIM), row(BRANCH_DIM),
                  full(N_BRANCH, BRANCH_DIM, d), full(d, N_BRANCH * d), full(1, N_BRANCH * d), full(d, d)],
        out_specs=row(d),
        out_shape=jax.ShapeDtypeStruct((t, d), F32),
        compiler_params=pltpu.CompilerParams(dimension_semantics=("arbitrary",), vmem_limit_bytes=56 * 1024 * 1024),
        name="merge_branches",
    )(x2d, gain.reshape(1, d), y_a, y_n, y_g, w_branch.astype(BF16), w_gate.astype(BF16), b_gate.reshape(1, -1), w_out.astype(BF16))


def hybrid_mixer(x2d, b, s, norm_gain, w_in, conv_a_w, nsa_qk_gain, cmp_pe, cmp_w1, cmp_w2, gdn_conv_w, gdn_a_log, gdn_dt_bias, gdn_out_gain, w_branch, w_gate, b_gate, w_out):
    t = b * s
    u3 = norm_proj(x2d, norm_gain, permute_in_proj(w_in)).reshape(b, s, D_U)
    y_a = short_conv(u3, conv_a_w)
    y_n = nsa_mixer_pallas(u3, nsa_qk_gain, cmp_pe, cmp_w1, cmp_w2, COL_NSA_Q, COL_NSA_KV, COL_SMALL)
    y_g = gdn_mixer(u3, COL_GDN_QKV, COL_GDN_Z, COL_SMALL, gdn_conv_w, gdn_a_log, gdn_dt_bias, gdn_out_gain)
    return merge_branches(x2d, norm_gain, y_a.reshape(t, BRANCH_DIM), y_n.reshape(t, BRANCH_DIM), y_g.reshape(t, BRANCH_DIM), w_branch, w_gate, b_gate, w_out)


MOE_ROWS = 512
ROUTE_LANES = 128
N_ROUTER = MOE_GROUPS + N_EXPERTS


def _moe_route_kernel(x_ref, g_ref, wr_ref, br_ref, tri_ref, h_ref, route_ref, cnt_ref, run_ref):
    @pl.when(pl.program_id(0) == 0)
    def _():
        run_ref[...] = jnp.zeros_like(run_ref)

    x = x_ref[...]
    h = x * lax.rsqrt(jnp.mean(x * x, axis=-1, keepdims=True) + RMS_EPS) * g_ref[...]
    bits = lax.bitcast_convert_type(h.astype(BF16).astype(F32), jnp.uint32)
    half = h.shape[1] // 2
    h_ref[...] = (bits[:, half:] & jnp.uint32(0xFFFF0000)) | (bits[:, :half] >> 16)
    logits = _dot(h.astype(BF16), wr_ref[...]) + br_ref[...]
    lane = lax.broadcasted_iota(jnp.int32, logits.shape, 1)
    first_of = lambda hit: jnp.min(jnp.where(hit, lane, ROUTE_LANES), axis=-1, keepdims=True)
    is_grp = lane < MOE_GROUPS
    lg = jnp.where(is_grp, logits, NEG_INF)
    m_g = jnp.max(lg, axis=-1, keepdims=True)
    grp = first_of(lg == m_g)
    p_grp = 1.0 / jnp.sum(jnp.where(is_grp, jnp.exp(lg - m_g), 0.0), axis=-1, keepdims=True)
    lo = MOE_GROUPS + grp * EXPERTS_PER_GROUP
    le = jnp.where((lane >= lo) & (lane < lo + EXPERTS_PER_GROUP), logits, NEG_INF)
    m1 = jnp.max(le, axis=-1, keepdims=True)
    i1 = first_of(le == m1)
    le2 = jnp.where(lane == i1, NEG_INF, le)
    m2 = jnp.max(le2, axis=-1, keepdims=True)
    i2 = first_of(le2 == m2)
    r = jnp.exp(m2 - m1)
    g1 = p_grp / (1.0 + r)
    g2 = p_grp * r / (1.0 + r)
    e1 = i1 - MOE_GROUPS
    e2 = i2 - MOE_GROUPS
    hit1 = lane == e1
    hit2 = lane == e2
    onehot = (hit1 | hit2).astype(BF16)
    before = _dot(tri_ref[...], onehot) + run_ref[...]
    r1 = jnp.sum(jnp.where(hit1, before, 0.0), axis=-1, keepdims=True)
    r2 = jnp.sum(jnp.where(hit2, before, 0.0), axis=-1, keepdims=True)
    run_ref[...] += jnp.sum(onehot.astype(F32), axis=0, keepdims=True)
    rec = jnp.zeros(logits.shape, F32)
    for k, v in enumerate((e1.astype(F32), e2.astype(F32), r1, r2, g1, g2)):
        rec = jnp.where(lane == k, v, rec)
    route_ref[...] = rec
    cnt_ref[...] = jnp.broadcast_to(run_ref[...], cnt_ref.shape)


def moe_route(x2d, gain, w_rg, b_rg, w_re, b_re, tm=512):
    t, d = x2d.shape
    pad = ROUTE_LANES - N_ROUTER
    wr = jnp.pad(jnp.concatenate([w_rg, w_re], axis=1), ((0, 0), (0, pad))).astype(BF16)
    br = jnp.pad(jnp.concatenate([b_rg, b_re]), (0, pad)).reshape(1, ROUTE_LANES)
    tri = (jnp.arange(tm)[:, None] > jnp.arange(tm)[None, :]).astype(BF16)
    full = lambda r, c: pl.BlockSpec((r, c), lambda i: (0, 0))
    return pl.pallas_call(
        _moe_route_kernel,
        grid=(t // tm,),
        in_specs=[pl.BlockSpec((tm, d), lambda i: (i, 0)), full(1, d), full(d, ROUTE_LANES), full(1, ROUTE_LANES), full(tm, tm)],
        out_specs=[pl.BlockSpec((tm, d // 2), lambda i: (i, 0)), pl.BlockSpec((tm, ROUTE_LANES), lambda i: (i, 0)), full(8, ROUTE_LANES)],
        out_shape=[jax.ShapeDtypeStruct((t, d // 2), jnp.uint32), jax.ShapeDtypeStruct((t, ROUTE_LANES), F32), jax.ShapeDtypeStruct((8, ROUTE_LANES), F32)],
        scratch_shapes=[pltpu.VMEM((1, ROUTE_LANES), F32)],
        compiler_params=pltpu.CompilerParams(dimension_semantics=("arbitrary",)),
        name="moe_route",
    )(x2d, gain.reshape(1, d), wr, br, tri)


def _row_copy(src_ref, src_row, dst_ref, dst_row, sem):
    return pltpu.make_async_copy(src_ref.at[src_row], dst_ref.at[dst_row], sem)


def _moe_dispatch_kernel(dest_ref, h_ref, buf_in_ref, buf_ref, sem):
    del buf_in_ref
    tm = h_ref.shape[0]
    base = pl.program_id(0) * tm

    def send(r, carry):
        for k in range(TOPK_IN_GROUP):
            _row_copy(h_ref, r, buf_ref, dest_ref[(base + r) * TOPK_IN_GROUP + k], sem).start(priority=k)
        return carry

    def drain(r, carry):
        for k in range(TOPK_IN_GROUP):
            _row_copy(h_ref, r, buf_ref, 0, sem).wait()
        return carry

    lax.fori_loop(0, tm, send, 0)
    lax.fori_loop(0, tm, drain, 0)


def moe_dispatch(dest, h3, n_rows, tm=512):
    t, _, d = h3.shape
    grid_spec = pltpu.PrefetchScalarGridSpec(
        num_scalar_prefetch=1, grid=(t // tm,),
        in_specs=[pl.BlockSpec((tm, 1, d), lambda i, dest: (i, 0, 0)), pl.BlockSpec(memory_space=pl.ANY)],
        out_specs=pl.BlockSpec(memory_space=pl.ANY),
        scratch_shapes=[pltpu.SemaphoreType.DMA(())],
    )
    return pl.pallas_call(
        _moe_dispatch_kernel, grid_spec=grid_spec,
        out_shape=jax.ShapeDtypeStruct((n_rows, 1, d), h3.dtype),
        input_output_aliases={2: 0},
        compiler_params=pltpu.CompilerParams(dimension_semantics=("arbitrary",), has_side_effects=True),
        name="moe_dispatch",
    )(dest, h3, jnp.zeros((n_rows, 1, d), h3.dtype))


def _moe_ffn_kernel(blk_e_ref, n_used_ref, x_ref, wg_ref, wu_ref, wd_ref, y_ref):
    del blk_e_ref
    used = pl.program_id(0) < n_used_ref[0]

    @pl.when(used)
    def _():
        w = x_ref[...]
        lo = lax.bitcast_convert_type(w << 16, F32)
        hi = lax.bitcast_convert_type(w & jnp.uint32(0xFFFF0000), F32)
        xb = jnp.concatenate([lo, hi], axis=1).astype(BF16)
        a = _dot(xb, wg_ref[0])
        mid = a * jax.nn.sigmoid(a) * _dot(xb, wu_ref[0])
        y_ref[...] = _dot(mid.astype(BF16), wd_ref[0])

    @pl.when(jnp.logical_not(used))
    def _():
        y_ref[...] = jnp.zeros_like(y_ref)


def moe_ffn(blk_expert, n_used, buf2d, w_eg, w_eu, w_ed):
    n_rows, packed_w = buf2d.shape
    d, ff = w_eg.shape[1:]
    w_in_spec = pl.BlockSpec((1, d, ff), lambda b, blk_e, n_used: (blk_e[b], 0, 0))
    grid_spec = pltpu.PrefetchScalarGridSpec(
        num_scalar_prefetch=2, grid=(n_rows // MOE_ROWS,),
        in_specs=[pl.BlockSpec((MOE_ROWS, packed_w), lambda b, blk_e, n_used: (b, 0)), w_in_spec, w_in_spec,
                  pl.BlockSpec((1, ff, d), lambda b, blk_e, n_used: (blk_e[b], 0, 0))],
        out_specs=pl.BlockSpec((MOE_ROWS, d), lambda b, blk_e, n_used: (b, 0)),
    )
    return pl.pallas_call(
        _moe_ffn_kernel, grid_spec=grid_spec,
        out_shape=jax.ShapeDtypeStruct((n_rows, d), F32),
        compiler_params=pltpu.CompilerParams(dimension_semantics=("arbitrary",)),
        name="moe_ffn",
    )(blk_expert, n_used, buf2d, w_eg.astype(BF16), w_eu.astype(BF16), w_ed.astype(BF16))


def _moe_combine_kernel(dest_ref, y_ref, gate_ref, out_ref, ya_ref, yb_ref, sem):
    tm = out_ref.shape[0]
    base = pl.program_id(0) * tm

    def fetch(r, carry):
        _row_copy(y_ref, dest_ref[(base + r) * TOPK_IN_GROUP], ya_ref, r, sem).start(priority=0)
        _row_copy(y_ref, dest_ref[(base + r) * TOPK_IN_GROUP + 1], yb_ref, r, sem).start(priority=1)
        return carry

    def drain(r, carry):
        _row_copy(y_ref, 0, ya_ref, r, sem).wait()
        _row_copy(y_ref, 0, yb_ref, r, sem).wait()
        return carry

    lax.fori_loop(0, tm, fetch, 0)
    lax.fori_loop(0, tm, drain, 0)
    gates = gate_ref[...]
    out_ref[...] = gates[:, :, 0:1] * ya_ref[...] + gates[:, :, 1:2] * yb_ref[...]


def moe_combine(dest, y3, gates3, tm=512):
    t = gates3.shape[0]
    d = y3.shape[2]
    grid_spec = pltpu.PrefetchScalarGridSpec(
        num_scalar_prefetch=1, grid=(t // tm,),
        in_specs=[pl.BlockSpec(memory_space=pl.ANY), pl.BlockSpec((tm, 1, TOPK_IN_GROUP), lambda i, dest: (i, 0, 0))],
        out_specs=pl.BlockSpec((tm, 1, d), lambda i, dest: (i, 0, 0)),
        scratch_shapes=[pltpu.VMEM((tm, 1, d), F32), pltpu.VMEM((tm, 1, d), F32), pltpu.SemaphoreType.DMA(())],
    )
    return pl.pallas_call(
        _moe_combine_kernel, grid_spec=grid_spec,
        out_shape=jax.ShapeDtypeStruct((t, 1, d), F32),
        compiler_params=pltpu.CompilerParams(dimension_semantics=("arbitrary",)),
        name="moe_combine",
    )(dest, y3, gates3)


def hier_moe_pallas(x2d, gain, w_rg, b_rg, w_re, b_re, w_eg, w_eu, w_ed):
    t, d = x2d.shape
    h, rec, cnt = moe_route(x2d, gain, w_rg, b_rg, w_re, b_re)
    counts = cnt[0, :N_EXPERTS].astype(jnp.int32)
    n_blk = (counts + MOE_ROWS - 1) // MOE_ROWS
    blk_end = jnp.cumsum(n_blk)
    pad_start = (blk_end - n_blk) * MOE_ROWS
    experts = rec[:, 0:2].astype(jnp.int32)
    dest = (pad_start[experts] + rec[:, 2:4].astype(jnp.int32)).reshape(-1)
    total_blk = t * TOPK_IN_GROUP // MOE_ROWS + N_EXPERTS
    blk_expert = jnp.minimum(jnp.searchsorted(blk_end, jnp.arange(total_blk), side='right'), N_EXPERTS - 1).astype(jnp.int32)
    buf = moe_dispatch(dest, h.reshape(t, 1, d // 2), total_blk * MOE_ROWS)
    y = moe_ffn(blk_expert, blk_end[-1:].astype(jnp.int32), buf.reshape(total_blk * MOE_ROWS, d // 2), w_eg, w_eu, w_ed)
    out = moe_combine(dest, y.reshape(total_blk * MOE_ROWS, 1, d), rec[:, 4:6].reshape(t, 1, TOPK_IN_GROUP))
    return out.reshape(t, d)


def kernel(x, norm_mix, w_in, conv_a_w, nsa_qk_gain, cmp_pe, cmp_w1, cmp_w2, gdn_conv_w, gdn_a_log, gdn_dt_bias, gdn_out_gain, w_branch, w_gate, b_gate, w_out, norm_ffn, w_router_group, b_router_group, w_router_expert, b_router_expert, w_expert_gate, w_expert_up, w_expert_down):
    b, s, dm = x.shape
    x = x.reshape(b * s, dm)
    for l in range(DEPTH):
        x = hybrid_mixer(x, b, s, norm_mix[l], w_in[l], conv_a_w[l], nsa_qk_gain[l], cmp_pe[l], cmp_w1[l], cmp_w2[l], gdn_conv_w[l], gdn_a_log[l], gdn_dt_bias[l], gdn_out_gain[l], w_branch[l], w_gate[l], b_gate[l], w_out[l])
        x = x + hier_moe_pallas(x, norm_ffn[l], w_router_group[l], b_router_group[l], w_router_expert[l], b_router_expert[l], w_expert_gate[l], w_expert_up[l], w_expert_down[l])
    return x.reshape(b, s, dm)
```

```python
import functools
import math

import jax
import jax.numpy as jnp
from jax import lax
from jax.experimental import pallas as pl
from jax.experimental.pallas import tpu as pltpu

D_MODEL = 1024
DEPTH = 4
CONV_DIM = 512
CONV_WIDTH = 3
NSA_HEADS = 8
NSA_KV_GROUPS = 2
NSA_HPG = NSA_HEADS // NSA_KV_GROUPS
NSA_HEAD_DIM = 64
NSA_DIM = NSA_HEADS * NSA_HEAD_DIM
CMP_LEN = 32
CMP_STRIDE = 16
CMP_HIDDEN = 256
SLC_LEN = 64
SLC_TOPN = 8
SLC_FORCE_BONUS = 1e6
WINDOW = 512
NSA_QBLOCK = 64
GDN_HEADS = 4
GDN_HEAD_DIM = 128
GDN_DIM = GDN_HEADS * GDN_HEAD_DIM
GDN_CONV = 4
GDN_CHUNK = 64
N_BRANCH = 3
BRANCH_DIM = 512
IN_SPLITS = (3 * CONV_DIM, NSA_DIM, 6 * NSA_KV_GROUPS * NSA_HEAD_DIM, 3 * NSA_HEADS, 3 * GDN_DIM, GDN_DIM, GDN_HEADS, GDN_HEADS)
D_IN = sum(IN_SPLITS)
COL_CONV = 0
COL_GDN_QKV = 3 * CONV_DIM
COL_NSA_Q = COL_GDN_QKV + 3 * GDN_DIM
COL_GDN_Z = COL_NSA_Q + NSA_DIM
COL_NSA_KV = COL_GDN_Z + GDN_DIM
COL_SMALL = COL_NSA_KV + 6 * NSA_KV_GROUPS * NSA_HEAD_DIM
SMALL_LANES = 128
D_U = COL_SMALL + SMALL_LANES
LANE_BETA = 3 * NSA_HEADS
LANE_DECAY = LANE_BETA + GDN_HEADS


def permute_in_proj(w_in):
    conv, nq, nkv, ng, gqkv, gz, gb, ga = split_last(w_in, IN_SPLITS)
    pad = jnp.zeros((w_in.shape[0], SMALL_LANES - LANE_DECAY - GDN_HEADS), w_in.dtype)
    return jnp.concatenate([conv, gqkv, nq, gz, nkv, ng, gb, ga, pad], axis=1)


def split_last(u, sizes):
    out, start = [], 0
    for n in sizes:
        out.append(u[..., start:start + n])
        start += n
    return out
MOE_GROUPS = 4
EXPERTS_PER_GROUP = 8
N_EXPERTS = MOE_GROUPS * EXPERTS_PER_GROUP
TOPK_IN_GROUP = 2
EXPERT_FF = 512
MOE_BLOCK = 256
RMS_EPS = 1e-6
NEG_INF = -1e30

F32 = jnp.float32
BF16 = jnp.bfloat16


def _norm_proj_kernel(x_ref, g_ref, w_ref, o_ref, h_ref):
    @pl.when(pl.program_id(1) == 0)
    def _():
        x = x_ref[...]
        y = x * lax.rsqrt(jnp.mean(x * x, axis=-1, keepdims=True) + RMS_EPS)
        h_ref[...] = (y * g_ref[...]).astype(BF16)

    o_ref[...] = jnp.dot(h_ref[...], w_ref[...], preferred_element_type=F32)


def norm_proj(x2d, gain, w, tm=1024, tn=D_U // 3):
    t, d = x2d.shape
    n = w.shape[1]
    n_blk = pl.cdiv(n, tn)
    wb = jnp.pad(w.astype(BF16), ((0, 0), (0, n_blk * tn - n)))
    return pl.pallas_call(
        _norm_proj_kernel,
        grid=(t // tm, n_blk),
        in_specs=[
            pl.BlockSpec((tm, d), lambda i, j: (i, 0)),
            pl.BlockSpec((1, d), lambda i, j: (0, 0)),
            pl.BlockSpec((d, tn), lambda i, j: (0, j)),
        ],
        out_specs=pl.BlockSpec((tm, tn), lambda i, j: (i, j)),
        out_shape=jax.ShapeDtypeStruct((t, n), F32),
        scratch_shapes=[pltpu.VMEM((tm, d), BF16)],
        compiler_params=pltpu.CompilerParams(dimension_semantics=("arbitrary", "arbitrary")),
        name="norm_proj",
    )(x2d, gain.reshape(1, d), wb)


KV_LANES = NSA_KV_GROUPS * NSA_HEAD_DIM
N_WIN_KEYS = WINDOW + NSA_QBLOCK
SEL_CHUNK = 8
IDS_PER_WORD = 4
M_INIT = -1e29


def _dot(a, b):
    return jnp.dot(a, b, preferred_element_type=F32)


def _dot_nt(a, b):
    return lax.dot_general(a, b, (((1,), (1,)), ((), ())), preferred_element_type=F32)


def _split_dot(x, m):
    hi = x.astype(BF16)
    lo = (x - hi.astype(F32)).astype(BF16)
    return _dot(hi, m) + _dot(lo, m)


def _head_slope(g, h):
    return 2.0 ** (-8.0 * (g * NSA_HPG + h + 1) / NSA_HEADS)


def _group_rms(x, bd, gain):
    ms = _split_dot(x * x, bd)
    return x * lax.rsqrt(ms + RMS_EPS) * gain


def _nsa_prep_kernel(uq_ref, uks_ref, uvs_ref, ukw_ref, uvw_ref, gq_ref, gks_ref, gkw_ref, bdq_ref, bdk_ref,
                     q_ref, ks_ref, vs_ref, kw_ref, vw_ref):
    scale = NSA_HEAD_DIM ** -0.5
    q_ref[...] = (_group_rms(uq_ref[...], bdq_ref[...], gq_ref[...]) * scale).astype(BF16)
    ks_ref[...] = _group_rms(uks_ref[...], bdk_ref[...], gks_ref[...]).astype(BF16)
    kw_ref[...] = _group_rms(ukw_ref[...], bdk_ref[...], gkw_ref[...]).astype(BF16)
    vs_ref[...] = uvs_ref[...].astype(BF16)
    vw_ref[...] = uvw_ref[...].astype(BF16)


def _block_diag_mean(n):
    i = jnp.arange(n) // NSA_HEAD_DIM
    return ((i[:, None] == i[None, :]).astype(F32) / NSA_HEAD_DIM).astype(BF16)


def nsa_prep(u, qk_gain, col_q, col_kv, tm=512):
    t = u.shape[0]
    assert t % tm == 0 and col_q % NSA_DIM == 0 and col_kv % KV_LANES == 0
    qb = col_q // NSA_DIM
    kb = col_kv // KV_LANES
    kv_spec = lambda j: pl.BlockSpec((tm, KV_LANES), lambda i, j=j: (i, kb + j))
    full = lambda r, c: pl.BlockSpec((r, c), lambda i: (0, 0))
    row = lambda c: pl.BlockSpec((tm, c), lambda i: (i, 0))
    gq = jnp.tile(qk_gain[0], NSA_HEADS).reshape(1, NSA_DIM)
    gks = jnp.tile(qk_gain[2], NSA_KV_GROUPS).reshape(1, KV_LANES)
    gkw = jnp.tile(qk_gain[3], NSA_KV_GROUPS).reshape(1, KV_LANES)
    return pl.pallas_call(
        _nsa_prep_kernel,
        grid=(t // tm,),
        in_specs=[pl.BlockSpec((tm, NSA_DIM), lambda i: (i, qb)), kv_spec(2), kv_spec(3), kv_spec(4), kv_spec(5),
                  full(1, NSA_DIM), full(1, KV_LANES), full(1, KV_LANES), full(NSA_DIM, NSA_DIM), full(KV_LANES, KV_LANES)],
        out_specs=[row(NSA_DIM), row(KV_LANES), row(KV_LANES), row(KV_LANES), row(KV_LANES)],
        out_shape=[jax.ShapeDtypeStruct((t, NSA_DIM), BF16)] + [jax.ShapeDtypeStruct((t, KV_LANES), BF16)] * 4,
        compiler_params=pltpu.CompilerParams(dimension_semantics=("arbitrary",)),
        name="nsa_prep",
    )(u, u, u, u, u, gq, gks, gkw, _block_diag_mean(NSA_DIM), _block_diag_mean(KV_LANES))


def _gelu_tanh(x):
    return 0.5 * x * (1.0 + jnp.tanh(0.7978845608028654 * (x + 0.044715 * x * x * x)))


def _nsa_compress_kernel(uk_ref, uv_ref, pe_ref, w1_ref, w2_ref, gk_ref, bd_ref, kc_ref, vc_ref):
    n_row = uk_ref.shape[1] // CMP_STRIDE
    d = NSA_HEAD_DIM
    for kv, (src, dst) in enumerate(((uk_ref, kc_ref), (uv_ref, vc_ref))):
        top = [jnp.zeros((n_row, CMP_HIDDEN), F32) for _ in range(NSA_KV_GROUPS)]
        bot = [jnp.zeros((n_row, CMP_HIDDEN), F32) for _ in range(NSA_KV_GROUPS)]
        for l in range(CMP_STRIDE):
            x2 = src[0, pl.ds(l, n_row, stride=CMP_STRIDE), :]
            l2 = l + CMP_STRIDE
            for g in range(NSA_KV_GROUPS):
                x = x2[:, g * d:(g + 1) * d]
                top[g] += _dot((x + pe_ref[kv, l:l + 1, :]).astype(BF16), w1_ref[kv, l * d:(l + 1) * d, :])
                bot[g] += _dot((x + pe_ref[kv, l2:l2 + 1, :]).astype(BF16), w1_ref[kv, l2 * d:(l2 + 1) * d, :])
        outs = []
        for g in range(NSA_KV_GROUPS):
            hid = top[g] + pltpu.roll(bot[g], n_row - 1, 0)
            outs.append(_dot(_gelu_tanh(hid).astype(BF16), w2_ref[kv]))
        y = jnp.concatenate(outs, axis=1)
        if kv == 0:
            y = _group_rms(y, bd_ref[...], gk_ref[...])
        dst[0] = y.astype(BF16)


def nsa_compress(u3, qk_gain, cmp_pe, cmp_w1, cmp_w2, col_kv):
    b, s, _ = u3.shape
    kb = col_kv // KV_LANES
    n_row = s // CMP_STRIDE
    full = lambda *shape: pl.BlockSpec(shape, lambda i: (0,) * len(shape))
    gk = jnp.tile(qk_gain[1], NSA_KV_GROUPS).reshape(1, KV_LANES)
    out_spec = pl.BlockSpec((1, n_row, KV_LANES), lambda i: (i, 0, 0))
    return pl.pallas_call(
        _nsa_compress_kernel,
        grid=(b,),
        in_specs=[pl.BlockSpec((1, s, KV_LANES), lambda i: (i, 0, kb)),
                  pl.BlockSpec((1, s, KV_LANES), lambda i: (i, 0, kb + 1)),
                  full(2, CMP_LEN, NSA_HEAD_DIM), full(2, CMP_LEN * NSA_HEAD_DIM, CMP_HIDDEN),
                  full(2, CMP_HIDDEN, NSA_HEAD_DIM), full(1, KV_LANES), full(KV_LANES, KV_LANES)],
        out_specs=[out_spec, out_spec],
        out_shape=[jax.ShapeDtypeStruct((b, n_row, KV_LANES), BF16)] * 2,
        compiler_params=pltpu.CompilerParams(dimension_semantics=("arbitrary",)),
        name="nsa_compress",
    )(u3, u3, cmp_pe, cmp_w1.astype(BF16), cmp_w2.astype(BF16), gk, _block_diag_mean(KV_LANES))


def _stack_heads(q, g):
    d = NSA_HEAD_DIM
    base = g * NSA_HPG * d
    return jnp.concatenate([q[:, base + h * d: base + (h + 1) * d] for h in range(NSA_HPG)], axis=0)


def _nsa_select_kernel(q_ref, kc_ref, vc_ref, ov_ref, oc_ref, sel_ref, flag_ref):
    qt = pl.program_id(1)
    qbl, d = NSA_QBLOCK, NSA_HEAD_DIM
    n_key = kc_ref.shape[1]
    n_slc = ov_ref.shape[1]
    t = qt * qbl + lax.broadcasted_iota(jnp.int32, (qbl, n_key), 0)
    c = lax.broadcasted_iota(jnp.int32, (qbl, n_key), 1)
    dist = (t - (c * CMP_STRIDE + CMP_LEN - 1)).astype(F32)
    ok = dist >= 0
    q = q_ref[0]
    imps = []
    for g in range(NSA_KV_GROUPS):
        kc = kc_ref[0, :, g * d:(g + 1) * d]
        vc = vc_ref[0, :, g * d:(g + 1) * d]
        s_all = _dot_nt(_stack_heads(q, g), kc)
        p_sum = jnp.zeros((qbl, n_key), F32)
        ps = []
        for h in range(NSA_HPG):
            s = jnp.where(ok, s_all[h * qbl:(h + 1) * qbl] - _head_slope(g, h) * dist, NEG_INF)
            m = jnp.max(s, axis=-1, keepdims=True)
            e = jnp.where(ok, jnp.exp(s - m), 0.0)
            l = jnp.sum(e, axis=-1, keepdims=True)
            p = e * jnp.where(l > 0, 1.0 / l, 0.0)
            ps.append(p.astype(BF16))
            p_sum += p
        o_all = _dot(jnp.concatenate(ps, axis=0), vc)
        for h in range(NSA_HPG):
            col = (g * NSA_HPG + h) * d
            oc_ref[0, :, col:col + d] = o_all[h * qbl:(h + 1) * qbl]
        imps.append(_split_dot(p_sum, ov_ref[...]))
    imp_t = jnp.concatenate(imps, axis=1).T
    j = lax.broadcasted_iota(jnp.int32, (n_slc, qbl), 0)
    forced = (j == 0) | (j == qt) | (j == qt - 1)
    visible = j <= qt
    sels = []
    for g in range(NSA_KV_GROUPS):
        score = jnp.where(visible, imp_t[g * n_slc:(g + 1) * n_slc] + jnp.where(forced, SLC_FORCE_BONUS, 0.0), NEG_INF)
        sel = jnp.zeros((n_slc, qbl), F32)
        for _ in range(min(SLC_TOPN, n_slc)):
            m = jnp.max(score, axis=0, keepdims=True)
            first = jnp.min(jnp.where(score == m, j, n_slc), axis=0, keepdims=True)
            pick = j == first
            sel = jnp.where(pick, 1.0, sel)
            score = jnp.where(pick, -3e38, score)
        sels.append(jnp.where(visible, sel, 0.0))
    sel_all = jnp.concatenate(sels, axis=0).T
    sel_ref[0] = sel_all.astype(BF16)
    flag_ref[0, 0] = jnp.broadcast_to(jnp.max(sel_all, axis=0, keepdims=True), flag_ref.shape[2:])


def nsa_select(q3, k_cmp, v_cmp):
    b, s, _ = q3.shape
    n_qt = s // NSA_QBLOCK
    n_slc = s // SLC_LEN
    n_key = k_cmp.shape[1]
    c_lo = jnp.arange(n_key) * CMP_STRIDE
    j_lo = jnp.arange(n_slc) * SLC_LEN
    overlap = ((c_lo[:, None] < j_lo[None, :] + SLC_LEN) & (c_lo[:, None] + CMP_LEN > j_lo[None, :])).astype(BF16)
    tile = lambda c: pl.BlockSpec((1, NSA_QBLOCK, c), lambda i, t: (i, t, 0))
    per_b = pl.BlockSpec((1, n_key, KV_LANES), lambda i, t: (i, 0, 0))
    return pl.pallas_call(
        _nsa_select_kernel,
        grid=(b, n_qt),
        in_specs=[tile(NSA_DIM), per_b, per_b, pl.BlockSpec((n_key, n_slc), lambda i, t: (0, 0))],
        out_specs=[tile(NSA_DIM), tile(2 * n_slc), pl.BlockSpec((1, 1, 8, 2 * n_slc), lambda i, t: (i, t, 0, 0))],
        out_shape=[jax.ShapeDtypeStruct((b, s, NSA_DIM), F32), jax.ShapeDtypeStruct((b, s, 2 * n_slc), BF16),
                   jax.ShapeDtypeStruct((b, n_qt, 8, 2 * n_slc), F32)],
        compiler_params=pltpu.CompilerParams(dimension_semantics=("arbitrary", "arbitrary")),
        name="nsa_select",
    )(q3, k_cmp, v_cmp, overlap)


def _nsa_attend_kernel(count_ref, list_ref, q_ref, ks_ref, vs_ref, kw_ref, vw_ref, sel_ref, oc_ref, gate_ref, out_ref,
                       ksel_ref, vsel_ref):
    bi, qt = pl.program_id(0), pl.program_id(1)
    n_qt = pl.num_programs(1)
    qbl, d, ch = NSA_QBLOCK, NSA_HEAD_DIM, SEL_CHUNK
    n_slc = sel_ref.shape[2] // NSA_KV_GROUPS
    n_word = n_slc // IDS_PER_WORD
    q = q_ref[0]
    gates = jax.nn.sigmoid(gate_ref[0])
    lane = lax.broadcasted_iota(jnp.int32, (1, ch * SLC_LEN), 1)
    slot_of_lane = lane // SLC_LEN
    t_sel = qt * qbl + lax.broadcasted_iota(jnp.int32, (qbl, ch * SLC_LEN), 0)
    j_iota = lax.broadcasted_iota(jnp.int32, (n_slc, ch * SLC_LEN), 0)
    win_start = jnp.maximum(qt - WINDOW // qbl, 0) * qbl
    t_win = qt * qbl + lax.broadcasted_iota(jnp.int32, (qbl, N_WIN_KEYS), 0)
    dist_win = t_win - (win_start + lax.broadcasted_iota(jnp.int32, (qbl, N_WIN_KEYS), 1))
    ok_win = (dist_win >= 0) & (dist_win < WINDOW)
    dist_win = dist_win.astype(F32)

    groups = range(NSA_KV_GROUPS)
    lanes = [slice(g * d, (g + 1) * d) for g in groups]
    qs = [_stack_heads(q, g) for g in groups]
    tile_g = [(bi * n_qt + qt) * NSA_KV_GROUPS + g for g in groups]
    n_sel = [count_ref[tg] for tg in tile_g]
    sel_g = [sel_ref[0, :, g * n_slc:(g + 1) * n_slc] for g in groups]

    def sel_scores(c, g):
        word0 = tile_g[g] * n_word + c * (ch // IDS_PER_WORD)
        j_row = jnp.full((1, ch * SLC_LEN), -1, jnp.int32)
        for slot in range(ch):
            valid = c * ch + slot < n_sel[g]
            jb = (list_ref[word0 + slot // IDS_PER_WORD] >> (8 * (slot % IDS_PER_WORD))) & 0xFF
            rows = pl.ds(pl.multiple_of(jb * SLC_LEN, SLC_LEN), SLC_LEN)
            ksel_ref[g, slot * SLC_LEN:(slot + 1) * SLC_LEN, :] = ks_ref[0, rows, lanes[g]]
            vsel_ref[g, slot * SLC_LEN:(slot + 1) * SLC_LEN, :] = vs_ref[0, rows, lanes[g]]
            j_row = jnp.where(slot_of_lane == slot, jnp.where(valid, jb, -1), j_row)
        s_all = _dot_nt(qs[g], ksel_ref[g])
        chosen = _dot(sel_g[g], (j_iota == j_row).astype(BF16))
        dist = t_sel - (j_row * SLC_LEN + lane % SLC_LEN)
        ok = (chosen > 0.5) & (dist >= 0)
        return s_all, ok, dist.astype(F32)

    def sel_softmax(scores, carry, g):
        s_all, ok, dist = scores
        m_old, l_old, acc = carry
        ps, ms, ls = [], [], []
        for h in range(NSA_HPG):
            rows_h = slice(h * qbl, (h + 1) * qbl)
            s = jnp.where(ok, s_all[rows_h] - _head_slope(g, h) * dist, NEG_INF)
            m_new = jnp.maximum(m_old[rows_h], jnp.max(s, axis=-1, keepdims=True))
            p = jnp.exp(s - m_new)
            alpha = jnp.exp(m_old[rows_h] - m_new)
            ls.append(alpha * l_old[rows_h] + jnp.sum(p, axis=-1, keepdims=True))
            ms.append(m_new)
            ps.append(p.astype(BF16))
        m_new = jnp.concatenate(ms, axis=0)
        alpha = jnp.exp(m_old - m_new)
        acc = alpha * acc + _dot(jnp.concatenate(ps, axis=0), vsel_ref[g])
        return m_new, jnp.concatenate(ls, axis=0), acc

    win_rows = pl.ds(pl.multiple_of(win_start, qbl), N_WIN_KEYS)
    s_win = [_dot_nt(qs[g], kw_ref[0, win_rows, lanes[g]]) for g in groups]
    sc0 = [sel_scores(0, g) for g in groups]
    p_win, inv_win = [], []
    for g in groups:
        ps, inv = [], []
        for h in range(NSA_HPG):
            s = jnp.where(ok_win, s_win[g][h * qbl:(h + 1) * qbl] - _head_slope(g, h) * dist_win, NEG_INF)
            e = jnp.exp(s - jnp.max(s, axis=-1, keepdims=True))
            inv.append(1.0 / jnp.sum(e, axis=-1, keepdims=True))
            ps.append(e.astype(BF16))
        p_win.append(jnp.concatenate(ps, axis=0))
        inv_win.append(jnp.concatenate(inv, axis=0))
    init = (jnp.full((NSA_HPG * qbl, 1), M_INIT, F32), jnp.zeros((NSA_HPG * qbl, 1), F32),
            jnp.zeros((NSA_HPG * qbl, d), F32))
    carry = [sel_softmax(sc0[g], init, g) for g in groups]
    o_win = [_dot(p_win[g], vw_ref[0, win_rows, lanes[g]]) * inv_win[g] for g in groups]

    for g in groups:
        rest = lax.fori_loop(1, (n_sel[g] + ch - 1) // ch,
                             lambda c, cr, g=g: sel_softmax(sel_scores(c, g), cr, g), carry[g])
        o_sel = rest[2] * (1.0 / rest[1])
        for h in range(NSA_HPG):
            hh = g * NSA_HPG + h
            col = hh * d
            rows_h = slice(h * qbl, (h + 1) * qbl)
            out_ref[0, :, col:col + d] = (gates[:, hh:hh + 1] * oc_ref[0, :, col:col + d]
                                          + gates[:, NSA_HEADS + hh:NSA_HEADS + hh + 1] * o_sel[rows_h]
                                          + gates[:, 2 * NSA_HEADS + hh:2 * NSA_HEADS + hh + 1] * o_win[g][rows_h])


def nsa_attend(counts, lists, q3, ks, vs, kw, vw, sel, o_cmp, u3, col_gate):
    b, s, _ = q3.shape
    n_qt = s // NSA_QBLOCK
    assert col_gate % 128 == 0 and s >= N_WIN_KEYS and SEL_CHUNK % IDS_PER_WORD == 0
    gb = col_gate // 128
    tile = lambda c: pl.BlockSpec((1, NSA_QBLOCK, c), lambda i, t, counts, lists: (i, t, 0))
    per_b = pl.BlockSpec((1, s, KV_LANES), lambda i, t, counts, lists: (i, 0, 0))
    grid_spec = pltpu.PrefetchScalarGridSpec(
        num_scalar_prefetch=2,
        grid=(b, n_qt),
        in_specs=[tile(NSA_DIM), per_b, per_b, per_b, per_b, tile(sel.shape[2]), tile(NSA_DIM),
                  pl.BlockSpec((1, NSA_QBLOCK, 128), lambda i, t, counts, lists: (i, t, gb))],
        out_specs=tile(NSA_DIM),
        scratch_shapes=[pltpu.VMEM((NSA_KV_GROUPS, SEL_CHUNK * SLC_LEN, NSA_HEAD_DIM), BF16),
                        pltpu.VMEM((NSA_KV_GROUPS, SEL_CHUNK * SLC_LEN, NSA_HEAD_DIM), BF16)],
    )
    return pl.pallas_call(
        _nsa_attend_kernel,
        grid_spec=grid_spec,
        out_shape=jax.ShapeDtypeStruct((b, s, NSA_DIM), F32),
        compiler_params=pltpu.CompilerParams(dimension_semantics=("arbitrary", "arbitrary")),
        name="nsa_attend",
    )(counts, lists, q3, ks, vs, kw, vw, sel, o_cmp, u3)


def _pack_union_lists(flags, n_slc):
    assert n_slc <= 256 and n_slc % IDS_PER_WORD == 0
    b, n_qt = flags.shape[:2]
    f = flags[:, :, 0, :].reshape(b, n_qt, NSA_KV_GROUPS, n_slc) > 0.5
    fi = f.astype(jnp.int32)
    counts = jnp.sum(fi, axis=-1)
    ids = jnp.arange(n_slc, dtype=jnp.int32)
    pos = jnp.where(f, jnp.cumsum(fi, axis=-1) - 1, counts[..., None] + jnp.cumsum(1 - fi, axis=-1) - 1)
    order = jnp.sum(jnp.where(pos[..., :, None] == ids, ids[:, None], 0), axis=-2)
    order = order.reshape(b, n_qt, NSA_KV_GROUPS, n_slc // IDS_PER_WORD, IDS_PER_WORD)
    words = jnp.sum(order << (8 * jnp.arange(IDS_PER_WORD, dtype=jnp.int32)), axis=-1, dtype=jnp.int32)
    return counts.reshape(-1), words.reshape(-1)


def nsa_mixer_pallas(u3, qk_gain, cmp_pe, cmp_w1, cmp_w2, col_q, col_kv, col_gate):
    b, s, d_in = u3.shape
    q, ks, vs, kw, vw = nsa_prep(u3.reshape(b * s, d_in), qk_gain, col_q, col_kv)
    k_cmp, v_cmp = nsa_compress(u3, qk_gain, cmp_pe, cmp_w1, cmp_w2, col_kv)
    q3 = q.reshape(b, s, NSA_DIM)
    r3 = lambda a: a.reshape(b, s, KV_LANES)
    o_cmp, sel, flags = nsa_select(q3, k_cmp, v_cmp)
    counts, lists = _pack_union_lists(flags, s // SLC_LEN)
    return nsa_attend(counts, lists, q3, r3(ks), r3(vs), r3(kw), r3(vw), sel, o_cmp, u3, col_gate)


GDN_TILE = 256
GDN_HALO = 8


def _dot3(a, b):
    ah = a.astype(BF16)
    bh = b.astype(BF16)
    al = (a - ah.astype(F32)).astype(BF16)
    bl = (b - bh.astype(F32)).astype(BF16)
    return _dot(ah, bh) + _dot(ah, bl) + _dot(al, bh)


def _dot_tn(a, b):
    return lax.dot_general(a, b, (((0,), (0,)), ((), ())), preferred_element_type=F32)


def _softplus(x):
    return jnp.maximum(x, 0.0) + jnp.log(1.0 + jnp.exp(-jnp.abs(x)))


def _l2_norm(x):
    return x * lax.rsqrt(jnp.sum(x * x, axis=-1, keepdims=True) + RMS_EPS)


def _gdn_kernel(qkv_ref, z_ref, small_ref, cw_ref, coef_ref, gain_ref, y_ref, xe_ref, state_ref):
    tt = pl.program_id(1)
    tile, c, hd = GDN_TILE, GDN_CHUNK, GDN_HEAD_DIM

    @pl.when(tt == 0)
    def _():
        xe_ref[0:GDN_HALO, :] = jnp.zeros((GDN_HALO, xe_ref.shape[1]), F32)
        state_ref[...] = jnp.zeros_like(state_ref)

    @pl.when(tt > 0)
    def _():
        xe_ref[0:GDN_HALO, :] = xe_ref[tile:tile + GDN_HALO, :]

    xe_ref[GDN_HALO:, :] = qkv_ref[0]
    conv = jnp.zeros((tile, xe_ref.shape[1]), F32)
    for j in range(GDN_CONV):
        conv += cw_ref[j:j + 1, :] * xe_ref[pl.ds(GDN_HALO - (GDN_CONV - 1) + j, tile), :]
    act = conv * jax.nn.sigmoid(conv)

    small = small_ref[0]
    beta_all = jax.nn.sigmoid(small)
    g_all = coef_ref[0:1, :] * _softplus(small + coef_ref[1:2, :])
    row = lax.broadcasted_iota(jnp.int32, (c, SMALL_LANES), 0)
    ri = lax.broadcasted_iota(jnp.int32, (c, c), 0)
    ci = lax.broadcasted_iota(jnp.int32, (c, c), 1)
    lower = ri >= ci
    strict = ri > ci

    n_chunk = tile // c
    pairs = [(n, h) for n in range(n_chunk) for h in range(GDN_HEADS)]
    gcs, gc_ts = [], []
    for n in range(n_chunk):
        gc = g_all[n * c:(n + 1) * c]
        shift = 1
        while shift < c:
            gc = gc + jnp.where(row >= shift, pltpu.roll(gc, shift, 0), 0.0)
            shift *= 2
        gcs.append(gc)
        gc_ts.append(gc.T)
    pre = []
    for n, h in pairs:
        rows = slice(n * c, (n + 1) * c)
        q = _l2_norm(act[rows, h * hd:(h + 1) * hd]) * hd ** -0.5
        k = _l2_norm(act[rows, GDN_DIM + h * hd:GDN_DIM + (h + 1) * hd])
        v = act[rows, 2 * GDN_DIM + h * hd:2 * GDN_DIM + (h + 1) * hd]
        beta = beta_all[rows, LANE_BETA + h:LANE_BETA + h + 1]
        gcol = gcs[n][:, LANE_DECAY + h:LANE_DECAY + h + 1]
        grow = gc_ts[n][LANE_DECAY + h:LANE_DECAY + h + 1, :]
        g_last = gcol[c - 1:c, :]
        decay = jnp.where(lower, jnp.exp(jnp.where(lower, gcol - grow, 0.0)), 0.0)
        kb = k * beta
        kh = k.astype(BF16)
        pre.append(dict(
            lmat=jnp.where(strict, _dot_nt(kb.astype(BF16), kh) * decay, 0.0),
            rhs=jnp.concatenate([v * beta, kb * jnp.exp(gcol)], axis=1),
            attn=(_dot_nt(q.astype(BF16), kh) * decay).astype(BF16),
            q_dec=(q * jnp.exp(gcol)).astype(BF16),
            k_dec=(k * jnp.exp(g_last - gcol)).astype(BF16),
            d_last=jnp.exp(g_last)))
    power = [p['lmat'].astype(BF16) for p in pre]
    rhs = [p['rhs'] - _dot(lm, p['rhs'].astype(BF16)) for lm, p in zip(power, pre)]
    for _ in range(5):
        power = [_dot(lm, lm).astype(BF16) for lm in power]
        rhs = [r + _dot(lm, r.astype(BF16)) for lm, r in zip(power, rhs)]

    state = [state_ref[h] for h in range(GDN_HEADS)]
    for n in range(n_chunk):
        rows = slice(n * c, (n + 1) * c)
        sb = [s.astype(BF16) for s in state]
        ps = [pre[n * GDN_HEADS + h] for h in range(GDN_HEADS)]
        rs = [rhs[n * GDN_HEADS + h] for h in range(GDN_HEADS)]
        v_new = [(r[:, :hd] - _dot(r[:, hd:].astype(BF16), s)).astype(BF16) for r, s in zip(rs, sb)]
        outs = [_dot(p['q_dec'], s) + _dot(p['attn'], vn) for p, s, vn in zip(ps, sb, v_new)]
        state = [s * p['d_last'] + _dot_tn(p['k_dec'], vn) for p, s, vn in zip(ps, state, v_new)]
        for h, o in enumerate(outs):
            o = o * lax.rsqrt(jnp.mean(o * o, axis=-1, keepdims=True) + RMS_EPS) * gain_ref[...]
            zz = z_ref[0, rows, h * hd:(h + 1) * hd]
            y_ref[0, rows, h * hd:(h + 1) * hd] = o * (zz * jax.nn.sigmoid(zz))
    for h in range(GDN_HEADS):
        state_ref[h] = state[h]


def gdn_mixer(u3, col_qkv, col_z, col_small, conv_w, a_log, dt_bias, out_gain):
    b, s, _ = u3.shape
    lane = jnp.arange(SMALL_LANES)
    in_decay = (lane >= LANE_DECAY) & (lane < LANE_DECAY + GDN_HEADS)
    idx = jnp.clip(lane - LANE_DECAY, 0, GDN_HEADS - 1)
    coef = jnp.stack([jnp.where(in_decay, -jnp.exp(a_log)[idx], 0.0), jnp.where(in_decay, dt_bias[idx], 0.0)])
    qkv_w = 3 * GDN_DIM
    assert col_qkv % qkv_w == 0 and col_z % GDN_DIM == 0 and col_small % SMALL_LANES == 0
    full = lambda r, cc: pl.BlockSpec((r, cc), lambda i, t: (0, 0))
    return pl.pallas_call(
        _gdn_kernel,
        grid=(b, s // GDN_TILE),
        in_specs=[pl.BlockSpec((1, GDN_TILE, qkv_w), lambda i, t: (i, t, col_qkv // qkv_w)),
                  pl.BlockSpec((1, GDN_TILE, GDN_DIM), lambda i, t: (i, t, col_z // GDN_DIM)),
                  pl.BlockSpec((1, GDN_TILE, SMALL_LANES), lambda i, t: (i, t, col_small // SMALL_LANES)),
                  full(GDN_CONV, qkv_w), full(2, SMALL_LANES), full(1, GDN_HEAD_DIM)],
        out_specs=pl.BlockSpec((1, GDN_TILE, GDN_DIM), lambda i, t: (i, t, 0)),
        out_shape=jax.ShapeDtypeStruct((b, s, GDN_DIM), F32),
        scratch_shapes=[pltpu.VMEM((GDN_TILE + GDN_HALO, qkv_w), F32), pltpu.VMEM((GDN_HEADS, GDN_HEAD_DIM, GDN_HEAD_DIM), F32)],
        compiler_params=pltpu.CompilerParams(dimension_semantics=("arbitrary", "arbitrary")),
        name="gdn_mixer",
    )(u3, u3, u3, conv_w, coef, out_gain.reshape(1, GDN_HEAD_DIM))


CONV_TILE = 512


def _short_conv_kernel(u_ref, cw_ref, y_ref, xe_ref):
    tile, cd = CONV_TILE, CONV_DIM

    @pl.when(pl.program_id(1) == 0)
    def _():
        xe_ref[0:GDN_HALO, :] = jnp.zeros((GDN_HALO, cd), F32)

    @pl.when(pl.program_id(1) > 0)
    def _():
        xe_ref[0:GDN_HALO, :] = xe_ref[tile:tile + GDN_HALO, :]

    xe_ref[GDN_HALO:, :] = u_ref[0, :, cd:2 * cd] * u_ref[0, :, 2 * cd:3 * cd]
    conv = jnp.zeros((tile, cd), F32)
    for j in range(CONV_WIDTH):
        conv += cw_ref[j:j + 1, :] * xe_ref[pl.ds(GDN_HALO - (CONV_WIDTH - 1) + j, tile), :]
    y_ref[0] = u_ref[0, :, 0:cd] * conv


def short_conv(u3, conv_w):
    b, s, _ = u3.shape
    return pl.pallas_call(
        _short_conv_kernel,
        grid=(b, s // CONV_TILE),
        in_specs=[pl.BlockSpec((1, CONV_TILE, 3 * CONV_DIM), lambda i, t: (i, t, 0)),
                  pl.BlockSpec((CONV_WIDTH, CONV_DIM), lambda i, t: (0, 0))],
        out_specs=pl.BlockSpec((1, CONV_TILE, CONV_DIM), lambda i, t: (i, t, 0)),
        out_shape=jax.ShapeDtypeStruct((b, s, CONV_DIM), F32),
        scratch_shapes=[pltpu.VMEM((CONV_TILE + GDN_HALO, CONV_DIM), F32)],
        compiler_params=pltpu.CompilerParams(dimension_semantics=("arbitrary", "arbitrary")),
        name="short_conv",
    )(u3, conv_w)


def _merge_kernel(x_ref, g_ref, ya_ref, yn_ref, yg_ref, wb_ref, wg_ref, bg_ref, wo_ref, o_ref):
    x = x_ref[...]
    d = x.shape[1]
    h = (x * lax.rsqrt(jnp.mean(x * x, axis=-1, keepdims=True) + RMS_EPS) * g_ref[...]).astype(BF16)
    merged = jnp.zeros(x.shape, F32)
    for r, y_ref in enumerate((ya_ref, yn_ref, yg_ref)):
        gate = jax.nn.sigmoid(_dot(h, wg_ref[:, r * d:(r + 1) * d]) + bg_ref[:, r * d:(r + 1) * d])
        merged += gate * _dot(y_ref[...].astype(BF16), wb_ref[r])
    o_ref[...] = x + _dot(merged.astype(BF16), wo_ref[...])


def merge_branches(x2d, gain, y_a, y_n, y_g, w_branch, w_gate, b_gate, w_out, tm=512):
    t, d = x2d.shape
    row = lambda c: pl.BlockSpec((tm, c), lambda i: (i, 0))
    full = lambda *shape: pl.BlockSpec(shape, lambda i: (0,) * len(shape))
    return pl.pallas_call(
        _merge_kernel,
        grid=(t // tm,),
        in_specs=[row(d), full(1, d), row(BRANCH_DIM), row(BRANCH_DIM), row(BRANCH_DIM),
                  full(N_BRANCH, BRANCH_DIM, d), full(d, N_BRANCH * d), full(1, N_BRANCH * d), full(d, d)],
        out_specs=row(d),
        out_shape=jax.ShapeDtypeStruct((t, d), F32),
        compiler_params=pltpu.CompilerParams(dimension_semantics=("arbitrary",), vmem_limit_bytes=56 * 1024 * 1024),
        name="merge_branches",
    )(x2d, gain.reshape(1, d), y_a, y_n, y_g, w_branch.astype(BF16), w_gate.astype(BF16), b_gate.reshape(1, -1), w_out.astype(BF16))


def hybrid_mixer(x2d, b, s, norm_gain, w_in, conv_a_w, nsa_qk_gain, cmp_pe, cmp_w1, cmp_w2, gdn_conv_w, gdn_a_log, gdn_dt_bias, gdn_out_gain, w_branch, w_gate, b_gate, w_out):
    t = b * s
    u3 = norm_proj(x2d, norm_gain, permute_in_proj(w_in)).reshape(b, s, D_U)
    y_a = short_conv(u3, conv_a_w)
    y_n = nsa_mixer_pallas(u3, nsa_qk_gain, cmp_pe, cmp_w1, cmp_w2, COL_NSA_Q, COL_NSA_KV, COL_SMALL)
    y_g = gdn_mixer(u3, COL_GDN_QKV, COL_GDN_Z, COL_SMALL, gdn_conv_w, gdn_a_log, gdn_dt_bias, gdn_out_gain)
    return merge_branches(x2d, norm_gain, y_a.reshape(t, BRANCH_DIM), y_n.reshape(t, BRANCH_DIM), y_g.reshape(t, BRANCH_DIM), w_branch, w_gate, b_gate, w_out)


MOE_ROWS = 512
ROUTE_LANES = 128
N_ROUTER = MOE_GROUPS + N_EXPERTS
ROW_DMA_UNROLL = 8


def _moe_route_kernel(x_ref, g_ref, wr_ref, br_ref, tri_ref, h_ref, route_ref, cnt_ref, run_ref):
    @pl.when(pl.program_id(0) == 0)
    def _():
        run_ref[...] = jnp.zeros_like(run_ref)

    x = x_ref[...]
    h = x * lax.rsqrt(jnp.mean(x * x, axis=-1, keepdims=True) + RMS_EPS) * g_ref[...]
    bits = lax.bitcast_convert_type(h.astype(BF16).astype(F32), jnp.uint32)
    half = h.shape[1] // 2
    h_ref[...] = (bits[:, half:] & jnp.uint32(0xFFFF0000)) | (bits[:, :half] >> 16)
    logits = _dot(h.astype(BF16), wr_ref[...]) + br_ref[...]
    lane = lax.broadcasted_iota(jnp.int32, logits.shape, 1)
    first_of = lambda hit: jnp.min(jnp.where(hit, lane, ROUTE_LANES), axis=-1, keepdims=True)
    is_grp = lane < MOE_GROUPS
    lg = jnp.where(is_grp, logits, NEG_INF)
    m_g = jnp.max(lg, axis=-1, keepdims=True)
    grp = first_of(lg == m_g)
    p_grp = 1.0 / jnp.sum(jnp.where(is_grp, jnp.exp(lg - m_g), 0.0), axis=-1, keepdims=True)
    lo = MOE_GROUPS + grp * EXPERTS_PER_GROUP
    le = jnp.where((lane >= lo) & (lane < lo + EXPERTS_PER_GROUP), logits, NEG_INF)
    m1 = jnp.max(le, axis=-1, keepdims=True)
    i1 = first_of(le == m1)
    le2 = jnp.where(lane == i1, NEG_INF, le)
    m2 = jnp.max(le2, axis=-1, keepdims=True)
    i2 = first_of(le2 == m2)
    r = jnp.exp(m2 - m1)
    g1 = p_grp / (1.0 + r)
    g2 = p_grp * r / (1.0 + r)
    e1 = i1 - MOE_GROUPS
    e2 = i2 - MOE_GROUPS
    hit1 = lane == e1
    hit2 = lane == e2
    onehot = (hit1 | hit2).astype(BF16)
    before = _dot(tri_ref[...], onehot) + run_ref[...]
    r1 = jnp.sum(jnp.where(hit1, before, 0.0), axis=-1, keepdims=True)
    r2 = jnp.sum(jnp.where(hit2, before, 0.0), axis=-1, keepdims=True)
    run_ref[...] += jnp.sum(onehot.astype(F32), axis=0, keepdims=True)
    rec = jnp.zeros(logits.shape, F32)
    for k, v in enumerate((e1.astype(F32), e2.astype(F32), r1, r2, g1, g2)):
        rec = jnp.where(lane == k, v, rec)
    route_ref[...] = rec
    cnt_ref[...] = jnp.broadcast_to(run_ref[...], cnt_ref.shape)


def moe_route(x2d, gain, w_rg, b_rg, w_re, b_re, tm=512):
    t, d = x2d.shape
    pad = ROUTE_LANES - N_ROUTER
    wr = jnp.pad(jnp.concatenate([w_rg, w_re], axis=1), ((0, 0), (0, pad))).astype(BF16)
    br = jnp.pad(jnp.concatenate([b_rg, b_re]), (0, pad)).reshape(1, ROUTE_LANES)
    tri = (jnp.arange(tm)[:, None] > jnp.arange(tm)[None, :]).astype(BF16)
    full = lambda r, c: pl.BlockSpec((r, c), lambda i: (0, 0))
    return pl.pallas_call(
        _moe_route_kernel,
        grid=(t // tm,),
        in_specs=[pl.BlockSpec((tm, d), lambda i: (i, 0)), full(1, d), full(d, ROUTE_LANES), full(1, ROUTE_LANES), full(tm, tm)],
        out_specs=[pl.BlockSpec((tm, d // 2), lambda i: (i, 0)), pl.BlockSpec((tm, ROUTE_LANES), lambda i: (i, 0)), full(8, ROUTE_LANES)],
        out_shape=[jax.ShapeDtypeStruct((t, d // 2), jnp.uint32), jax.ShapeDtypeStruct((t, ROUTE_LANES), F32), jax.ShapeDtypeStruct((8, ROUTE_LANES), F32)],
        scratch_shapes=[pltpu.VMEM((1, ROUTE_LANES), F32)],
        compiler_params=pltpu.CompilerParams(dimension_semantics=("arbitrary",)),
        name="moe_route",
    )(x2d, gain.reshape(1, d), wr, br, tri)


def _row_copy(src_ref, src_row, dst_ref, dst_row, sem):
    return pltpu.make_async_copy(src_ref.at[src_row], dst_ref.at[dst_row], sem)


def _moe_dispatch_kernel(dest_ref, h_ref, buf_in_ref, buf_ref, sem):
    del buf_in_ref
    tm = h_ref.shape[0]
    base = pl.program_id(0) * tm

    def send(r, carry):
        for k in range(TOPK_IN_GROUP):
            _row_copy(h_ref, r, buf_ref, dest_ref[(base + r) * TOPK_IN_GROUP + k], sem).start(priority=k)
        return carry

    lax.fori_loop(0, tm, send, 0, unroll=ROW_DMA_UNROLL)
    for k in range(TOPK_IN_GROUP):
        pltpu.make_async_copy(h_ref, buf_ref.at[pl.ds(0, tm)], sem).wait()


def moe_dispatch(dest, h3, n_rows, tm=512):
    t, _, d = h3.shape
    grid_spec = pltpu.PrefetchScalarGridSpec(
        num_scalar_prefetch=1, grid=(t // tm,),
        in_specs=[pl.BlockSpec((tm, 1, d), lambda i, dest: (i, 0, 0)), pl.BlockSpec(memory_space=pl.ANY)],
        out_specs=pl.BlockSpec(memory_space=pl.ANY),
        scratch_shapes=[pltpu.SemaphoreType.DMA(())],
    )
    return pl.pallas_call(
        _moe_dispatch_kernel, grid_spec=grid_spec,
        out_shape=jax.ShapeDtypeStruct((n_rows, 1, d), h3.dtype),
        input_output_aliases={2: 0},
        compiler_params=pltpu.CompilerParams(dimension_semantics=("arbitrary",), has_side_effects=True),
        name="moe_dispatch",
    )(dest, h3, jnp.zeros((n_rows, 1, d), h3.dtype))


def _moe_ffn_kernel(blk_e_ref, n_used_ref, x_ref, wg_ref, wu_ref, wd_ref, y_ref):
    del blk_e_ref
    used = pl.program_id(0) < n_used_ref[0]

    @pl.when(used)
    def _():
        w = x_ref[...]
        lo = lax.bitcast_convert_type(w << 16, F32)
        hi = lax.bitcast_convert_type(w & jnp.uint32(0xFFFF0000), F32)
        xb = jnp.concatenate([lo, hi], axis=1).astype(BF16)
        a = _dot(xb, wg_ref[0])
        mid = a * jax.nn.sigmoid(a) * _dot(xb, wu_ref[0])
        y_ref[...] = _dot(mid.astype(BF16), wd_ref[0])

    @pl.when(jnp.logical_not(used))
    def _():
        y_ref[...] = jnp.zeros_like(y_ref)


def moe_ffn(blk_expert, n_used, buf2d, w_eg, w_eu, w_ed):
    n_rows, packed_w = buf2d.shape
    d, ff = w_eg.shape[1:]
    w_in_spec = pl.BlockSpec((1, d, ff), lambda b, blk_e, n_used: (blk_e[b], 0, 0))
    grid_spec = pltpu.PrefetchScalarGridSpec(
        num_scalar_prefetch=2, grid=(n_rows // MOE_ROWS,),
        in_specs=[pl.BlockSpec((MOE_ROWS, packed_w), lambda b, blk_e, n_used: (b, 0)), w_in_spec, w_in_spec,
                  pl.BlockSpec((1, ff, d), lambda b, blk_e, n_used: (blk_e[b], 0, 0))],
        out_specs=pl.BlockSpec((MOE_ROWS, d), lambda b, blk_e, n_used: (b, 0)),
    )
    return pl.pallas_call(
        _moe_ffn_kernel, grid_spec=grid_spec,
        out_shape=jax.ShapeDtypeStruct((n_rows, d), F32),
        compiler_params=pltpu.CompilerParams(dimension_semantics=("arbitrary",)),
        name="moe_ffn",
    )(blk_expert, n_used, buf2d, w_eg.astype(BF16), w_eu.astype(BF16), w_ed.astype(BF16))


def _moe_combine_kernel(dest_ref, y_ref, gate_ref, out_ref, ya_ref, yb_ref, sem):
    tm = out_ref.shape[0]
    base = pl.program_id(0) * tm

    def fetch(r, carry):
        _row_copy(y_ref, dest_ref[(base + r) * TOPK_IN_GROUP], ya_ref, r, sem).start(priority=0)
        _row_copy(y_ref, dest_ref[(base + r) * TOPK_IN_GROUP + 1], yb_ref, r, sem).start(priority=1)
        return carry

    lax.fori_loop(0, tm, fetch, 0, unroll=ROW_DMA_UNROLL)
    pltpu.make_async_copy(y_ref.at[pl.ds(0, tm)], ya_ref, sem).wait()
    pltpu.make_async_copy(y_ref.at[pl.ds(0, tm)], yb_ref, sem).wait()
    gates = gate_ref[...]
    out_ref[...] = gates[:, :, 0:1] * ya_ref[...] + gates[:, :, 1:2] * yb_ref[...]


def moe_combine(dest, y3, gates3, tm=512):
    t = gates3.shape[0]
    d = y3.shape[2]
    grid_spec = pltpu.PrefetchScalarGridSpec(
        num_scalar_prefetch=1, grid=(t // tm,),
        in_specs=[pl.BlockSpec(memory_space=pl.ANY), pl.BlockSpec((tm, 1, TOPK_IN_GROUP), lambda i, dest: (i, 0, 0))],
        out_specs=pl.BlockSpec((tm, 1, d), lambda i, dest: (i, 0, 0)),
        scratch_shapes=[pltpu.VMEM((tm, 1, d), F32), pltpu.VMEM((tm, 1, d), F32), pltpu.SemaphoreType.DMA(())],
    )
    return pl.pallas_call(
        _moe_combine_kernel, grid_spec=grid_spec,
        out_shape=jax.ShapeDtypeStruct((t, 1, d), F32),
        compiler_params=pltpu.CompilerParams(dimension_semantics=("arbitrary",)),
        name="moe_combine",
    )(dest, y3, gates3)


def hier_moe_pallas(x2d, gain, w_rg, b_rg, w_re, b_re, w_eg, w_eu, w_ed):
    t, d = x2d.shape
    h, rec, cnt = moe_route(x2d, gain, w_rg, b_rg, w_re, b_re)
    counts = cnt[0, :N_EXPERTS].astype(jnp.int32)
    n_blk = (counts + MOE_ROWS - 1) // MOE_ROWS
    blk_end = jnp.cumsum(n_blk)
    pad_start = (blk_end - n_blk) * MOE_ROWS
    experts = rec[:, 0:2].astype(jnp.int32)
    dest = (pad_start[experts] + rec[:, 2:4].astype(jnp.int32)).reshape(-1)
    total_blk = t * TOPK_IN_GROUP // MOE_ROWS + N_EXPERTS
    blk_expert = jnp.minimum(jnp.searchsorted(blk_end, jnp.arange(total_blk), side='right'), N_EXPERTS - 1).astype(jnp.int32)
    buf = moe_dispatch(dest, h.reshape(t, 1, d // 2), total_blk * MOE_ROWS)
    y = moe_ffn(blk_expert, blk_end[-1:].astype(jnp.int32), buf.reshape(total_blk * MOE_ROWS, d // 2), w_eg, w_eu, w_ed)
    out = moe_combine(dest, y.reshape(total_blk * MOE_ROWS, 1, d), rec[:, 4:6].reshape(t, 1, TOPK_IN_GROUP))
    return out.reshape(t, d)


def kernel(x, norm_mix, w_in, conv_a_w, nsa_qk_gain, cmp_pe, cmp_w1, cmp_w2, gdn_conv_w, gdn_a_log, gdn_dt_bias, gdn_out_gain, w_branch, w_gate, b_gate, w_out, norm_ffn, w_router_group, b_router_group, w_router_expert, b_router_expert, w_expert_gate, w_expert_up, w_expert_down):
    b, s, dm = x.shape
    x = x.reshape(b * s, dm)
    for l in range(DEPTH):
        x = hybrid_mixer(x, b, s, norm_mix[l], w_in[l], conv_a_w[l], nsa_qk_gain[l], cmp_pe[l], cmp_w1[l], cmp_w2[l], gdn_conv_w[l], gdn_a_log[l], gdn_dt_bias[l], gdn_out_gain[l], w_branch[l], w_gate[l], b_gate[l], w_out[l])
        x = x + hier_moe_pallas(x, norm_ffn[l], w_router_group[l], b_router_group[l], w_router_expert[l], b_router_expert[l], w_expert_gate[l], w_expert_up[l], w_expert_down[l])
    return x.reshape(b, s, dm)
```

```python
import functools
import math

import jax
import jax.numpy as jnp
from jax import lax
from jax.experimental import pallas as pl
from jax.experimental.pallas import tpu as pltpu

D_MODEL = 1024
DEPTH = 4
CONV_DIM = 512
CONV_WIDTH = 3
NSA_HEADS = 8
NSA_KV_GROUPS = 2
NSA_HPG = NSA_HEADS // NSA_KV_GROUPS
NSA_HEAD_DIM = 64
NSA_DIM = NSA_HEADS * NSA_HEAD_DIM
CMP_LEN = 32
CMP_STRIDE = 16
CMP_HIDDEN = 256
SLC_LEN = 64
SLC_TOPN = 8
SLC_FORCE_BONUS = 1e6
WINDOW = 512
NSA_QBLOCK = 64
GDN_HEADS = 4
GDN_HEAD_DIM = 128
GDN_DIM = GDN_HEADS * GDN_HEAD_DIM
GDN_CONV = 4
GDN_CHUNK = 64
N_BRANCH = 3
BRANCH_DIM = 512
IN_SPLITS = (3 * CONV_DIM, NSA_DIM, 6 * NSA_KV_GROUPS * NSA_HEAD_DIM, 3 * NSA_HEADS, 3 * GDN_DIM, GDN_DIM, GDN_HEADS, GDN_HEADS)
D_IN = sum(IN_SPLITS)
COL_CONV = 0
COL_GDN_QKV = 3 * CONV_DIM
COL_NSA_Q = COL_GDN_QKV + 3 * GDN_DIM
COL_GDN_Z = COL_NSA_Q + NSA_DIM
COL_NSA_KV = COL_GDN_Z + GDN_DIM
COL_SMALL = COL_NSA_KV + 6 * NSA_KV_GROUPS * NSA_HEAD_DIM
SMALL_LANES = 128
D_U = COL_SMALL + SMALL_LANES
LANE_BETA = 3 * NSA_HEADS
LANE_DECAY = LANE_BETA + GDN_HEADS


def permute_in_proj(w_in):
    conv, nq, nkv, ng, gqkv, gz, gb, ga = split_last(w_in, IN_SPLITS)
    pad = jnp.zeros((w_in.shape[0], SMALL_LANES - LANE_DECAY - GDN_HEADS), w_in.dtype)
    return jnp.concatenate([conv, gqkv, nq, gz, nkv, ng, gb, ga, pad], axis=1)


def split_last(u, sizes):
    out, start = [], 0
    for n in sizes:
        out.append(u[..., start:start + n])
        start += n
    return out
MOE_GROUPS = 4
EXPERTS_PER_GROUP = 8
N_EXPERTS = MOE_GROUPS * EXPERTS_PER_GROUP
TOPK_IN_GROUP = 2
EXPERT_FF = 512
MOE_BLOCK = 256
RMS_EPS = 1e-6
NEG_INF = -1e30

F32 = jnp.float32
BF16 = jnp.bfloat16


def _norm_proj_kernel(x_ref, g_ref, w_ref, o_ref, h_ref):
    @pl.when(pl.program_id(1) == 0)
    def _():
        x = x_ref[...]
        y = x * lax.rsqrt(jnp.mean(x * x, axis=-1, keepdims=True) + RMS_EPS)
        h_ref[...] = (y * g_ref[...]).astype(BF16)

    o_ref[...] = jnp.dot(h_ref[...], w_ref[...], preferred_element_type=F32)


def norm_proj(x2d, gain, w, tm=1024, tn=D_U // 3):
    t, d = x2d.shape
    n = w.shape[1]
    n_blk = pl.cdiv(n, tn)
    wb = jnp.pad(w.astype(BF16), ((0, 0), (0, n_blk * tn - n)))
    return pl.pallas_call(
        _norm_proj_kernel,
        grid=(t // tm, n_blk),
        in_specs=[
            pl.BlockSpec((tm, d), lambda i, j: (i, 0)),
            pl.BlockSpec((1, d), lambda i, j: (0, 0)),
            pl.BlockSpec((d, tn), lambda i, j: (0, j)),
        ],
        out_specs=pl.BlockSpec((tm, tn), lambda i, j: (i, j)),
        out_shape=jax.ShapeDtypeStruct((t, n), F32),
        scratch_shapes=[pltpu.VMEM((tm, d), BF16)],
        compiler_params=pltpu.CompilerParams(dimension_semantics=("arbitrary", "arbitrary")),
        name="norm_proj",
    )(x2d, gain.reshape(1, d), wb)


KV_LANES = NSA_KV_GROUPS * NSA_HEAD_DIM
N_WIN_KEYS = WINDOW + NSA_QBLOCK
SEL_CHUNK = 8
IDS_PER_WORD = 4
M_INIT = -1e29


def _dot(a, b):
    return jnp.dot(a, b, preferred_element_type=F32)


def _dot_nt(a, b):
    return lax.dot_general(a, b, (((1,), (1,)), ((), ())), preferred_element_type=F32)


def _split_dot(x, m):
    hi = x.astype(BF16)
    lo = (x - hi.astype(F32)).astype(BF16)
    return _dot(hi, m) + _dot(lo, m)


def _head_slope(g, h):
    return 2.0 ** (-8.0 * (g * NSA_HPG + h + 1) / NSA_HEADS)


def _group_rms(x, bd, gain):
    ms = _split_dot(x * x, bd)
    return x * lax.rsqrt(ms + RMS_EPS) * gain


def _nsa_prep_kernel(uq_ref, uks_ref, uvs_ref, ukw_ref, uvw_ref, gq_ref, gks_ref, gkw_ref, bdq_ref, bdk_ref,
                     q_ref, ks_ref, vs_ref, kw_ref, vw_ref):
    scale = NSA_HEAD_DIM ** -0.5
    q_ref[...] = (_group_rms(uq_ref[...], bdq_ref[...], gq_ref[...]) * scale).astype(BF16)
    ks_ref[...] = _group_rms(uks_ref[...], bdk_ref[...], gks_ref[...]).astype(BF16)
    kw_ref[...] = _group_rms(ukw_ref[...], bdk_ref[...], gkw_ref[...]).astype(BF16)
    vs_ref[...] = uvs_ref[...].astype(BF16)
    vw_ref[...] = uvw_ref[...].astype(BF16)


def _block_diag_mean(n):
    i = jnp.arange(n) // NSA_HEAD_DIM
    return ((i[:, None] == i[None, :]).astype(F32) / NSA_HEAD_DIM).astype(BF16)


def nsa_prep(u, qk_gain, col_q, col_kv, tm=512):
    t = u.shape[0]
    assert t % tm == 0 and col_q % NSA_DIM == 0 and col_kv % KV_LANES == 0
    qb = col_q // NSA_DIM
    kb = col_kv // KV_LANES
    kv_spec = lambda j: pl.BlockSpec((tm, KV_LANES), lambda i, j=j: (i, kb + j))
    full = lambda r, c: pl.BlockSpec((r, c), lambda i: (0, 0))
    row = lambda c: pl.BlockSpec((tm, c), lambda i: (i, 0))
    gq = jnp.tile(qk_gain[0], NSA_HEADS).reshape(1, NSA_DIM)
    gks = jnp.tile(qk_gain[2], NSA_KV_GROUPS).reshape(1, KV_LANES)
    gkw = jnp.tile(qk_gain[3], NSA_KV_GROUPS).reshape(1, KV_LANES)
    return pl.pallas_call(
        _nsa_prep_kernel,
        grid=(t // tm,),
        in_specs=[pl.BlockSpec((tm, NSA_DIM), lambda i: (i, qb)), kv_spec(2), kv_spec(3), kv_spec(4), kv_spec(5),
                  full(1, NSA_DIM), full(1, KV_LANES), full(1, KV_LANES), full(NSA_DIM, NSA_DIM), full(KV_LANES, KV_LANES)],
        out_specs=[row(NSA_DIM), row(KV_LANES), row(KV_LANES), row(KV_LANES), row(KV_LANES)],
        out_shape=[jax.ShapeDtypeStruct((t, NSA_DIM), BF16)] + [jax.ShapeDtypeStruct((t, KV_LANES), BF16)] * 4,
        compiler_params=pltpu.CompilerParams(dimension_semantics=("arbitrary",)),
        name="nsa_prep",
    )(u, u, u, u, u, gq, gks, gkw, _block_diag_mean(NSA_DIM), _block_diag_mean(KV_LANES))


def _gelu_tanh(x):
    return 0.5 * x * (1.0 + jnp.tanh(0.7978845608028654 * (x + 0.044715 * x * x * x)))


def _nsa_compress_kernel(uk_ref, uv_ref, pe_ref, w1_ref, w2_ref, gk_ref, bd_ref, kc_ref, vc_ref):
    n_row = uk_ref.shape[1] // CMP_STRIDE
    d = NSA_HEAD_DIM
    for kv, (src, dst) in enumerate(((uk_ref, kc_ref), (uv_ref, vc_ref))):
        top = [jnp.zeros((n_row, CMP_HIDDEN), F32) for _ in range(NSA_KV_GROUPS)]
        bot = [jnp.zeros((n_row, CMP_HIDDEN), F32) for _ in range(NSA_KV_GROUPS)]
        for l in range(CMP_STRIDE):
            x2 = src[0, pl.ds(l, n_row, stride=CMP_STRIDE), :]
            l2 = l + CMP_STRIDE
            for g in range(NSA_KV_GROUPS):
                x = x2[:, g * d:(g + 1) * d]
                top[g] += _dot((x + pe_ref[kv, l:l + 1, :]).astype(BF16), w1_ref[kv, l * d:(l + 1) * d, :])
                bot[g] += _dot((x + pe_ref[kv, l2:l2 + 1, :]).astype(BF16), w1_ref[kv, l2 * d:(l2 + 1) * d, :])
        outs = []
        for g in range(NSA_KV_GROUPS):
            hid = top[g] + pltpu.roll(bot[g], n_row - 1, 0)
            outs.append(_dot(_gelu_tanh(hid).astype(BF16), w2_ref[kv]))
        y = jnp.concatenate(outs, axis=1)
        if kv == 0:
            y = _group_rms(y, bd_ref[...], gk_ref[...])
        dst[0] = y.astype(BF16)


def nsa_compress(u3, qk_gain, cmp_pe, cmp_w1, cmp_w2, col_kv):
    b, s, _ = u3.shape
    kb = col_kv // KV_LANES
    n_row = s // CMP_STRIDE
    full = lambda *shape: pl.BlockSpec(shape, lambda i: (0,) * len(shape))
    gk = jnp.tile(qk_gain[1], NSA_KV_GROUPS).reshape(1, KV_LANES)
    out_spec = pl.BlockSpec((1, n_row, KV_LANES), lambda i: (i, 0, 0))
    return pl.pallas_call(
        _nsa_compress_kernel,
        grid=(b,),
        in_specs=[pl.BlockSpec((1, s, KV_LANES), lambda i: (i, 0, kb)),
                  pl.BlockSpec((1, s, KV_LANES), lambda i: (i, 0, kb + 1)),
                  full(2, CMP_LEN, NSA_HEAD_DIM), full(2, CMP_LEN * NSA_HEAD_DIM, CMP_HIDDEN),
                  full(2, CMP_HIDDEN, NSA_HEAD_DIM), full(1, KV_LANES), full(KV_LANES, KV_LANES)],
        out_specs=[out_spec, out_spec],
        out_shape=[jax.ShapeDtypeStruct((b, n_row, KV_LANES), BF16)] * 2,
        compiler_params=pltpu.CompilerParams(dimension_semantics=("arbitrary",)),
        name="nsa_compress",
    )(u3, u3, cmp_pe, cmp_w1.astype(BF16), cmp_w2.astype(BF16), gk, _block_diag_mean(KV_LANES))


def _stack_heads(q, g):
    d = NSA_HEAD_DIM
    base = g * NSA_HPG * d
    return jnp.concatenate([q[:, base + h * d: base + (h + 1) * d] for h in range(NSA_HPG)], axis=0)


def _nsa_select_kernel(q_ref, kc_ref, vc_ref, ov_ref, oc_ref, sel_ref, flag_ref):
    qt = pl.program_id(1)
    qbl, d = NSA_QBLOCK, NSA_HEAD_DIM
    n_key = kc_ref.shape[1]
    n_slc = ov_ref.shape[1]
    t = qt * qbl + lax.broadcasted_iota(jnp.int32, (qbl, n_key), 0)
    c = lax.broadcasted_iota(jnp.int32, (qbl, n_key), 1)
    dist = (t - (c * CMP_STRIDE + CMP_LEN - 1)).astype(F32)
    ok = dist >= 0
    q = q_ref[0]
    imps = []
    for g in range(NSA_KV_GROUPS):
        kc = kc_ref[0, :, g * d:(g + 1) * d]
        vc = vc_ref[0, :, g * d:(g + 1) * d]
        s_all = _dot_nt(_stack_heads(q, g), kc)
        p_sum = jnp.zeros((qbl, n_key), F32)
        ps = []
        for h in range(NSA_HPG):
            s = jnp.where(ok, s_all[h * qbl:(h + 1) * qbl] - _head_slope(g, h) * dist, NEG_INF)
            m = jnp.max(s, axis=-1, keepdims=True)
            e = jnp.where(ok, jnp.exp(s - m), 0.0)
            l = jnp.sum(e, axis=-1, keepdims=True)
            p = e * jnp.where(l > 0, 1.0 / l, 0.0)
            ps.append(p.astype(BF16))
            p_sum += p
        o_all = _dot(jnp.concatenate(ps, axis=0), vc)
        for h in range(NSA_HPG):
            col = (g * NSA_HPG + h) * d
            oc_ref[0, :, col:col + d] = o_all[h * qbl:(h + 1) * qbl]
        imps.append(_split_dot(p_sum, ov_ref[...]))
    imp_t = jnp.concatenate(imps, axis=1).T
    j = lax.broadcasted_iota(jnp.int32, (n_slc, qbl), 0)
    forced = (j == 0) | (j == qt) | (j == qt - 1)
    visible = j <= qt
    sels = []
    for g in range(NSA_KV_GROUPS):
        score = jnp.where(visible, imp_t[g * n_slc:(g + 1) * n_slc] + jnp.where(forced, SLC_FORCE_BONUS, 0.0), NEG_INF)
        sel = jnp.zeros((n_slc, qbl), F32)
        for _ in range(min(SLC_TOPN, n_slc)):
            m = jnp.max(score, axis=0, keepdims=True)
            first = jnp.min(jnp.where(score == m, j, n_slc), axis=0, keepdims=True)
            pick = j == first
            sel = jnp.where(pick, 1.0, sel)
            score = jnp.where(pick, -3e38, score)
        sels.append(jnp.where(visible, sel, 0.0))
    sel_all = jnp.concatenate(sels, axis=0).T
    sel_ref[0] = sel_all.astype(BF16)
    flag_ref[0, 0] = jnp.broadcast_to(jnp.max(sel_all, axis=0, keepdims=True), flag_ref.shape[2:])


def nsa_select(q3, k_cmp, v_cmp):
    b, s, _ = q3.shape
    n_qt = s // NSA_QBLOCK
    n_slc = s // SLC_LEN
    n_key = k_cmp.shape[1]
    c_lo = jnp.arange(n_key) * CMP_STRIDE
    j_lo = jnp.arange(n_slc) * SLC_LEN
    overlap = ((c_lo[:, None] < j_lo[None, :] + SLC_LEN) & (c_lo[:, None] + CMP_LEN > j_lo[None, :])).astype(BF16)
    tile = lambda c: pl.BlockSpec((1, NSA_QBLOCK, c), lambda i, t: (i, t, 0))
    per_b = pl.BlockSpec((1, n_key, KV_LANES), lambda i, t: (i, 0, 0))
    return pl.pallas_call(
        _nsa_select_kernel,
        grid=(b, n_qt),
        in_specs=[tile(NSA_DIM), per_b, per_b, pl.BlockSpec((n_key, n_slc), lambda i, t: (0, 0))],
        out_specs=[tile(NSA_DIM), tile(2 * n_slc), pl.BlockSpec((1, 1, 8, 2 * n_slc), lambda i, t: (i, t, 0, 0))],
        out_shape=[jax.ShapeDtypeStruct((b, s, NSA_DIM), F32), jax.ShapeDtypeStruct((b, s, 2 * n_slc), BF16),
                   jax.ShapeDtypeStruct((b, n_qt, 8, 2 * n_slc), F32)],
        compiler_params=pltpu.CompilerParams(dimension_semantics=("arbitrary", "arbitrary")),
        name="nsa_select",
    )(q3, k_cmp, v_cmp, overlap)


def _nsa_attend_kernel(count_ref, list_ref, q_ref, ks_ref, vs_ref, kw_ref, vw_ref, sel_ref, oc_ref, gate_ref, out_ref,
                       ksel_ref, vsel_ref):
    bi, qt = pl.program_id(0), pl.program_id(1)
    n_qt = pl.num_programs(1)
    qbl, d, ch = NSA_QBLOCK, NSA_HEAD_DIM, SEL_CHUNK
    n_slc = sel_ref.shape[2] // NSA_KV_GROUPS
    n_word = n_slc // IDS_PER_WORD
    q = q_ref[0]
    gates = jax.nn.sigmoid(gate_ref[0])
    lane = lax.broadcasted_iota(jnp.int32, (1, ch * SLC_LEN), 1)
    slot_of_lane = lane // SLC_LEN
    t_sel = qt * qbl + lax.broadcasted_iota(jnp.int32, (qbl, ch * SLC_LEN), 0)
    j_iota = lax.broadcasted_iota(jnp.int32, (n_slc, ch * SLC_LEN), 0)
    win_start = jnp.maximum(qt - WINDOW // qbl, 0) * qbl
    t_win = qt * qbl + lax.broadcasted_iota(jnp.int32, (qbl, N_WIN_KEYS), 0)
    dist_win = t_win - (win_start + lax.broadcasted_iota(jnp.int32, (qbl, N_WIN_KEYS), 1))
    ok_win = (dist_win >= 0) & (dist_win < WINDOW)
    dist_win = dist_win.astype(F32)

    groups = range(NSA_KV_GROUPS)
    lanes = [slice(g * d, (g + 1) * d) for g in groups]
    qs = [_stack_heads(q, g) for g in groups]
    tile_g = [(bi * n_qt + qt) * NSA_KV_GROUPS + g for g in groups]
    n_sel = [count_ref[tg] for tg in tile_g]
    sel_g = [sel_ref[0, :, g * n_slc:(g + 1) * n_slc] for g in groups]

    def sel_scores(c, g):
        word0 = tile_g[g] * n_word + c * (ch // IDS_PER_WORD)
        j_row = jnp.full((1, ch * SLC_LEN), -1, jnp.int32)
        for slot in range(ch):
            valid = c * ch + slot < n_sel[g]
            jb = (list_ref[word0 + slot // IDS_PER_WORD] >> (8 * (slot % IDS_PER_WORD))) & 0xFF
            rows = pl.ds(pl.multiple_of(jb * SLC_LEN, SLC_LEN), SLC_LEN)
            ksel_ref[g, slot * SLC_LEN:(slot + 1) * SLC_LEN, :] = ks_ref[0, rows, lanes[g]]
            vsel_ref[g, slot * SLC_LEN:(slot + 1) * SLC_LEN, :] = vs_ref[0, rows, lanes[g]]
            j_row = jnp.where(slot_of_lane == slot, jnp.where(valid, jb, -1), j_row)
        s_all = _dot_nt(qs[g], ksel_ref[g])
        chosen = _dot(sel_g[g], (j_iota == j_row).astype(BF16))
        dist = t_sel - (j_row * SLC_LEN + lane % SLC_LEN)
        ok = (chosen > 0.5) & (dist >= 0)
        return s_all, ok, dist.astype(F32)

    def sel_softmax(scores, carry, g):
        s_all, ok, dist = scores
        m_old, l_old, acc = carry
        ps, ms, ls = [], [], []
        for h in range(NSA_HPG):
            rows_h = slice(h * qbl, (h + 1) * qbl)
            s = jnp.where(ok, s_all[rows_h] - _head_slope(g, h) * dist, NEG_INF)
            m_new = jnp.maximum(m_old[rows_h], jnp.max(s, axis=-1, keepdims=True))
            p = jnp.exp(s - m_new)
            alpha = jnp.exp(m_old[rows_h] - m_new)
            ls.append(alpha * l_old[rows_h] + jnp.sum(p, axis=-1, keepdims=True))
            ms.append(m_new)
            ps.append(p.astype(BF16))
        m_new = jnp.concatenate(ms, axis=0)
        alpha = jnp.exp(m_old - m_new)
        acc = alpha * acc + _dot(jnp.concatenate(ps, axis=0), vsel_ref[g])
        return m_new, jnp.concatenate(ls, axis=0), acc

    win_rows = pl.ds(pl.multiple_of(win_start, qbl), N_WIN_KEYS)
    s_win = [_dot_nt(qs[g], kw_ref[0, win_rows, lanes[g]]) for g in groups]
    sc0 = [sel_scores(0, g) for g in groups]
    p_win, inv_win = [], []
    for g in groups:
        ps, inv = [], []
        for h in range(NSA_HPG):
            s = jnp.where(ok_win, s_win[g][h * qbl:(h + 1) * qbl] - _head_slope(g, h) * dist_win, NEG_INF)
            e = jnp.exp(s - jnp.max(s, axis=-1, keepdims=True))
            inv.append(1.0 / jnp.sum(e, axis=-1, keepdims=True))
            ps.append(e.astype(BF16))
        p_win.append(jnp.concatenate(ps, axis=0))
        inv_win.append(jnp.concatenate(inv, axis=0))
    init = (jnp.full((NSA_HPG * qbl, 1), M_INIT, F32), jnp.zeros((NSA_HPG * qbl, 1), F32),
            jnp.zeros((NSA_HPG * qbl, d), F32))
    carry = [sel_softmax(sc0[g], init, g) for g in groups]
    o_win = [_dot(p_win[g], vw_ref[0, win_rows, lanes[g]]) * inv_win[g] for g in groups]

    for g in groups:
        rest = lax.fori_loop(1, (n_sel[g] + ch - 1) // ch,
                             lambda c, cr, g=g: sel_softmax(sel_scores(c, g), cr, g), carry[g])
        o_sel = rest[2] * (1.0 / rest[1])
        for h in range(NSA_HPG):
            hh = g * NSA_HPG + h
            col = hh * d
            rows_h = slice(h * qbl, (h + 1) * qbl)
            out_ref[0, :, col:col + d] = (gates[:, hh:hh + 1] * oc_ref[0, :, col:col + d]
                                          + gates[:, NSA_HEADS + hh:NSA_HEADS + hh + 1] * o_sel[rows_h]
                                          + gates[:, 2 * NSA_HEADS + hh:2 * NSA_HEADS + hh + 1] * o_win[g][rows_h])


def nsa_attend(counts, lists, q3, ks, vs, kw, vw, sel, o_cmp, u3, col_gate):
    b, s, _ = q3.shape
    n_qt = s // NSA_QBLOCK
    assert col_gate % 128 == 0 and s >= N_WIN_KEYS and SEL_CHUNK % IDS_PER_WORD == 0
    gb = col_gate // 128
    tile = lambda c: pl.BlockSpec((1, NSA_QBLOCK, c), lambda i, t, counts, lists: (i, t, 0))
    per_b = pl.BlockSpec((1, s, KV_LANES), lambda i, t, counts, lists: (i, 0, 0))
    grid_spec = pltpu.PrefetchScalarGridSpec(
        num_scalar_prefetch=2,
        grid=(b, n_qt),
        in_specs=[tile(NSA_DIM), per_b, per_b, per_b, per_b, tile(sel.shape[2]), tile(NSA_DIM),
                  pl.BlockSpec((1, NSA_QBLOCK, 128), lambda i, t, counts, lists: (i, t, gb))],
        out_specs=tile(NSA_DIM),
        scratch_shapes=[pltpu.VMEM((NSA_KV_GROUPS, SEL_CHUNK * SLC_LEN, NSA_HEAD_DIM), BF16),
                        pltpu.VMEM((NSA_KV_GROUPS, SEL_CHUNK * SLC_LEN, NSA_HEAD_DIM), BF16)],
    )
    return pl.pallas_call(
        _nsa_attend_kernel,
        grid_spec=grid_spec,
        out_shape=jax.ShapeDtypeStruct((b, s, NSA_DIM), F32),
        compiler_params=pltpu.CompilerParams(dimension_semantics=("arbitrary", "arbitrary")),
        name="nsa_attend",
    )(counts, lists, q3, ks, vs, kw, vw, sel, o_cmp, u3)


def _pack_union_lists(flags, n_slc):
    assert n_slc <= 256 and n_slc % IDS_PER_WORD == 0
    b, n_qt = flags.shape[:2]
    f = flags[:, :, 0, :].reshape(b, n_qt, NSA_KV_GROUPS, n_slc) > 0.5
    fi = f.astype(jnp.int32)
    counts = jnp.sum(fi, axis=-1)
    ids = jnp.arange(n_slc, dtype=jnp.int32)
    pos = jnp.where(f, jnp.cumsum(fi, axis=-1) - 1, counts[..., None] + jnp.cumsum(1 - fi, axis=-1) - 1)
    order = jnp.sum(jnp.where(pos[..., :, None] == ids, ids[:, None], 0), axis=-2)
    order = order.reshape(b, n_qt, NSA_KV_GROUPS, n_slc // IDS_PER_WORD, IDS_PER_WORD)
    words = jnp.sum(order << (8 * jnp.arange(IDS_PER_WORD, dtype=jnp.int32)), axis=-1, dtype=jnp.int32)
    return counts.reshape(-1), words.reshape(-1)


def nsa_mixer_pallas(u3, qk_gain, cmp_pe, cmp_w1, cmp_w2, col_q, col_kv, col_gate):
    b, s, d_in = u3.shape
    q, ks, vs, kw, vw = nsa_prep(u3.reshape(b * s, d_in), qk_gain, col_q, col_kv)
    k_cmp, v_cmp = nsa_compress(u3, qk_gain, cmp_pe, cmp_w1, cmp_w2, col_kv)
    q3 = q.reshape(b, s, NSA_DIM)
    r3 = lambda a: a.reshape(b, s, KV_LANES)
    o_cmp, sel, flags = nsa_select(q3, k_cmp, v_cmp)
    counts, lists = _pack_union_lists(flags, s // SLC_LEN)
    return nsa_attend(counts, lists, q3, r3(ks), r3(vs), r3(kw), r3(vw), sel, o_cmp, u3, col_gate)


GDN_TILE = 256
GDN_HALO = 8


def _dot3(a, b):
    ah = a.astype(BF16)
    bh = b.astype(BF16)
    al = (a - ah.astype(F32)).astype(BF16)
    bl = (b - bh.astype(F32)).astype(BF16)
    return _dot(ah, bh) + _dot(ah, bl) + _dot(al, bh)


def _dot_tn(a, b):
    return lax.dot_general(a, b, (((0,), (0,)), ((), ())), preferred_element_type=F32)


def _softplus(x):
    return jnp.maximum(x, 0.0) + jnp.log(1.0 + jnp.exp(-jnp.abs(x)))


def _l2_norm(x):
    return x * lax.rsqrt(jnp.sum(x * x, axis=-1, keepdims=True) + RMS_EPS)


def _gdn_kernel(qkv_ref, z_ref, small_ref, cw_ref, coef_ref, gain_ref, y_ref, xe_ref, state_ref):
    tt = pl.program_id(1)
    tile, c, hd = GDN_TILE, GDN_CHUNK, GDN_HEAD_DIM

    @pl.when(tt == 0)
    def _():
        xe_ref[0:GDN_HALO, :] = jnp.zeros((GDN_HALO, xe_ref.shape[1]), F32)
        state_ref[...] = jnp.zeros_like(state_ref)

    @pl.when(tt > 0)
    def _():
        xe_ref[0:GDN_HALO, :] = xe_ref[tile:tile + GDN_HALO, :]

    xe_ref[GDN_HALO:, :] = qkv_ref[0]
    conv = jnp.zeros((tile, xe_ref.shape[1]), F32)
    for j in range(GDN_CONV):
        conv += cw_ref[j:j + 1, :] * xe_ref[pl.ds(GDN_HALO - (GDN_CONV - 1) + j, tile), :]
    act = conv * jax.nn.sigmoid(conv)

    small = small_ref[0]
    beta_all = jax.nn.sigmoid(small)
    g_all = coef_ref[0:1, :] * _softplus(small + coef_ref[1:2, :])
    row = lax.broadcasted_iota(jnp.int32, (c, SMALL_LANES), 0)
    ri = lax.broadcasted_iota(jnp.int32, (c, c), 0)
    ci = lax.broadcasted_iota(jnp.int32, (c, c), 1)
    lower = ri >= ci
    strict = ri > ci

    n_chunk = tile // c
    pairs = [(n, h) for n in range(n_chunk) for h in range(GDN_HEADS)]
    gcs, gc_ts = [], []
    for n in range(n_chunk):
        gc = g_all[n * c:(n + 1) * c]
        shift = 1
        while shift < c:
            gc = gc + jnp.where(row >= shift, pltpu.roll(gc, shift, 0), 0.0)
            shift *= 2
        gcs.append(gc)
        gc_ts.append(gc.T)
    pre = []
    for n, h in pairs:
        rows = slice(n * c, (n + 1) * c)
        q = _l2_norm(act[rows, h * hd:(h + 1) * hd]) * hd ** -0.5
        k = _l2_norm(act[rows, GDN_DIM + h * hd:GDN_DIM + (h + 1) * hd])
        v = act[rows, 2 * GDN_DIM + h * hd:2 * GDN_DIM + (h + 1) * hd]
        beta = beta_all[rows, LANE_BETA + h:LANE_BETA + h + 1]
        gcol = gcs[n][:, LANE_DECAY + h:LANE_DECAY + h + 1]
        grow = gc_ts[n][LANE_DECAY + h:LANE_DECAY + h + 1, :]
        g_last = gcol[c - 1:c, :]
        decay = jnp.where(lower, jnp.exp(jnp.where(lower, gcol - grow, 0.0)), 0.0)
        kb = k * beta
        kh = k.astype(BF16)
        pre.append(dict(
            lmat=jnp.where(strict, _dot_nt(kb.astype(BF16), kh) * decay, 0.0),
            rhs=jnp.concatenate([v * beta, kb * jnp.exp(gcol)], axis=1),
            attn=(_dot_nt(q.astype(BF16), kh) * decay).astype(BF16),
            q_dec=(q * jnp.exp(gcol)).astype(BF16),
            k_dec=(k * jnp.exp(g_last - gcol)).astype(BF16),
            d_last=jnp.exp(g_last)))
    eye = (ri == ci).astype(F32)
    power = [p['lmat'].astype(BF16) for p in pre]
    t_inv = [eye - p['lmat'] for p in pre]
    for _ in range(5):
        power = [_dot(lm, lm).astype(BF16) for lm in power]
        t_inv = [t + _dot(lm, t.astype(BF16)) for lm, t in zip(power, t_inv)]
    rhs = [_dot(t.astype(BF16), p['rhs'].astype(BF16)) for t, p in zip(t_inv, pre)]

    state = [state_ref[h] for h in range(GDN_HEADS)]
    for n in range(n_chunk):
        rows = slice(n * c, (n + 1) * c)
        sb = [s.astype(BF16) for s in state]
        ps = [pre[n * GDN_HEADS + h] for h in range(GDN_HEADS)]
        rs = [rhs[n * GDN_HEADS + h] for h in range(GDN_HEADS)]
        v_new = [(r[:, :hd] - _dot(r[:, hd:].astype(BF16), s)).astype(BF16) for r, s in zip(rs, sb)]
        outs = [_dot(p['q_dec'], s) + _dot(p['attn'], vn) for p, s, vn in zip(ps, sb, v_new)]
        state = [s * p['d_last'] + _dot_tn(p['k_dec'], vn) for p, s, vn in zip(ps, state, v_new)]
        for h, o in enumerate(outs):
            o = o * lax.rsqrt(jnp.mean(o * o, axis=-1, keepdims=True) + RMS_EPS) * gain_ref[...]
            zz = z_ref[0, rows, h * hd:(h + 1) * hd]
            y_ref[0, rows, h * hd:(h + 1) * hd] = o * (zz * jax.nn.sigmoid(zz))
    for h in range(GDN_HEADS):
        state_ref[h] = state[h]


def gdn_mixer(u3, col_qkv, col_z, col_small, conv_w, a_log, dt_bias, out_gain):
    b, s, _ = u3.shape
    lane = jnp.arange(SMALL_LANES)
    in_decay = (lane >= LANE_DECAY) & (lane < LANE_DECAY + GDN_HEADS)
    idx = jnp.clip(lane - LANE_DECAY, 0, GDN_HEADS - 1)
    coef = jnp.stack([jnp.where(in_decay, -jnp.exp(a_log)[idx], 0.0), jnp.where(in_decay, dt_bias[idx], 0.0)])
    qkv_w = 3 * GDN_DIM
    assert col_qkv % qkv_w == 0 and col_z % GDN_DIM == 0 and col_small % SMALL_LANES == 0
    full = lambda r, cc: pl.BlockSpec((r, cc), lambda i, t: (0, 0))
    return pl.pallas_call(
        _gdn_kernel,
        grid=(b, s // GDN_TILE),
        in_specs=[pl.BlockSpec((1, GDN_TILE, qkv_w), lambda i, t: (i, t, col_qkv // qkv_w)),
                  pl.BlockSpec((1, GDN_TILE, GDN_DIM), lambda i, t: (i, t, col_z // GDN_DIM)),
                  pl.BlockSpec((1, GDN_TILE, SMALL_LANES), lambda i, t: (i, t, col_small // SMALL_LANES)),
                  full(GDN_CONV, qkv_w), full(2, SMALL_LANES), full(1, GDN_HEAD_DIM)],
        out_specs=pl.BlockSpec((1, GDN_TILE, GDN_DIM), lambda i, t: (i, t, 0)),
        out_shape=jax.ShapeDtypeStruct((b, s, GDN_DIM), F32),
        scratch_shapes=[pltpu.VMEM((GDN_TILE + GDN_HALO, qkv_w), F32), pltpu.VMEM((GDN_HEADS, GDN_HEAD_DIM, GDN_HEAD_DIM), F32)],
        compiler_params=pltpu.CompilerParams(dimension_semantics=("arbitrary", "arbitrary")),
        name="gdn_mixer",
    )(u3, u3, u3, conv_w, coef, out_gain.reshape(1, GDN_HEAD_DIM))


CONV_TILE = 512


def _short_conv_kernel(u_ref, cw_ref, y_ref, xe_ref):
    tile, cd = CONV_TILE, CONV_DIM

    @pl.when(pl.program_id(1) == 0)
    def _():
        xe_ref[0:GDN_HALO, :] = jnp.zeros((GDN_HALO, cd), F32)

    @pl.when(pl.program_id(1) > 0)
    def _():
        xe_ref[0:GDN_HALO, :] = xe_ref[tile:tile + GDN_HALO, :]

    xe_ref[GDN_HALO:, :] = u_ref[0, :, cd:2 * cd] * u_ref[0, :, 2 * cd:3 * cd]
    conv = jnp.zeros((tile, cd), F32)
    for j in range(CONV_WIDTH):
        conv += cw_ref[j:j + 1, :] * xe_ref[pl.ds(GDN_HALO - (CONV_WIDTH - 1) + j, tile), :]
    y_ref[0] = u_ref[0, :, 0:cd] * conv


def short_conv(u3, conv_w):
    b, s, _ = u3.shape
    return pl.pallas_call(
        _short_conv_kernel,
        grid=(b, s // CONV_TILE),
        in_specs=[pl.BlockSpec((1, CONV_TILE, 3 * CONV_DIM), lambda i, t: (i, t, 0)),
                  pl.BlockSpec((CONV_WIDTH, CONV_DIM), lambda i, t: (0, 0))],
        out_specs=pl.BlockSpec((1, CONV_TILE, CONV_DIM), lambda i, t: (i, t, 0)),
        out_shape=jax.ShapeDtypeStruct((b, s, CONV_DIM), F32),
        scratch_shapes=[pltpu.VMEM((CONV_TILE + GDN_HALO, CONV_DIM), F32)],
        compiler_params=pltpu.CompilerParams(dimension_semantics=("arbitrary", "arbitrary")),
        name="short_conv",
    )(u3, conv_w)


def _merge_kernel(x_ref, g_ref, ya_ref, yn_ref, yg_ref, wb_ref, wg_ref, bg_ref, wo_ref, o_ref):
    x = x_ref[...]
    d = x.shape[1]
    h = (x * lax.rsqrt(jnp.mean(x * x, axis=-1, keepdims=True) + RMS_EPS) * g_ref[...]).astype(BF16)
    merged = jnp.zeros(x.shape, F32)
    for r, y_ref in enumerate((ya_ref, yn_ref, yg_ref)):
        gate = jax.nn.sigmoid(_dot(h, wg_ref[:, r * d:(r + 1) * d]) + bg_ref[:, r * d:(r + 1) * d])
        merged += gate * _dot(y_ref[...].astype(BF16), wb_ref[r])
    o_ref[...] = x + _dot(merged.astype(BF16), wo_ref[...])


def merge_branches(x2d, gain, y_a, y_n, y_g, w_branch, w_gate, b_gate, w_out, tm=512):
    t, d = x2d.shape
    row = lambda c: pl.BlockSpec((tm, c), lambda i: (i, 0))
    full = lambda *shape: pl.BlockSpec(shape, lambda i: (0,) * len(shape))
    return pl.pallas_call(
        _merge_kernel,
        grid=(t // tm,),
        in_specs=[row(d), full(1, d), row(BRANCH_DIM), row(BRANCH_DIM), row(BRANCH_DIM),
                  full(N_BRANCH, BRANCH_DIM, d), full(d, N_BRANCH * d), full(1, N_BRANCH * d), full(d, d)],
        out_specs=row(d),
        out_shape=jax.ShapeDtypeStruct((t, d), F32),
        compiler_params=pltpu.CompilerParams(dimension_semantics=("arbitrary",), vmem_limit_bytes=56 * 1024 * 1024),
        name="merge_branches",
    )(x2d, gain.reshape(1, d), y_a, y_n, y_g, w_branch.astype(BF16), w_gate.astype(BF16), b_gate.reshape(1, -1), w_out.astype(BF16))


def hybrid_mixer(x2d, b, s, norm_gain, w_in, conv_a_w, nsa_qk_gain, cmp_pe, cmp_w1, cmp_w2, gdn_conv_w, gdn_a_log, gdn_dt_bias, gdn_out_gain, w_branch, w_gate, b_gate, w_out):
    t = b * s
    u3 = norm_proj(x2d, norm_gain, permute_in_proj(w_in)).reshape(b, s, D_U)
    y_a = short_conv(u3, conv_a_w)
    y_n = nsa_mixer_pallas(u3, nsa_qk_gain, cmp_pe, cmp_w1, cmp_w2, COL_NSA_Q, COL_NSA_KV, COL_SMALL)
    y_g = gdn_mixer(u3, COL_GDN_QKV, COL_GDN_Z, COL_SMALL, gdn_conv_w, gdn_a_log, gdn_dt_bias, gdn_out_gain)
    return merge_branches(x2d, norm_gain, y_a.reshape(t, BRANCH_DIM), y_n.reshape(t, BRANCH_DIM), y_g.reshape(t, BRANCH_DIM), w_branch, w_gate, b_gate, w_out)


MOE_ROWS = 512
ROUTE_LANES = 128
N_ROUTER = MOE_GROUPS + N_EXPERTS
ROW_DMA_UNROLL = 8


def _moe_route_kernel(x_ref, g_ref, wr_ref, br_ref, tri_ref, h_ref, route_ref, cnt_ref, run_ref):
    @pl.when(pl.program_id(0) == 0)
    def _():
        run_ref[...] = jnp.zeros_like(run_ref)

    x = x_ref[...]
    h = x * lax.rsqrt(jnp.mean(x * x, axis=-1, keepdims=True) + RMS_EPS) * g_ref[...]
    bits = lax.bitcast_convert_type(h.astype(BF16).astype(F32), jnp.uint32)
    half = h.shape[1] // 2
    packed = (bits[:, half:] & jnp.uint32(0xFFFF0000)) | (bits[:, :half] >> 16)
    h_ref[...] = packed.reshape(h_ref.shape)
    logits = _dot(h.astype(BF16), wr_ref[...]) + br_ref[...]
    lane = lax.broadcasted_iota(jnp.int32, logits.shape, 1)
    first_of = lambda hit: jnp.min(jnp.where(hit, lane, ROUTE_LANES), axis=-1, keepdims=True)
    is_grp = lane < MOE_GROUPS
    lg = jnp.where(is_grp, logits, NEG_INF)
    m_g = jnp.max(lg, axis=-1, keepdims=True)
    grp = first_of(lg == m_g)
    p_grp = 1.0 / jnp.sum(jnp.where(is_grp, jnp.exp(lg - m_g), 0.0), axis=-1, keepdims=True)
    lo = MOE_GROUPS + grp * EXPERTS_PER_GROUP
    le = jnp.where((lane >= lo) & (lane < lo + EXPERTS_PER_GROUP), logits, NEG_INF)
    m1 = jnp.max(le, axis=-1, keepdims=True)
    i1 = first_of(le == m1)
    le2 = jnp.where(lane == i1, NEG_INF, le)
    m2 = jnp.max(le2, axis=-1, keepdims=True)
    i2 = first_of(le2 == m2)
    r = jnp.exp(m2 - m1)
    g1 = p_grp / (1.0 + r)
    g2 = p_grp * r / (1.0 + r)
    e1 = i1 - MOE_GROUPS
    e2 = i2 - MOE_GROUPS
    hit1 = lane == e1
    hit2 = lane == e2
    onehot = (hit1 | hit2).astype(BF16)
    before = _dot(tri_ref[...], onehot) + run_ref[...]
    r1 = jnp.sum(jnp.where(hit1, before, 0.0), axis=-1, keepdims=True)
    r2 = jnp.sum(jnp.where(hit2, before, 0.0), axis=-1, keepdims=True)
    run_ref[...] += jnp.sum(onehot.astype(F32), axis=0, keepdims=True)
    rec = jnp.zeros(logits.shape, F32)
    for k, v in enumerate((e1.astype(F32), e2.astype(F32), r1, r2, g1, g2)):
        rec = jnp.where(lane == k, v, rec)
    route_ref[...] = rec
    cnt_ref[...] = jnp.broadcast_to(run_ref[...], cnt_ref.shape)


def moe_route(x2d, gain, w_rg, b_rg, w_re, b_re, tm=512):
    t, d = x2d.shape
    pad = ROUTE_LANES - N_ROUTER
    wr = jnp.pad(jnp.concatenate([w_rg, w_re], axis=1), ((0, 0), (0, pad))).astype(BF16)
    br = jnp.pad(jnp.concatenate([b_rg, b_re]), (0, pad)).reshape(1, ROUTE_LANES)
    tri = (jnp.arange(tm)[:, None] > jnp.arange(tm)[None, :]).astype(BF16)
    full = lambda r, c: pl.BlockSpec((r, c), lambda i: (0, 0))
    return pl.pallas_call(
        _moe_route_kernel,
        grid=(t // tm,),
        in_specs=[pl.BlockSpec((tm, d), lambda i: (i, 0)), full(1, d), full(d, ROUTE_LANES), full(1, ROUTE_LANES), full(tm, tm)],
        out_specs=[pl.BlockSpec((tm, 1, d // 2), lambda i: (i, 0, 0)), pl.BlockSpec((tm, ROUTE_LANES), lambda i: (i, 0)), full(8, ROUTE_LANES)],
        out_shape=[jax.ShapeDtypeStruct((t, 1, d // 2), jnp.uint32), jax.ShapeDtypeStruct((t, ROUTE_LANES), F32), jax.ShapeDtypeStruct((8, ROUTE_LANES), F32)],
        scratch_shapes=[pltpu.VMEM((1, ROUTE_LANES), F32)],
        compiler_params=pltpu.CompilerParams(dimension_semantics=("arbitrary",)),
        name="moe_route",
    )(x2d, gain.reshape(1, d), wr, br, tri)


def _row_copy(src_ref, src_row, dst_ref, dst_row, sem):
    return pltpu.make_async_copy(src_ref.at[src_row], dst_ref.at[dst_row], sem)


def _moe_dispatch_kernel(dest_ref, h_ref, buf_in_ref, buf_ref, sem):
    del buf_in_ref
    tm = h_ref.shape[0]
    base = pl.program_id(0) * tm

    def send(r, carry):
        for k in range(TOPK_IN_GROUP):
            _row_copy(h_ref, r, buf_ref, dest_ref[(base + r) * TOPK_IN_GROUP + k], sem).start(priority=k)
        return carry

    lax.fori_loop(0, tm, send, 0, unroll=ROW_DMA_UNROLL)
    for k in range(TOPK_IN_GROUP):
        pltpu.make_async_copy(h_ref, buf_ref.at[pl.ds(0, tm)], sem).wait()


def moe_dispatch(dest, h3, n_rows, tm=512):
    t, _, d = h3.shape
    grid_spec = pltpu.PrefetchScalarGridSpec(
        num_scalar_prefetch=1, grid=(t // tm,),
        in_specs=[pl.BlockSpec((tm, 1, d), lambda i, dest: (i, 0, 0)), pl.BlockSpec(memory_space=pl.ANY)],
        out_specs=pl.BlockSpec(memory_space=pl.ANY),
        scratch_shapes=[pltpu.SemaphoreType.DMA(())],
    )
    return pl.pallas_call(
        _moe_dispatch_kernel, grid_spec=grid_spec,
        out_shape=jax.ShapeDtypeStruct((n_rows, 1, d), h3.dtype),
        input_output_aliases={2: 0},
        compiler_params=pltpu.CompilerParams(dimension_semantics=("arbitrary",), has_side_effects=True),
        name="moe_dispatch",
    )(dest, h3, jnp.zeros((n_rows, 1, d), h3.dtype))


def _moe_ffn_kernel(blk_e_ref, n_used_ref, x_ref, wg_ref, wu_ref, wd_ref, y_ref):
    del blk_e_ref
    used = pl.program_id(0) < n_used_ref[0]

    @pl.when(used)
    def _():
        w = x_ref[...].reshape(x_ref.shape[0], x_ref.shape[2])
        lo = lax.bitcast_convert_type(w << 16, F32)
        hi = lax.bitcast_convert_type(w & jnp.uint32(0xFFFF0000), F32)
        xb = jnp.concatenate([lo, hi], axis=1).astype(BF16)
        a = _dot(xb, wg_ref[0])
        mid = a * jax.nn.sigmoid(a) * _dot(xb, wu_ref[0])
        y_ref[...] = _dot(mid.astype(BF16), wd_ref[0]).reshape(y_ref.shape)

    @pl.when(jnp.logical_not(used))
    def _():
        y_ref[...] = jnp.zeros_like(y_ref)


def moe_ffn(blk_expert, n_used, buf3, w_eg, w_eu, w_ed):
    n_rows, _, packed_w = buf3.shape
    d, ff = w_eg.shape[1:]
    w_in_spec = pl.BlockSpec((1, d, ff), lambda b, blk_e, n_used: (blk_e[b], 0, 0))
    grid_spec = pltpu.PrefetchScalarGridSpec(
        num_scalar_prefetch=2, grid=(n_rows // MOE_ROWS,),
        in_specs=[pl.BlockSpec((MOE_ROWS, 1, packed_w), lambda b, blk_e, n_used: (b, 0, 0)), w_in_spec, w_in_spec,
                  pl.BlockSpec((1, ff, d), lambda b, blk_e, n_used: (blk_e[b], 0, 0))],
        out_specs=pl.BlockSpec((MOE_ROWS, 1, d), lambda b, blk_e, n_used: (b, 0, 0)),
    )
    return pl.pallas_call(
        _moe_ffn_kernel, grid_spec=grid_spec,
        out_shape=jax.ShapeDtypeStruct((n_rows, 1, d), F32),
        compiler_params=pltpu.CompilerParams(dimension_semantics=("arbitrary",)),
        name="moe_ffn",
    )(blk_expert, n_used, buf3, w_eg.astype(BF16), w_eu.astype(BF16), w_ed.astype(BF16))


def _moe_combine_kernel(dest_ref, y_ref, rec_ref, x_ref, out_ref, ya_ref, yb_ref, sem):
    tm = out_ref.shape[0]
    base = pl.program_id(0) * tm

    def fetch(r, carry):
        _row_copy(y_ref, dest_ref[(base + r) * TOPK_IN_GROUP], ya_ref, r, sem).start(priority=0)
        _row_copy(y_ref, dest_ref[(base + r) * TOPK_IN_GROUP + 1], yb_ref, r, sem).start(priority=1)
        return carry

    lax.fori_loop(0, tm, fetch, 0, unroll=ROW_DMA_UNROLL)
    pltpu.make_async_copy(y_ref.at[pl.ds(0, tm)], ya_ref, sem).wait()
    pltpu.make_async_copy(y_ref.at[pl.ds(0, tm)], yb_ref, sem).wait()
    rec = rec_ref[...]
    ya = ya_ref[...].reshape(out_ref.shape)
    yb = yb_ref[...].reshape(out_ref.shape)
    out_ref[...] = x_ref[...] + rec[:, 4:5] * ya + rec[:, 5:6] * yb


def moe_combine(dest, y3, rec, x2d, tm=512):
    t, d = x2d.shape
    grid_spec = pltpu.PrefetchScalarGridSpec(
        num_scalar_prefetch=1, grid=(t // tm,),
        in_specs=[pl.BlockSpec(memory_space=pl.ANY), pl.BlockSpec((tm, ROUTE_LANES), lambda i, dest: (i, 0)),
                  pl.BlockSpec((tm, d), lambda i, dest: (i, 0))],
        out_specs=pl.BlockSpec((tm, d), lambda i, dest: (i, 0)),
        scratch_shapes=[pltpu.VMEM((tm, 1, d), F32), pltpu.VMEM((tm, 1, d), F32), pltpu.SemaphoreType.DMA(())],
    )
    return pl.pallas_call(
        _moe_combine_kernel, grid_spec=grid_spec,
        out_shape=jax.ShapeDtypeStruct((t, d), F32),
        compiler_params=pltpu.CompilerParams(dimension_semantics=("arbitrary",)),
        name="moe_combine",
    )(dest, y3, rec, x2d)


def hier_moe_pallas(x2d, gain, w_rg, b_rg, w_re, b_re, w_eg, w_eu, w_ed):
    t, d = x2d.shape
    h, rec, cnt = moe_route(x2d, gain, w_rg, b_rg, w_re, b_re)
    counts = cnt[0, :N_EXPERTS].astype(jnp.int32)
    n_blk = (counts + MOE_ROWS - 1) // MOE_ROWS
    blk_end = jnp.cumsum(n_blk)
    pad_start = (blk_end - n_blk) * MOE_ROWS
    experts = rec[:, 0:2].astype(jnp.int32)
    dest = (pad_start[experts] + rec[:, 2:4].astype(jnp.int32)).reshape(-1)
    total_blk = t * TOPK_IN_GROUP // MOE_ROWS + N_EXPERTS
    blk_expert = jnp.minimum(jnp.searchsorted(blk_end, jnp.arange(total_blk), side='right'), N_EXPERTS - 1).astype(jnp.int32)
    buf = moe_dispatch(dest, h, total_blk * MOE_ROWS)
    y = moe_ffn(blk_expert, blk_end[-1:].astype(jnp.int32), buf, w_eg, w_eu, w_ed)
    return moe_combine(dest, y, rec, x2d)


def kernel(x, norm_mix, w_in, conv_a_w, nsa_qk_gain, cmp_pe, cmp_w1, cmp_w2, gdn_conv_w, gdn_a_log, gdn_dt_bias, gdn_out_gain, w_branch, w_gate, b_gate, w_out, norm_ffn, w_router_group, b_router_group, w_router_expert, b_router_expert, w_expert_gate, w_expert_up, w_expert_down):
    b, s, dm = x.shape
    x = x.reshape(b * s, dm)
    for l in range(DEPTH):
        x = hybrid_mixer(x, b, s, norm_mix[l], w_in[l], conv_a_w[l], nsa_qk_gain[l], cmp_pe[l], cmp_w1[l], cmp_w2[l], gdn_conv_w[l], gdn_a_log[l], gdn_dt_bias[l], gdn_out_gain[l], w_branch[l], w_gate[l], b_gate[l], w_out[l])
        x = hier_moe_pallas(x, norm_ffn[l], w_router_group[l], b_router_group[l], w_router_expert[l], b_router_expert[l], w_expert_gate[l], w_expert_up[l], w_expert_down[l])
    return x.reshape(b, s, dm)
```

```python
import functools
import math

import jax
import jax.numpy as jnp
from jax import lax
from jax.experimental import pallas as pl
from jax.experimental.pallas import tpu as pltpu

D_MODEL = 1024
DEPTH = 4
CONV_DIM = 512
CONV_WIDTH = 3
NSA_HEADS = 8
NSA_KV_GROUPS = 2
NSA_HPG = NSA_HEADS // NSA_KV_GROUPS
NSA_HEAD_DIM = 64
NSA_DIM = NSA_HEADS * NSA_HEAD_DIM
CMP_LEN = 32
CMP_STRIDE = 16
CMP_HIDDEN = 256
SLC_LEN = 64
SLC_TOPN = 8
SLC_FORCE_BONUS = 1e6
WINDOW = 512
NSA_QBLOCK = 64
GDN_HEADS = 4
GDN_HEAD_DIM = 128
GDN_DIM = GDN_HEADS * GDN_HEAD_DIM
GDN_CONV = 4
GDN_CHUNK = 64
N_BRANCH = 3
BRANCH_DIM = 512
IN_SPLITS = (3 * CONV_DIM, NSA_DIM, 6 * NSA_KV_GROUPS * NSA_HEAD_DIM, 3 * NSA_HEADS, 3 * GDN_DIM, GDN_DIM, GDN_HEADS, GDN_HEADS)
D_IN = sum(IN_SPLITS)
COL_CONV = 0
COL_GDN_QKV = 3 * CONV_DIM
COL_NSA_Q = COL_GDN_QKV + 3 * GDN_DIM
COL_GDN_Z = COL_NSA_Q + NSA_DIM
COL_NSA_KV = COL_GDN_Z + GDN_DIM
COL_SMALL = COL_NSA_KV + 6 * NSA_KV_GROUPS * NSA_HEAD_DIM
SMALL_LANES = 128
D_U = COL_SMALL + SMALL_LANES
LANE_BETA = 3 * NSA_HEADS
LANE_DECAY = LANE_BETA + GDN_HEADS


def permute_in_proj(w_in):
    conv, nq, nkv, ng, gqkv, gz, gb, ga = split_last(w_in, IN_SPLITS)
    pad = jnp.zeros((w_in.shape[0], SMALL_LANES - LANE_DECAY - GDN_HEADS), w_in.dtype)
    return jnp.concatenate([conv, gqkv, nq, gz, nkv, ng, gb, ga, pad], axis=1)


def split_last(u, sizes):
    out, start = [], 0
    for n in sizes:
        out.append(u[..., start:start + n])
        start += n
    return out
MOE_GROUPS = 4
EXPERTS_PER_GROUP = 8
N_EXPERTS = MOE_GROUPS * EXPERTS_PER_GROUP
TOPK_IN_GROUP = 2
EXPERT_FF = 512
MOE_BLOCK = 256
RMS_EPS = 1e-6
NEG_INF = -1e30

F32 = jnp.float32
BF16 = jnp.bfloat16


def _norm_proj_kernel(x_ref, g_ref, w_ref, o_ref, h_ref):
    @pl.when(pl.program_id(1) == 0)
    def _():
        x = x_ref[...]
        y = x * lax.rsqrt(jnp.mean(x * x, axis=-1, keepdims=True) + RMS_EPS)
        h_ref[...] = (y * g_ref[...]).astype(BF16)

    o_ref[...] = jnp.dot(h_ref[...], w_ref[...], preferred_element_type=F32)


def norm_proj(x2d, gain, w, tm=1024, tn=D_U // 3):
    t, d = x2d.shape
    n = w.shape[1]
    n_blk = pl.cdiv(n, tn)
    wb = jnp.pad(w.astype(BF16), ((0, 0), (0, n_blk * tn - n)))
    return pl.pallas_call(
        _norm_proj_kernel,
        grid=(t // tm, n_blk),
        in_specs=[
            pl.BlockSpec((tm, d), lambda i, j: (i, 0)),
            pl.BlockSpec((1, d), lambda i, j: (0, 0)),
            pl.BlockSpec((d, tn), lambda i, j: (0, j)),
        ],
        out_specs=pl.BlockSpec((tm, tn), lambda i, j: (i, j)),
        out_shape=jax.ShapeDtypeStruct((t, n), F32),
        scratch_shapes=[pltpu.VMEM((tm, d), BF16)],
        compiler_params=pltpu.CompilerParams(dimension_semantics=("arbitrary", "arbitrary")),
        name="norm_proj",
    )(x2d, gain.reshape(1, d), wb)


KV_LANES = NSA_KV_GROUPS * NSA_HEAD_DIM
N_WIN_KEYS = WINDOW + NSA_QBLOCK
SEL_CHUNK = 8
IDS_PER_WORD = 4
SELECT_TILES = 4
M_INIT = -1e29


def _dot(a, b):
    return jnp.dot(a, b, preferred_element_type=F32)


def _dot_nt(a, b):
    return lax.dot_general(a, b, (((1,), (1,)), ((), ())), preferred_element_type=F32)


def _split_dot(x, m):
    hi = x.astype(BF16)
    lo = (x - hi.astype(F32)).astype(BF16)
    return _dot(hi, m) + _dot(lo, m)


def _head_slope(g, h):
    return 2.0 ** (-8.0 * (g * NSA_HPG + h + 1) / NSA_HEADS)


def _group_rms(x, bd, gain):
    ms = _split_dot(x * x, bd)
    return x * lax.rsqrt(ms + RMS_EPS) * gain


def _nsa_prep_kernel(uq_ref, uks_ref, uvs_ref, ukw_ref, uvw_ref, gq_ref, gks_ref, gkw_ref, bdq_ref, bdk_ref,
                     q_ref, ks_ref, vs_ref, kw_ref, vw_ref):
    scale = NSA_HEAD_DIM ** -0.5
    q_ref[...] = (_group_rms(uq_ref[...], bdq_ref[...], gq_ref[...]) * scale).astype(BF16)
    ks_ref[...] = _group_rms(uks_ref[...], bdk_ref[...], gks_ref[...]).astype(BF16)
    kw_ref[...] = _group_rms(ukw_ref[...], bdk_ref[...], gkw_ref[...]).astype(BF16)
    vs_ref[...] = uvs_ref[...].astype(BF16)
    vw_ref[...] = uvw_ref[...].astype(BF16)


def _block_diag_mean(n):
    i = jnp.arange(n) // NSA_HEAD_DIM
    return ((i[:, None] == i[None, :]).astype(F32) / NSA_HEAD_DIM).astype(BF16)


def nsa_prep(u, qk_gain, col_q, col_kv, tm=512):
    t = u.shape[0]
    assert t % tm == 0 and col_q % NSA_DIM == 0 and col_kv % KV_LANES == 0
    qb = col_q // NSA_DIM
    kb = col_kv // KV_LANES
    kv_spec = lambda j: pl.BlockSpec((tm, KV_LANES), lambda i, j=j: (i, kb + j))
    full = lambda r, c: pl.BlockSpec((r, c), lambda i: (0, 0))
    row = lambda c: pl.BlockSpec((tm, c), lambda i: (i, 0))
    gq = jnp.tile(qk_gain[0], NSA_HEADS).reshape(1, NSA_DIM)
    gks = jnp.tile(qk_gain[2], NSA_KV_GROUPS).reshape(1, KV_LANES)
    gkw = jnp.tile(qk_gain[3], NSA_KV_GROUPS).reshape(1, KV_LANES)
    return pl.pallas_call(
        _nsa_prep_kernel,
        grid=(t // tm,),
        in_specs=[pl.BlockSpec((tm, NSA_DIM), lambda i: (i, qb)), kv_spec(2), kv_spec(3), kv_spec(4), kv_spec(5),
                  full(1, NSA_DIM), full(1, KV_LANES), full(1, KV_LANES), full(NSA_DIM, NSA_DIM), full(KV_LANES, KV_LANES)],
        out_specs=[row(NSA_DIM), row(KV_LANES), row(KV_LANES), row(KV_LANES), row(KV_LANES)],
        out_shape=[jax.ShapeDtypeStruct((t, NSA_DIM), BF16)] + [jax.ShapeDtypeStruct((t, KV_LANES), BF16)] * 4,
        compiler_params=pltpu.CompilerParams(dimension_semantics=("arbitrary",)),
        name="nsa_prep",
    )(u, u, u, u, u, gq, gks, gkw, _block_diag_mean(NSA_DIM), _block_diag_mean(KV_LANES))


def _gelu_tanh(x):
    return 0.5 * x * (1.0 + jnp.tanh(0.7978845608028654 * (x + 0.044715 * x * x * x)))


def _nsa_compress_kernel(uk_ref, uv_ref, pe_ref, w1_ref, w2_ref, gk_ref, bd_ref, kc_ref, vc_ref):
    n_row = uk_ref.shape[1] // CMP_STRIDE
    d = NSA_HEAD_DIM
    for kv, (src, dst) in enumerate(((uk_ref, kc_ref), (uv_ref, vc_ref))):
        top = [jnp.zeros((n_row, CMP_HIDDEN), F32) for _ in range(NSA_KV_GROUPS)]
        bot = [jnp.zeros((n_row, CMP_HIDDEN), F32) for _ in range(NSA_KV_GROUPS)]
        for l in range(CMP_STRIDE):
            x2 = src[0, pl.ds(l, n_row, stride=CMP_STRIDE), :]
            l2 = l + CMP_STRIDE
            for g in range(NSA_KV_GROUPS):
                x = x2[:, g * d:(g + 1) * d]
                top[g] += _dot((x + pe_ref[kv, l:l + 1, :]).astype(BF16), w1_ref[kv, l * d:(l + 1) * d, :])
                bot[g] += _dot((x + pe_ref[kv, l2:l2 + 1, :]).astype(BF16), w1_ref[kv, l2 * d:(l2 + 1) * d, :])
        outs = []
        for g in range(NSA_KV_GROUPS):
            hid = top[g] + pltpu.roll(bot[g], n_row - 1, 0)
            outs.append(_dot(_gelu_tanh(hid).astype(BF16), w2_ref[kv]))
        y = jnp.concatenate(outs, axis=1)
        if kv == 0:
            y = _group_rms(y, bd_ref[...], gk_ref[...])
        dst[0] = y.astype(BF16)


def nsa_compress(u3, qk_gain, cmp_pe, cmp_w1, cmp_w2, col_kv):
    b, s, _ = u3.shape
    kb = col_kv // KV_LANES
    n_row = s // CMP_STRIDE
    full = lambda *shape: pl.BlockSpec(shape, lambda i: (0,) * len(shape))
    gk = jnp.tile(qk_gain[1], NSA_KV_GROUPS).reshape(1, KV_LANES)
    out_spec = pl.BlockSpec((1, n_row, KV_LANES), lambda i: (i, 0, 0))
    return pl.pallas_call(
        _nsa_compress_kernel,
        grid=(b,),
        in_specs=[pl.BlockSpec((1, s, KV_LANES), lambda i: (i, 0, kb)),
                  pl.BlockSpec((1, s, KV_LANES), lambda i: (i, 0, kb + 1)),
                  full(2, CMP_LEN, NSA_HEAD_DIM), full(2, CMP_LEN * NSA_HEAD_DIM, CMP_HIDDEN),
                  full(2, CMP_HIDDEN, NSA_HEAD_DIM), full(1, KV_LANES), full(KV_LANES, KV_LANES)],
        out_specs=[out_spec, out_spec],
        out_shape=[jax.ShapeDtypeStruct((b, n_row, KV_LANES), BF16)] * 2,
        compiler_params=pltpu.CompilerParams(dimension_semantics=("arbitrary",)),
        name="nsa_compress",
    )(u3, u3, cmp_pe, cmp_w1.astype(BF16), cmp_w2.astype(BF16), gk, _block_diag_mean(KV_LANES))


def _stack_heads(q, g):
    d = NSA_HEAD_DIM
    base = g * NSA_HPG * d
    return jnp.concatenate([q[:, base + h * d: base + (h + 1) * d] for h in range(NSA_HPG)], axis=0)


def _nsa_select_kernel(q_ref, kc_ref, vc_ref, ov_ref, oc_ref, sel_ref, flag_ref):
    qbl, d = NSA_QBLOCK, NSA_HEAD_DIM
    n_key = kc_ref.shape[1]
    n_slc = ov_ref.shape[1]
    row = lax.broadcasted_iota(jnp.int32, (qbl, n_key), 0)
    last = lax.broadcasted_iota(jnp.int32, (qbl, n_key), 1) * CMP_STRIDE + CMP_LEN - 1
    j = lax.broadcasted_iota(jnp.int32, (n_slc, qbl), 0)
    tiles = range(SELECT_TILES)
    qts = [pl.program_id(1) * SELECT_TILES + i for i in tiles]
    tile_rows = [slice(i * qbl, (i + 1) * qbl) for i in tiles]

    def importance(i):
        dist = (qts[i] * qbl + row - last).astype(F32)
        ok = dist >= 0
        q = q_ref[0, tile_rows[i], :]
        imps = []
        for g in range(NSA_KV_GROUPS):
            kc = kc_ref[0, :, g * d:(g + 1) * d]
            vc = vc_ref[0, :, g * d:(g + 1) * d]
            s_all = _dot_nt(_stack_heads(q, g), kc)
            p_sum = jnp.zeros((qbl, n_key), F32)
            ps = []
            for h in range(NSA_HPG):
                s = jnp.where(ok, s_all[h * qbl:(h + 1) * qbl] - _head_slope(g, h) * dist, NEG_INF)
                m = jnp.max(s, axis=-1, keepdims=True)
                e = jnp.where(ok, jnp.exp(s - m), 0.0)
                l = jnp.sum(e, axis=-1, keepdims=True)
                p = e * jnp.where(l > 0, 1.0 / l, 0.0)
                ps.append(p.astype(BF16))
                p_sum += p
            o_all = _dot(jnp.concatenate(ps, axis=0), vc)
            for h in range(NSA_HPG):
                col = (g * NSA_HPG + h) * d
                oc_ref[0, tile_rows[i], col:col + d] = o_all[h * qbl:(h + 1) * qbl]
            imps.append(_split_dot(p_sum, ov_ref[...]))
        return jnp.concatenate(imps, axis=1).T

    def top_blocks(i, imp_t):
        forced = (j == 0) | (j == qts[i]) | (j == qts[i] - 1)
        visible = j <= qts[i]
        sels = []
        for g in range(NSA_KV_GROUPS):
            score = jnp.where(visible, imp_t[g * n_slc:(g + 1) * n_slc] + jnp.where(forced, SLC_FORCE_BONUS, 0.0), NEG_INF)
            sel = jnp.zeros((n_slc, qbl), F32)
            for _ in range(min(SLC_TOPN, n_slc)):
                m = jnp.max(score, axis=0, keepdims=True)
                first = jnp.min(jnp.where(score == m, j, n_slc), axis=0, keepdims=True)
                pick = j == first
                sel = jnp.where(pick, 1.0, sel)
                score = jnp.where(pick, -3e38, score)
            sels.append(jnp.where(visible, sel, 0.0))
        sel_all = jnp.concatenate(sels, axis=0).T
        sel_ref[0, tile_rows[i], :] = sel_all.astype(BF16)
        flag_ref[0, i] = jnp.broadcast_to(jnp.max(sel_all, axis=0, keepdims=True), flag_ref.shape[2:])

    imp_ts = [importance(i) for i in tiles]
    for i in tiles:
        top_blocks(i, imp_ts[i])


def nsa_select(q3, k_cmp, v_cmp):
    b, s, _ = q3.shape
    n_qt = s // NSA_QBLOCK
    n_slc = s // SLC_LEN
    n_key = k_cmp.shape[1]
    c_lo = jnp.arange(n_key) * CMP_STRIDE
    j_lo = jnp.arange(n_slc) * SLC_LEN
    overlap = ((c_lo[:, None] < j_lo[None, :] + SLC_LEN) & (c_lo[:, None] + CMP_LEN > j_lo[None, :])).astype(BF16)
    rows = SELECT_TILES * NSA_QBLOCK
    assert s % rows == 0
    tile = lambda c: pl.BlockSpec((1, rows, c), lambda i, t: (i, t, 0))
    per_b = pl.BlockSpec((1, n_key, KV_LANES), lambda i, t: (i, 0, 0))
    return pl.pallas_call(
        _nsa_select_kernel,
        grid=(b, s // rows),
        in_specs=[tile(NSA_DIM), per_b, per_b, pl.BlockSpec((n_key, n_slc), lambda i, t: (0, 0))],
        out_specs=[tile(NSA_DIM), tile(2 * n_slc), pl.BlockSpec((1, SELECT_TILES, 8, 2 * n_slc), lambda i, t: (i, t, 0, 0))],
        out_shape=[jax.ShapeDtypeStruct((b, s, NSA_DIM), F32), jax.ShapeDtypeStruct((b, s, 2 * n_slc), BF16),
                   jax.ShapeDtypeStruct((b, n_qt, 8, 2 * n_slc), F32)],
        compiler_params=pltpu.CompilerParams(dimension_semantics=("arbitrary", "arbitrary")),
        name="nsa_select",
    )(q3, k_cmp, v_cmp, overlap)


def _nsa_attend_kernel(count_ref, list_ref, q_ref, ks_ref, vs_ref, kw_ref, vw_ref, sel_ref, oc_ref, gate_ref, out_ref,
                       ksel_ref, vsel_ref):
    bi, qt = pl.program_id(0), pl.program_id(1)
    n_qt = pl.num_programs(1)
    qbl, d, ch = NSA_QBLOCK, NSA_HEAD_DIM, SEL_CHUNK
    n_slc = sel_ref.shape[2] // NSA_KV_GROUPS
    n_word = n_slc // IDS_PER_WORD
    q = q_ref[0]
    gates = jax.nn.sigmoid(gate_ref[0])
    lane = lax.broadcasted_iota(jnp.int32, (1, ch * SLC_LEN), 1)
    slot_of_lane = lane // SLC_LEN
    t_sel = qt * qbl + lax.broadcasted_iota(jnp.int32, (qbl, ch * SLC_LEN), 0)
    j_iota = lax.broadcasted_iota(jnp.int32, (n_slc, ch * SLC_LEN), 0)
    win_start = jnp.maximum(qt - WINDOW // qbl, 0) * qbl
    t_win = qt * qbl + lax.broadcasted_iota(jnp.int32, (qbl, N_WIN_KEYS), 0)
    dist_win = t_win - (win_start + lax.broadcasted_iota(jnp.int32, (qbl, N_WIN_KEYS), 1))
    ok_win = (dist_win >= 0) & (dist_win < WINDOW)
    dist_win = dist_win.astype(F32)

    groups = range(NSA_KV_GROUPS)
    lanes = [slice(g * d, (g + 1) * d) for g in groups]
    qs = [_stack_heads(q, g) for g in groups]
    tile_g = [(bi * n_qt + qt) * NSA_KV_GROUPS + g for g in groups]
    n_sel = [count_ref[tg] for tg in tile_g]
    sel_g = [sel_ref[0, :, g * n_slc:(g + 1) * n_slc] for g in groups]

    def sel_scores(c, g):
        word0 = tile_g[g] * n_word + c * (ch // IDS_PER_WORD)
        j_row = jnp.full((1, ch * SLC_LEN), -1, jnp.int32)
        for slot in range(ch):
            valid = c * ch + slot < n_sel[g]
            jb = (list_ref[word0 + slot // IDS_PER_WORD] >> (8 * (slot % IDS_PER_WORD))) & 0xFF
            rows = pl.ds(pl.multiple_of(jb * SLC_LEN, SLC_LEN), SLC_LEN)
            ksel_ref[g, slot * SLC_LEN:(slot + 1) * SLC_LEN, :] = ks_ref[0, rows, lanes[g]]
            vsel_ref[g, slot * SLC_LEN:(slot + 1) * SLC_LEN, :] = vs_ref[0, rows, lanes[g]]
            j_row = jnp.where(slot_of_lane == slot, jnp.where(valid, jb, -1), j_row)
        s_all = _dot_nt(qs[g], ksel_ref[g])
        chosen = _dot(sel_g[g], (j_iota == j_row).astype(BF16))
        dist = t_sel - (j_row * SLC_LEN + lane % SLC_LEN)
        ok = (chosen > 0.5) & (dist >= 0)
        return s_all, ok, dist.astype(F32)

    def sel_softmax(scores, carry, g):
        s_all, ok, dist = scores
        m_old, l_old, acc = carry
        ps, ms, ls = [], [], []
        for h in range(NSA_HPG):
            rows_h = slice(h * qbl, (h + 1) * qbl)
            s = jnp.where(ok, s_all[rows_h] - _head_slope(g, h) * dist, NEG_INF)
            m_new = jnp.maximum(m_old[rows_h], jnp.max(s, axis=-1, keepdims=True))
            p = jnp.exp(s - m_new)
            alpha = jnp.exp(m_old[rows_h] - m_new)
            ls.append(alpha * l_old[rows_h] + jnp.sum(p, axis=-1, keepdims=True))
            ms.append(m_new)
            ps.append(p.astype(BF16))
        m_new = jnp.concatenate(ms, axis=0)
        alpha = jnp.exp(m_old - m_new)
        acc = alpha * acc + _dot(jnp.concatenate(ps, axis=0), vsel_ref[g])
        return m_new, jnp.concatenate(ls, axis=0), acc

    win_rows = pl.ds(pl.multiple_of(win_start, qbl), N_WIN_KEYS)
    s_win = [_dot_nt(qs[g], kw_ref[0, win_rows, lanes[g]]) for g in groups]
    sc0 = [sel_scores(0, g) for g in groups]
    p_win, inv_win = [], []
    for g in groups:
        ps, inv = [], []
        for h in range(NSA_HPG):
            s = jnp.where(ok_win, s_win[g][h * qbl:(h + 1) * qbl] - _head_slope(g, h) * dist_win, NEG_INF)
            e = jnp.exp(s - jnp.max(s, axis=-1, keepdims=True))
            inv.append(1.0 / jnp.sum(e, axis=-1, keepdims=True))
            ps.append(e.astype(BF16))
        p_win.append(jnp.concatenate(ps, axis=0))
        inv_win.append(jnp.concatenate(inv, axis=0))
    init = (jnp.full((NSA_HPG * qbl, 1), M_INIT, F32), jnp.zeros((NSA_HPG * qbl, 1), F32),
            jnp.zeros((NSA_HPG * qbl, d), F32))
    carry = [sel_softmax(sc0[g], init, g) for g in groups]
    o_win = [_dot(p_win[g], vw_ref[0, win_rows, lanes[g]]) * inv_win[g] for g in groups]

    for g in groups:
        rest = lax.fori_loop(1, (n_sel[g] + ch - 1) // ch,
                             lambda c, cr, g=g: sel_softmax(sel_scores(c, g), cr, g), carry[g])
        o_sel = rest[2] * (1.0 / rest[1])
        for h in range(NSA_HPG):
            hh = g * NSA_HPG + h
            col = hh * d
            rows_h = slice(h * qbl, (h + 1) * qbl)
            out_ref[0, :, col:col + d] = (gates[:, hh:hh + 1] * oc_ref[0, :, col:col + d]
                                          + gates[:, NSA_HEADS + hh:NSA_HEADS + hh + 1] * o_sel[rows_h]
                                          + gates[:, 2 * NSA_HEADS + hh:2 * NSA_HEADS + hh + 1] * o_win[g][rows_h])


def nsa_attend(counts, lists, q3, ks, vs, kw, vw, sel, o_cmp, u3, col_gate):
    b, s, _ = q3.shape
    n_qt = s // NSA_QBLOCK
    assert col_gate % 128 == 0 and s >= N_WIN_KEYS and SEL_CHUNK % IDS_PER_WORD == 0
    gb = col_gate // 128
    tile = lambda c: pl.BlockSpec((1, NSA_QBLOCK, c), lambda i, t, counts, lists: (i, t, 0))
    per_b = pl.BlockSpec((1, s, KV_LANES), lambda i, t, counts, lists: (i, 0, 0))
    grid_spec = pltpu.PrefetchScalarGridSpec(
        num_scalar_prefetch=2,
        grid=(b, n_qt),
        in_specs=[tile(NSA_DIM), per_b, per_b, per_b, per_b, tile(sel.shape[2]), tile(NSA_DIM),
                  pl.BlockSpec((1, NSA_QBLOCK, 128), lambda i, t, counts, lists: (i, t, gb))],
        out_specs=tile(NSA_DIM),
        scratch_shapes=[pltpu.VMEM((NSA_KV_GROUPS, SEL_CHUNK * SLC_LEN, NSA_HEAD_DIM), BF16),
                        pltpu.VMEM((NSA_KV_GROUPS, SEL_CHUNK * SLC_LEN, NSA_HEAD_DIM), BF16)],
    )
    return pl.pallas_call(
        _nsa_attend_kernel,
        grid_spec=grid_spec,
        out_shape=jax.ShapeDtypeStruct((b, s, NSA_DIM), F32),
        compiler_params=pltpu.CompilerParams(dimension_semantics=("arbitrary", "arbitrary")),
        name="nsa_attend",
    )(counts, lists, q3, ks, vs, kw, vw, sel, o_cmp, u3)


def _pack_union_lists(flags, n_slc):
    assert n_slc <= 256 and n_slc % IDS_PER_WORD == 0
    b, n_qt = flags.shape[:2]
    f = flags[:, :, 0, :].reshape(b, n_qt, NSA_KV_GROUPS, n_slc) > 0.5
    fi = f.astype(jnp.int32)
    counts = jnp.sum(fi, axis=-1)
    ids = jnp.arange(n_slc, dtype=jnp.int32)
    ahead = (fi.astype(F32) @ (ids[:, None] <= ids[None, :]).astype(F32)).astype(jnp.int32)
    pos = jnp.where(f, ahead - 1, counts[..., None] + ids - ahead)
    order = jnp.sum(jnp.where(pos[..., :, None] == ids, ids[:, None], 0), axis=-2)
    order = order.reshape(b, n_qt, NSA_KV_GROUPS, n_slc // IDS_PER_WORD, IDS_PER_WORD)
    words = jnp.sum(order << (8 * jnp.arange(IDS_PER_WORD, dtype=jnp.int32)), axis=-1, dtype=jnp.int32)
    return counts.reshape(-1), words.reshape(-1)


def nsa_mixer_pallas(u3, qk_gain, cmp_pe, cmp_w1, cmp_w2, col_q, col_kv, col_gate):
    b, s, d_in = u3.shape
    q, ks, vs, kw, vw = nsa_prep(u3.reshape(b * s, d_in), qk_gain, col_q, col_kv)
    k_cmp, v_cmp = nsa_compress(u3, qk_gain, cmp_pe, cmp_w1, cmp_w2, col_kv)
    q3 = q.reshape(b, s, NSA_DIM)
    r3 = lambda a: a.reshape(b, s, KV_LANES)
    o_cmp, sel, flags = nsa_select(q3, k_cmp, v_cmp)
    counts, lists = _pack_union_lists(flags, s // SLC_LEN)
    return nsa_attend(counts, lists, q3, r3(ks), r3(vs), r3(kw), r3(vw), sel, o_cmp, u3, col_gate)


GDN_TILE = 256
GDN_HALO = 8


def _dot3(a, b):
    ah = a.astype(BF16)
    bh = b.astype(BF16)
    al = (a - ah.astype(F32)).astype(BF16)
    bl = (b - bh.astype(F32)).astype(BF16)
    return _dot(ah, bh) + _dot(ah, bl) + _dot(al, bh)


def _dot_tn(a, b):
    return lax.dot_general(a, b, (((0,), (0,)), ((), ())), preferred_element_type=F32)


def _softplus(x):
    return jnp.maximum(x, 0.0) + jnp.log(1.0 + jnp.exp(-jnp.abs(x)))


def _l2_norm(x):
    return x * lax.rsqrt(jnp.sum(x * x, axis=-1, keepdims=True) + RMS_EPS)


def _gdn_kernel(qkv_ref, z_ref, small_ref, cw_ref, coef_ref, gain_ref, y_ref, xe_ref, state_ref):
    tt = pl.program_id(1)
    tile, c, hd = GDN_TILE, GDN_CHUNK, GDN_HEAD_DIM

    @pl.when(tt == 0)
    def _():
        xe_ref[0:GDN_HALO, :] = jnp.zeros((GDN_HALO, xe_ref.shape[1]), F32)
        state_ref[...] = jnp.zeros_like(state_ref)

    @pl.when(tt > 0)
    def _():
        xe_ref[0:GDN_HALO, :] = xe_ref[tile:tile + GDN_HALO, :]

    xe_ref[GDN_HALO:, :] = qkv_ref[0]
    conv = jnp.zeros((tile, xe_ref.shape[1]), F32)
    for j in range(GDN_CONV):
        conv += cw_ref[j:j + 1, :] * xe_ref[pl.ds(GDN_HALO - (GDN_CONV - 1) + j, tile), :]
    act = conv * jax.nn.sigmoid(conv)

    small = small_ref[0]
    beta_all = jax.nn.sigmoid(small)
    g_all = coef_ref[0:1, :] * _softplus(small + coef_ref[1:2, :])
    row = lax.broadcasted_iota(jnp.int32, (c, SMALL_LANES), 0)
    ri = lax.broadcasted_iota(jnp.int32, (c, c), 0)
    ci = lax.broadcasted_iota(jnp.int32, (c, c), 1)
    lower = ri >= ci
    strict = ri > ci

    n_chunk = tile // c
    pairs = [(n, h) for n in range(n_chunk) for h in range(GDN_HEADS)]
    gcs, gc_ts = [], []
    for n in range(n_chunk):
        gc = g_all[n * c:(n + 1) * c]
        shift = 1
        while shift < c:
            gc = gc + jnp.where(row >= shift, pltpu.roll(gc, shift, 0), 0.0)
            shift *= 2
        gcs.append(gc)
        gc_ts.append(gc.T)
    pre = []
    for n, h in pairs:
        rows = slice(n * c, (n + 1) * c)
        q = _l2_norm(act[rows, h * hd:(h + 1) * hd]) * hd ** -0.5
        k = _l2_norm(act[rows, GDN_DIM + h * hd:GDN_DIM + (h + 1) * hd])
        v = act[rows, 2 * GDN_DIM + h * hd:2 * GDN_DIM + (h + 1) * hd]
        beta = beta_all[rows, LANE_BETA + h:LANE_BETA + h + 1]
        gcol = gcs[n][:, LANE_DECAY + h:LANE_DECAY + h + 1]
        grow = gc_ts[n][LANE_DECAY + h:LANE_DECAY + h + 1, :]
        g_last = gcol[c - 1:c, :]
        decay = jnp.where(lower, jnp.exp(jnp.where(lower, gcol - grow, 0.0)), 0.0)
        kb = k * beta
        kh = k.astype(BF16)
        pre.append(dict(
            lmat=jnp.where(strict, _dot_nt(kb.astype(BF16), kh) * decay, 0.0),
            rhs=jnp.concatenate([v * beta, kb * jnp.exp(gcol)], axis=1),
            attn=(_dot_nt(q.astype(BF16), kh) * decay).astype(BF16),
            q_dec=(q * jnp.exp(gcol)).astype(BF16),
            k_dec=(k * jnp.exp(g_last - gcol)).astype(BF16),
            d_last=jnp.exp(g_last)))
    eye = (ri == ci).astype(F32)
    power = [p['lmat'].astype(BF16) for p in pre]
    t_inv = [eye - p['lmat'] for p in pre]
    for _ in range(5):
        power = [_dot(lm, lm).astype(BF16) for lm in power]
        t_inv = [t + _dot(lm, t.astype(BF16)) for lm, t in zip(power, t_inv)]
    rhs = [_dot(t.astype(BF16), p['rhs'].astype(BF16)) for t, p in zip(t_inv, pre)]

    state = [state_ref[h] for h in range(GDN_HEADS)]
    for n in range(n_chunk):
        rows = slice(n * c, (n + 1) * c)
        sb = [s.astype(BF16) for s in state]
        ps = [pre[n * GDN_HEADS + h] for h in range(GDN_HEADS)]
        rs = [rhs[n * GDN_HEADS + h] for h in range(GDN_HEADS)]
        v_new = [(r[:, :hd] - _dot(r[:, hd:].astype(BF16), s)).astype(BF16) for r, s in zip(rs, sb)]
        outs = [_dot(p['q_dec'], s) + _dot(p['attn'], vn) for p, s, vn in zip(ps, sb, v_new)]
        state = [s * p['d_last'] + _dot_tn(p['k_dec'], vn) for p, s, vn in zip(ps, state, v_new)]
        for h, o in enumerate(outs):
            o = o * lax.rsqrt(jnp.mean(o * o, axis=-1, keepdims=True) + RMS_EPS) * gain_ref[...]
            zz = z_ref[0, rows, h * hd:(h + 1) * hd]
            y_ref[0, rows, h * hd:(h + 1) * hd] = o * (zz * jax.nn.sigmoid(zz))
    for h in range(GDN_HEADS):
        state_ref[h] = state[h]


def gdn_mixer(u3, col_qkv, col_z, col_small, conv_w, a_log, dt_bias, out_gain):
    b, s, _ = u3.shape
    lane = jnp.arange(SMALL_LANES)
    in_decay = (lane >= LANE_DECAY) & (lane < LANE_DECAY + GDN_HEADS)
    idx = jnp.clip(lane - LANE_DECAY, 0, GDN_HEADS - 1)
    coef = jnp.stack([jnp.where(in_decay, -jnp.exp(a_log)[idx], 0.0), jnp.where(in_decay, dt_bias[idx], 0.0)])
    qkv_w = 3 * GDN_DIM
    assert col_qkv % qkv_w == 0 and col_z % GDN_DIM == 0 and col_small % SMALL_LANES == 0
    full = lambda r, cc: pl.BlockSpec((r, cc), lambda i, t: (0, 0))
    return pl.pallas_call(
        _gdn_kernel,
        grid=(b, s // GDN_TILE),
        in_specs=[pl.BlockSpec((1, GDN_TILE, qkv_w), lambda i, t: (i, t, col_qkv // qkv_w)),
                  pl.BlockSpec((1, GDN_TILE, GDN_DIM), lambda i, t: (i, t, col_z // GDN_DIM)),
                  pl.BlockSpec((1, GDN_TILE, SMALL_LANES), lambda i, t: (i, t, col_small // SMALL_LANES)),
                  full(GDN_CONV, qkv_w), full(2, SMALL_LANES), full(1, GDN_HEAD_DIM)],
        out_specs=pl.BlockSpec((1, GDN_TILE, GDN_DIM), lambda i, t: (i, t, 0)),
        out_shape=jax.ShapeDtypeStruct((b, s, GDN_DIM), F32),
        scratch_shapes=[pltpu.VMEM((GDN_TILE + GDN_HALO, qkv_w), F32), pltpu.VMEM((GDN_HEADS, GDN_HEAD_DIM, GDN_HEAD_DIM), F32)],
        compiler_params=pltpu.CompilerParams(dimension_semantics=("arbitrary", "arbitrary")),
        name="gdn_mixer",
    )(u3, u3, u3, conv_w, coef, out_gain.reshape(1, GDN_HEAD_DIM))


CONV_TILE = 512


def _short_conv_kernel(u_ref, cw_ref, y_ref, xe_ref):
    tile, cd = CONV_TILE, CONV_DIM

    @pl.when(pl.program_id(1) == 0)
    def _():
        xe_ref[0:GDN_HALO, :] = jnp.zeros((GDN_HALO, cd), F32)

    @pl.when(pl.program_id(1) > 0)
    def _():
        xe_ref[0:GDN_HALO, :] = xe_ref[tile:tile + GDN_HALO, :]

    xe_ref[GDN_HALO:, :] = u_ref[0, :, cd:2 * cd] * u_ref[0, :, 2 * cd:3 * cd]
    conv = jnp.zeros((tile, cd), F32)
    for j in range(CONV_WIDTH):
        conv += cw_ref[j:j + 1, :] * xe_ref[pl.ds(GDN_HALO - (CONV_WIDTH - 1) + j, tile), :]
    y_ref[0] = u_ref[0, :, 0:cd] * conv


def short_conv(u3, conv_w):
    b, s, _ = u3.shape
    return pl.pallas_call(
        _short_conv_kernel,
        grid=(b, s // CONV_TILE),
        in_specs=[pl.BlockSpec((1, CONV_TILE, 3 * CONV_DIM), lambda i, t: (i, t, 0)),
                  pl.BlockSpec((CONV_WIDTH, CONV_DIM), lambda i, t: (0, 0))],
        out_specs=pl.BlockSpec((1, CONV_TILE, CONV_DIM), lambda i, t: (i, t, 0)),
        out_shape=jax.ShapeDtypeStruct((b, s, CONV_DIM), F32),
        scratch_shapes=[pltpu.VMEM((CONV_TILE + GDN_HALO, CONV_DIM), F32)],
        compiler_params=pltpu.CompilerParams(dimension_semantics=("arbitrary", "arbitrary")),
        name="short_conv",
    )(u3, conv_w)


def _merge_kernel(x_ref, g_ref, ya_ref, yn_ref, yg_ref, wb_ref, wg_ref, bg_ref, wo_ref, o_ref):
    x = x_ref[...]
    d = x.shape[1]
    h = (x * lax.rsqrt(jnp.mean(x * x, axis=-1, keepdims=True) + RMS_EPS) * g_ref[...]).astype(BF16)
    merged = jnp.zeros(x.shape, F32)
    for r, y_ref in enumerate((ya_ref, yn_ref, yg_ref)):
        gate = jax.nn.sigmoid(_dot(h, wg_ref[:, r * d:(r + 1) * d]) + bg_ref[:, r * d:(r + 1) * d])
        merged += gate * _dot(y_ref[...].astype(BF16), wb_ref[r])
    o_ref[...] = x + _dot(merged.astype(BF16), wo_ref[...])


def merge_branches(x2d, gain, y_a, y_n, y_g, w_branch, w_gate, b_gate, w_out, tm=512):
    t, d = x2d.shape
    row = lambda c: pl.BlockSpec((tm, c), lambda i: (i, 0))
    full = lambda *shape: pl.BlockSpec(shape, lambda i: (0,) * len(shape))
    return pl.pallas_call(
        _merge_kernel,
        grid=(t // tm,),
        in_specs=[row(d), full(1, d), row(BRANCH_DIM), row(BRANCH_DIM), row(BRANCH_DIM),
                  full(N_BRANCH, BRANCH_DIM, d), full(d, N_BRANCH * d), full(1, N_BRANCH * d), full(d, d)],
        out_specs=row(d),
        out_shape=jax.ShapeDtypeStruct((t, d), F32),
        compiler_params=pltpu.CompilerParams(dimension_semantics=("arbitrary",), vmem_limit_bytes=56 * 1024 * 1024),
        name="merge_branches",
    )(x2d, gain.reshape(1, d), y_a, y_n, y_g, w_branch.astype(BF16), w_gate.astype(BF16), b_gate.reshape(1, -1), w_out.astype(BF16))


def hybrid_mixer(x2d, b, s, norm_gain, w_in, conv_a_w, nsa_qk_gain, cmp_pe, cmp_w1, cmp_w2, gdn_conv_w, gdn_a_log, gdn_dt_bias, gdn_out_gain, w_branch, w_gate, b_gate, w_out):
    t = b * s
    u3 = norm_proj(x2d, norm_gain, permute_in_proj(w_in)).reshape(b, s, D_U)
    y_a = short_conv(u3, conv_a_w)
    y_n = nsa_mixer_pallas(u3, nsa_qk_gain, cmp_pe, cmp_w1, cmp_w2, COL_NSA_Q, COL_NSA_KV, COL_SMALL)
    y_g = gdn_mixer(u3, COL_GDN_QKV, COL_GDN_Z, COL_SMALL, gdn_conv_w, gdn_a_log, gdn_dt_bias, gdn_out_gain)
    return merge_branches(x2d, norm_gain, y_a.reshape(t, BRANCH_DIM), y_n.reshape(t, BRANCH_DIM), y_g.reshape(t, BRANCH_DIM), w_branch, w_gate, b_gate, w_out)


MOE_ROWS = 512
ROUTE_LANES = 128
N_ROUTER = MOE_GROUPS + N_EXPERTS
ROW_DMA_UNROLL = 8


def _moe_route_kernel(x_ref, g_ref, wr_ref, br_ref, tri_ref, h_ref, route_ref, cnt_ref, run_ref):
    @pl.when(pl.program_id(0) == 0)
    def _():
        run_ref[...] = jnp.zeros_like(run_ref)

    x = x_ref[...]
    h = x * lax.rsqrt(jnp.mean(x * x, axis=-1, keepdims=True) + RMS_EPS) * g_ref[...]
    bits = lax.bitcast_convert_type(h.astype(BF16).astype(F32), jnp.uint32)
    half = h.shape[1] // 2
    packed = (bits[:, half:] & jnp.uint32(0xFFFF0000)) | (bits[:, :half] >> 16)
    h_ref[...] = packed.reshape(h_ref.shape)
    logits = _dot(h.astype(BF16), wr_ref[...]) + br_ref[...]
    lane = lax.broadcasted_iota(jnp.int32, logits.shape, 1)
    first_of = lambda hit: jnp.min(jnp.where(hit, lane, ROUTE_LANES), axis=-1, keepdims=True)
    is_grp = lane < MOE_GROUPS
    lg = jnp.where(is_grp, logits, NEG_INF)
    m_g = jnp.max(lg, axis=-1, keepdims=True)
    grp = first_of(lg == m_g)
    p_grp = 1.0 / jnp.sum(jnp.where(is_grp, jnp.exp(lg - m_g), 0.0), axis=-1, keepdims=True)
    lo = MOE_GROUPS + grp * EXPERTS_PER_GROUP
    le = jnp.where((lane >= lo) & (lane < lo + EXPERTS_PER_GROUP), logits, NEG_INF)
    m1 = jnp.max(le, axis=-1, keepdims=True)
    i1 = first_of(le == m1)
    le2 = jnp.where(lane == i1, NEG_INF, le)
    m2 = jnp.max(le2, axis=-1, keepdims=True)
    i2 = first_of(le2 == m2)
    r = jnp.exp(m2 - m1)
    g1 = p_grp / (1.0 + r)
    g2 = p_grp * r / (1.0 + r)
    e1 = i1 - MOE_GROUPS
    e2 = i2 - MOE_GROUPS
    hit1 = lane == e1
    hit2 = lane == e2
    onehot = (hit1 | hit2).astype(BF16)
    before = _dot(tri_ref[...], onehot) + run_ref[...]
    r1 = jnp.sum(jnp.where(hit1, before, 0.0), axis=-1, keepdims=True)
    r2 = jnp.sum(jnp.where(hit2, before, 0.0), axis=-1, keepdims=True)
    run_ref[...] += jnp.sum(onehot.astype(F32), axis=0, keepdims=True)
    rec = jnp.zeros(logits.shape, F32)
    for k, v in enumerate((e1.astype(F32), e2.astype(F32), r1, r2, g1, g2)):
        rec = jnp.where(lane == k, v, rec)
    route_ref[...] = rec
    cnt_ref[...] = jnp.broadcast_to(run_ref[...], cnt_ref.shape)


def moe_route(x2d, gain, w_rg, b_rg, w_re, b_re, tm=512):
    t, d = x2d.shape
    pad = ROUTE_LANES - N_ROUTER
    wr = jnp.pad(jnp.concatenate([w_rg, w_re], axis=1), ((0, 0), (0, pad))).astype(BF16)
    br = jnp.pad(jnp.concatenate([b_rg, b_re]), (0, pad)).reshape(1, ROUTE_LANES)
    tri = (jnp.arange(tm)[:, None] > jnp.arange(tm)[None, :]).astype(BF16)
    full = lambda r, c: pl.BlockSpec((r, c), lambda i: (0, 0))
    return pl.pallas_call(
        _moe_route_kernel,
        grid=(t // tm,),
        in_specs=[pl.BlockSpec((tm, d), lambda i: (i, 0)), full(1, d), full(d, ROUTE_LANES), full(1, ROUTE_LANES), full(tm, tm)],
        out_specs=[pl.BlockSpec((tm, 1, d // 2), lambda i: (i, 0, 0)), pl.BlockSpec((tm, ROUTE_LANES), lambda i: (i, 0)), full(8, ROUTE_LANES)],
        out_shape=[jax.ShapeDtypeStruct((t, 1, d // 2), jnp.uint32), jax.ShapeDtypeStruct((t, ROUTE_LANES), F32), jax.ShapeDtypeStruct((8, ROUTE_LANES), F32)],
        scratch_shapes=[pltpu.VMEM((1, ROUTE_LANES), F32)],
        compiler_params=pltpu.CompilerParams(dimension_semantics=("arbitrary",)),
        name="moe_route",
    )(x2d, gain.reshape(1, d), wr, br, tri)


def _row_copy(src_ref, src_row, dst_ref, dst_row, sem):
    return pltpu.make_async_copy(src_ref.at[src_row], dst_ref.at[dst_row], sem)


def _moe_dispatch_kernel(dest_ref, h_ref, buf_in_ref, buf_ref, sem):
    del buf_in_ref
    tm = h_ref.shape[0]
    base = pl.program_id(0) * tm

    def send(r, carry):
        for k in range(TOPK_IN_GROUP):
            _row_copy(h_ref, r, buf_ref, dest_ref[(base + r) * TOPK_IN_GROUP + k], sem).start(priority=k)
        return carry

    lax.fori_loop(0, tm, send, 0, unroll=ROW_DMA_UNROLL)
    for k in range(TOPK_IN_GROUP):
        pltpu.make_async_copy(h_ref, buf_ref.at[pl.ds(0, tm)], sem).wait()


def moe_dispatch(dest, h3, n_rows, tm=512):
    t, _, d = h3.shape
    grid_spec = pltpu.PrefetchScalarGridSpec(
        num_scalar_prefetch=1, grid=(t // tm,),
        in_specs=[pl.BlockSpec((tm, 1, d), lambda i, dest: (i, 0, 0)), pl.BlockSpec(memory_space=pl.ANY)],
        out_specs=pl.BlockSpec(memory_space=pl.ANY),
        scratch_shapes=[pltpu.SemaphoreType.DMA(())],
    )
    return pl.pallas_call(
        _moe_dispatch_kernel, grid_spec=grid_spec,
        out_shape=jax.ShapeDtypeStruct((n_rows, 1, d), h3.dtype),
        input_output_aliases={2: 0},
        compiler_params=pltpu.CompilerParams(dimension_semantics=("arbitrary",), has_side_effects=True),
        name="moe_dispatch",
    )(dest, h3, jnp.zeros((n_rows, 1, d), h3.dtype))


def _moe_ffn_kernel(blk_e_ref, n_used_ref, x_ref, wg_ref, wu_ref, wd_ref, y_ref):
    del blk_e_ref
    used = pl.program_id(0) < n_used_ref[0]

    @pl.when(used)
    def _():
        w = x_ref[...].reshape(x_ref.shape[0], x_ref.shape[2])
        lo = lax.bitcast_convert_type(w << 16, F32)
        hi = lax.bitcast_convert_type(w & jnp.uint32(0xFFFF0000), F32)
        xb = jnp.concatenate([lo, hi], axis=1).astype(BF16)
        a = _dot(xb, wg_ref[0])
        mid = a * jax.nn.sigmoid(a) * _dot(xb, wu_ref[0])
        y_ref[...] = _dot(mid.astype(BF16), wd_ref[0]).reshape(y_ref.shape)

    @pl.when(jnp.logical_not(used))
    def _():
        y_ref[...] = jnp.zeros_like(y_ref)


def moe_ffn(blk_expert, n_used, buf3, w_eg, w_eu, w_ed):
    n_rows, _, packed_w = buf3.shape
    d, ff = w_eg.shape[1:]
    w_in_spec = pl.BlockSpec((1, d, ff), lambda b, blk_e, n_used: (blk_e[b], 0, 0))
    grid_spec = pltpu.PrefetchScalarGridSpec(
        num_scalar_prefetch=2, grid=(n_rows // MOE_ROWS,),
        in_specs=[pl.BlockSpec((MOE_ROWS, 1, packed_w), lambda b, blk_e, n_used: (b, 0, 0)), w_in_spec, w_in_spec,
                  pl.BlockSpec((1, ff, d), lambda b, blk_e, n_used: (blk_e[b], 0, 0))],
        out_specs=pl.BlockSpec((MOE_ROWS, 1, d), lambda b, blk_e, n_used: (b, 0, 0)),
    )
    return pl.pallas_call(
        _moe_ffn_kernel, grid_spec=grid_spec,
        out_shape=jax.ShapeDtypeStruct((n_rows, 1, d), F32),
        compiler_params=pltpu.CompilerParams(dimension_semantics=("arbitrary",)),
        name="moe_ffn",
    )(blk_expert, n_used, buf3, w_eg.astype(BF16), w_eu.astype(BF16), w_ed.astype(BF16))


def _moe_combine_kernel(dest_ref, y_ref, rec_ref, x_ref, out_ref, ya_ref, yb_ref, sem):
    tm = out_ref.shape[0]
    base = pl.program_id(0) * tm

    def fetch(r, carry):
        _row_copy(y_ref, dest_ref[(base + r) * TOPK_IN_GROUP], ya_ref, r, sem).start(priority=0)
        _row_copy(y_ref, dest_ref[(base + r) * TOPK_IN_GROUP + 1], yb_ref, r, sem).start(priority=1)
        return carry

    lax.fori_loop(0, tm, fetch, 0, unroll=ROW_DMA_UNROLL)
    pltpu.make_async_copy(y_ref.at[pl.ds(0, tm)], ya_ref, sem).wait()
    pltpu.make_async_copy(y_ref.at[pl.ds(0, tm)], yb_ref, sem).wait()
    rec = rec_ref[...]
    ya = ya_ref[...].reshape(out_ref.shape)
    yb = yb_ref[...].reshape(out_ref.shape)
    out_ref[...] = x_ref[...] + rec[:, 4:5] * ya + rec[:, 5:6] * yb


def moe_combine(dest, y3, rec, x2d, tm=512):
    t, d = x2d.shape
    grid_spec = pltpu.PrefetchScalarGridSpec(
        num_scalar_prefetch=1, grid=(t // tm,),
        in_specs=[pl.BlockSpec(memory_space=pl.ANY), pl.BlockSpec((tm, ROUTE_LANES), lambda i, dest: (i, 0)),
                  pl.BlockSpec((tm, d), lambda i, dest: (i, 0))],
        out_specs=pl.BlockSpec((tm, d), lambda i, dest: (i, 0)),
        scratch_shapes=[pltpu.VMEM((tm, 1, d), F32), pltpu.VMEM((tm, 1, d), F32), pltpu.SemaphoreType.DMA(())],
    )
    return pl.pallas_call(
        _moe_combine_kernel, grid_spec=grid_spec,
        out_shape=jax.ShapeDtypeStruct((t, d), F32),
        compiler_params=pltpu.CompilerParams(dimension_semantics=("arbitrary",)),
        name="moe_combine",
    )(dest, y3, rec, x2d)


def hier_moe_pallas(x2d, gain, w_rg, b_rg, w_re, b_re, w_eg, w_eu, w_ed):
    t, d = x2d.shape
    h, rec, cnt = moe_route(x2d, gain, w_rg, b_rg, w_re, b_re)
    counts = cnt[0, :N_EXPERTS].astype(jnp.int32)
    n_blk = (counts + MOE_ROWS - 1) // MOE_ROWS
    blk_end = jnp.cumsum(n_blk)
    pad_start = (blk_end - n_blk) * MOE_ROWS
    experts = rec[:, 0:2].astype(jnp.int32)
    dest = (pad_start[experts] + rec[:, 2:4].astype(jnp.int32)).reshape(-1)
    total_blk = t * TOPK_IN_GROUP // MOE_ROWS + N_EXPERTS
    blk_expert = jnp.minimum(jnp.searchsorted(blk_end, jnp.arange(total_blk), side='right'), N_EXPERTS - 1).astype(jnp.int32)
    buf = moe_dispatch(dest, h, total_blk * MOE_ROWS)
    y = moe_ffn(blk_expert, blk_end[-1:].astype(jnp.int32), buf, w_eg, w_eu, w_ed)
    return moe_combine(dest, y, rec, x2d)


def kernel(x, norm_mix, w_in, conv_a_w, nsa_qk_gain, cmp_pe, cmp_w1, cmp_w2, gdn_conv_w, gdn_a_log, gdn_dt_bias, gdn_out_gain, w_branch, w_gate, b_gate, w_out, norm_ffn, w_router_group, b_router_group, w_router_expert, b_router_expert, w_expert_gate, w_expert_up, w_expert_down):
    b, s, dm = x.shape
    x = x.reshape(b * s, dm)
    for l in range(DEPTH):
        x = hybrid_mixer(x, b, s, norm_mix[l], w_in[l], conv_a_w[l], nsa_qk_gain[l], cmp_pe[l], cmp_w1[l], cmp_w2[l], gdn_conv_w[l], gdn_a_log[l], gdn_dt_bias[l], gdn_out_gain[l], w_branch[l], w_gate[l], b_gate[l], w_out[l])
        x = hier_moe_pallas(x, norm_ffn[l], w_router_group[l], b_router_group[l], w_router_expert[l], b_router_expert[l], w_expert_gate[l], w_expert_up[l], w_expert_down[l])
    return x.reshape(b, s, dm)
```

```python
import functools
import math

import jax
import jax.numpy as jnp
from jax import lax
from jax.experimental import pallas as pl
from jax.experimental.pallas import tpu as pltpu

D_MODEL = 1024
DEPTH = 4
CONV_DIM = 512
CONV_WIDTH = 3
NSA_HEADS = 8
NSA_KV_GROUPS = 2
NSA_HPG = NSA_HEADS // NSA_KV_GROUPS
NSA_HEAD_DIM = 64
NSA_DIM = NSA_HEADS * NSA_HEAD_DIM
CMP_LEN = 32
CMP_STRIDE = 16
CMP_HIDDEN = 256
SLC_LEN = 64
SLC_TOPN = 8
SLC_FORCE_BONUS = 1e6
WINDOW = 512
NSA_QBLOCK = 64
GDN_HEADS = 4
GDN_HEAD_DIM = 128
GDN_DIM = GDN_HEADS * GDN_HEAD_DIM
GDN_CONV = 4
GDN_CHUNK = 64
N_BRANCH = 3
BRANCH_DIM = 512
IN_SPLITS = (3 * CONV_DIM, NSA_DIM, 6 * NSA_KV_GROUPS * NSA_HEAD_DIM, 3 * NSA_HEADS, 3 * GDN_DIM, GDN_DIM, GDN_HEADS, GDN_HEADS)
D_IN = sum(IN_SPLITS)
COL_CONV = 0
COL_GDN_QKV = 3 * CONV_DIM
COL_NSA_Q = COL_GDN_QKV + 3 * GDN_DIM
COL_GDN_Z = COL_NSA_Q + NSA_DIM
COL_NSA_KV = COL_GDN_Z + GDN_DIM
COL_SMALL = COL_NSA_KV + 6 * NSA_KV_GROUPS * NSA_HEAD_DIM
SMALL_LANES = 128
D_U = COL_SMALL + SMALL_LANES
LANE_BETA = 3 * NSA_HEADS
LANE_DECAY = LANE_BETA + GDN_HEADS


def permute_in_proj(w_in):
    conv, nq, nkv, ng, gqkv, gz, gb, ga = split_last(w_in, IN_SPLITS)
    pad = jnp.zeros((w_in.shape[0], SMALL_LANES - LANE_DECAY - GDN_HEADS), w_in.dtype)
    return jnp.concatenate([conv, gqkv, nq, gz, nkv, ng, gb, ga, pad], axis=1)


def split_last(u, sizes):
    out, start = [], 0
    for n in sizes:
        out.append(u[..., start:start + n])
        start += n
    return out
MOE_GROUPS = 4
EXPERTS_PER_GROUP = 8
N_EXPERTS = MOE_GROUPS * EXPERTS_PER_GROUP
TOPK_IN_GROUP = 2
EXPERT_FF = 512
MOE_BLOCK = 256
RMS_EPS = 1e-6
NEG_INF = -1e30

F32 = jnp.float32
BF16 = jnp.bfloat16


def _norm_proj_kernel(x_ref, g_ref, w_ref, o_ref, h_ref):
    @pl.when(pl.program_id(1) == 0)
    def _():
        x = x_ref[...]
        y = x * lax.rsqrt(jnp.mean(x * x, axis=-1, keepdims=True) + RMS_EPS)
        h_ref[...] = (y * g_ref[...]).astype(BF16)

    o_ref[...] = jnp.dot(h_ref[...], w_ref[...], preferred_element_type=F32)


def norm_proj(x2d, gain, w, tm=1024, tn=D_U // 3):
    t, d = x2d.shape
    n = w.shape[1]
    n_blk = pl.cdiv(n, tn)
    wb = jnp.pad(w.astype(BF16), ((0, 0), (0, n_blk * tn - n)))
    return pl.pallas_call(
        _norm_proj_kernel,
        grid=(t // tm, n_blk),
        in_specs=[
            pl.BlockSpec((tm, d), lambda i, j: (i, 0)),
            pl.BlockSpec((1, d), lambda i, j: (0, 0)),
            pl.BlockSpec((d, tn), lambda i, j: (0, j)),
        ],
        out_specs=pl.BlockSpec((tm, tn), lambda i, j: (i, j)),
        out_shape=jax.ShapeDtypeStruct((t, n), F32),
        scratch_shapes=[pltpu.VMEM((tm, d), BF16)],
        compiler_params=pltpu.CompilerParams(dimension_semantics=("arbitrary", "arbitrary")),
        name="norm_proj",
    )(x2d, gain.reshape(1, d), wb)


KV_LANES = NSA_KV_GROUPS * NSA_HEAD_DIM
N_WIN_KEYS = WINDOW + NSA_QBLOCK
FIRST_CHUNK = (8, 16)
REST_CHUNK = 8
IDS_PER_WORD = 4
SELECT_TILES = 4
M_INIT = -1e29


def _dot(a, b):
    return jnp.dot(a, b, preferred_element_type=F32)


def _dot_nt(a, b):
    return lax.dot_general(a, b, (((1,), (1,)), ((), ())), preferred_element_type=F32)


def _split_dot(x, m):
    hi = x.astype(BF16)
    lo = (x - hi.astype(F32)).astype(BF16)
    return _dot(hi, m) + _dot(lo, m)


def _head_slope(g, h):
    return 2.0 ** (-8.0 * (g * NSA_HPG + h + 1) / NSA_HEADS)


def _group_rms(x, bd, gain):
    ms = _split_dot(x * x, bd)
    return x * lax.rsqrt(ms + RMS_EPS) * gain


def _nsa_prep_kernel(uq_ref, uks_ref, uvs_ref, ukw_ref, uvw_ref, gq_ref, gks_ref, gkw_ref, bdq_ref, bdk_ref,
                     q_ref, ks_ref, vs_ref, kw_ref, vw_ref):
    scale = NSA_HEAD_DIM ** -0.5
    q_ref[...] = (_group_rms(uq_ref[...], bdq_ref[...], gq_ref[...]) * scale).astype(BF16)
    ks_ref[...] = _group_rms(uks_ref[...], bdk_ref[...], gks_ref[...]).astype(BF16)
    kw_ref[...] = _group_rms(ukw_ref[...], bdk_ref[...], gkw_ref[...]).astype(BF16)
    vs_ref[...] = uvs_ref[...].astype(BF16)
    vw_ref[...] = uvw_ref[...].astype(BF16)


def _block_diag_mean(n):
    i = jnp.arange(n) // NSA_HEAD_DIM
    return ((i[:, None] == i[None, :]).astype(F32) / NSA_HEAD_DIM).astype(BF16)


def nsa_prep(u, qk_gain, col_q, col_kv, tm=512):
    t = u.shape[0]
    assert t % tm == 0 and col_q % NSA_DIM == 0 and col_kv % KV_LANES == 0
    qb = col_q // NSA_DIM
    kb = col_kv // KV_LANES
    kv_spec = lambda j: pl.BlockSpec((tm, KV_LANES), lambda i, j=j: (i, kb + j))
    full = lambda r, c: pl.BlockSpec((r, c), lambda i: (0, 0))
    row = lambda c: pl.BlockSpec((tm, c), lambda i: (i, 0))
    gq = jnp.tile(qk_gain[0], NSA_HEADS).reshape(1, NSA_DIM)
    gks = jnp.tile(qk_gain[2], NSA_KV_GROUPS).reshape(1, KV_LANES)
    gkw = jnp.tile(qk_gain[3], NSA_KV_GROUPS).reshape(1, KV_LANES)
    return pl.pallas_call(
        _nsa_prep_kernel,
        grid=(t // tm,),
        in_specs=[pl.BlockSpec((tm, NSA_DIM), lambda i: (i, qb)), kv_spec(2), kv_spec(3), kv_spec(4), kv_spec(5),
                  full(1, NSA_DIM), full(1, KV_LANES), full(1, KV_LANES), full(NSA_DIM, NSA_DIM), full(KV_LANES, KV_LANES)],
        out_specs=[row(NSA_DIM), row(KV_LANES), row(KV_LANES), row(KV_LANES), row(KV_LANES)],
        out_shape=[jax.ShapeDtypeStruct((t, NSA_DIM), BF16)] + [jax.ShapeDtypeStruct((t, KV_LANES), BF16)] * 4,
        compiler_params=pltpu.CompilerParams(dimension_semantics=("arbitrary",)),
        name="nsa_prep",
    )(u, u, u, u, u, gq, gks, gkw, _block_diag_mean(NSA_DIM), _block_diag_mean(KV_LANES))


def _gelu_tanh(x):
    return 0.5 * x * (1.0 + jnp.tanh(0.7978845608028654 * (x + 0.044715 * x * x * x)))


def _nsa_compress_kernel(uk_ref, uv_ref, pe_ref, w1_ref, w2_ref, gk_ref, bd_ref, kc_ref, vc_ref):
    n_row = uk_ref.shape[1] // CMP_STRIDE
    d = NSA_HEAD_DIM
    for kv, (src, dst) in enumerate(((uk_ref, kc_ref), (uv_ref, vc_ref))):
        top = [jnp.zeros((n_row, CMP_HIDDEN), F32) for _ in range(NSA_KV_GROUPS)]
        bot = [jnp.zeros((n_row, CMP_HIDDEN), F32) for _ in range(NSA_KV_GROUPS)]
        for l in range(CMP_STRIDE):
            x2 = src[0, pl.ds(l, n_row, stride=CMP_STRIDE), :]
            l2 = l + CMP_STRIDE
            for g in range(NSA_KV_GROUPS):
                x = x2[:, g * d:(g + 1) * d]
                top[g] += _dot((x + pe_ref[kv, l:l + 1, :]).astype(BF16), w1_ref[kv, l * d:(l + 1) * d, :])
                bot[g] += _dot((x + pe_ref[kv, l2:l2 + 1, :]).astype(BF16), w1_ref[kv, l2 * d:(l2 + 1) * d, :])
        outs = []
        for g in range(NSA_KV_GROUPS):
            hid = top[g] + pltpu.roll(bot[g], n_row - 1, 0)
            outs.append(_dot(_gelu_tanh(hid).astype(BF16), w2_ref[kv]))
        y = jnp.concatenate(outs, axis=1)
        if kv == 0:
            y = _group_rms(y, bd_ref[...], gk_ref[...])
        dst[0] = y.astype(BF16)


def nsa_compress(u3, qk_gain, cmp_pe, cmp_w1, cmp_w2, col_kv):
    b, s, _ = u3.shape
    kb = col_kv // KV_LANES
    n_row = s // CMP_STRIDE
    full = lambda *shape: pl.BlockSpec(shape, lambda i: (0,) * len(shape))
    gk = jnp.tile(qk_gain[1], NSA_KV_GROUPS).reshape(1, KV_LANES)
    out_spec = pl.BlockSpec((1, n_row, KV_LANES), lambda i: (i, 0, 0))
    return pl.pallas_call(
        _nsa_compress_kernel,
        grid=(b,),
        in_specs=[pl.BlockSpec((1, s, KV_LANES), lambda i: (i, 0, kb)),
                  pl.BlockSpec((1, s, KV_LANES), lambda i: (i, 0, kb + 1)),
                  full(2, CMP_LEN, NSA_HEAD_DIM), full(2, CMP_LEN * NSA_HEAD_DIM, CMP_HIDDEN),
                  full(2, CMP_HIDDEN, NSA_HEAD_DIM), full(1, KV_LANES), full(KV_LANES, KV_LANES)],
        out_specs=[out_spec, out_spec],
        out_shape=[jax.ShapeDtypeStruct((b, n_row, KV_LANES), BF16)] * 2,
        compiler_params=pltpu.CompilerParams(dimension_semantics=("arbitrary",)),
        name="nsa_compress",
    )(u3, u3, cmp_pe, cmp_w1.astype(BF16), cmp_w2.astype(BF16), gk, _block_diag_mean(KV_LANES))


def _stack_heads(q, g):
    d = NSA_HEAD_DIM
    base = g * NSA_HPG * d
    return jnp.concatenate([q[:, base + h * d: base + (h + 1) * d] for h in range(NSA_HPG)], axis=0)


def _nsa_select_kernel(q_ref, kc_ref, vc_ref, ov_ref, oc_ref, sel_ref, flag_ref):
    qbl, d = NSA_QBLOCK, NSA_HEAD_DIM
    n_key = kc_ref.shape[1]
    n_slc = ov_ref.shape[1]
    row = lax.broadcasted_iota(jnp.int32, (qbl, n_key), 0)
    last = lax.broadcasted_iota(jnp.int32, (qbl, n_key), 1) * CMP_STRIDE + CMP_LEN - 1
    j = lax.broadcasted_iota(jnp.int32, (n_slc, qbl), 0)
    tiles = range(SELECT_TILES)
    qts = [pl.program_id(1) * SELECT_TILES + i for i in tiles]
    tile_rows = [slice(i * qbl, (i + 1) * qbl) for i in tiles]

    def importance(i):
        dist = (qts[i] * qbl + row - last).astype(F32)
        ok = dist >= 0
        q = q_ref[0, tile_rows[i], :]
        imps = []
        for g in range(NSA_KV_GROUPS):
            kc = kc_ref[0, :, g * d:(g + 1) * d]
            vc = vc_ref[0, :, g * d:(g + 1) * d]
            s_all = _dot_nt(_stack_heads(q, g), kc)
            p_sum = jnp.zeros((qbl, n_key), F32)
            ps = []
            for h in range(NSA_HPG):
                s = jnp.where(ok, s_all[h * qbl:(h + 1) * qbl] - _head_slope(g, h) * dist, NEG_INF)
                m = jnp.max(s, axis=-1, keepdims=True)
                e = jnp.where(ok, jnp.exp(s - m), 0.0)
                l = jnp.sum(e, axis=-1, keepdims=True)
                p = e * jnp.where(l > 0, 1.0 / l, 0.0)
                ps.append(p.astype(BF16))
                p_sum += p
            o_all = _dot(jnp.concatenate(ps, axis=0), vc)
            for h in range(NSA_HPG):
                col = (g * NSA_HPG + h) * d
                oc_ref[0, tile_rows[i], col:col + d] = o_all[h * qbl:(h + 1) * qbl]
            imps.append(_split_dot(p_sum, ov_ref[...]))
        return jnp.concatenate(imps, axis=1).T

    def top_blocks(i, imp_t):
        forced = (j == 0) | (j == qts[i]) | (j == qts[i] - 1)
        visible = j <= qts[i]
        sels = []
        for g in range(NSA_KV_GROUPS):
            score = jnp.where(visible, imp_t[g * n_slc:(g + 1) * n_slc] + jnp.where(forced, SLC_FORCE_BONUS, 0.0), NEG_INF)
            sel = jnp.zeros((n_slc, qbl), F32)
            for _ in range(min(SLC_TOPN, n_slc)):
                m = jnp.max(score, axis=0, keepdims=True)
                first = jnp.min(jnp.where(score == m, j, n_slc), axis=0, keepdims=True)
                pick = j == first
                sel = jnp.where(pick, 1.0, sel)
                score = jnp.where(pick, -3e38, score)
            sels.append(jnp.where(visible, sel, 0.0))
        sel_all = jnp.concatenate(sels, axis=0).T
        sel_ref[0, tile_rows[i], :] = sel_all.astype(BF16)
        flag_ref[0, i] = jnp.broadcast_to(jnp.max(sel_all, axis=0, keepdims=True), flag_ref.shape[2:])

    imp_ts = [importance(i) for i in tiles]
    for i in tiles:
        top_blocks(i, imp_ts[i])


def nsa_select(q3, k_cmp, v_cmp):
    b, s, _ = q3.shape
    n_qt = s // NSA_QBLOCK
    n_slc = s // SLC_LEN
    n_key = k_cmp.shape[1]
    c_lo = jnp.arange(n_key) * CMP_STRIDE
    j_lo = jnp.arange(n_slc) * SLC_LEN
    overlap = ((c_lo[:, None] < j_lo[None, :] + SLC_LEN) & (c_lo[:, None] + CMP_LEN > j_lo[None, :])).astype(BF16)
    rows = SELECT_TILES * NSA_QBLOCK
    assert s % rows == 0
    tile = lambda c: pl.BlockSpec((1, rows, c), lambda i, t: (i, t, 0))
    per_b = pl.BlockSpec((1, n_key, KV_LANES), lambda i, t: (i, 0, 0))
    return pl.pallas_call(
        _nsa_select_kernel,
        grid=(b, s // rows),
        in_specs=[tile(NSA_DIM), per_b, per_b, pl.BlockSpec((n_key, n_slc), lambda i, t: (0, 0))],
        out_specs=[tile(NSA_DIM), tile(2 * n_slc), pl.BlockSpec((1, SELECT_TILES, 8, 2 * n_slc), lambda i, t: (i, t, 0, 0))],
        out_shape=[jax.ShapeDtypeStruct((b, s, NSA_DIM), F32), jax.ShapeDtypeStruct((b, s, 2 * n_slc), BF16),
                   jax.ShapeDtypeStruct((b, n_qt, 8, 2 * n_slc), F32)],
        compiler_params=pltpu.CompilerParams(dimension_semantics=("arbitrary", "arbitrary")),
        name="nsa_select",
    )(q3, k_cmp, v_cmp, overlap)


def _nsa_attend_kernel(count_ref, list_ref, q_ref, ks_ref, vs_ref, kw_ref, vw_ref, sel_ref, oc_ref, gate_ref, out_ref,
                       ksel_ref, vsel_ref):
    bi, qt = pl.program_id(0), pl.program_id(1)
    n_qt = pl.num_programs(1)
    qbl, d = NSA_QBLOCK, NSA_HEAD_DIM
    n_slc = sel_ref.shape[2] // NSA_KV_GROUPS
    n_word = n_slc // IDS_PER_WORD
    q = q_ref[0]
    gates = jax.nn.sigmoid(gate_ref[0])

    def chunk_iotas(width):
        lane = lax.broadcasted_iota(jnp.int32, (1, width * SLC_LEN), 1)
        return dict(lane=lane, slot_of_lane=lane // SLC_LEN,
                    t_sel=qt * qbl + lax.broadcasted_iota(jnp.int32, (qbl, width * SLC_LEN), 0),
                    j_iota=lax.broadcasted_iota(jnp.int32, (n_slc, width * SLC_LEN), 0))

    iotas = {width: chunk_iotas(width) for width in set(FIRST_CHUNK + (REST_CHUNK,))}
    win_start = jnp.maximum(qt - WINDOW // qbl, 0) * qbl
    t_win = qt * qbl + lax.broadcasted_iota(jnp.int32, (qbl, N_WIN_KEYS), 0)
    dist_win = t_win - (win_start + lax.broadcasted_iota(jnp.int32, (qbl, N_WIN_KEYS), 1))
    ok_win = (dist_win >= 0) & (dist_win < WINDOW)
    dist_win = dist_win.astype(F32)

    groups = range(NSA_KV_GROUPS)
    lanes = [slice(g * d, (g + 1) * d) for g in groups]
    qs = [_stack_heads(q, g) for g in groups]
    tile_g = [(bi * n_qt + qt) * NSA_KV_GROUPS + g for g in groups]
    n_sel = [count_ref[tg] for tg in tile_g]
    sel_g = [sel_ref[0, :, g * n_slc:(g + 1) * n_slc] for g in groups]

    def sel_scores(first, width, g):
        io = iotas[width]
        word0 = tile_g[g] * n_word + first // IDS_PER_WORD
        j_row = jnp.full((1, width * SLC_LEN), -1, jnp.int32)
        for slot in range(width):
            valid = first + slot < n_sel[g]
            jb = (list_ref[word0 + slot // IDS_PER_WORD] >> (8 * (slot % IDS_PER_WORD))) & 0xFF
            rows = pl.ds(pl.multiple_of(jb * SLC_LEN, SLC_LEN), SLC_LEN)
            ksel_ref[g, slot * SLC_LEN:(slot + 1) * SLC_LEN, :] = ks_ref[0, rows, lanes[g]]
            vsel_ref[g, slot * SLC_LEN:(slot + 1) * SLC_LEN, :] = vs_ref[0, rows, lanes[g]]
            j_row = jnp.where(io['slot_of_lane'] == slot, jnp.where(valid, jb, -1), j_row)
        s_all = _dot_nt(qs[g], ksel_ref[g, 0:width * SLC_LEN, :])
        chosen = _dot(sel_g[g], (io['j_iota'] == j_row).astype(BF16))
        dist = io['t_sel'] - (j_row * SLC_LEN + io['lane'] % SLC_LEN)
        ok = (chosen > 0.5) & (dist >= 0)
        return s_all, ok, dist.astype(F32)

    def sel_softmax(scores, carry, g):
        s_all, ok, dist = scores
        m_old, l_old, acc = carry
        ps, ms, ls = [], [], []
        for h in range(NSA_HPG):
            rows_h = slice(h * qbl, (h + 1) * qbl)
            s = jnp.where(ok, s_all[rows_h] - _head_slope(g, h) * dist, NEG_INF)
            m_new = jnp.maximum(m_old[rows_h], jnp.max(s, axis=-1, keepdims=True))
            p = jnp.exp(s - m_new)
            alpha = jnp.exp(m_old[rows_h] - m_new)
            ls.append(alpha * l_old[rows_h] + jnp.sum(p, axis=-1, keepdims=True))
            ms.append(m_new)
            ps.append(p.astype(BF16))
        m_new = jnp.concatenate(ms, axis=0)
        alpha = jnp.exp(m_old - m_new)
        acc = alpha * acc + _dot(jnp.concatenate(ps, axis=0), vsel_ref[g, 0:s_all.shape[1], :])
        return m_new, jnp.concatenate(ls, axis=0), acc

    win_rows = pl.ds(pl.multiple_of(win_start, qbl), N_WIN_KEYS)
    s_win = [_dot_nt(qs[g], kw_ref[0, win_rows, lanes[g]]) for g in groups]
    sc0 = [sel_scores(0, FIRST_CHUNK[g], g) for g in groups]
    p_win, inv_win = [], []
    for g in groups:
        ps, inv = [], []
        for h in range(NSA_HPG):
            s = jnp.where(ok_win, s_win[g][h * qbl:(h + 1) * qbl] - _head_slope(g, h) * dist_win, NEG_INF)
            e = jnp.exp(s - jnp.max(s, axis=-1, keepdims=True))
            inv.append(1.0 / jnp.sum(e, axis=-1, keepdims=True))
            ps.append(e.astype(BF16))
        p_win.append(jnp.concatenate(ps, axis=0))
        inv_win.append(jnp.concatenate(inv, axis=0))
    init = (jnp.full((NSA_HPG * qbl, 1), M_INIT, F32), jnp.zeros((NSA_HPG * qbl, 1), F32),
            jnp.zeros((NSA_HPG * qbl, d), F32))
    carry = [sel_softmax(sc0[g], init, g) for g in groups]
    o_win = [_dot(p_win[g], vw_ref[0, win_rows, lanes[g]]) * inv_win[g] for g in groups]

    for g in groups:
        n_rest = (jnp.maximum(n_sel[g] - FIRST_CHUNK[g], 0) + REST_CHUNK - 1) // REST_CHUNK
        rest = lax.fori_loop(0, n_rest, lambda c, cr, g=g: sel_softmax(
            sel_scores(FIRST_CHUNK[g] + c * REST_CHUNK, REST_CHUNK, g), cr, g), carry[g])
        o_sel = rest[2] * (1.0 / rest[1])
        for h in range(NSA_HPG):
            hh = g * NSA_HPG + h
            col = hh * d
            rows_h = slice(h * qbl, (h + 1) * qbl)
            out_ref[0, :, col:col + d] = (gates[:, hh:hh + 1] * oc_ref[0, :, col:col + d]
                                          + gates[:, NSA_HEADS + hh:NSA_HEADS + hh + 1] * o_sel[rows_h]
                                          + gates[:, 2 * NSA_HEADS + hh:2 * NSA_HEADS + hh + 1] * o_win[g][rows_h])


def nsa_attend(counts, lists, q3, ks, vs, kw, vw, sel, o_cmp, u3, col_gate):
    b, s, _ = q3.shape
    n_qt = s // NSA_QBLOCK
    chunks = FIRST_CHUNK + (REST_CHUNK,)
    assert col_gate % 128 == 0 and s >= N_WIN_KEYS and all(c % IDS_PER_WORD == 0 for c in chunks)
    gb = col_gate // 128
    tile = lambda c: pl.BlockSpec((1, NSA_QBLOCK, c), lambda i, t, counts, lists: (i, t, 0))
    per_b = pl.BlockSpec((1, s, KV_LANES), lambda i, t, counts, lists: (i, 0, 0))
    grid_spec = pltpu.PrefetchScalarGridSpec(
        num_scalar_prefetch=2,
        grid=(b, n_qt),
        in_specs=[tile(NSA_DIM), per_b, per_b, per_b, per_b, tile(sel.shape[2]), tile(NSA_DIM),
                  pl.BlockSpec((1, NSA_QBLOCK, 128), lambda i, t, counts, lists: (i, t, gb))],
        out_specs=tile(NSA_DIM),
        scratch_shapes=[pltpu.VMEM((NSA_KV_GROUPS, max(chunks) * SLC_LEN, NSA_HEAD_DIM), BF16),
                        pltpu.VMEM((NSA_KV_GROUPS, max(chunks) * SLC_LEN, NSA_HEAD_DIM), BF16)],
    )
    return pl.pallas_call(
        _nsa_attend_kernel,
        grid_spec=grid_spec,
        out_shape=jax.ShapeDtypeStruct((b, s, NSA_DIM), F32),
        compiler_params=pltpu.CompilerParams(dimension_semantics=("arbitrary", "arbitrary")),
        name="nsa_attend",
    )(counts, lists, q3, ks, vs, kw, vw, sel, o_cmp, u3)


def _pack_union_lists(flags, n_slc):
    assert n_slc <= 256 and n_slc % IDS_PER_WORD == 0
    b, n_qt = flags.shape[:2]
    f = flags[:, :, 0, :].reshape(b, n_qt, NSA_KV_GROUPS, n_slc) > 0.5
    fi = f.astype(jnp.int32)
    counts = jnp.sum(fi, axis=-1)
    ids = jnp.arange(n_slc, dtype=jnp.int32)
    ahead = (fi.astype(F32) @ (ids[:, None] <= ids[None, :]).astype(F32)).astype(jnp.int32)
    pos = jnp.where(f, ahead - 1, counts[..., None] + ids - ahead)
    order = jnp.sum(jnp.where(pos[..., :, None] == ids, ids[:, None], 0), axis=-2)
    order = order.reshape(b, n_qt, NSA_KV_GROUPS, n_slc // IDS_PER_WORD, IDS_PER_WORD)
    words = jnp.sum(order << (8 * jnp.arange(IDS_PER_WORD, dtype=jnp.int32)), axis=-1, dtype=jnp.int32)
    return counts.reshape(-1), words.reshape(-1)


def nsa_mixer_pallas(u3, qk_gain, cmp_pe, cmp_w1, cmp_w2, col_q, col_kv, col_gate):
    b, s, d_in = u3.shape
    q, ks, vs, kw, vw = nsa_prep(u3.reshape(b * s, d_in), qk_gain, col_q, col_kv)
    k_cmp, v_cmp = nsa_compress(u3, qk_gain, cmp_pe, cmp_w1, cmp_w2, col_kv)
    q3 = q.reshape(b, s, NSA_DIM)
    r3 = lambda a: a.reshape(b, s, KV_LANES)
    o_cmp, sel, flags = nsa_select(q3, k_cmp, v_cmp)
    counts, lists = _pack_union_lists(flags, s // SLC_LEN)
    return nsa_attend(counts, lists, q3, r3(ks), r3(vs), r3(kw), r3(vw), sel, o_cmp, u3, col_gate)


GDN_TILE = 256
GDN_HALO = 8


def _dot3(a, b):
    ah = a.astype(BF16)
    bh = b.astype(BF16)
    al = (a - ah.astype(F32)).astype(BF16)
    bl = (b - bh.astype(F32)).astype(BF16)
    return _dot(ah, bh) + _dot(ah, bl) + _dot(al, bh)


def _dot_tn(a, b):
    return lax.dot_general(a, b, (((0,), (0,)), ((), ())), preferred_element_type=F32)


def _softplus(x):
    return jnp.maximum(x, 0.0) + jnp.log(1.0 + jnp.exp(-jnp.abs(x)))


def _l2_norm(x):
    return x * lax.rsqrt(jnp.sum(x * x, axis=-1, keepdims=True) + RMS_EPS)


def _gdn_kernel(qkv_ref, z_ref, small_ref, cw_ref, coef_ref, gain_ref, y_ref, xe_ref, state_ref):
    tt = pl.program_id(1)
    tile, c, hd = GDN_TILE, GDN_CHUNK, GDN_HEAD_DIM

    @pl.when(tt == 0)
    def _():
        xe_ref[0:GDN_HALO, :] = jnp.zeros((GDN_HALO, xe_ref.shape[1]), F32)
        state_ref[...] = jnp.zeros_like(state_ref)

    @pl.when(tt > 0)
    def _():
        xe_ref[0:GDN_HALO, :] = xe_ref[tile:tile + GDN_HALO, :]

    xe_ref[GDN_HALO:, :] = qkv_ref[0]
    conv = jnp.zeros((tile, xe_ref.shape[1]), F32)
    for j in range(GDN_CONV):
        conv += cw_ref[j:j + 1, :] * xe_ref[pl.ds(GDN_HALO - (GDN_CONV - 1) + j, tile), :]
    act = conv * jax.nn.sigmoid(conv)

    small = small_ref[0]
    beta_all = jax.nn.sigmoid(small)
    g_all = coef_ref[0:1, :] * _softplus(small + coef_ref[1:2, :])
    row = lax.broadcasted_iota(jnp.int32, (c, SMALL_LANES), 0)
    ri = lax.broadcasted_iota(jnp.int32, (c, c), 0)
    ci = lax.broadcasted_iota(jnp.int32, (c, c), 1)
    lower = ri >= ci
    strict = ri > ci

    n_chunk = tile // c
    pairs = [(n, h) for n in range(n_chunk) for h in range(GDN_HEADS)]
    gcs, gc_ts = [], []
    for n in range(n_chunk):
        gc = g_all[n * c:(n + 1) * c]
        shift = 1
        while shift < c:
            gc = gc + jnp.where(row >= shift, pltpu.roll(gc, shift, 0), 0.0)
            shift *= 2
        gcs.append(gc)
        gc_ts.append(gc.T)
    pre = []
    for n, h in pairs:
        rows = slice(n * c, (n + 1) * c)
        q = _l2_norm(act[rows, h * hd:(h + 1) * hd]) * hd ** -0.5
        k = _l2_norm(act[rows, GDN_DIM + h * hd:GDN_DIM + (h + 1) * hd])
        v = act[rows, 2 * GDN_DIM + h * hd:2 * GDN_DIM + (h + 1) * hd]
        beta = beta_all[rows, LANE_BETA + h:LANE_BETA + h + 1]
        gcol = gcs[n][:, LANE_DECAY + h:LANE_DECAY + h + 1]
        grow = gc_ts[n][LANE_DECAY + h:LANE_DECAY + h + 1, :]
        g_last = gcol[c - 1:c, :]
        decay = jnp.where(lower, jnp.exp(jnp.where(lower, gcol - grow, 0.0)), 0.0)
        kb = k * beta
        kh = k.astype(BF16)
        pre.append(dict(
            lmat=jnp.where(strict, _dot_nt(kb.astype(BF16), kh) * decay, 0.0),
            rhs=jnp.concatenate([v * beta, kb * jnp.exp(gcol)], axis=1),
            attn=(_dot_nt(q.astype(BF16), kh) * decay).astype(BF16),
            q_dec=(q * jnp.exp(gcol)).astype(BF16),
            k_dec=(k * jnp.exp(g_last - gcol)).astype(BF16),
            d_last=jnp.exp(g_last)))
    eye = (ri == ci).astype(F32)
    power = [p['lmat'].astype(BF16) for p in pre]
    t_inv = [eye - p['lmat'] for p in pre]
    for _ in range(5):
        power = [_dot(lm, lm).astype(BF16) for lm in power]
        t_inv = [t + _dot(lm, t.astype(BF16)) for lm, t in zip(power, t_inv)]
    rhs = [_dot(t.astype(BF16), p['rhs'].astype(BF16)) for t, p in zip(t_inv, pre)]

    state = [state_ref[h] for h in range(GDN_HEADS)]
    for n in range(n_chunk):
        rows = slice(n * c, (n + 1) * c)
        sb = [s.astype(BF16) for s in state]
        ps = [pre[n * GDN_HEADS + h] for h in range(GDN_HEADS)]
        rs = [rhs[n * GDN_HEADS + h] for h in range(GDN_HEADS)]
        v_new = [(r[:, :hd] - _dot(r[:, hd:].astype(BF16), s)).astype(BF16) for r, s in zip(rs, sb)]
        outs = [_dot(p['q_dec'], s) + _dot(p['attn'], vn) for p, s, vn in zip(ps, sb, v_new)]
        state = [s * p['d_last'] + _dot_tn(p['k_dec'], vn) for p, s, vn in zip(ps, state, v_new)]
        for h, o in enumerate(outs):
            o = o * lax.rsqrt(jnp.mean(o * o, axis=-1, keepdims=True) + RMS_EPS) * gain_ref[...]
            zz = z_ref[0, rows, h * hd:(h + 1) * hd]
            y_ref[0, rows, h * hd:(h + 1) * hd] = o * (zz * jax.nn.sigmoid(zz))
    for h in range(GDN_HEADS):
        state_ref[h] = state[h]


def gdn_mixer(u3, col_qkv, col_z, col_small, conv_w, a_log, dt_bias, out_gain):
    b, s, _ = u3.shape
    lane = jnp.arange(SMALL_LANES)
    in_decay = (lane >= LANE_DECAY) & (lane < LANE_DECAY + GDN_HEADS)
    idx = jnp.clip(lane - LANE_DECAY, 0, GDN_HEADS - 1)
    coef = jnp.stack([jnp.where(in_decay, -jnp.exp(a_log)[idx], 0.0), jnp.where(in_decay, dt_bias[idx], 0.0)])
    qkv_w = 3 * GDN_DIM
    assert col_qkv % qkv_w == 0 and col_z % GDN_DIM == 0 and col_small % SMALL_LANES == 0
    full = lambda r, cc: pl.BlockSpec((r, cc), lambda i, t: (0, 0))
    return pl.pallas_call(
        _gdn_kernel,
        grid=(b, s // GDN_TILE),
        in_specs=[pl.BlockSpec((1, GDN_TILE, qkv_w), lambda i, t: (i, t, col_qkv // qkv_w)),
                  pl.BlockSpec((1, GDN_TILE, GDN_DIM), lambda i, t: (i, t, col_z // GDN_DIM)),
                  pl.BlockSpec((1, GDN_TILE, SMALL_LANES), lambda i, t: (i, t, col_small // SMALL_LANES)),
                  full(GDN_CONV, qkv_w), full(2, SMALL_LANES), full(1, GDN_HEAD_DIM)],
        out_specs=pl.BlockSpec((1, GDN_TILE, GDN_DIM), lambda i, t: (i, t, 0)),
        out_shape=jax.ShapeDtypeStruct((b, s, GDN_DIM), F32),
        scratch_shapes=[pltpu.VMEM((GDN_TILE + GDN_HALO, qkv_w), F32), pltpu.VMEM((GDN_HEADS, GDN_HEAD_DIM, GDN_HEAD_DIM), F32)],
        compiler_params=pltpu.CompilerParams(dimension_semantics=("arbitrary", "arbitrary")),
        name="gdn_mixer",
    )(u3, u3, u3, conv_w, coef, out_gain.reshape(1, GDN_HEAD_DIM))


CONV_TILE = 512


def _short_conv_kernel(u_ref, cw_ref, y_ref, xe_ref):
    tile, cd = CONV_TILE, CONV_DIM

    @pl.when(pl.program_id(1) == 0)
    def _():
        xe_ref[0:GDN_HALO, :] = jnp.zeros((GDN_HALO, cd), F32)

    @pl.when(pl.program_id(1) > 0)
    def _():
        xe_ref[0:GDN_HALO, :] = xe_ref[tile:tile + GDN_HALO, :]

    xe_ref[GDN_HALO:, :] = u_ref[0, :, cd:2 * cd] * u_ref[0, :, 2 * cd:3 * cd]
    conv = jnp.zeros((tile, cd), F32)
    for j in range(CONV_WIDTH):
        conv += cw_ref[j:j + 1, :] * xe_ref[pl.ds(GDN_HALO - (CONV_WIDTH - 1) + j, tile), :]
    y_ref[0] = u_ref[0, :, 0:cd] * conv


def short_conv(u3, conv_w):
    b, s, _ = u3.shape
    return pl.pallas_call(
        _short_conv_kernel,
        grid=(b, s // CONV_TILE),
        in_specs=[pl.BlockSpec((1, CONV_TILE, 3 * CONV_DIM), lambda i, t: (i, t, 0)),
                  pl.BlockSpec((CONV_WIDTH, CONV_DIM), lambda i, t: (0, 0))],
        out_specs=pl.BlockSpec((1, CONV_TILE, CONV_DIM), lambda i, t: (i, t, 0)),
        out_shape=jax.ShapeDtypeStruct((b, s, CONV_DIM), F32),
        scratch_shapes=[pltpu.VMEM((CONV_TILE + GDN_HALO, CONV_DIM), F32)],
        compiler_params=pltpu.CompilerParams(dimension_semantics=("arbitrary", "arbitrary")),
        name="short_conv",
    )(u3, conv_w)


def _merge_kernel(x_ref, g_ref, ya_ref, yn_ref, yg_ref, wb_ref, wg_ref, bg_ref, wo_ref, o_ref):
    x = x_ref[...]
    d = x.shape[1]
    h = (x * lax.rsqrt(jnp.mean(x * x, axis=-1, keepdims=True) + RMS_EPS) * g_ref[...]).astype(BF16)
    merged = jnp.zeros(x.shape, F32)
    for r, y_ref in enumerate((ya_ref, yn_ref, yg_ref)):
        gate = jax.nn.sigmoid(_dot(h, wg_ref[:, r * d:(r + 1) * d]) + bg_ref[:, r * d:(r + 1) * d])
        merged += gate * _dot(y_ref[...].astype(BF16), wb_ref[r])
    o_ref[...] = x + _dot(merged.astype(BF16), wo_ref[...])


def merge_branches(x2d, gain, y_a, y_n, y_g, w_branch, w_gate, b_gate, w_out, tm=512):
    t, d = x2d.shape
    row = lambda c: pl.BlockSpec((tm, c), lambda i: (i, 0))
    full = lambda *shape: pl.BlockSpec(shape, lambda i: (0,) * len(shape))
    return pl.pallas_call(
        _merge_kernel,
        grid=(t // tm,),
        in_specs=[row(d), full(1, d), row(BRANCH_DIM), row(BRANCH_DIM), row(BRANCH_DIM),
                  full(N_BRANCH, BRANCH_DIM, d), full(d, N_BRANCH * d), full(1, N_BRANCH * d), full(d, d)],
        out_specs=row(d),
        out_shape=jax.ShapeDtypeStruct((t, d), F32),
        compiler_params=pltpu.CompilerParams(dimension_semantics=("arbitrary",), vmem_limit_bytes=56 * 1024 * 1024),
        name="merge_branches",
    )(x2d, gain.reshape(1, d), y_a, y_n, y_g, w_branch.astype(BF16), w_gate.astype(BF16), b_gate.reshape(1, -1), w_out.astype(BF16))


def hybrid_mixer(x2d, b, s, norm_gain, w_in, conv_a_w, nsa_qk_gain, cmp_pe, cmp_w1, cmp_w2, gdn_conv_w, gdn_a_log, gdn_dt_bias, gdn_out_gain, w_branch, w_gate, b_gate, w_out):
    t = b * s
    u3 = norm_proj(x2d, norm_gain, permute_in_proj(w_in)).reshape(b, s, D_U)
    y_a = short_conv(u3, conv_a_w)
    y_n = nsa_mixer_pallas(u3, nsa_qk_gain, cmp_pe, cmp_w1, cmp_w2, COL_NSA_Q, COL_NSA_KV, COL_SMALL)
    y_g = gdn_mixer(u3, COL_GDN_QKV, COL_GDN_Z, COL_SMALL, gdn_conv_w, gdn_a_log, gdn_dt_bias, gdn_out_gain)
    return merge_branches(x2d, norm_gain, y_a.reshape(t, BRANCH_DIM), y_n.reshape(t, BRANCH_DIM), y_g.reshape(t, BRANCH_DIM), w_branch, w_gate, b_gate, w_out)


MOE_ROWS = 512
ROUTE_LANES = 128
N_ROUTER = MOE_GROUPS + N_EXPERTS
ROW_DMA_UNROLL = 8


def _moe_route_kernel(x_ref, g_ref, wr_ref, br_ref, tri_ref, h_ref, route_ref, cnt_ref, run_ref):
    @pl.when(pl.program_id(0) == 0)
    def _():
        run_ref[...] = jnp.zeros_like(run_ref)

    x = x_ref[...]
    h = x * lax.rsqrt(jnp.mean(x * x, axis=-1, keepdims=True) + RMS_EPS) * g_ref[...]
    bits = lax.bitcast_convert_type(h.astype(BF16).astype(F32), jnp.uint32)
    half = h.shape[1] // 2
    packed = (bits[:, half:] & jnp.uint32(0xFFFF0000)) | (bits[:, :half] >> 16)
    h_ref[...] = packed.reshape(h_ref.shape)
    logits = _dot(h.astype(BF16), wr_ref[...]) + br_ref[...]
    lane = lax.broadcasted_iota(jnp.int32, logits.shape, 1)
    first_of = lambda hit: jnp.min(jnp.where(hit, lane, ROUTE_LANES), axis=-1, keepdims=True)
    is_grp = lane < MOE_GROUPS
    lg = jnp.where(is_grp, logits, NEG_INF)
    m_g = jnp.max(lg, axis=-1, keepdims=True)
    grp = first_of(lg == m_g)
    p_grp = 1.0 / jnp.sum(jnp.where(is_grp, jnp.exp(lg - m_g), 0.0), axis=-1, keepdims=True)
    lo = MOE_GROUPS + grp * EXPERTS_PER_GROUP
    le = jnp.where((lane >= lo) & (lane < lo + EXPERTS_PER_GROUP), logits, NEG_INF)
    m1 = jnp.max(le, axis=-1, keepdims=True)
    i1 = first_of(le == m1)
    le2 = jnp.where(lane == i1, NEG_INF, le)
    m2 = jnp.max(le2, axis=-1, keepdims=True)
    i2 = first_of(le2 == m2)
    r = jnp.exp(m2 - m1)
    g1 = p_grp / (1.0 + r)
    g2 = p_grp * r / (1.0 + r)
    e1 = i1 - MOE_GROUPS
    e2 = i2 - MOE_GROUPS
    hit1 = lane == e1
    hit2 = lane == e2
    onehot = (hit1 | hit2).astype(BF16)
    before = _dot(tri_ref[...], onehot) + run_ref[...]
    r1 = jnp.sum(jnp.where(hit1, before, 0.0), axis=-1, keepdims=True)
    r2 = jnp.sum(jnp.where(hit2, before, 0.0), axis=-1, keepdims=True)
    run_ref[...] += jnp.sum(onehot.astype(F32), axis=0, keepdims=True)
    rec = jnp.zeros(logits.shape, F32)
    for k, v in enumerate((e1.astype(F32), e2.astype(F32), r1, r2, g1, g2)):
        rec = jnp.where(lane == k, v, rec)
    route_ref[...] = rec
    cnt_ref[...] = jnp.broadcast_to(run_ref[...], cnt_ref.shape)


def moe_route(x2d, gain, w_rg, b_rg, w_re, b_re, tm=512):
    t, d = x2d.shape
    pad = ROUTE_LANES - N_ROUTER
    wr = jnp.pad(jnp.concatenate([w_rg, w_re], axis=1), ((0, 0), (0, pad))).astype(BF16)
    br = jnp.pad(jnp.concatenate([b_rg, b_re]), (0, pad)).reshape(1, ROUTE_LANES)
    tri = (jnp.arange(tm)[:, None] > jnp.arange(tm)[None, :]).astype(BF16)
    full = lambda r, c: pl.BlockSpec((r, c), lambda i: (0, 0))
    return pl.pallas_call(
        _moe_route_kernel,
        grid=(t // tm,),
        in_specs=[pl.BlockSpec((tm, d), lambda i: (i, 0)), full(1, d), full(d, ROUTE_LANES), full(1, ROUTE_LANES), full(tm, tm)],
        out_specs=[pl.BlockSpec((tm, 1, d // 2), lambda i: (i, 0, 0)), pl.BlockSpec((tm, ROUTE_LANES), lambda i: (i, 0)), full(8, ROUTE_LANES)],
        out_shape=[jax.ShapeDtypeStruct((t, 1, d // 2), jnp.uint32), jax.ShapeDtypeStruct((t, ROUTE_LANES), F32), jax.ShapeDtypeStruct((8, ROUTE_LANES), F32)],
        scratch_shapes=[pltpu.VMEM((1, ROUTE_LANES), F32)],
        compiler_params=pltpu.CompilerParams(dimension_semantics=("arbitrary",)),
        name="moe_route",
    )(x2d, gain.reshape(1, d), wr, br, tri)


def _row_copy(src_ref, src_row, dst_ref, dst_row, sem):
    return pltpu.make_async_copy(src_ref.at[src_row], dst_ref.at[dst_row], sem)


def _moe_dispatch_kernel(dest_ref, h_ref, buf_in_ref, buf_ref, sem):
    del buf_in_ref
    tm = h_ref.shape[0]
    base = pl.program_id(0) * tm

    def send(r, carry):
        for k in range(TOPK_IN_GROUP):
            _row_copy(h_ref, r, buf_ref, dest_ref[(base + r) * TOPK_IN_GROUP + k], sem).start(priority=k)
        return carry

    lax.fori_loop(0, tm, send, 0, unroll=ROW_DMA_UNROLL)
    for k in range(TOPK_IN_GROUP):
        pltpu.make_async_copy(h_ref, buf_ref.at[pl.ds(0, tm)], sem).wait()


def moe_dispatch(dest, h3, n_rows, tm=512):
    t, _, d = h3.shape
    grid_spec = pltpu.PrefetchScalarGridSpec(
        num_scalar_prefetch=1, grid=(t // tm,),
        in_specs=[pl.BlockSpec((tm, 1, d), lambda i, dest: (i, 0, 0)), pl.BlockSpec(memory_space=pl.ANY)],
        out_specs=pl.BlockSpec(memory_space=pl.ANY),
        scratch_shapes=[pltpu.SemaphoreType.DMA(())],
    )
    return pl.pallas_call(
        _moe_dispatch_kernel, grid_spec=grid_spec,
        out_shape=jax.ShapeDtypeStruct((n_rows, 1, d), h3.dtype),
        input_output_aliases={2: 0},
        compiler_params=pltpu.CompilerParams(dimension_semantics=("arbitrary",), has_side_effects=True),
        name="moe_dispatch",
    )(dest, h3, jnp.zeros((n_rows, 1, d), h3.dtype))


def _moe_ffn_kernel(blk_e_ref, n_used_ref, x_ref, wg_ref, wu_ref, wd_ref, y_ref):
    del blk_e_ref
    used = pl.program_id(0) < n_used_ref[0]

    @pl.when(used)
    def _():
        w = x_ref[...].reshape(x_ref.shape[0], x_ref.shape[2])
        lo = lax.bitcast_convert_type(w << 16, F32)
        hi = lax.bitcast_convert_type(w & jnp.uint32(0xFFFF0000), F32)
        xb = jnp.concatenate([lo, hi], axis=1).astype(BF16)
        a = _dot(xb, wg_ref[0])
        mid = a * jax.nn.sigmoid(a) * _dot(xb, wu_ref[0])
        y_ref[...] = _dot(mid.astype(BF16), wd_ref[0]).reshape(y_ref.shape)

    @pl.when(jnp.logical_not(used))
    def _():
        y_ref[...] = jnp.zeros_like(y_ref)


def moe_ffn(blk_expert, n_used, buf3, w_eg, w_eu, w_ed):
    n_rows, _, packed_w = buf3.shape
    d, ff = w_eg.shape[1:]
    w_in_spec = pl.BlockSpec((1, d, ff), lambda b, blk_e, n_used: (blk_e[b], 0, 0))
    grid_spec = pltpu.PrefetchScalarGridSpec(
        num_scalar_prefetch=2, grid=(n_rows // MOE_ROWS,),
        in_specs=[pl.BlockSpec((MOE_ROWS, 1, packed_w), lambda b, blk_e, n_used: (b, 0, 0)), w_in_spec, w_in_spec,
                  pl.BlockSpec((1, ff, d), lambda b, blk_e, n_used: (blk_e[b], 0, 0))],
        out_specs=pl.BlockSpec((MOE_ROWS, 1, d), lambda b, blk_e, n_used: (b, 0, 0)),
    )
    return pl.pallas_call(
        _moe_ffn_kernel, grid_spec=grid_spec,
        out_shape=jax.ShapeDtypeStruct((n_rows, 1, d), F32),
        compiler_params=pltpu.CompilerParams(dimension_semantics=("arbitrary",)),
        name="moe_ffn",
    )(blk_expert, n_used, buf3, w_eg.astype(BF16), w_eu.astype(BF16), w_ed.astype(BF16))


def _moe_combine_kernel(dest_ref, y_ref, rec_ref, x_ref, out_ref, ya_ref, yb_ref, sem):
    tm = out_ref.shape[0]
    base = pl.program_id(0) * tm

    def fetch(r, carry):
        _row_copy(y_ref, dest_ref[(base + r) * TOPK_IN_GROUP], ya_ref, r, sem).start(priority=0)
        _row_copy(y_ref, dest_ref[(base + r) * TOPK_IN_GROUP + 1], yb_ref, r, sem).start(priority=1)
        return carry

    lax.fori_loop(0, tm, fetch, 0, unroll=ROW_DMA_UNROLL)
    pltpu.make_async_copy(y_ref.at[pl.ds(0, tm)], ya_ref, sem).wait()
    pltpu.make_async_copy(y_ref.at[pl.ds(0, tm)], yb_ref, sem).wait()
    rec = rec_ref[...]
    ya = ya_ref[...].reshape(out_ref.shape)
    yb = yb_ref[...].reshape(out_ref.shape)
    out_ref[...] = x_ref[...] + rec[:, 4:5] * ya + rec[:, 5:6] * yb


def moe_combine(dest, y3, rec, x2d, tm=512):
    t, d = x2d.shape
    grid_spec = pltpu.PrefetchScalarGridSpec(
        num_scalar_prefetch=1, grid=(t // tm,),
        in_specs=[pl.BlockSpec(memory_space=pl.ANY), pl.BlockSpec((tm, ROUTE_LANES), lambda i, dest: (i, 0)),
                  pl.BlockSpec((tm, d), lambda i, dest: (i, 0))],
        out_specs=pl.BlockSpec((tm, d), lambda i, dest: (i, 0)),
        scratch_shapes=[pltpu.VMEM((tm, 1, d), F32), pltpu.VMEM((tm, 1, d), F32), pltpu.SemaphoreType.DMA(())],
    )
    return pl.pallas_call(
        _moe_combine_kernel, grid_spec=grid_spec,
        out_shape=jax.ShapeDtypeStruct((t, d), F32),
        compiler_params=pltpu.CompilerParams(dimension_semantics=("arbitrary",)),
        name="moe_combine",
    )(dest, y3, rec, x2d)


def hier_moe_pallas(x2d, gain, w_rg, b_rg, w_re, b_re, w_eg, w_eu, w_ed):
    t, d = x2d.shape
    h, rec, cnt = moe_route(x2d, gain, w_rg, b_rg, w_re, b_re)
    counts = cnt[0, :N_EXPERTS].astype(jnp.int32)
    n_blk = (counts + MOE_ROWS - 1) // MOE_ROWS
    blk_end = jnp.cumsum(n_blk)
    pad_start = (blk_end - n_blk) * MOE_ROWS
    experts = rec[:, 0:2].astype(jnp.int32)
    dest = (pad_start[experts] + rec[:, 2:4].astype(jnp.int32)).reshape(-1)
    total_blk = t * TOPK_IN_GROUP // MOE_ROWS + N_EXPERTS
    blk_expert = jnp.sum(jnp.arange(total_blk)[:, None] >= blk_end[None, :], axis=1, dtype=jnp.int32)
    blk_expert = jnp.minimum(blk_expert, N_EXPERTS - 1)
    buf = moe_dispatch(dest, h, total_blk * MOE_ROWS)
    y = moe_ffn(blk_expert, blk_end[-1:].astype(jnp.int32), buf, w_eg, w_eu, w_ed)
    return moe_combine(dest, y, rec, x2d)


def kernel(x, norm_mix, w_in, conv_a_w, nsa_qk_gain, cmp_pe, cmp_w1, cmp_w2, gdn_conv_w, gdn_a_log, gdn_dt_bias, gdn_out_gain, w_branch, w_gate, b_gate, w_out, norm_ffn, w_router_group, b_router_group, w_router_expert, b_router_expert, w_expert_gate, w_expert_up, w_expert_down):
    b, s, dm = x.shape
    x = x.reshape(b * s, dm)
    for l in range(DEPTH):
        x = hybrid_mixer(x, b, s, norm_mix[l], w_in[l], conv_a_w[l], nsa_qk_gain[l], cmp_pe[l], cmp_w1[l], cmp_w2[l], gdn_conv_w[l], gdn_a_log[l], gdn_dt_bias[l], gdn_out_gain[l], w_branch[l], w_gate[l], b_gate[l], w_out[l])
        x = hier_moe_pallas(x, norm_ffn[l], w_router_group[l], b_router_group[l], w_router_expert[l], b_router_expert[l], w_expert_gate[l], w_expert_up[l], w_expert_down[l])
    return x.reshape(b, s, dm)
```

```python
import functools

import jax
import jax.numpy as jnp
from jax import lax
from jax.experimental import pallas as pl
from jax.experimental.pallas import tpu as pltpu

D_MODEL = 1024
DEPTH = 4
CONV_DIM = 512
CONV_WIDTH = 3
NSA_HEADS = 8
NSA_KV_GROUPS = 2
NSA_HPG = NSA_HEADS // NSA_KV_GROUPS
NSA_HEAD_DIM = 64
NSA_DIM = NSA_HEADS * NSA_HEAD_DIM
CMP_LEN = 32
CMP_STRIDE = 16
CMP_HIDDEN = 256
SLC_LEN = 64
SLC_TOPN = 8
SLC_FORCE_BONUS = 1e6
WINDOW = 512
NSA_QBLOCK = 64
GDN_HEADS = 4
GDN_HEAD_DIM = 128
GDN_DIM = GDN_HEADS * GDN_HEAD_DIM
GDN_CONV = 4
GDN_CHUNK = 64
N_BRANCH = 3
BRANCH_DIM = 512
IN_SPLITS = (3 * CONV_DIM, NSA_DIM, 6 * NSA_KV_GROUPS * NSA_HEAD_DIM, 3 * NSA_HEADS, 3 * GDN_DIM, GDN_DIM, GDN_HEADS, GDN_HEADS)
D_IN = sum(IN_SPLITS)
COL_CONV = 0
COL_GDN_QKV = 3 * CONV_DIM
COL_NSA_Q = COL_GDN_QKV + 3 * GDN_DIM
COL_GDN_Z = COL_NSA_Q + NSA_DIM
COL_NSA_KV = COL_GDN_Z + GDN_DIM
COL_SMALL = COL_NSA_KV + 6 * NSA_KV_GROUPS * NSA_HEAD_DIM
SMALL_LANES = 128
D_U = COL_SMALL + SMALL_LANES
LANE_BETA = 3 * NSA_HEADS
LANE_DECAY = LANE_BETA + GDN_HEADS


def permute_in_proj(w_in):
    conv, nq, nkv, ng, gqkv, gz, gb, ga = split_last(w_in, IN_SPLITS)
    pad = jnp.zeros((w_in.shape[0], SMALL_LANES - LANE_DECAY - GDN_HEADS), w_in.dtype)
    return jnp.concatenate([conv, gqkv, nq, gz, nkv, ng, gb, ga, pad], axis=1)


def split_last(u, sizes):
    out, start = [], 0
    for n in sizes:
        out.append(u[..., start:start + n])
        start += n
    return out
MOE_GROUPS = 4
EXPERTS_PER_GROUP = 8
N_EXPERTS = MOE_GROUPS * EXPERTS_PER_GROUP
TOPK_IN_GROUP = 2
EXPERT_FF = 512
RMS_EPS = 1e-6
NEG_INF = -1e30

F32 = jnp.float32
BF16 = jnp.bfloat16
VMEM_LIMIT_BIG = 56 * 1024 * 1024


def _norm_proj_kernel(x_ref, g_ref, w_ref, o_ref, h_ref):
    @pl.when(pl.program_id(1) == 0)
    def _():
        x = x_ref[...]
        y = x * lax.rsqrt(jnp.mean(x * x, axis=-1, keepdims=True) + RMS_EPS)
        h_ref[...] = (y * g_ref[...]).astype(BF16)

    o_ref[...] = jnp.dot(h_ref[...], w_ref[...], preferred_element_type=F32)


def norm_proj(x2d, gain, w, tm=512, tn=D_U):
    t, d = x2d.shape
    n = w.shape[1]
    n_blk = pl.cdiv(n, tn)
    wb = jnp.pad(w.astype(BF16), ((0, 0), (0, n_blk * tn - n)))
    return pl.pallas_call(
        _norm_proj_kernel,
        grid=(t // tm, n_blk),
        in_specs=[
            pl.BlockSpec((tm, d), lambda i, j: (i, 0)),
            pl.BlockSpec((1, d), lambda i, j: (0, 0)),
            pl.BlockSpec((d, tn), lambda i, j: (0, j)),
        ],
        out_specs=pl.BlockSpec((tm, tn), lambda i, j: (i, j)),
        out_shape=jax.ShapeDtypeStruct((t, n), F32),
        scratch_shapes=[pltpu.VMEM((tm, d), BF16)],
        compiler_params=pltpu.CompilerParams(dimension_semantics=("arbitrary", "arbitrary"),
                                             vmem_limit_bytes=VMEM_LIMIT_BIG),
        name="norm_proj",
    )(x2d, gain.reshape(1, d), wb)


KV_LANES = NSA_KV_GROUPS * NSA_HEAD_DIM
N_WIN_KEYS = WINDOW + NSA_QBLOCK
FIRST_CHUNK = (8, 12)
REST_CHUNK = 8
IDS_PER_WORD = 4
SELECT_TILES = 4
M_INIT = -1e29


def _dot(a, b):
    return jnp.dot(a, b, preferred_element_type=F32)


def _dot_nt(a, b):
    return lax.dot_general(a, b, (((1,), (1,)), ((), ())), preferred_element_type=F32)


def _split_dot(x, m):
    hi = x.astype(BF16)
    lo = (x - hi.astype(F32)).astype(BF16)
    return _dot(hi, m) + _dot(lo, m)


def _head_slope(g, h):
    return 2.0 ** (-8.0 * (g * NSA_HPG + h + 1) / NSA_HEADS)


def _group_rms(x, bd, gain):
    ms = _split_dot(x * x, bd)
    return x * lax.rsqrt(ms + RMS_EPS) * gain


def _nsa_prep_kernel(uq_ref, uks_ref, uvs_ref, ukw_ref, uvw_ref, gq_ref, gks_ref, gkw_ref, bdq_ref, bdk_ref,
                     q_ref, ks_ref, vs_ref, kw_ref, vw_ref):
    scale = NSA_HEAD_DIM ** -0.5
    q_ref[...] = (_group_rms(uq_ref[...], bdq_ref[...], gq_ref[...]) * scale).astype(BF16)
    ks_ref[...] = _group_rms(uks_ref[...], bdk_ref[...], gks_ref[...]).astype(BF16)
    kw_ref[...] = _group_rms(ukw_ref[...], bdk_ref[...], gkw_ref[...]).astype(BF16)
    vs_ref[...] = uvs_ref[...].astype(BF16)
    vw_ref[...] = uvw_ref[...].astype(BF16)


def _block_diag_mean(n):
    i = jnp.arange(n) // NSA_HEAD_DIM
    return ((i[:, None] == i[None, :]).astype(F32) / NSA_HEAD_DIM).astype(BF16)


def nsa_prep(u, qk_gain, col_q, col_kv, tm=512):
    t = u.shape[0]
    assert t % tm == 0 and col_q % NSA_DIM == 0 and col_kv % KV_LANES == 0
    qb = col_q // NSA_DIM
    kb = col_kv // KV_LANES
    kv_spec = lambda j: pl.BlockSpec((tm, KV_LANES), lambda i, j=j: (i, kb + j))
    full = lambda r, c: pl.BlockSpec((r, c), lambda i: (0, 0))
    row = lambda c: pl.BlockSpec((tm, c), lambda i: (i, 0))
    gq = jnp.tile(qk_gain[0], NSA_HEADS).reshape(1, NSA_DIM)
    gks = jnp.tile(qk_gain[2], NSA_KV_GROUPS).reshape(1, KV_LANES)
    gkw = jnp.tile(qk_gain[3], NSA_KV_GROUPS).reshape(1, KV_LANES)
    return pl.pallas_call(
        _nsa_prep_kernel,
        grid=(t // tm,),
        in_specs=[pl.BlockSpec((tm, NSA_DIM), lambda i: (i, qb)), kv_spec(2), kv_spec(3), kv_spec(4), kv_spec(5),
                  full(1, NSA_DIM), full(1, KV_LANES), full(1, KV_LANES), full(NSA_DIM, NSA_DIM), full(KV_LANES, KV_LANES)],
        out_specs=[row(NSA_DIM), row(KV_LANES), row(KV_LANES), row(KV_LANES), row(KV_LANES)],
        out_shape=[jax.ShapeDtypeStruct((t, NSA_DIM), BF16)] + [jax.ShapeDtypeStruct((t, KV_LANES), BF16)] * 4,
        compiler_params=pltpu.CompilerParams(dimension_semantics=("arbitrary",)),
        name="nsa_prep",
    )(u, u, u, u, u, gq, gks, gkw, _block_diag_mean(NSA_DIM), _block_diag_mean(KV_LANES))


def _gelu_tanh(x):
    return 0.5 * x * (1.0 + jnp.tanh(0.7978845608028654 * (x + 0.044715 * x * x * x)))


def _nsa_compress_kernel(uk_ref, uv_ref, pe_ref, w1_ref, w2_ref, gk_ref, bd_ref, kc_ref, vc_ref):
    n_row = uk_ref.shape[1] // CMP_STRIDE
    d = NSA_HEAD_DIM
    for kv, (src, dst) in enumerate(((uk_ref, kc_ref), (uv_ref, vc_ref))):
        top = [jnp.zeros((n_row, CMP_HIDDEN), F32) for _ in range(NSA_KV_GROUPS)]
        bot = [jnp.zeros((n_row, CMP_HIDDEN), F32) for _ in range(NSA_KV_GROUPS)]
        for l in range(CMP_STRIDE):
            x2 = src[0, pl.ds(l, n_row, stride=CMP_STRIDE), :]
            l2 = l + CMP_STRIDE
            for g in range(NSA_KV_GROUPS):
                x = x2[:, g * d:(g + 1) * d]
                top[g] += _dot((x + pe_ref[kv, l:l + 1, :]).astype(BF16), w1_ref[kv, l * d:(l + 1) * d, :])
                bot[g] += _dot((x + pe_ref[kv, l2:l2 + 1, :]).astype(BF16), w1_ref[kv, l2 * d:(l2 + 1) * d, :])
        outs = []
        for g in range(NSA_KV_GROUPS):
            hid = top[g] + pltpu.roll(bot[g], n_row - 1, 0)
            outs.append(_dot(_gelu_tanh(hid).astype(BF16), w2_ref[kv]))
        y = jnp.concatenate(outs, axis=1)
        if kv == 0:
            y = _group_rms(y, bd_ref[...], gk_ref[...])
        dst[0] = y.astype(BF16)


def nsa_compress(u3, qk_gain, cmp_pe, cmp_w1, cmp_w2, col_kv):
    b, s, _ = u3.shape
    kb = col_kv // KV_LANES
    n_row = s // CMP_STRIDE
    full = lambda *shape: pl.BlockSpec(shape, lambda i: (0,) * len(shape))
    gk = jnp.tile(qk_gain[1], NSA_KV_GROUPS).reshape(1, KV_LANES)
    out_spec = pl.BlockSpec((1, n_row, KV_LANES), lambda i: (i, 0, 0))
    return pl.pallas_call(
        _nsa_compress_kernel,
        grid=(b,),
        in_specs=[pl.BlockSpec((1, s, KV_LANES), lambda i: (i, 0, kb)),
                  pl.BlockSpec((1, s, KV_LANES), lambda i: (i, 0, kb + 1)),
                  full(2, CMP_LEN, NSA_HEAD_DIM), full(2, CMP_LEN * NSA_HEAD_DIM, CMP_HIDDEN),
                  full(2, CMP_HIDDEN, NSA_HEAD_DIM), full(1, KV_LANES), full(KV_LANES, KV_LANES)],
        out_specs=[out_spec, out_spec],
        out_shape=[jax.ShapeDtypeStruct((b, n_row, KV_LANES), BF16)] * 2,
        compiler_params=pltpu.CompilerParams(dimension_semantics=("arbitrary",)),
        name="nsa_compress",
    )(u3, u3, cmp_pe, cmp_w1.astype(BF16), cmp_w2.astype(BF16), gk, _block_diag_mean(KV_LANES))


def _stack_heads(q, g):
    d = NSA_HEAD_DIM
    base = g * NSA_HPG * d
    return jnp.concatenate([q[:, base + h * d: base + (h + 1) * d] for h in range(NSA_HPG)], axis=0)


def _nsa_select_kernel(q_ref, kc_ref, vc_ref, ov_ref, oc_ref, sel_ref, flag_ref):
    qbl, d = NSA_QBLOCK, NSA_HEAD_DIM
    n_key = kc_ref.shape[1]
    n_slc = ov_ref.shape[1]
    row = lax.broadcasted_iota(jnp.int32, (qbl, n_key), 0)
    last = lax.broadcasted_iota(jnp.int32, (qbl, n_key), 1) * CMP_STRIDE + CMP_LEN - 1
    j = lax.broadcasted_iota(jnp.int32, (n_slc, qbl), 0)
    tiles = range(SELECT_TILES)
    qts = [pl.program_id(1) * SELECT_TILES + i for i in tiles]
    tile_rows = [slice(i * qbl, (i + 1) * qbl) for i in tiles]

    def importance(i):
        dist = (qts[i] * qbl + row - last).astype(F32)
        ok = dist >= 0
        q = q_ref[0, tile_rows[i], :]
        imps = []
        for g in range(NSA_KV_GROUPS):
            kc = kc_ref[0, :, g * d:(g + 1) * d]
            vc = vc_ref[0, :, g * d:(g + 1) * d]
            s_all = _dot_nt(_stack_heads(q, g), kc)
            p_sum = jnp.zeros((qbl, n_key), F32)
            ps = []
            for h in range(NSA_HPG):
                s = jnp.where(ok, s_all[h * qbl:(h + 1) * qbl] - _head_slope(g, h) * dist, NEG_INF)
                m = jnp.max(s, axis=-1, keepdims=True)
                e = jnp.where(ok, jnp.exp(s - m), 0.0)
                l = jnp.sum(e, axis=-1, keepdims=True)
                p = e * jnp.where(l > 0, 1.0 / l, 0.0)
                ps.append(p.astype(BF16))
                p_sum += p
            o_all = _dot(jnp.concatenate(ps, axis=0), vc)
            for h in range(NSA_HPG):
                col = (g * NSA_HPG + h) * d
                oc_ref[0, tile_rows[i], col:col + d] = o_all[h * qbl:(h + 1) * qbl]
            imps.append(_split_dot(p_sum, ov_ref[...]))
        return jnp.concatenate(imps, axis=1).T

    def top_blocks(i, imp_t):
        forced = (j == 0) | (j == qts[i]) | (j == qts[i] - 1)
        visible = j <= qts[i]
        sels = []
        for g in range(NSA_KV_GROUPS):
            score = jnp.where(visible, imp_t[g * n_slc:(g + 1) * n_slc] + jnp.where(forced, SLC_FORCE_BONUS, 0.0), NEG_INF)
            sel = jnp.zeros((n_slc, qbl), F32)
            for _ in range(min(SLC_TOPN, n_slc)):
                m = jnp.max(score, axis=0, keepdims=True)
                first = jnp.min(jnp.where(score == m, j, n_slc), axis=0, keepdims=True)
                pick = j == first
                sel = jnp.where(pick, 1.0, sel)
                score = jnp.where(pick, -3e38, score)
            sels.append(jnp.where(visible, sel, 0.0))
        sel_all = jnp.concatenate(sels, axis=0).T
        sel_ref[0, tile_rows[i], :] = sel_all.astype(BF16)
        flag_ref[0, i] = jnp.broadcast_to(jnp.max(sel_all, axis=0, keepdims=True), flag_ref.shape[2:])

    imp_ts = [importance(i) for i in tiles]
    for i in tiles:
        top_blocks(i, imp_ts[i])


def nsa_select(q3, k_cmp, v_cmp):
    b, s, _ = q3.shape
    n_qt = s // NSA_QBLOCK
    n_slc = s // SLC_LEN
    n_key = k_cmp.shape[1]
    c_lo = jnp.arange(n_key) * CMP_STRIDE
    j_lo = jnp.arange(n_slc) * SLC_LEN
    overlap = ((c_lo[:, None] < j_lo[None, :] + SLC_LEN) & (c_lo[:, None] + CMP_LEN > j_lo[None, :])).astype(BF16)
    rows = SELECT_TILES * NSA_QBLOCK
    assert s % rows == 0
    tile = lambda c: pl.BlockSpec((1, rows, c), lambda i, t: (i, t, 0))
    per_b = pl.BlockSpec((1, n_key, KV_LANES), lambda i, t: (i, 0, 0))
    return pl.pallas_call(
        _nsa_select_kernel,
        grid=(b, s // rows),
        in_specs=[tile(NSA_DIM), per_b, per_b, pl.BlockSpec((n_key, n_slc), lambda i, t: (0, 0))],
        out_specs=[tile(NSA_DIM), tile(2 * n_slc), pl.BlockSpec((1, SELECT_TILES, 8, 2 * n_slc), lambda i, t: (i, t, 0, 0))],
        out_shape=[jax.ShapeDtypeStruct((b, s, NSA_DIM), F32), jax.ShapeDtypeStruct((b, s, 2 * n_slc), BF16),
                   jax.ShapeDtypeStruct((b, n_qt, 8, 2 * n_slc), F32)],
        compiler_params=pltpu.CompilerParams(dimension_semantics=("arbitrary", "arbitrary")),
        name="nsa_select",
    )(q3, k_cmp, v_cmp, overlap)


def _nsa_attend_kernel(count_ref, list_ref, q_ref, ks_ref, vs_ref, kw_ref, vw_ref, sel_ref, oc_ref, gate_ref, out_ref,
                       ksel_ref, vsel_ref):
    bi, qt = pl.program_id(0), pl.program_id(1)
    n_qt = pl.num_programs(1)
    qbl, d = NSA_QBLOCK, NSA_HEAD_DIM
    n_slc = sel_ref.shape[2] // NSA_KV_GROUPS
    n_word = n_slc // IDS_PER_WORD
    q = q_ref[0]
    gates = jax.nn.sigmoid(gate_ref[0])

    def chunk_iotas(width):
        lane = lax.broadcasted_iota(jnp.int32, (1, width * SLC_LEN), 1)
        return dict(lane=lane, slot_of_lane=lane // SLC_LEN,
                    t_sel=qt * qbl + lax.broadcasted_iota(jnp.int32, (qbl, width * SLC_LEN), 0),
                    j_iota=lax.broadcasted_iota(jnp.int32, (n_slc, width * SLC_LEN), 0))

    iotas = {width: chunk_iotas(width) for width in set(FIRST_CHUNK + (REST_CHUNK,))}
    win_start = jnp.maximum(qt - WINDOW // qbl, 0) * qbl
    t_win = qt * qbl + lax.broadcasted_iota(jnp.int32, (qbl, N_WIN_KEYS), 0)
    dist_win = t_win - (win_start + lax.broadcasted_iota(jnp.int32, (qbl, N_WIN_KEYS), 1))
    ok_win = (dist_win >= 0) & (dist_win < WINDOW)
    dist_win = dist_win.astype(F32)

    groups = range(NSA_KV_GROUPS)
    lanes = [slice(g * d, (g + 1) * d) for g in groups]
    qs = [_stack_heads(q, g) for g in groups]
    tile_g = [(bi * n_qt + qt) * NSA_KV_GROUPS + g for g in groups]
    n_sel = [count_ref[tg] for tg in tile_g]
    sel_g = [sel_ref[0, :, g * n_slc:(g + 1) * n_slc] for g in groups]

    def sel_scores(first, width, g):
        io = iotas[width]
        word0 = tile_g[g] * n_word + first // IDS_PER_WORD
        j_row = jnp.full((1, width * SLC_LEN), -1, jnp.int32)
        for slot in range(width):
            valid = first + slot < n_sel[g]
            jb = (list_ref[word0 + slot // IDS_PER_WORD] >> (8 * (slot % IDS_PER_WORD))) & 0xFF
            rows = pl.ds(pl.multiple_of(jb * SLC_LEN, SLC_LEN), SLC_LEN)
            ksel_ref[g, slot * SLC_LEN:(slot + 1) * SLC_LEN, :] = ks_ref[0, rows, lanes[g]]
            vsel_ref[g, slot * SLC_LEN:(slot + 1) * SLC_LEN, :] = vs_ref[0, rows, lanes[g]]
            j_row = jnp.where(io['slot_of_lane'] == slot, jnp.where(valid, jb, -1), j_row)
        s_all = _dot_nt(qs[g], ksel_ref[g, 0:width * SLC_LEN, :])
        chosen = _dot(sel_g[g], (io['j_iota'] == j_row).astype(BF16))
        dist = io['t_sel'] - (j_row * SLC_LEN + io['lane'] % SLC_LEN)
        ok = (chosen > 0.5) & (dist >= 0)
        return s_all, ok, dist.astype(F32)

    def sel_softmax(scores, carry, g):
        s_all, ok, dist = scores
        m_old, l_old, acc = carry
        ps, ms, ls = [], [], []
        for h in range(NSA_HPG):
            rows_h = slice(h * qbl, (h + 1) * qbl)
            s = jnp.where(ok, s_all[rows_h] - _head_slope(g, h) * dist, NEG_INF)
            m_new = jnp.maximum(m_old[rows_h], jnp.max(s, axis=-1, keepdims=True))
            p = jnp.exp(s - m_new)
            alpha = jnp.exp(m_old[rows_h] - m_new)
            ls.append(alpha * l_old[rows_h] + jnp.sum(p, axis=-1, keepdims=True))
            ms.append(m_new)
            ps.append(p.astype(BF16))
        m_new = jnp.concatenate(ms, axis=0)
        alpha = jnp.exp(m_old - m_new)
        acc = alpha * acc + _dot(jnp.concatenate(ps, axis=0), vsel_ref[g, 0:s_all.shape[1], :])
        return m_new, jnp.concatenate(ls, axis=0), acc

    win_rows = pl.ds(pl.multiple_of(win_start, qbl), N_WIN_KEYS)
    s_win = [_dot_nt(qs[g], kw_ref[0, win_rows, lanes[g]]) for g in groups]
    sc0 = [sel_scores(0, FIRST_CHUNK[g], g) for g in groups]
    p_win, inv_win = [], []
    for g in groups:
        ps, inv = [], []
        for h in range(NSA_HPG):
            s = jnp.where(ok_win, s_win[g][h * qbl:(h + 1) * qbl] - _head_slope(g, h) * dist_win, NEG_INF)
            e = jnp.exp(s - jnp.max(s, axis=-1, keepdims=True))
            inv.append(1.0 / jnp.sum(e, axis=-1, keepdims=True))
            ps.append(e.astype(BF16))
        p_win.append(jnp.concatenate(ps, axis=0))
        inv_win.append(jnp.concatenate(inv, axis=0))
    init = (jnp.full((NSA_HPG * qbl, 1), M_INIT, F32), jnp.zeros((NSA_HPG * qbl, 1), F32),
            jnp.zeros((NSA_HPG * qbl, d), F32))
    carry = [sel_softmax(sc0[g], init, g) for g in groups]
    o_win = [_dot(p_win[g], vw_ref[0, win_rows, lanes[g]]) * inv_win[g] for g in groups]

    for g in groups:
        n_rest = (jnp.maximum(n_sel[g] - FIRST_CHUNK[g], 0) + REST_CHUNK - 1) // REST_CHUNK
        rest = lax.fori_loop(0, n_rest, lambda c, cr, g=g: sel_softmax(
            sel_scores(FIRST_CHUNK[g] + c * REST_CHUNK, REST_CHUNK, g), cr, g), carry[g])
        o_sel = rest[2] * (1.0 / rest[1])
        for h in range(NSA_HPG):
            hh = g * NSA_HPG + h
            col = hh * d
            rows_h = slice(h * qbl, (h + 1) * qbl)
            out_ref[0, :, col:col + d] = (gates[:, hh:hh + 1] * oc_ref[0, :, col:col + d]
                                          + gates[:, NSA_HEADS + hh:NSA_HEADS + hh + 1] * o_sel[rows_h]
                                          + gates[:, 2 * NSA_HEADS + hh:2 * NSA_HEADS + hh + 1] * o_win[g][rows_h])


def nsa_attend(counts, lists, q3, ks, vs, kw, vw, sel, o_cmp, u3, col_gate):
    b, s, _ = q3.shape
    n_qt = s // NSA_QBLOCK
    chunks = FIRST_CHUNK + (REST_CHUNK,)
    assert col_gate % 128 == 0 and s >= N_WIN_KEYS and all(c % IDS_PER_WORD == 0 for c in chunks)
    gb = col_gate // 128
    tile = lambda c: pl.BlockSpec((1, NSA_QBLOCK, c), lambda i, t, counts, lists: (i, t, 0))
    per_b = pl.BlockSpec((1, s, KV_LANES), lambda i, t, counts, lists: (i, 0, 0))
    grid_spec = pltpu.PrefetchScalarGridSpec(
        num_scalar_prefetch=2,
        grid=(b, n_qt),
        in_specs=[tile(NSA_DIM), per_b, per_b, per_b, per_b, tile(sel.shape[2]), tile(NSA_DIM),
                  pl.BlockSpec((1, NSA_QBLOCK, 128), lambda i, t, counts, lists: (i, t, gb))],
        out_specs=tile(NSA_DIM),
        scratch_shapes=[pltpu.VMEM((NSA_KV_GROUPS, max(chunks) * SLC_LEN, NSA_HEAD_DIM), BF16),
                        pltpu.VMEM((NSA_KV_GROUPS, max(chunks) * SLC_LEN, NSA_HEAD_DIM), BF16)],
    )
    return pl.pallas_call(
        _nsa_attend_kernel,
        grid_spec=grid_spec,
        out_shape=jax.ShapeDtypeStruct((b, s, NSA_DIM), F32),
        compiler_params=pltpu.CompilerParams(dimension_semantics=("arbitrary", "arbitrary")),
        name="nsa_attend",
    )(counts, lists, q3, ks, vs, kw, vw, sel, o_cmp, u3)


def _pack_union_lists(flags, n_slc):
    assert n_slc <= 256 and n_slc % IDS_PER_WORD == 0
    b, n_qt = flags.shape[:2]
    f = flags[:, :, 0, :].reshape(b, n_qt, NSA_KV_GROUPS, n_slc) > 0.5
    fi = f.astype(jnp.int32)
    counts = jnp.sum(fi, axis=-1)
    ids = jnp.arange(n_slc, dtype=jnp.int32)
    ahead = (fi.astype(F32) @ (ids[:, None] <= ids[None, :]).astype(F32)).astype(jnp.int32)
    pos = jnp.where(f, ahead - 1, counts[..., None] + ids - ahead)
    order = jnp.sum(jnp.where(pos[..., :, None] == ids, ids[:, None], 0), axis=-2)
    order = order.reshape(b, n_qt, NSA_KV_GROUPS, n_slc // IDS_PER_WORD, IDS_PER_WORD)
    words = jnp.sum(order << (8 * jnp.arange(IDS_PER_WORD, dtype=jnp.int32)), axis=-1, dtype=jnp.int32)
    return counts.reshape(-1), words.reshape(-1)


def nsa_mixer_pallas(u3, qk_gain, cmp_pe, cmp_w1, cmp_w2, col_q, col_kv, col_gate):
    b, s, d_in = u3.shape
    q, ks, vs, kw, vw = nsa_prep(u3.reshape(b * s, d_in), qk_gain, col_q, col_kv)
    k_cmp, v_cmp = nsa_compress(u3, qk_gain, cmp_pe, cmp_w1, cmp_w2, col_kv)
    q3 = q.reshape(b, s, NSA_DIM)
    r3 = lambda a: a.reshape(b, s, KV_LANES)
    o_cmp, sel, flags = nsa_select(q3, k_cmp, v_cmp)
    counts, lists = _pack_union_lists(flags, s // SLC_LEN)
    return nsa_attend(counts, lists, q3, r3(ks), r3(vs), r3(kw), r3(vw), sel, o_cmp, u3, col_gate)


GDN_TILE = 512
GDN_HALO = 8


def _dot_tn(a, b):
    return lax.dot_general(a, b, (((0,), (0,)), ((), ())), preferred_element_type=F32)


def _softplus(x):
    return jnp.maximum(x, 0.0) + jnp.log(1.0 + jnp.exp(-jnp.abs(x)))


def _l2_norm(x):
    return x * lax.rsqrt(jnp.sum(x * x, axis=-1, keepdims=True) + RMS_EPS)


def _gdn_kernel(qkv_ref, z_ref, small_ref, cw_ref, coef_ref, gain_ref, y_ref, xe_ref, state_ref):
    tt = pl.program_id(1)
    tile, c, hd = GDN_TILE, GDN_CHUNK, GDN_HEAD_DIM

    @pl.when(tt == 0)
    def _():
        xe_ref[0:GDN_HALO, :] = jnp.zeros((GDN_HALO, xe_ref.shape[1]), F32)
        state_ref[...] = jnp.zeros_like(state_ref)

    @pl.when(tt > 0)
    def _():
        xe_ref[0:GDN_HALO, :] = xe_ref[tile:tile + GDN_HALO, :]

    xe_ref[GDN_HALO:, :] = qkv_ref[0]
    conv = jnp.zeros((tile, xe_ref.shape[1]), F32)
    for j in range(GDN_CONV):
        conv += cw_ref[j:j + 1, :] * xe_ref[pl.ds(GDN_HALO - (GDN_CONV - 1) + j, tile), :]
    act = conv * jax.nn.sigmoid(conv)

    small = small_ref[0]
    beta_all = jax.nn.sigmoid(small)
    g_all = coef_ref[0:1, :] * _softplus(small + coef_ref[1:2, :])
    row = lax.broadcasted_iota(jnp.int32, (c, SMALL_LANES), 0)
    ri = lax.broadcasted_iota(jnp.int32, (c, c), 0)
    ci = lax.broadcasted_iota(jnp.int32, (c, c), 1)
    lower = ri >= ci
    strict = ri > ci

    n_chunk = tile // c
    pairs = [(n, h) for n in range(n_chunk) for h in range(GDN_HEADS)]
    gcs, gc_ts = [], []
    for n in range(n_chunk):
        gc = g_all[n * c:(n + 1) * c]
        shift = 1
        while shift < c:
            gc = gc + jnp.where(row >= shift, pltpu.roll(gc, shift, 0), 0.0)
            shift *= 2
        gcs.append(gc)
        gc_ts.append(gc.T)
    pre = []
    for n, h in pairs:
        rows = slice(n * c, (n + 1) * c)
        q = _l2_norm(act[rows, h * hd:(h + 1) * hd]) * hd ** -0.5
        k = _l2_norm(act[rows, GDN_DIM + h * hd:GDN_DIM + (h + 1) * hd])
        v = act[rows, 2 * GDN_DIM + h * hd:2 * GDN_DIM + (h + 1) * hd]
        beta = beta_all[rows, LANE_BETA + h:LANE_BETA + h + 1]
        gcol = gcs[n][:, LANE_DECAY + h:LANE_DECAY + h + 1]
        grow = gc_ts[n][LANE_DECAY + h:LANE_DECAY + h + 1, :]
        g_last = gcol[c - 1:c, :]
        decay = jnp.where(lower, jnp.exp(jnp.where(lower, gcol - grow, 0.0)), 0.0)
        kb = k * beta
        kh = k.astype(BF16)
        pre.append(dict(
            lmat=jnp.where(strict, _dot_nt(kb.astype(BF16), kh) * decay, 0.0),
            rhs=jnp.concatenate([v * beta, kb * jnp.exp(gcol)], axis=1),
            attn=(_dot_nt(q.astype(BF16), kh) * decay).astype(BF16),
            q_dec=(q * jnp.exp(gcol)).astype(BF16),
            k_dec=(k * jnp.exp(g_last - gcol)).astype(BF16),
            d_last=jnp.exp(g_last)))
    eye = (ri == ci).astype(F32)
    power = [p['lmat'].astype(BF16) for p in pre]
    t_inv = [eye - p['lmat'] for p in pre]
    for _ in range(5):
        power = [_dot(lm, lm).astype(BF16) for lm in power]
        t_inv = [t + _dot(lm, t.astype(BF16)) for lm, t in zip(power, t_inv)]
    rhs = [_dot(t.astype(BF16), p['rhs'].astype(BF16)) for t, p in zip(t_inv, pre)]

    state = [state_ref[h] for h in range(GDN_HEADS)]
    for n in range(n_chunk):
        rows = slice(n * c, (n + 1) * c)
        sb = [s.astype(BF16) for s in state]
        ps = [pre[n * GDN_HEADS + h] for h in range(GDN_HEADS)]
        rs = [rhs[n * GDN_HEADS + h] for h in range(GDN_HEADS)]
        v_new = [(r[:, :hd] - _dot(r[:, hd:].astype(BF16), s)).astype(BF16) for r, s in zip(rs, sb)]
        outs = [_dot(p['q_dec'], s) + _dot(p['attn'], vn) for p, s, vn in zip(ps, sb, v_new)]
        state = [s * p['d_last'] + _dot_tn(p['k_dec'], vn) for p, s, vn in zip(ps, state, v_new)]
        for h, o in enumerate(outs):
            o = o * lax.rsqrt(jnp.mean(o * o, axis=-1, keepdims=True) + RMS_EPS) * gain_ref[...]
            zz = z_ref[0, rows, h * hd:(h + 1) * hd]
            y_ref[0, rows, h * hd:(h + 1) * hd] = o * (zz * jax.nn.sigmoid(zz))
    for h in range(GDN_HEADS):
        state_ref[h] = state[h]


def gdn_mixer(u3, col_qkv, col_z, col_small, conv_w, a_log, dt_bias, out_gain):
    b, s, _ = u3.shape
    lane = jnp.arange(SMALL_LANES)
    in_decay = (lane >= LANE_DECAY) & (lane < LANE_DECAY + GDN_HEADS)
    idx = jnp.clip(lane - LANE_DECAY, 0, GDN_HEADS - 1)
    coef = jnp.stack([jnp.where(in_decay, -jnp.exp(a_log)[idx], 0.0), jnp.where(in_decay, dt_bias[idx], 0.0)])
    qkv_w = 3 * GDN_DIM
    assert col_qkv % qkv_w == 0 and col_z % GDN_DIM == 0 and col_small % SMALL_LANES == 0
    full = lambda r, cc: pl.BlockSpec((r, cc), lambda i, t: (0, 0))
    return pl.pallas_call(
        _gdn_kernel,
        grid=(b, s // GDN_TILE),
        in_specs=[pl.BlockSpec((1, GDN_TILE, qkv_w), lambda i, t: (i, t, col_qkv // qkv_w)),
                  pl.BlockSpec((1, GDN_TILE, GDN_DIM), lambda i, t: (i, t, col_z // GDN_DIM)),
                  pl.BlockSpec((1, GDN_TILE, SMALL_LANES), lambda i, t: (i, t, col_small // SMALL_LANES)),
                  full(GDN_CONV, qkv_w), full(2, SMALL_LANES), full(1, GDN_HEAD_DIM)],
        out_specs=pl.BlockSpec((1, GDN_TILE, GDN_DIM), lambda i, t: (i, t, 0)),
        out_shape=jax.ShapeDtypeStruct((b, s, GDN_DIM), F32),
        scratch_shapes=[pltpu.VMEM((GDN_TILE + GDN_HALO, qkv_w), F32), pltpu.VMEM((GDN_HEADS, GDN_HEAD_DIM, GDN_HEAD_DIM), F32)],
        compiler_params=pltpu.CompilerParams(dimension_semantics=("arbitrary", "arbitrary")),
        name="gdn_mixer",
    )(u3, u3, u3, conv_w, coef, out_gain.reshape(1, GDN_HEAD_DIM))


def _merge_kernel(tiles_per_seq, x_ref, g_ref, u_ref, cw_ref, yn_ref, yg_ref, wb_ref, wg_ref, bg_ref, wo_ref, o_ref, xe_ref):
    x = x_ref[...]
    tm, d = x.shape
    cd = CONV_DIM
    first_of_seq = pl.program_id(0) % tiles_per_seq == 0

    @pl.when(first_of_seq)
    def _():
        xe_ref[0:GDN_HALO, :] = jnp.zeros((GDN_HALO, cd), F32)

    @pl.when(jnp.logical_not(first_of_seq))
    def _():
        xe_ref[0:GDN_HALO, :] = xe_ref[tm:tm + GDN_HALO, :]

    xe_ref[GDN_HALO:, :] = u_ref[:, cd:2 * cd] * u_ref[:, 2 * cd:3 * cd]
    conv = jnp.zeros((tm, cd), F32)
    for j in range(CONV_WIDTH):
        conv += cw_ref[j:j + 1, :] * xe_ref[pl.ds(GDN_HALO - (CONV_WIDTH - 1) + j, tm), :]
    y_a = u_ref[:, 0:cd] * conv

    h = (x * lax.rsqrt(jnp.mean(x * x, axis=-1, keepdims=True) + RMS_EPS) * g_ref[...]).astype(BF16)
    merged = jnp.zeros(x.shape, F32)
    for r, y in enumerate((y_a, yn_ref[...], yg_ref[...])):
        gate = jax.nn.sigmoid(_dot(h, wg_ref[:, r * d:(r + 1) * d]) + bg_ref[:, r * d:(r + 1) * d])
        merged += gate * _dot(y.astype(BF16), wb_ref[r])
    o_ref[...] = x + _dot(merged.astype(BF16), wo_ref[...])


def merge_branches(x2d, seq, gain, u2d, conv_w, y_n, y_g, w_branch, w_gate, b_gate, w_out, tm=512):
    t, d = x2d.shape
    assert seq % tm == 0 and COL_CONV == 0
    row = lambda c: pl.BlockSpec((tm, c), lambda i: (i, 0))
    full = lambda *shape: pl.BlockSpec(shape, lambda i: (0,) * len(shape))
    return pl.pallas_call(
        functools.partial(_merge_kernel, seq // tm),
        grid=(t // tm,),
        in_specs=[row(d), full(1, d), row(3 * CONV_DIM), full(CONV_WIDTH, CONV_DIM), row(BRANCH_DIM), row(BRANCH_DIM),
                  full(N_BRANCH, BRANCH_DIM, d), full(d, N_BRANCH * d), full(1, N_BRANCH * d), full(d, d)],
        out_specs=row(d),
        out_shape=jax.ShapeDtypeStruct((t, d), F32),
        scratch_shapes=[pltpu.VMEM((tm + GDN_HALO, CONV_DIM), F32)],
        compiler_params=pltpu.CompilerParams(dimension_semantics=("arbitrary",), vmem_limit_bytes=VMEM_LIMIT_BIG),
        name="merge_branches",
    )(x2d, gain.reshape(1, d), u2d, conv_w, y_n, y_g, w_branch.astype(BF16), w_gate.astype(BF16), b_gate.reshape(1, -1), w_out.astype(BF16))


def hybrid_mixer(x2d, b, s, norm_gain, w_in, conv_a_w, nsa_qk_gain, cmp_pe, cmp_w1, cmp_w2, gdn_conv_w, gdn_a_log, gdn_dt_bias, gdn_out_gain, w_branch, w_gate, b_gate, w_out):
    t = b * s
    u3 = norm_proj(x2d, norm_gain, permute_in_proj(w_in)).reshape(b, s, D_U)
    y_n = nsa_mixer_pallas(u3, nsa_qk_gain, cmp_pe, cmp_w1, cmp_w2, COL_NSA_Q, COL_NSA_KV, COL_SMALL)
    y_g = gdn_mixer(u3, COL_GDN_QKV, COL_GDN_Z, COL_SMALL, gdn_conv_w, gdn_a_log, gdn_dt_bias, gdn_out_gain)
    return merge_branches(x2d, s, norm_gain, u3.reshape(t, D_U), conv_a_w, y_n.reshape(t, BRANCH_DIM), y_g.reshape(t, BRANCH_DIM), w_branch, w_gate, b_gate, w_out)


MOE_ROWS = 512
ROUTE_LANES = 128
N_ROUTER = MOE_GROUPS + N_EXPERTS
ROW_DMA_UNROLL = 8


def _moe_route_kernel(x_ref, g_ref, wr_ref, br_ref, tri_ref, h_ref, route_ref, cnt_ref, run_ref):
    @pl.when(pl.program_id(0) == 0)
    def _():
        run_ref[...] = jnp.zeros_like(run_ref)

    x = x_ref[...]
    h = x * lax.rsqrt(jnp.mean(x * x, axis=-1, keepdims=True) + RMS_EPS) * g_ref[...]
    bits = lax.bitcast_convert_type(h.astype(BF16).astype(F32), jnp.uint32)
    half = h.shape[1] // 2
    packed = (bits[:, half:] & jnp.uint32(0xFFFF0000)) | (bits[:, :half] >> 16)
    h_ref[...] = packed.reshape(h_ref.shape)
    logits = _dot(h.astype(BF16), wr_ref[...]) + br_ref[...]
    lane = lax.broadcasted_iota(jnp.int32, logits.shape, 1)
    first_of = lambda hit: jnp.min(jnp.where(hit, lane, ROUTE_LANES), axis=-1, keepdims=True)
    is_grp = lane < MOE_GROUPS
    lg = jnp.where(is_grp, logits, NEG_INF)
    m_g = jnp.max(lg, axis=-1, keepdims=True)
    grp = first_of(lg == m_g)
    p_grp = 1.0 / jnp.sum(jnp.where(is_grp, jnp.exp(lg - m_g), 0.0), axis=-1, keepdims=True)
    lo = MOE_GROUPS + grp * EXPERTS_PER_GROUP
    le = jnp.where((lane >= lo) & (lane < lo + EXPERTS_PER_GROUP), logits, NEG_INF)
    m1 = jnp.max(le, axis=-1, keepdims=True)
    i1 = first_of(le == m1)
    le2 = jnp.where(lane == i1, NEG_INF, le)
    m2 = jnp.max(le2, axis=-1, keepdims=True)
    i2 = first_of(le2 == m2)
    r = jnp.exp(m2 - m1)
    g1 = p_grp / (1.0 + r)
    g2 = p_grp * r / (1.0 + r)
    e1 = i1 - MOE_GROUPS
    e2 = i2 - MOE_GROUPS
    hit1 = lane == e1
    hit2 = lane == e2
    onehot = (hit1 | hit2).astype(BF16)
    before = _dot(tri_ref[...], onehot) + run_ref[...]
    r1 = jnp.sum(jnp.where(hit1, before, 0.0), axis=-1, keepdims=True)
    r2 = jnp.sum(jnp.where(hit2, before, 0.0), axis=-1, keepdims=True)
    run_ref[...] += jnp.sum(onehot.astype(F32), axis=0, keepdims=True)
    rec = jnp.zeros(logits.shape, F32)
    for k, v in enumerate((e1.astype(F32), e2.astype(F32), r1, r2, g1, g2)):
        rec = jnp.where(lane == k, v, rec)
    route_ref[...] = rec
    cnt_ref[...] = jnp.broadcast_to(run_ref[...], cnt_ref.shape)


def moe_route(x2d, gain, w_rg, b_rg, w_re, b_re, tm=512):
    t, d = x2d.shape
    pad = ROUTE_LANES - N_ROUTER
    wr = jnp.pad(jnp.concatenate([w_rg, w_re], axis=1), ((0, 0), (0, pad))).astype(BF16)
    br = jnp.pad(jnp.concatenate([b_rg, b_re]), (0, pad)).reshape(1, ROUTE_LANES)
    tri = (jnp.arange(tm)[:, None] > jnp.arange(tm)[None, :]).astype(BF16)
    full = lambda r, c: pl.BlockSpec((r, c), lambda i: (0, 0))
    return pl.pallas_call(
        _moe_route_kernel,
        grid=(t // tm,),
        in_specs=[pl.BlockSpec((tm, d), lambda i: (i, 0)), full(1, d), full(d, ROUTE_LANES), full(1, ROUTE_LANES), full(tm, tm)],
        out_specs=[pl.BlockSpec((tm, 1, d // 2), lambda i: (i, 0, 0)), pl.BlockSpec((tm, ROUTE_LANES), lambda i: (i, 0)), full(8, ROUTE_LANES)],
        out_shape=[jax.ShapeDtypeStruct((t, 1, d // 2), jnp.uint32), jax.ShapeDtypeStruct((t, ROUTE_LANES), F32), jax.ShapeDtypeStruct((8, ROUTE_LANES), F32)],
        scratch_shapes=[pltpu.VMEM((1, ROUTE_LANES), F32)],
        compiler_params=pltpu.CompilerParams(dimension_semantics=("arbitrary",)),
        name="moe_route",
    )(x2d, gain.reshape(1, d), wr, br, tri)


def _row_copy(src_ref, src_row, dst_ref, dst_row, sem):
    return pltpu.make_async_copy(src_ref.at[src_row], dst_ref.at[dst_row], sem)


def _moe_dispatch_kernel(dest_ref, h_ref, buf_in_ref, buf_ref, sem):
    del buf_in_ref
    tm = h_ref.shape[0]
    base = pl.program_id(0) * tm

    def send(r, carry):
        for k in range(TOPK_IN_GROUP):
            _row_copy(h_ref, r, buf_ref, dest_ref[(base + r) * TOPK_IN_GROUP + k], sem).start(priority=k)
        return carry

    lax.fori_loop(0, tm, send, 0, unroll=ROW_DMA_UNROLL)
    for k in range(TOPK_IN_GROUP):
        pltpu.make_async_copy(h_ref, buf_ref.at[pl.ds(0, tm)], sem).wait()


def moe_dispatch(dest, h3, n_rows, tm=512):
    t, _, d = h3.shape
    grid_spec = pltpu.PrefetchScalarGridSpec(
        num_scalar_prefetch=1, grid=(t // tm,),
        in_specs=[pl.BlockSpec((tm, 1, d), lambda i, dest: (i, 0, 0)), pl.BlockSpec(memory_space=pl.ANY)],
        out_specs=pl.BlockSpec(memory_space=pl.ANY),
        scratch_shapes=[pltpu.SemaphoreType.DMA(())],
    )
    return pl.pallas_call(
        _moe_dispatch_kernel, grid_spec=grid_spec,
        out_shape=jax.ShapeDtypeStruct((n_rows, 1, d), h3.dtype),
        input_output_aliases={2: 0},
        compiler_params=pltpu.CompilerParams(dimension_semantics=("arbitrary",), has_side_effects=True),
        name="moe_dispatch",
    )(dest, h3, jnp.zeros((n_rows, 1, d), h3.dtype))


def _moe_ffn_kernel(blk_e_ref, n_used_ref, x_ref, wg_ref, wu_ref, wd_ref, y_ref):
    del blk_e_ref
    used = pl.program_id(0) < n_used_ref[0]

    @pl.when(used)
    def _():
        w = x_ref[...].reshape(x_ref.shape[0], x_ref.shape[2])
        lo = lax.bitcast_convert_type(w << 16, F32)
        hi = lax.bitcast_convert_type(w & jnp.uint32(0xFFFF0000), F32)
        xb = jnp.concatenate([lo, hi], axis=1).astype(BF16)
        a = _dot(xb, wg_ref[0])
        mid = a * jax.nn.sigmoid(a) * _dot(xb, wu_ref[0])
        y_ref[...] = _dot(mid.astype(BF16), wd_ref[0]).reshape(y_ref.shape)

    @pl.when(jnp.logical_not(used))
    def _():
        y_ref[...] = jnp.zeros_like(y_ref)


def moe_ffn(blk_expert, n_used, buf3, w_eg, w_eu, w_ed):
    n_rows, _, packed_w = buf3.shape
    d, ff = w_eg.shape[1:]
    w_in_spec = pl.BlockSpec((1, d, ff), lambda b, blk_e, n_used: (blk_e[b], 0, 0))
    grid_spec = pltpu.PrefetchScalarGridSpec(
        num_scalar_prefetch=2, grid=(n_rows // MOE_ROWS,),
        in_specs=[pl.BlockSpec((MOE_ROWS, 1, packed_w), lambda b, blk_e, n_used: (b, 0, 0)), w_in_spec, w_in_spec,
                  pl.BlockSpec((1, ff, d), lambda b, blk_e, n_used: (blk_e[b], 0, 0))],
        out_specs=pl.BlockSpec((MOE_ROWS, 1, d), lambda b, blk_e, n_used: (b, 0, 0)),
    )
    return pl.pallas_call(
        _moe_ffn_kernel, grid_spec=grid_spec,
        out_shape=jax.ShapeDtypeStruct((n_rows, 1, d), F32),
        compiler_params=pltpu.CompilerParams(dimension_semantics=("arbitrary",)),
        name="moe_ffn",
    )(blk_expert, n_used, buf3, w_eg.astype(BF16), w_eu.astype(BF16), w_ed.astype(BF16))


def _moe_combine_kernel(dest_ref, y_ref, rec_ref, x_ref, out_ref, ya_ref, yb_ref, sem):
    tm = out_ref.shape[0]
    base = pl.program_id(0) * tm

    def fetch(r, carry):
        _row_copy(y_ref, dest_ref[(base + r) * TOPK_IN_GROUP], ya_ref, r, sem).start(priority=0)
        _row_copy(y_ref, dest_ref[(base + r) * TOPK_IN_GROUP + 1], yb_ref, r, sem).start(priority=1)
        return carry

    lax.fori_loop(0, tm, fetch, 0, unroll=ROW_DMA_UNROLL)
    pltpu.make_async_copy(y_ref.at[pl.ds(0, tm)], ya_ref, sem).wait()
    pltpu.make_async_copy(y_ref.at[pl.ds(0, tm)], yb_ref, sem).wait()
    rec = rec_ref[...]
    ya = ya_ref[...].reshape(out_ref.shape)
    yb = yb_ref[...].reshape(out_ref.shape)
    out_ref[...] = x_ref[...] + rec[:, 4:5] * ya + rec[:, 5:6] * yb


def moe_combine(dest, y3, rec, x2d, tm=512):
    t, d = x2d.shape
    grid_spec = pltpu.PrefetchScalarGridSpec(
        num_scalar_prefetch=1, grid=(t // tm,),
        in_specs=[pl.BlockSpec(memory_space=pl.ANY), pl.BlockSpec((tm, ROUTE_LANES), lambda i, dest: (i, 0)),
                  pl.BlockSpec((tm, d), lambda i, dest: (i, 0))],
        out_specs=pl.BlockSpec((tm, d), lambda i, dest: (i, 0)),
        scratch_shapes=[pltpu.VMEM((tm, 1, d), F32), pltpu.VMEM((tm, 1, d), F32), pltpu.SemaphoreType.DMA(())],
    )
    return pl.pallas_call(
        _moe_combine_kernel, grid_spec=grid_spec,
        out_shape=jax.ShapeDtypeStruct((t, d), F32),
        compiler_params=pltpu.CompilerParams(dimension_semantics=("arbitrary",)),
        name="moe_combine",
    )(dest, y3, rec, x2d)


def hier_moe_pallas(x2d, gain, w_rg, b_rg, w_re, b_re, w_eg, w_eu, w_ed):
    t, d = x2d.shape
    h, rec, cnt = moe_route(x2d, gain, w_rg, b_rg, w_re, b_re)
    counts = cnt[0, :N_EXPERTS].astype(jnp.int32)
    n_blk = (counts + MOE_ROWS - 1) // MOE_ROWS
    blk_end = jnp.cumsum(n_blk)
    pad_start = (blk_end - n_blk) * MOE_ROWS
    experts = rec[:, 0:2].astype(jnp.int32)
    dest = (pad_start[experts] + rec[:, 2:4].astype(jnp.int32)).reshape(-1)
    total_blk = t * TOPK_IN_GROUP // MOE_ROWS + N_EXPERTS
    blk_expert = jnp.sum(jnp.arange(total_blk)[:, None] >= blk_end[None, :], axis=1, dtype=jnp.int32)
    blk_expert = jnp.minimum(blk_expert, N_EXPERTS - 1)
    buf = moe_dispatch(dest, h, total_blk * MOE_ROWS)
    y = moe_ffn(blk_expert, blk_end[-1:].astype(jnp.int32), buf, w_eg, w_eu, w_ed)
    return moe_combine(dest, y, rec, x2d)


def kernel(x, norm_mix, w_in, conv_a_w, nsa_qk_gain, cmp_pe, cmp_w1, cmp_w2, gdn_conv_w, gdn_a_log, gdn_dt_bias, gdn_out_gain, w_branch, w_gate, b_gate, w_out, norm_ffn, w_router_group, b_router_group, w_router_expert, b_router_expert, w_expert_gate, w_expert_up, w_expert_down):
    b, s, dm = x.shape
    x = x.reshape(b * s, dm)
    for l in range(DEPTH):
        x = hybrid_mixer(x, b, s, norm_mix[l], w_in[l], conv_a_w[l], nsa_qk_gain[l], cmp_pe[l], cmp_w1[l], cmp_w2[l], gdn_conv_w[l], gdn_a_log[l], gdn_dt_bias[l], gdn_out_gain[l], w_branch[l], w_gate[l], b_gate[l], w_out[l])
        x = hier_moe_pallas(x, norm_ffn[l], w_router_group[l], b_router_group[l], w_router_expert[l], b_router_expert[l], w_expert_gate[l], w_expert_up[l], w_expert_down[l])
    return x.reshape(b, s, dm)
```

```python
import functools

import jax
import jax.numpy as jnp
from jax import lax
from jax.experimental import pallas as pl
from jax.experimental.pallas import tpu as pltpu

D_MODEL = 1024
DEPTH = 4
CONV_DIM = 512
CONV_WIDTH = 3
NSA_HEADS = 8
NSA_KV_GROUPS = 2
NSA_HPG = NSA_HEADS // NSA_KV_GROUPS
NSA_HEAD_DIM = 64
NSA_DIM = NSA_HEADS * NSA_HEAD_DIM
CMP_LEN = 32
CMP_STRIDE = 16
CMP_HIDDEN = 256
SLC_LEN = 64
SLC_TOPN = 8
SLC_FORCE_BONUS = 1e6
WINDOW = 512
NSA_QBLOCK = 64
GDN_HEADS = 4
GDN_HEAD_DIM = 128
GDN_DIM = GDN_HEADS * GDN_HEAD_DIM
GDN_CONV = 4
GDN_CHUNK = 64
N_BRANCH = 3
BRANCH_DIM = 512
IN_SPLITS = (3 * CONV_DIM, NSA_DIM, 6 * NSA_KV_GROUPS * NSA_HEAD_DIM, 3 * NSA_HEADS, 3 * GDN_DIM, GDN_DIM, GDN_HEADS, GDN_HEADS)
D_IN = sum(IN_SPLITS)
COL_CONV = 0
COL_GDN_QKV = 3 * CONV_DIM
COL_NSA_Q = COL_GDN_QKV + 3 * GDN_DIM
COL_GDN_Z = COL_NSA_Q + NSA_DIM
COL_NSA_KV = COL_GDN_Z + GDN_DIM
COL_SMALL = COL_NSA_KV + 6 * NSA_KV_GROUPS * NSA_HEAD_DIM
SMALL_LANES = 128
D_U = COL_SMALL + SMALL_LANES
LANE_BETA = 3 * NSA_HEADS
LANE_DECAY = LANE_BETA + GDN_HEADS


def permute_in_proj(w_in):
    conv, nq, nkv, ng, gqkv, gz, gb, ga = split_last(w_in, IN_SPLITS)
    pad = jnp.zeros((w_in.shape[0], SMALL_LANES - LANE_DECAY - GDN_HEADS), w_in.dtype)
    return jnp.concatenate([conv, gqkv, nq, gz, nkv, ng, gb, ga, pad], axis=1)


def split_last(u, sizes):
    out, start = [], 0
    for n in sizes:
        out.append(u[..., start:start + n])
        start += n
    return out
MOE_GROUPS = 4
EXPERTS_PER_GROUP = 8
N_EXPERTS = MOE_GROUPS * EXPERTS_PER_GROUP
TOPK_IN_GROUP = 2
EXPERT_FF = 512
RMS_EPS = 1e-6
NEG_INF = -1e30

F32 = jnp.float32
BF16 = jnp.bfloat16
VMEM_LIMIT_BIG = 56 * 1024 * 1024


def _norm_proj_kernel(x_ref, g_ref, w_ref, o_ref, h_ref):
    @pl.when(pl.program_id(1) == 0)
    def _():
        x = x_ref[...]
        y = x * lax.rsqrt(jnp.mean(x * x, axis=-1, keepdims=True) + RMS_EPS)
        h_ref[...] = (y * g_ref[...]).astype(BF16)

    o_ref[...] = jnp.dot(h_ref[...], w_ref[...], preferred_element_type=F32)


def norm_proj(x2d, gain, w, tm=512, tn=D_U):
    t, d = x2d.shape
    n = w.shape[1]
    n_blk = pl.cdiv(n, tn)
    wb = jnp.pad(w.astype(BF16), ((0, 0), (0, n_blk * tn - n)))
    return pl.pallas_call(
        _norm_proj_kernel,
        grid=(t // tm, n_blk),
        in_specs=[
            pl.BlockSpec((tm, d), lambda i, j: (i, 0)),
            pl.BlockSpec((1, d), lambda i, j: (0, 0)),
            pl.BlockSpec((d, tn), lambda i, j: (0, j)),
        ],
        out_specs=pl.BlockSpec((tm, tn), lambda i, j: (i, j)),
        out_shape=jax.ShapeDtypeStruct((t, n), F32),
        scratch_shapes=[pltpu.VMEM((tm, d), BF16)],
        compiler_params=pltpu.CompilerParams(dimension_semantics=("arbitrary", "arbitrary"),
                                             vmem_limit_bytes=VMEM_LIMIT_BIG),
        name="norm_proj",
    )(x2d, gain.reshape(1, d), wb)


KV_LANES = NSA_KV_GROUPS * NSA_HEAD_DIM
N_WIN_KEYS = WINDOW + NSA_QBLOCK
FIRST_CHUNK = (8, 12)
REST_CHUNK = 8
IDS_PER_WORD = 4
SELECT_TILES = 4
M_INIT = -1e29


def _dot(a, b):
    return jnp.dot(a, b, preferred_element_type=F32)


def _dot_nt(a, b):
    return lax.dot_general(a, b, (((1,), (1,)), ((), ())), preferred_element_type=F32)


def _split_dot(x, m):
    hi = x.astype(BF16)
    lo = (x - hi.astype(F32)).astype(BF16)
    return _dot(hi, m) + _dot(lo, m)


def _head_slope(g, h):
    return 2.0 ** (-8.0 * (g * NSA_HPG + h + 1) / NSA_HEADS)


def _group_rms(x, bd, gain):
    ms = _split_dot(x * x, bd)
    return x * lax.rsqrt(ms + RMS_EPS) * gain


def _nsa_prep_kernel(uq_ref, uks_ref, uvs_ref, ukw_ref, uvw_ref, gq_ref, gks_ref, gkw_ref, bdq_ref, bdk_ref,
                     q_ref, ks_ref, vs_ref, kw_ref, vw_ref):
    scale = NSA_HEAD_DIM ** -0.5
    q_ref[...] = (_group_rms(uq_ref[...], bdq_ref[...], gq_ref[...]) * scale).astype(BF16)
    ks_ref[...] = _group_rms(uks_ref[...], bdk_ref[...], gks_ref[...]).astype(BF16)
    kw_ref[...] = _group_rms(ukw_ref[...], bdk_ref[...], gkw_ref[...]).astype(BF16)
    vs_ref[...] = uvs_ref[...].astype(BF16)
    vw_ref[...] = uvw_ref[...].astype(BF16)


def _block_diag_mean(n):
    i = jnp.arange(n) // NSA_HEAD_DIM
    return ((i[:, None] == i[None, :]).astype(F32) / NSA_HEAD_DIM).astype(BF16)


def nsa_prep(u, qk_gain, col_q, col_kv, tm=512):
    t = u.shape[0]
    assert t % tm == 0 and col_q % NSA_DIM == 0 and col_kv % KV_LANES == 0
    qb = col_q // NSA_DIM
    kb = col_kv // KV_LANES
    kv_spec = lambda j: pl.BlockSpec((tm, KV_LANES), lambda i, j=j: (i, kb + j))
    full = lambda r, c: pl.BlockSpec((r, c), lambda i: (0, 0))
    row = lambda c: pl.BlockSpec((tm, c), lambda i: (i, 0))
    gq = jnp.tile(qk_gain[0], NSA_HEADS).reshape(1, NSA_DIM)
    gks = jnp.tile(qk_gain[2], NSA_KV_GROUPS).reshape(1, KV_LANES)
    gkw = jnp.tile(qk_gain[3], NSA_KV_GROUPS).reshape(1, KV_LANES)
    return pl.pallas_call(
        _nsa_prep_kernel,
        grid=(t // tm,),
        in_specs=[pl.BlockSpec((tm, NSA_DIM), lambda i: (i, qb)), kv_spec(2), kv_spec(3), kv_spec(4), kv_spec(5),
                  full(1, NSA_DIM), full(1, KV_LANES), full(1, KV_LANES), full(NSA_DIM, NSA_DIM), full(KV_LANES, KV_LANES)],
        out_specs=[row(NSA_DIM), row(KV_LANES), row(KV_LANES), row(KV_LANES), row(KV_LANES)],
        out_shape=[jax.ShapeDtypeStruct((t, NSA_DIM), BF16)] + [jax.ShapeDtypeStruct((t, KV_LANES), BF16)] * 4,
        compiler_params=pltpu.CompilerParams(dimension_semantics=("arbitrary",)),
        name="nsa_prep",
    )(u, u, u, u, u, gq, gks, gkw, _block_diag_mean(NSA_DIM), _block_diag_mean(KV_LANES))


def _gelu_tanh(x):
    return 0.5 * x * (1.0 + jnp.tanh(0.7978845608028654 * (x + 0.044715 * x * x * x)))


def _nsa_compress_kernel(uk_ref, uv_ref, pe_ref, w1_ref, w2_ref, gk_ref, bd_ref, kc_ref, vc_ref):
    n_row = uk_ref.shape[1] // CMP_STRIDE
    d = NSA_HEAD_DIM
    for kv, (src, dst) in enumerate(((uk_ref, kc_ref), (uv_ref, vc_ref))):
        top = [jnp.zeros((n_row, CMP_HIDDEN), F32) for _ in range(NSA_KV_GROUPS)]
        bot = [jnp.zeros((n_row, CMP_HIDDEN), F32) for _ in range(NSA_KV_GROUPS)]
        for l in range(CMP_STRIDE):
            x2 = src[0, pl.ds(l, n_row, stride=CMP_STRIDE), :]
            l2 = l + CMP_STRIDE
            for g in range(NSA_KV_GROUPS):
                x = x2[:, g * d:(g + 1) * d]
                top[g] += _dot((x + pe_ref[kv, l:l + 1, :]).astype(BF16), w1_ref[kv, l * d:(l + 1) * d, :])
                bot[g] += _dot((x + pe_ref[kv, l2:l2 + 1, :]).astype(BF16), w1_ref[kv, l2 * d:(l2 + 1) * d, :])
        outs = []
        for g in range(NSA_KV_GROUPS):
            hid = top[g] + pltpu.roll(bot[g], n_row - 1, 0)
            outs.append(_dot(_gelu_tanh(hid).astype(BF16), w2_ref[kv]))
        y = jnp.concatenate(outs, axis=1)
        if kv == 0:
            y = _group_rms(y, bd_ref[...], gk_ref[...])
        dst[0] = y.astype(BF16)


def nsa_compress(u3, qk_gain, cmp_pe, cmp_w1, cmp_w2, col_kv):
    b, s, _ = u3.shape
    kb = col_kv // KV_LANES
    n_row = s // CMP_STRIDE
    full = lambda *shape: pl.BlockSpec(shape, lambda i: (0,) * len(shape))
    gk = jnp.tile(qk_gain[1], NSA_KV_GROUPS).reshape(1, KV_LANES)
    out_spec = pl.BlockSpec((1, n_row, KV_LANES), lambda i: (i, 0, 0))
    return pl.pallas_call(
        _nsa_compress_kernel,
        grid=(b,),
        in_specs=[pl.BlockSpec((1, s, KV_LANES), lambda i: (i, 0, kb)),
                  pl.BlockSpec((1, s, KV_LANES), lambda i: (i, 0, kb + 1)),
                  full(2, CMP_LEN, NSA_HEAD_DIM), full(2, CMP_LEN * NSA_HEAD_DIM, CMP_HIDDEN),
                  full(2, CMP_HIDDEN, NSA_HEAD_DIM), full(1, KV_LANES), full(KV_LANES, KV_LANES)],
        out_specs=[out_spec, out_spec],
        out_shape=[jax.ShapeDtypeStruct((b, n_row, KV_LANES), BF16)] * 2,
        compiler_params=pltpu.CompilerParams(dimension_semantics=("arbitrary",)),
        name="nsa_compress",
    )(u3, u3, cmp_pe, cmp_w1.astype(BF16), cmp_w2.astype(BF16), gk, _block_diag_mean(KV_LANES))


def _stack_heads(q, g):
    d = NSA_HEAD_DIM
    base = g * NSA_HPG * d
    return jnp.concatenate([q[:, base + h * d: base + (h + 1) * d] for h in range(NSA_HPG)], axis=0)


def _nsa_select_kernel(q_ref, kc_ref, vc_ref, ov_ref, oc_ref, sel_ref, flag_ref):
    qbl, d = NSA_QBLOCK, NSA_HEAD_DIM
    n_key = kc_ref.shape[1]
    n_slc = ov_ref.shape[1]
    row = lax.broadcasted_iota(jnp.int32, (qbl, n_key), 0)
    last = lax.broadcasted_iota(jnp.int32, (qbl, n_key), 1) * CMP_STRIDE + CMP_LEN - 1
    j = lax.broadcasted_iota(jnp.int32, (n_slc, qbl), 0)
    tiles = range(SELECT_TILES)
    qts = [pl.program_id(1) * SELECT_TILES + i for i in tiles]
    tile_rows = [slice(i * qbl, (i + 1) * qbl) for i in tiles]

    def importance(i):
        dist = (qts[i] * qbl + row - last).astype(F32)
        ok = dist >= 0
        q = q_ref[0, tile_rows[i], :]
        imps = []
        for g in range(NSA_KV_GROUPS):
            kc = kc_ref[0, :, g * d:(g + 1) * d]
            vc = vc_ref[0, :, g * d:(g + 1) * d]
            s_all = _dot_nt(_stack_heads(q, g), kc)
            p_sum = jnp.zeros((qbl, n_key), F32)
            ps = []
            for h in range(NSA_HPG):
                s = jnp.where(ok, s_all[h * qbl:(h + 1) * qbl] - _head_slope(g, h) * dist, NEG_INF)
                m = jnp.max(s, axis=-1, keepdims=True)
                e = jnp.where(ok, jnp.exp(s - m), 0.0)
                l = jnp.sum(e, axis=-1, keepdims=True)
                p = e * jnp.where(l > 0, 1.0 / l, 0.0)
                ps.append(p.astype(BF16))
                p_sum += p
            o_all = _dot(jnp.concatenate(ps, axis=0), vc)
            for h in range(NSA_HPG):
                col = (g * NSA_HPG + h) * d
                oc_ref[0, tile_rows[i], col:col + d] = o_all[h * qbl:(h + 1) * qbl]
            imps.append(_split_dot(p_sum, ov_ref[...]))
        return jnp.concatenate(imps, axis=1).T

    def top_blocks(i, imp_t):
        forced = (j == 0) | (j == qts[i]) | (j == qts[i] - 1)
        visible = j <= qts[i]
        sels = []
        for g in range(NSA_KV_GROUPS):
            score = jnp.where(visible, imp_t[g * n_slc:(g + 1) * n_slc] + jnp.where(forced, SLC_FORCE_BONUS, 0.0), NEG_INF)
            sel = jnp.zeros((n_slc, qbl), F32)
            for _ in range(min(SLC_TOPN, n_slc)):
                m = jnp.max(score, axis=0, keepdims=True)
                first = jnp.min(jnp.where(score == m, j, n_slc), axis=0, keepdims=True)
                pick = j == first
                sel = jnp.where(pick, 1.0, sel)
                score = jnp.where(pick, -3e38, score)
            sels.append(jnp.where(visible, sel, 0.0))
        sel_all = jnp.concatenate(sels, axis=0).T
        sel_ref[0, tile_rows[i], :] = sel_all.astype(BF16)
        flag_ref[0, i] = jnp.broadcast_to(jnp.max(sel_all, axis=0, keepdims=True), flag_ref.shape[2:])

    imp_ts = [importance(i) for i in tiles]
    for i in tiles:
        top_blocks(i, imp_ts[i])


def nsa_select(q3, k_cmp, v_cmp):
    b, s, _ = q3.shape
    n_qt = s // NSA_QBLOCK
    n_slc = s // SLC_LEN
    n_key = k_cmp.shape[1]
    c_lo = jnp.arange(n_key) * CMP_STRIDE
    j_lo = jnp.arange(n_slc) * SLC_LEN
    overlap = ((c_lo[:, None] < j_lo[None, :] + SLC_LEN) & (c_lo[:, None] + CMP_LEN > j_lo[None, :])).astype(BF16)
    rows = SELECT_TILES * NSA_QBLOCK
    assert s % rows == 0
    tile = lambda c: pl.BlockSpec((1, rows, c), lambda i, t: (i, t, 0))
    per_b = pl.BlockSpec((1, n_key, KV_LANES), lambda i, t: (i, 0, 0))
    return pl.pallas_call(
        _nsa_select_kernel,
        grid=(b, s // rows),
        in_specs=[tile(NSA_DIM), per_b, per_b, pl.BlockSpec((n_key, n_slc), lambda i, t: (0, 0))],
        out_specs=[tile(NSA_DIM), tile(2 * n_slc), pl.BlockSpec((1, SELECT_TILES, 8, 2 * n_slc), lambda i, t: (i, t, 0, 0))],
        out_shape=[jax.ShapeDtypeStruct((b, s, NSA_DIM), F32), jax.ShapeDtypeStruct((b, s, 2 * n_slc), BF16),
                   jax.ShapeDtypeStruct((b, n_qt, 8, 2 * n_slc), F32)],
        compiler_params=pltpu.CompilerParams(dimension_semantics=("arbitrary", "arbitrary")),
        name="nsa_select",
    )(q3, k_cmp, v_cmp, overlap)


def _nsa_attend_kernel(count_ref, list_ref, q_ref, ks_ref, vs_ref, kw_ref, vw_ref, sel_ref, oc_ref, gate_ref, out_ref,
                       ksel_ref, vsel_ref):
    bi, qt = pl.program_id(0), pl.program_id(1)
    n_qt = pl.num_programs(1)
    qbl, d = NSA_QBLOCK, NSA_HEAD_DIM
    n_slc = sel_ref.shape[2] // NSA_KV_GROUPS
    n_word = n_slc // IDS_PER_WORD
    q = q_ref[0]
    gates = jax.nn.sigmoid(gate_ref[0])

    def chunk_iotas(width):
        lane = lax.broadcasted_iota(jnp.int32, (1, width * SLC_LEN), 1)
        return dict(lane=lane, slot_of_lane=lane // SLC_LEN,
                    t_sel=qt * qbl + lax.broadcasted_iota(jnp.int32, (qbl, width * SLC_LEN), 0),
                    j_iota=lax.broadcasted_iota(jnp.int32, (n_slc, width * SLC_LEN), 0))

    iotas = {width: chunk_iotas(width) for width in set(FIRST_CHUNK + (REST_CHUNK,))}
    win_start = jnp.maximum(qt - WINDOW // qbl, 0) * qbl
    t_win = qt * qbl + lax.broadcasted_iota(jnp.int32, (qbl, N_WIN_KEYS), 0)
    dist_win = t_win - (win_start + lax.broadcasted_iota(jnp.int32, (qbl, N_WIN_KEYS), 1))
    ok_win = (dist_win >= 0) & (dist_win < WINDOW)
    dist_win = dist_win.astype(F32)

    groups = range(NSA_KV_GROUPS)
    lanes = [slice(g * d, (g + 1) * d) for g in groups]
    qs = [_stack_heads(q, g) for g in groups]
    tile_g = [(bi * n_qt + qt) * NSA_KV_GROUPS + g for g in groups]
    n_sel = [count_ref[tg] for tg in tile_g]
    sel_g = [sel_ref[0, :, g * n_slc:(g + 1) * n_slc] for g in groups]

    def sel_scores(first, width, g):
        io = iotas[width]
        j_row = jnp.full((1, width * SLC_LEN), -1, jnp.int32)
        for slot in range(width):
            valid = first + slot < n_sel[g]
            word = jnp.minimum((first + slot) // IDS_PER_WORD, n_word - 1)
            jb = (list_ref[tile_g[g] * n_word + word] >> (8 * (slot % IDS_PER_WORD))) & 0xFF
            rows = pl.ds(pl.multiple_of(jb * SLC_LEN, SLC_LEN), SLC_LEN)
            ksel_ref[g, slot * SLC_LEN:(slot + 1) * SLC_LEN, :] = ks_ref[0, rows, lanes[g]]
            vsel_ref[g, slot * SLC_LEN:(slot + 1) * SLC_LEN, :] = vs_ref[0, rows, lanes[g]]
            j_row = jnp.where(io['slot_of_lane'] == slot, jnp.where(valid, jb, -1), j_row)
        s_all = _dot_nt(qs[g], ksel_ref[g, 0:width * SLC_LEN, :])
        chosen = _dot(sel_g[g], (io['j_iota'] == j_row).astype(BF16))
        dist = io['t_sel'] - (j_row * SLC_LEN + io['lane'] % SLC_LEN)
        ok = (chosen > 0.5) & (dist >= 0)
        return s_all, ok, dist.astype(F32)

    def sel_softmax(scores, carry, g):
        s_all, ok, dist = scores
        m_old, l_old, acc = carry
        ps, ms, ls = [], [], []
        for h in range(NSA_HPG):
            rows_h = slice(h * qbl, (h + 1) * qbl)
            s = jnp.where(ok, s_all[rows_h] - _head_slope(g, h) * dist, NEG_INF)
            m_new = jnp.maximum(m_old[rows_h], jnp.max(s, axis=-1, keepdims=True))
            p = jnp.exp(s - m_new)
            alpha = jnp.exp(m_old[rows_h] - m_new)
            ls.append(alpha * l_old[rows_h] + jnp.sum(p, axis=-1, keepdims=True))
            ms.append(m_new)
            ps.append(p.astype(BF16))
        m_new = jnp.concatenate(ms, axis=0)
        alpha = jnp.exp(m_old - m_new)
        acc = alpha * acc + _dot(jnp.concatenate(ps, axis=0), vsel_ref[g, 0:s_all.shape[1], :])
        return m_new, jnp.concatenate(ls, axis=0), acc

    win_rows = pl.ds(pl.multiple_of(win_start, qbl), N_WIN_KEYS)
    s_win = [_dot_nt(qs[g], kw_ref[0, win_rows, lanes[g]]) for g in groups]
    sc0 = [sel_scores(0, FIRST_CHUNK[g], g) for g in groups]
    p_win, inv_win = [], []
    for g in groups:
        ps, inv = [], []
        for h in range(NSA_HPG):
            s = jnp.where(ok_win, s_win[g][h * qbl:(h + 1) * qbl] - _head_slope(g, h) * dist_win, NEG_INF)
            e = jnp.exp(s - jnp.max(s, axis=-1, keepdims=True))
            inv.append(1.0 / jnp.sum(e, axis=-1, keepdims=True))
            ps.append(e.astype(BF16))
        p_win.append(jnp.concatenate(ps, axis=0))
        inv_win.append(jnp.concatenate(inv, axis=0))
    init = (jnp.full((NSA_HPG * qbl, 1), M_INIT, F32), jnp.zeros((NSA_HPG * qbl, 1), F32),
            jnp.zeros((NSA_HPG * qbl, d), F32))
    carry = [sel_softmax(sc0[g], init, g) for g in groups]
    o_win = [_dot(p_win[g], vw_ref[0, win_rows, lanes[g]]) * inv_win[g] for g in groups]

    for g in groups:
        n_rest = (jnp.maximum(n_sel[g] - FIRST_CHUNK[g], 0) + REST_CHUNK - 1) // REST_CHUNK
        rest = lax.fori_loop(0, n_rest, lambda c, cr, g=g: sel_softmax(
            sel_scores(FIRST_CHUNK[g] + c * REST_CHUNK, REST_CHUNK, g), cr, g), carry[g])
        o_sel = rest[2] * (1.0 / rest[1])
        for h in range(NSA_HPG):
            hh = g * NSA_HPG + h
            col = hh * d
            rows_h = slice(h * qbl, (h + 1) * qbl)
            out_ref[0, :, col:col + d] = (gates[:, hh:hh + 1] * oc_ref[0, :, col:col + d]
                                          + gates[:, NSA_HEADS + hh:NSA_HEADS + hh + 1] * o_sel[rows_h]
                                          + gates[:, 2 * NSA_HEADS + hh:2 * NSA_HEADS + hh + 1] * o_win[g][rows_h])


def nsa_attend(counts, lists, q3, ks, vs, kw, vw, sel, o_cmp, u3, col_gate):
    b, s, _ = q3.shape
    n_qt = s // NSA_QBLOCK
    chunks = FIRST_CHUNK + (REST_CHUNK,)
    assert col_gate % 128 == 0 and s >= N_WIN_KEYS and all(c % IDS_PER_WORD == 0 for c in chunks)
    gb = col_gate // 128
    tile = lambda c: pl.BlockSpec((1, NSA_QBLOCK, c), lambda i, t, counts, lists: (i, t, 0))
    per_b = pl.BlockSpec((1, s, KV_LANES), lambda i, t, counts, lists: (i, 0, 0))
    grid_spec = pltpu.PrefetchScalarGridSpec(
        num_scalar_prefetch=2,
        grid=(b, n_qt),
        in_specs=[tile(NSA_DIM), per_b, per_b, per_b, per_b, tile(sel.shape[2]), tile(NSA_DIM),
                  pl.BlockSpec((1, NSA_QBLOCK, 128), lambda i, t, counts, lists: (i, t, gb))],
        out_specs=tile(NSA_DIM),
        scratch_shapes=[pltpu.VMEM((NSA_KV_GROUPS, max(chunks) * SLC_LEN, NSA_HEAD_DIM), BF16),
                        pltpu.VMEM((NSA_KV_GROUPS, max(chunks) * SLC_LEN, NSA_HEAD_DIM), BF16)],
    )
    return pl.pallas_call(
        _nsa_attend_kernel,
        grid_spec=grid_spec,
        out_shape=jax.ShapeDtypeStruct((b, s, NSA_DIM), F32),
        compiler_params=pltpu.CompilerParams(dimension_semantics=("arbitrary", "arbitrary")),
        name="nsa_attend",
    )(counts, lists, q3, ks, vs, kw, vw, sel, o_cmp, u3)


def _pack_union_lists(flags, n_slc):
    assert n_slc <= 256 and n_slc % IDS_PER_WORD == 0
    b, n_qt = flags.shape[:2]
    f = flags[:, :, 0, :].reshape(b, n_qt, NSA_KV_GROUPS, n_slc) > 0.5
    fi = f.astype(jnp.int32)
    counts = jnp.sum(fi, axis=-1)
    ids = jnp.arange(n_slc, dtype=jnp.int32)
    ahead = (fi.astype(F32) @ (ids[:, None] <= ids[None, :]).astype(F32)).astype(jnp.int32)
    pos = jnp.where(f, ahead - 1, counts[..., None] + ids - ahead)
    order = jnp.sum(jnp.where(pos[..., :, None] == ids, ids[:, None], 0), axis=-2)
    order = order.reshape(b, n_qt, NSA_KV_GROUPS, n_slc // IDS_PER_WORD, IDS_PER_WORD)
    words = jnp.sum(order << (8 * jnp.arange(IDS_PER_WORD, dtype=jnp.int32)), axis=-1, dtype=jnp.int32)
    return counts.reshape(-1), words.reshape(-1)


def nsa_mixer_pallas(u3, qk_gain, cmp_pe, cmp_w1, cmp_w2, col_q, col_kv, col_gate):
    b, s, d_in = u3.shape
    q, ks, vs, kw, vw = nsa_prep(u3.reshape(b * s, d_in), qk_gain, col_q, col_kv)
    k_cmp, v_cmp = nsa_compress(u3, qk_gain, cmp_pe, cmp_w1, cmp_w2, col_kv)
    q3 = q.reshape(b, s, NSA_DIM)
    r3 = lambda a: a.reshape(b, s, KV_LANES)
    o_cmp, sel, flags = nsa_select(q3, k_cmp, v_cmp)
    counts, lists = _pack_union_lists(flags, s // SLC_LEN)
    return nsa_attend(counts, lists, q3, r3(ks), r3(vs), r3(kw), r3(vw), sel, o_cmp, u3, col_gate)


GDN_TILE = 512
GDN_HALO = 8


def _dot_tn(a, b):
    return lax.dot_general(a, b, (((0,), (0,)), ((), ())), preferred_element_type=F32)


def _softplus(x):
    return jnp.maximum(x, 0.0) + jnp.log(1.0 + jnp.exp(-jnp.abs(x)))


def _l2_norm(x):
    return x * lax.rsqrt(jnp.sum(x * x, axis=-1, keepdims=True) + RMS_EPS)


def _gdn_kernel(qkv_ref, z_ref, small_ref, cw_ref, coef_ref, gain_ref, y_ref, xe_ref, state_ref):
    tt = pl.program_id(1)
    tile, c, hd = GDN_TILE, GDN_CHUNK, GDN_HEAD_DIM

    @pl.when(tt == 0)
    def _():
        xe_ref[0:GDN_HALO, :] = jnp.zeros((GDN_HALO, xe_ref.shape[1]), F32)
        state_ref[...] = jnp.zeros_like(state_ref)

    @pl.when(tt > 0)
    def _():
        xe_ref[0:GDN_HALO, :] = xe_ref[tile:tile + GDN_HALO, :]

    xe_ref[GDN_HALO:, :] = qkv_ref[0]
    conv = jnp.zeros((tile, xe_ref.shape[1]), F32)
    for j in range(GDN_CONV):
        conv += cw_ref[j:j + 1, :] * xe_ref[pl.ds(GDN_HALO - (GDN_CONV - 1) + j, tile), :]
    act = conv * jax.nn.sigmoid(conv)

    small = small_ref[0]
    beta_all = jax.nn.sigmoid(small)
    g_all = coef_ref[0:1, :] * _softplus(small + coef_ref[1:2, :])
    row = lax.broadcasted_iota(jnp.int32, (c, SMALL_LANES), 0)
    ri = lax.broadcasted_iota(jnp.int32, (c, c), 0)
    ci = lax.broadcasted_iota(jnp.int32, (c, c), 1)
    lower = ri >= ci
    strict = ri > ci

    n_chunk = tile // c
    pairs = [(n, h) for n in range(n_chunk) for h in range(GDN_HEADS)]
    gcs, gc_ts = [], []
    for n in range(n_chunk):
        gc = g_all[n * c:(n + 1) * c]
        shift = 1
        while shift < c:
            gc = gc + jnp.where(row >= shift, pltpu.roll(gc, shift, 0), 0.0)
            shift *= 2
        gcs.append(gc)
        gc_ts.append(gc.T)
    pre = []
    for n, h in pairs:
        rows = slice(n * c, (n + 1) * c)
        q = _l2_norm(act[rows, h * hd:(h + 1) * hd]) * hd ** -0.5
        k = _l2_norm(act[rows, GDN_DIM + h * hd:GDN_DIM + (h + 1) * hd])
        v = act[rows, 2 * GDN_DIM + h * hd:2 * GDN_DIM + (h + 1) * hd]
        beta = beta_all[rows, LANE_BETA + h:LANE_BETA + h + 1]
        gcol = gcs[n][:, LANE_DECAY + h:LANE_DECAY + h + 1]
        grow = gc_ts[n][LANE_DECAY + h:LANE_DECAY + h + 1, :]
        g_last = gcol[c - 1:c, :]
        decay = jnp.where(lower, jnp.exp(jnp.where(lower, gcol - grow, 0.0)), 0.0)
        kb = k * beta
        kh = k.astype(BF16)
        pre.append(dict(
            lmat=jnp.where(strict, _dot_nt(kb.astype(BF16), kh) * decay, 0.0),
            rhs=jnp.concatenate([v * beta, kb * jnp.exp(gcol)], axis=1),
            attn=(_dot_nt(q.astype(BF16), kh) * decay).astype(BF16),
            q_dec=(q * jnp.exp(gcol)).astype(BF16),
            k_dec=(k * jnp.exp(g_last - gcol)).astype(BF16),
            d_last=jnp.exp(g_last)))
    eye = (ri == ci).astype(F32)
    power = [p['lmat'].astype(BF16) for p in pre]
    t_inv = [eye - p['lmat'] for p in pre]
    for _ in range(5):
        power = [_dot(lm, lm).astype(BF16) for lm in power]
        t_inv = [t + _dot(lm, t.astype(BF16)) for lm, t in zip(power, t_inv)]
    rhs = [_dot(t.astype(BF16), p['rhs'].astype(BF16)) for t, p in zip(t_inv, pre)]

    state = [state_ref[h] for h in range(GDN_HEADS)]
    for n in range(n_chunk):
        rows = slice(n * c, (n + 1) * c)
        sb = [s.astype(BF16) for s in state]
        ps = [pre[n * GDN_HEADS + h] for h in range(GDN_HEADS)]
        rs = [rhs[n * GDN_HEADS + h] for h in range(GDN_HEADS)]
        v_new = [(r[:, :hd] - _dot(r[:, hd:].astype(BF16), s)).astype(BF16) for r, s in zip(rs, sb)]
        outs = [_dot(p['q_dec'], s) + _dot(p['attn'], vn) for p, s, vn in zip(ps, sb, v_new)]
        state = [s * p['d_last'] + _dot_tn(p['k_dec'], vn) for p, s, vn in zip(ps, state, v_new)]
        for h, o in enumerate(outs):
            o = o * lax.rsqrt(jnp.mean(o * o, axis=-1, keepdims=True) + RMS_EPS) * gain_ref[...]
            zz = z_ref[0, rows, h * hd:(h + 1) * hd]
            y_ref[0, rows, h * hd:(h + 1) * hd] = o * (zz * jax.nn.sigmoid(zz))
    for h in range(GDN_HEADS):
        state_ref[h] = state[h]


def gdn_mixer(u3, col_qkv, col_z, col_small, conv_w, a_log, dt_bias, out_gain):
    b, s, _ = u3.shape
    lane = jnp.arange(SMALL_LANES)
    in_decay = (lane >= LANE_DECAY) & (lane < LANE_DECAY + GDN_HEADS)
    idx = jnp.clip(lane - LANE_DECAY, 0, GDN_HEADS - 1)
    coef = jnp.stack([jnp.where(in_decay, -jnp.exp(a_log)[idx], 0.0), jnp.where(in_decay, dt_bias[idx], 0.0)])
    qkv_w = 3 * GDN_DIM
    assert col_qkv % qkv_w == 0 and col_z % GDN_DIM == 0 and col_small % SMALL_LANES == 0
    full = lambda r, cc: pl.BlockSpec((r, cc), lambda i, t: (0, 0))
    return pl.pallas_call(
        _gdn_kernel,
        grid=(b, s // GDN_TILE),
        in_specs=[pl.BlockSpec((1, GDN_TILE, qkv_w), lambda i, t: (i, t, col_qkv // qkv_w)),
                  pl.BlockSpec((1, GDN_TILE, GDN_DIM), lambda i, t: (i, t, col_z // GDN_DIM)),
                  pl.BlockSpec((1, GDN_TILE, SMALL_LANES), lambda i, t: (i, t, col_small // SMALL_LANES)),
                  full(GDN_CONV, qkv_w), full(2, SMALL_LANES), full(1, GDN_HEAD_DIM)],
        out_specs=pl.BlockSpec((1, GDN_TILE, GDN_DIM), lambda i, t: (i, t, 0)),
        out_shape=jax.ShapeDtypeStruct((b, s, GDN_DIM), F32),
        scratch_shapes=[pltpu.VMEM((GDN_TILE + GDN_HALO, qkv_w), F32), pltpu.VMEM((GDN_HEADS, GDN_HEAD_DIM, GDN_HEAD_DIM), F32)],
        compiler_params=pltpu.CompilerParams(dimension_semantics=("arbitrary", "arbitrary")),
        name="gdn_mixer",
    )(u3, u3, u3, conv_w, coef, out_gain.reshape(1, GDN_HEAD_DIM))


def _merge_kernel(tiles_per_seq, x_ref, g_ref, u_ref, cw_ref, yn_ref, yg_ref, wb_ref, wg_ref, bg_ref, wo_ref, o_ref, xe_ref):
    x = x_ref[...]
    tm, d = x.shape
    cd = CONV_DIM
    first_of_seq = pl.program_id(0) % tiles_per_seq == 0

    @pl.when(first_of_seq)
    def _():
        xe_ref[0:GDN_HALO, :] = jnp.zeros((GDN_HALO, cd), F32)

    @pl.when(jnp.logical_not(first_of_seq))
    def _():
        xe_ref[0:GDN_HALO, :] = xe_ref[tm:tm + GDN_HALO, :]

    xe_ref[GDN_HALO:, :] = u_ref[:, cd:2 * cd] * u_ref[:, 2 * cd:3 * cd]
    conv = jnp.zeros((tm, cd), F32)
    for j in range(CONV_WIDTH):
        conv += cw_ref[j:j + 1, :] * xe_ref[pl.ds(GDN_HALO - (CONV_WIDTH - 1) + j, tm), :]
    y_a = u_ref[:, 0:cd] * conv

    h = (x * lax.rsqrt(jnp.mean(x * x, axis=-1, keepdims=True) + RMS_EPS) * g_ref[...]).astype(BF16)
    merged = jnp.zeros(x.shape, F32)
    for r, y in enumerate((y_a, yn_ref[...], yg_ref[...])):
        gate = jax.nn.sigmoid(_dot(h, wg_ref[:, r * d:(r + 1) * d]) + bg_ref[:, r * d:(r + 1) * d])
        merged += gate * _dot(y.astype(BF16), wb_ref[r])
    o_ref[...] = x + _dot(merged.astype(BF16), wo_ref[...])


def merge_branches(x2d, seq, gain, u2d, conv_w, y_n, y_g, w_branch, w_gate, b_gate, w_out, tm=512):
    t, d = x2d.shape
    assert seq % tm == 0 and COL_CONV == 0
    row = lambda c: pl.BlockSpec((tm, c), lambda i: (i, 0))
    full = lambda *shape: pl.BlockSpec(shape, lambda i: (0,) * len(shape))
    return pl.pallas_call(
        functools.partial(_merge_kernel, seq // tm),
        grid=(t // tm,),
        in_specs=[row(d), full(1, d), row(3 * CONV_DIM), full(CONV_WIDTH, CONV_DIM), row(BRANCH_DIM), row(BRANCH_DIM),
                  full(N_BRANCH, BRANCH_DIM, d), full(d, N_BRANCH * d), full(1, N_BRANCH * d), full(d, d)],
        out_specs=row(d),
        out_shape=jax.ShapeDtypeStruct((t, d), F32),
        scratch_shapes=[pltpu.VMEM((tm + GDN_HALO, CONV_DIM), F32)],
        compiler_params=pltpu.CompilerParams(dimension_semantics=("arbitrary",), vmem_limit_bytes=VMEM_LIMIT_BIG),
        name="merge_branches",
    )(x2d, gain.reshape(1, d), u2d, conv_w, y_n, y_g, w_branch.astype(BF16), w_gate.astype(BF16), b_gate.reshape(1, -1), w_out.astype(BF16))


def hybrid_mixer(x2d, b, s, norm_gain, w_in, conv_a_w, nsa_qk_gain, cmp_pe, cmp_w1, cmp_w2, gdn_conv_w, gdn_a_log, gdn_dt_bias, gdn_out_gain, w_branch, w_gate, b_gate, w_out):
    t = b * s
    u3 = norm_proj(x2d, norm_gain, permute_in_proj(w_in)).reshape(b, s, D_U)
    y_n = nsa_mixer_pallas(u3, nsa_qk_gain, cmp_pe, cmp_w1, cmp_w2, COL_NSA_Q, COL_NSA_KV, COL_SMALL)
    y_g = gdn_mixer(u3, COL_GDN_QKV, COL_GDN_Z, COL_SMALL, gdn_conv_w, gdn_a_log, gdn_dt_bias, gdn_out_gain)
    return merge_branches(x2d, s, norm_gain, u3.reshape(t, D_U), conv_a_w, y_n.reshape(t, BRANCH_DIM), y_g.reshape(t, BRANCH_DIM), w_branch, w_gate, b_gate, w_out)


MOE_ROWS = 512
ROUTE_LANES = 128
N_ROUTER = MOE_GROUPS + N_EXPERTS
ROW_DMA_UNROLL = 8


def _moe_route_kernel(x_ref, g_ref, wr_ref, br_ref, tri_ref, h_ref, route_ref, cnt_ref, run_ref):
    @pl.when(pl.program_id(0) == 0)
    def _():
        run_ref[...] = jnp.zeros_like(run_ref)

    x = x_ref[...]
    h = x * lax.rsqrt(jnp.mean(x * x, axis=-1, keepdims=True) + RMS_EPS) * g_ref[...]
    bits = lax.bitcast_convert_type(h.astype(BF16).astype(F32), jnp.uint32)
    half = h.shape[1] // 2
    packed = (bits[:, half:] & jnp.uint32(0xFFFF0000)) | (bits[:, :half] >> 16)
    h_ref[...] = packed.reshape(h_ref.shape)
    logits = _dot(h.astype(BF16), wr_ref[...]) + br_ref[...]
    lane = lax.broadcasted_iota(jnp.int32, logits.shape, 1)
    first_of = lambda hit: jnp.min(jnp.where(hit, lane, ROUTE_LANES), axis=-1, keepdims=True)
    is_grp = lane < MOE_GROUPS
    lg = jnp.where(is_grp, logits, NEG_INF)
    m_g = jnp.max(lg, axis=-1, keepdims=True)
    grp = first_of(lg == m_g)
    p_grp = 1.0 / jnp.sum(jnp.where(is_grp, jnp.exp(lg - m_g), 0.0), axis=-1, keepdims=True)
    lo = MOE_GROUPS + grp * EXPERTS_PER_GROUP
    le = jnp.where((lane >= lo) & (lane < lo + EXPERTS_PER_GROUP), logits, NEG_INF)
    m1 = jnp.max(le, axis=-1, keepdims=True)
    i1 = first_of(le == m1)
    le2 = jnp.where(lane == i1, NEG_INF, le)
    m2 = jnp.max(le2, axis=-1, keepdims=True)
    i2 = first_of(le2 == m2)
    r = jnp.exp(m2 - m1)
    g1 = p_grp / (1.0 + r)
    g2 = p_grp * r / (1.0 + r)
    e1 = i1 - MOE_GROUPS
    e2 = i2 - MOE_GROUPS
    hit1 = lane == e1
    hit2 = lane == e2
    onehot = (hit1 | hit2).astype(BF16)
    before = _dot(tri_ref[...], onehot) + run_ref[...]
    r1 = jnp.sum(jnp.where(hit1, before, 0.0), axis=-1, keepdims=True)
    r2 = jnp.sum(jnp.where(hit2, before, 0.0), axis=-1, keepdims=True)
    run_ref[...] += jnp.sum(onehot.astype(F32), axis=0, keepdims=True)
    rec = jnp.zeros(logits.shape, F32)
    for k, v in enumerate((e1.astype(F32), e2.astype(F32), r1, r2, g1, g2)):
        rec = jnp.where(lane == k, v, rec)
    route_ref[...] = rec
    cnt_ref[...] = jnp.broadcast_to(run_ref[...], cnt_ref.shape)


def moe_route(x2d, gain, w_rg, b_rg, w_re, b_re, tm=512):
    t, d = x2d.shape
    pad = ROUTE_LANES - N_ROUTER
    wr = jnp.pad(jnp.concatenate([w_rg, w_re], axis=1), ((0, 0), (0, pad))).astype(BF16)
    br = jnp.pad(jnp.concatenate([b_rg, b_re]), (0, pad)).reshape(1, ROUTE_LANES)
    tri = (jnp.arange(tm)[:, None] > jnp.arange(tm)[None, :]).astype(BF16)
    full = lambda r, c: pl.BlockSpec((r, c), lambda i: (0, 0))
    return pl.pallas_call(
        _moe_route_kernel,
        grid=(t // tm,),
        in_specs=[pl.BlockSpec((tm, d), lambda i: (i, 0)), full(1, d), full(d, ROUTE_LANES), full(1, ROUTE_LANES), full(tm, tm)],
        out_specs=[pl.BlockSpec((tm, 1, d // 2), lambda i: (i, 0, 0)), pl.BlockSpec((tm, ROUTE_LANES), lambda i: (i, 0)), full(8, ROUTE_LANES)],
        out_shape=[jax.ShapeDtypeStruct((t, 1, d // 2), jnp.uint32), jax.ShapeDtypeStruct((t, ROUTE_LANES), F32), jax.ShapeDtypeStruct((8, ROUTE_LANES), F32)],
        scratch_shapes=[pltpu.VMEM((1, ROUTE_LANES), F32)],
        compiler_params=pltpu.CompilerParams(dimension_semantics=("arbitrary",)),
        name="moe_route",
    )(x2d, gain.reshape(1, d), wr, br, tri)


def _row_copy(src_ref, src_row, dst_ref, dst_row, sem):
    return pltpu.make_async_copy(src_ref.at[src_row], dst_ref.at[dst_row], sem)


def _moe_dispatch_kernel(dest_ref, h_ref, buf_in_ref, buf_ref, sem):
    del buf_in_ref
    tm = h_ref.shape[0]
    base = pl.program_id(0) * tm

    def send(r, carry):
        for k in range(TOPK_IN_GROUP):
            _row_copy(h_ref, r, buf_ref, dest_ref[(base + r) * TOPK_IN_GROUP + k], sem).start(priority=k)
        return carry

    lax.fori_loop(0, tm, send, 0, unroll=ROW_DMA_UNROLL)
    for k in range(TOPK_IN_GROUP):
        pltpu.make_async_copy(h_ref, buf_ref.at[pl.ds(0, tm)], sem).wait()


def moe_dispatch(dest, h3, n_rows, tm=512):
    t, _, d = h3.shape
    grid_spec = pltpu.PrefetchScalarGridSpec(
        num_scalar_prefetch=1, grid=(t // tm,),
        in_specs=[pl.BlockSpec((tm, 1, d), lambda i, dest: (i, 0, 0)), pl.BlockSpec(memory_space=pl.ANY)],
        out_specs=pl.BlockSpec(memory_space=pl.ANY),
        scratch_shapes=[pltpu.SemaphoreType.DMA(())],
    )
    return pl.pallas_call(
        _moe_dispatch_kernel, grid_spec=grid_spec,
        out_shape=jax.ShapeDtypeStruct((n_rows, 1, d), h3.dtype),
        input_output_aliases={2: 0},
        compiler_params=pltpu.CompilerParams(dimension_semantics=("arbitrary",), has_side_effects=True),
        name="moe_dispatch",
    )(dest, h3, jnp.zeros((n_rows, 1, d), h3.dtype))


def _moe_ffn_kernel(blk_e_ref, n_used_ref, x_ref, wg_ref, wu_ref, wd_ref, y_ref):
    del blk_e_ref
    used = pl.program_id(0) < n_used_ref[0]

    @pl.when(used)
    def _():
        w = x_ref[...].reshape(x_ref.shape[0], x_ref.shape[2])
        lo = lax.bitcast_convert_type(w << 16, F32)
        hi = lax.bitcast_convert_type(w & jnp.uint32(0xFFFF0000), F32)
        xb = jnp.concatenate([lo, hi], axis=1).astype(BF16)
        a = _dot(xb, wg_ref[0])
        mid = a * jax.nn.sigmoid(a) * _dot(xb, wu_ref[0])
        y_ref[...] = _dot(mid.astype(BF16), wd_ref[0]).reshape(y_ref.shape)

    @pl.when(jnp.logical_not(used))
    def _():
        y_ref[...] = jnp.zeros_like(y_ref)


def moe_ffn(blk_expert, n_used, buf3, w_eg, w_eu, w_ed):
    n_rows, _, packed_w = buf3.shape
    d, ff = w_eg.shape[1:]
    w_in_spec = pl.BlockSpec((1, d, ff), lambda b, blk_e, n_used: (blk_e[b], 0, 0))
    grid_spec = pltpu.PrefetchScalarGridSpec(
        num_scalar_prefetch=2, grid=(n_rows // MOE_ROWS,),
        in_specs=[pl.BlockSpec((MOE_ROWS, 1, packed_w), lambda b, blk_e, n_used: (b, 0, 0)), w_in_spec, w_in_spec,
                  pl.BlockSpec((1, ff, d), lambda b, blk_e, n_used: (blk_e[b], 0, 0))],
        out_specs=pl.BlockSpec((MOE_ROWS, 1, d), lambda b, blk_e, n_used: (b, 0, 0)),
    )
    return pl.pallas_call(
        _moe_ffn_kernel, grid_spec=grid_spec,
        out_shape=jax.ShapeDtypeStruct((n_rows, 1, d), F32),
        compiler_params=pltpu.CompilerParams(dimension_semantics=("arbitrary",)),
        name="moe_ffn",
    )(blk_expert, n_used, buf3, w_eg.astype(BF16), w_eu.astype(BF16), w_ed.astype(BF16))


def _moe_combine_kernel(dest_ref, y_ref, rec_ref, x_ref, out_ref, ya_ref, yb_ref, sem):
    tm = out_ref.shape[0]
    base = pl.program_id(0) * tm

    def fetch(r, carry):
        _row_copy(y_ref, dest_ref[(base + r) * TOPK_IN_GROUP], ya_ref, r, sem).start(priority=0)
        _row_copy(y_ref, dest_ref[(base + r) * TOPK_IN_GROUP + 1], yb_ref, r, sem).start(priority=1)
        return carry

    lax.fori_loop(0, tm, fetch, 0, unroll=ROW_DMA_UNROLL)
    pltpu.make_async_copy(y_ref.at[pl.ds(0, tm)], ya_ref, sem).wait()
    pltpu.make_async_copy(y_ref.at[pl.ds(0, tm)], yb_ref, sem).wait()
    rec = rec_ref[...]
    ya = ya_ref[...].reshape(out_ref.shape)
    yb = yb_ref[...].reshape(out_ref.shape)
    out_ref[...] = x_ref[...] + rec[:, 4:5] * ya + rec[:, 5:6] * yb


def moe_combine(dest, y3, rec, x2d, tm=512):
    t, d = x2d.shape
    grid_spec = pltpu.PrefetchScalarGridSpec(
        num_scalar_prefetch=1, grid=(t // tm,),
        in_specs=[pl.BlockSpec(memory_space=pl.ANY), pl.BlockSpec((tm, ROUTE_LANES), lambda i, dest: (i, 0)),
                  pl.BlockSpec((tm, d), lambda i, dest: (i, 0))],
        out_specs=pl.BlockSpec((tm, d), lambda i, dest: (i, 0)),
        scratch_shapes=[pltpu.VMEM((tm, 1, d), F32), pltpu.VMEM((tm, 1, d), F32), pltpu.SemaphoreType.DMA(())],
    )
    return pl.pallas_call(
        _moe_combine_kernel, grid_spec=grid_spec,
        out_shape=jax.ShapeDtypeStruct((t, d), F32),
        compiler_params=pltpu.CompilerParams(dimension_semantics=("arbitrary",)),
        name="moe_combine",
    )(dest, y3, rec, x2d)


def hier_moe_pallas(x2d, gain, w_rg, b_rg, w_re, b_re, w_eg, w_eu, w_ed):
    t, d = x2d.shape
    h, rec, cnt = moe_route(x2d, gain, w_rg, b_rg, w_re, b_re)
    counts = cnt[0, :N_EXPERTS].astype(jnp.int32)
    n_blk = (counts + MOE_ROWS - 1) // MOE_ROWS
    blk_end = jnp.cumsum(n_blk)
    pad_start = (blk_end - n_blk) * MOE_ROWS
    experts = rec[:, 0:2].astype(jnp.int32)
    dest = (pad_start[experts] + rec[:, 2:4].astype(jnp.int32)).reshape(-1)
    total_blk = t * TOPK_IN_GROUP // MOE_ROWS + N_EXPERTS
    blk_expert = jnp.sum(jnp.arange(total_blk)[:, None] >= blk_end[None, :], axis=1, dtype=jnp.int32)
    blk_expert = jnp.minimum(blk_expert, N_EXPERTS - 1)
    buf = moe_dispatch(dest, h, total_blk * MOE_ROWS)
    y = moe_ffn(blk_expert, blk_end[-1:].astype(jnp.int32), buf, w_eg, w_eu, w_ed)
    return moe_combine(dest, y, rec, x2d)


def kernel(x, norm_mix, w_in, conv_a_w, nsa_qk_gain, cmp_pe, cmp_w1, cmp_w2, gdn_conv_w, gdn_a_log, gdn_dt_bias, gdn_out_gain, w_branch, w_gate, b_gate, w_out, norm_ffn, w_router_group, b_router_group, w_router_expert, b_router_expert, w_expert_gate, w_expert_up, w_expert_down):
    b, s, dm = x.shape
    x = x.reshape(b * s, dm)
    for l in range(DEPTH):
        x = hybrid_mixer(x, b, s, norm_mix[l], w_in[l], conv_a_w[l], nsa_qk_gain[l], cmp_pe[l], cmp_w1[l], cmp_w2[l], gdn_conv_w[l], gdn_a_log[l], gdn_dt_bias[l], gdn_out_gain[l], w_branch[l], w_gate[l], b_gate[l], w_out[l])
        x = hier_moe_pallas(x, norm_ffn[l], w_router_group[l], b_router_group[l], w_router_expert[l], b_router_expert[l], w_expert_gate[l], w_expert_up[l], w_expert_down[l])
    return x.reshape(b, s, dm)
```

```python
import functools

import jax
import jax.numpy as jnp
from jax import lax
from jax.experimental import pallas as pl
from jax.experimental.pallas import tpu as pltpu

D_MODEL = 1024
DEPTH = 4
CONV_DIM = 512
CONV_WIDTH = 3
NSA_HEADS = 8
NSA_KV_GROUPS = 2
NSA_HPG = NSA_HEADS // NSA_KV_GROUPS
NSA_HEAD_DIM = 64
NSA_DIM = NSA_HEADS * NSA_HEAD_DIM
CMP_LEN = 32
CMP_STRIDE = 16
CMP_HIDDEN = 256
SLC_LEN = 64
SLC_TOPN = 8
SLC_FORCE_BONUS = 1e6
WINDOW = 512
NSA_QBLOCK = 64
GDN_HEADS = 4
GDN_HEAD_DIM = 128
GDN_DIM = GDN_HEADS * GDN_HEAD_DIM
GDN_CONV = 4
GDN_CHUNK = 64
N_BRANCH = 3
BRANCH_DIM = 512
IN_SPLITS = (3 * CONV_DIM, NSA_DIM, 6 * NSA_KV_GROUPS * NSA_HEAD_DIM, 3 * NSA_HEADS, 3 * GDN_DIM, GDN_DIM, GDN_HEADS, GDN_HEADS)
D_IN = sum(IN_SPLITS)
COL_CONV = 0
COL_GDN_QKV = 3 * CONV_DIM
COL_NSA_Q = COL_GDN_QKV + 3 * GDN_DIM
COL_GDN_Z = COL_NSA_Q + NSA_DIM
COL_NSA_KV = COL_GDN_Z + GDN_DIM
COL_SMALL = COL_NSA_KV + 6 * NSA_KV_GROUPS * NSA_HEAD_DIM
SMALL_LANES = 128
D_U = COL_SMALL + SMALL_LANES
LANE_BETA = 3 * NSA_HEADS
LANE_DECAY = LANE_BETA + GDN_HEADS


def permute_in_proj(w_in):
    conv, nq, nkv, ng, gqkv, gz, gb, ga = split_last(w_in, IN_SPLITS)
    pad = jnp.zeros((w_in.shape[0], SMALL_LANES - LANE_DECAY - GDN_HEADS), w_in.dtype)
    return jnp.concatenate([conv, gqkv, nq, gz, nkv, ng, gb, ga, pad], axis=1)


def split_last(u, sizes):
    out, start = [], 0
    for n in sizes:
        out.append(u[..., start:start + n])
        start += n
    return out
MOE_GROUPS = 4
EXPERTS_PER_GROUP = 8
N_EXPERTS = MOE_GROUPS * EXPERTS_PER_GROUP
TOPK_IN_GROUP = 2
EXPERT_FF = 512
RMS_EPS = 1e-6
NEG_INF = -1e30

F32 = jnp.float32
BF16 = jnp.bfloat16
VMEM_LIMIT_BIG = 56 * 1024 * 1024


def _norm_proj_kernel(x_ref, g_ref, w_ref, o_ref, h_ref):
    @pl.when(pl.program_id(1) == 0)
    def _():
        x = x_ref[...]
        y = x * lax.rsqrt(jnp.mean(x * x, axis=-1, keepdims=True) + RMS_EPS)
        h_ref[...] = (y * g_ref[...]).astype(BF16)

    o_ref[...] = jnp.dot(h_ref[...], w_ref[...], preferred_element_type=F32)


def norm_proj(x2d, gain, w, tm=512, tn=D_U):
    t, d = x2d.shape
    n = w.shape[1]
    n_blk = pl.cdiv(n, tn)
    wb = jnp.pad(w.astype(BF16), ((0, 0), (0, n_blk * tn - n)))
    return pl.pallas_call(
        _norm_proj_kernel,
        grid=(t // tm, n_blk),
        in_specs=[
            pl.BlockSpec((tm, d), lambda i, j: (i, 0)),
            pl.BlockSpec((1, d), lambda i, j: (0, 0)),
            pl.BlockSpec((d, tn), lambda i, j: (0, j)),
        ],
        out_specs=pl.BlockSpec((tm, tn), lambda i, j: (i, j)),
        out_shape=jax.ShapeDtypeStruct((t, n), F32),
        scratch_shapes=[pltpu.VMEM((tm, d), BF16)],
        compiler_params=pltpu.CompilerParams(dimension_semantics=("arbitrary", "arbitrary"),
                                             vmem_limit_bytes=VMEM_LIMIT_BIG),
        name="norm_proj",
    )(x2d, gain.reshape(1, d), wb)


KV_LANES = NSA_KV_GROUPS * NSA_HEAD_DIM
N_WIN_KEYS = WINDOW + NSA_QBLOCK
FIRST_CHUNK = (8, 12)
REST_CHUNK = 8
IDS_PER_WORD = 4
SELECT_TILES = 8
M_INIT = -1e29


def _dot(a, b):
    return jnp.dot(a, b, preferred_element_type=F32)


def _dot_nt(a, b):
    return lax.dot_general(a, b, (((1,), (1,)), ((), ())), preferred_element_type=F32)


def _split_dot(x, m):
    hi = x.astype(BF16)
    lo = (x - hi.astype(F32)).astype(BF16)
    return _dot(hi, m) + _dot(lo, m)


def _head_slope(g, h):
    return 2.0 ** (-8.0 * (g * NSA_HPG + h + 1) / NSA_HEADS)


def _group_rms(x, bd, gain):
    ms = _split_dot(x * x, bd)
    return x * lax.rsqrt(ms + RMS_EPS) * gain


def _nsa_prep_kernel(uq_ref, uks_ref, uvs_ref, ukw_ref, uvw_ref, gq_ref, gks_ref, gkw_ref, bdq_ref, bdk_ref,
                     q_ref, ks_ref, vs_ref, kw_ref, vw_ref):
    scale = NSA_HEAD_DIM ** -0.5
    q_ref[...] = (_group_rms(uq_ref[...], bdq_ref[...], gq_ref[...]) * scale).astype(BF16)
    ks_ref[...] = _group_rms(uks_ref[...], bdk_ref[...], gks_ref[...]).astype(BF16)
    kw_ref[...] = _group_rms(ukw_ref[...], bdk_ref[...], gkw_ref[...]).astype(BF16)
    vs_ref[...] = uvs_ref[...].astype(BF16)
    vw_ref[...] = uvw_ref[...].astype(BF16)


def _block_diag_mean(n):
    i = jnp.arange(n) // NSA_HEAD_DIM
    return ((i[:, None] == i[None, :]).astype(F32) / NSA_HEAD_DIM).astype(BF16)


def nsa_prep(u, qk_gain, col_q, col_kv, tm=512):
    t = u.shape[0]
    assert t % tm == 0 and col_q % NSA_DIM == 0 and col_kv % KV_LANES == 0
    qb = col_q // NSA_DIM
    kb = col_kv // KV_LANES
    kv_spec = lambda j: pl.BlockSpec((tm, KV_LANES), lambda i, j=j: (i, kb + j))
    full = lambda r, c: pl.BlockSpec((r, c), lambda i: (0, 0))
    row = lambda c: pl.BlockSpec((tm, c), lambda i: (i, 0))
    gq = jnp.tile(qk_gain[0], NSA_HEADS).reshape(1, NSA_DIM)
    gks = jnp.tile(qk_gain[2], NSA_KV_GROUPS).reshape(1, KV_LANES)
    gkw = jnp.tile(qk_gain[3], NSA_KV_GROUPS).reshape(1, KV_LANES)
    return pl.pallas_call(
        _nsa_prep_kernel,
        grid=(t // tm,),
        in_specs=[pl.BlockSpec((tm, NSA_DIM), lambda i: (i, qb)), kv_spec(2), kv_spec(3), kv_spec(4), kv_spec(5),
                  full(1, NSA_DIM), full(1, KV_LANES), full(1, KV_LANES), full(NSA_DIM, NSA_DIM), full(KV_LANES, KV_LANES)],
        out_specs=[row(NSA_DIM), row(KV_LANES), row(KV_LANES), row(KV_LANES), row(KV_LANES)],
        out_shape=[jax.ShapeDtypeStruct((t, NSA_DIM), BF16)] + [jax.ShapeDtypeStruct((t, KV_LANES), BF16)] * 4,
        compiler_params=pltpu.CompilerParams(dimension_semantics=("arbitrary",)),
        name="nsa_prep",
    )(u, u, u, u, u, gq, gks, gkw, _block_diag_mean(NSA_DIM), _block_diag_mean(KV_LANES))


def _gelu_tanh(x):
    return 0.5 * x * (1.0 + jnp.tanh(0.7978845608028654 * (x + 0.044715 * x * x * x)))


def _nsa_compress_kernel(uk_ref, uv_ref, pe_ref, w1_ref, w2_ref, gk_ref, bd_ref, kc_ref, vc_ref):
    n_row = uk_ref.shape[1] // CMP_STRIDE
    d = NSA_HEAD_DIM
    for kv, (src, dst) in enumerate(((uk_ref, kc_ref), (uv_ref, vc_ref))):
        top = [jnp.zeros((n_row, CMP_HIDDEN), F32) for _ in range(NSA_KV_GROUPS)]
        bot = [jnp.zeros((n_row, CMP_HIDDEN), F32) for _ in range(NSA_KV_GROUPS)]
        for l in range(CMP_STRIDE):
            x2 = src[0, pl.ds(l, n_row, stride=CMP_STRIDE), :]
            l2 = l + CMP_STRIDE
            for g in range(NSA_KV_GROUPS):
                x = x2[:, g * d:(g + 1) * d]
                top[g] += _dot((x + pe_ref[kv, l:l + 1, :]).astype(BF16), w1_ref[kv, l * d:(l + 1) * d, :])
                bot[g] += _dot((x + pe_ref[kv, l2:l2 + 1, :]).astype(BF16), w1_ref[kv, l2 * d:(l2 + 1) * d, :])
        outs = []
        for g in range(NSA_KV_GROUPS):
            hid = top[g] + pltpu.roll(bot[g], n_row - 1, 0)
            outs.append(_dot(_gelu_tanh(hid).astype(BF16), w2_ref[kv]))
        y = jnp.concatenate(outs, axis=1)
        if kv == 0:
            y = _group_rms(y, bd_ref[...], gk_ref[...])
        dst[0] = y.astype(BF16)


def nsa_compress(u3, qk_gain, cmp_pe, cmp_w1, cmp_w2, col_kv):
    b, s, _ = u3.shape
    kb = col_kv // KV_LANES
    n_row = s // CMP_STRIDE
    full = lambda *shape: pl.BlockSpec(shape, lambda i: (0,) * len(shape))
    gk = jnp.tile(qk_gain[1], NSA_KV_GROUPS).reshape(1, KV_LANES)
    out_spec = pl.BlockSpec((1, n_row, KV_LANES), lambda i: (i, 0, 0))
    return pl.pallas_call(
        _nsa_compress_kernel,
        grid=(b,),
        in_specs=[pl.BlockSpec((1, s, KV_LANES), lambda i: (i, 0, kb)),
                  pl.BlockSpec((1, s, KV_LANES), lambda i: (i, 0, kb + 1)),
                  full(2, CMP_LEN, NSA_HEAD_DIM), full(2, CMP_LEN * NSA_HEAD_DIM, CMP_HIDDEN),
                  full(2, CMP_HIDDEN, NSA_HEAD_DIM), full(1, KV_LANES), full(KV_LANES, KV_LANES)],
        out_specs=[out_spec, out_spec],
        out_shape=[jax.ShapeDtypeStruct((b, n_row, KV_LANES), BF16)] * 2,
        compiler_params=pltpu.CompilerParams(dimension_semantics=("arbitrary",)),
        name="nsa_compress",
    )(u3, u3, cmp_pe, cmp_w1.astype(BF16), cmp_w2.astype(BF16), gk, _block_diag_mean(KV_LANES))


def _stack_heads(q, g):
    d = NSA_HEAD_DIM
    base = g * NSA_HPG * d
    return jnp.concatenate([q[:, base + h * d: base + (h + 1) * d] for h in range(NSA_HPG)], axis=0)


def _nsa_select_kernel(q_ref, kc_ref, vc_ref, ov_ref, oc_ref, sel_ref, flag_ref):
    qbl, d = NSA_QBLOCK, NSA_HEAD_DIM
    n_key = kc_ref.shape[1]
    n_slc = ov_ref.shape[1]
    row = lax.broadcasted_iota(jnp.int32, (qbl, n_key), 0)
    last = lax.broadcasted_iota(jnp.int32, (qbl, n_key), 1) * CMP_STRIDE + CMP_LEN - 1
    j = lax.broadcasted_iota(jnp.int32, (n_slc, qbl), 0)
    tiles = range(SELECT_TILES)
    qts = [pl.program_id(1) * SELECT_TILES + i for i in tiles]
    tile_rows = [slice(i * qbl, (i + 1) * qbl) for i in tiles]

    def importance(i):
        dist = (qts[i] * qbl + row - last).astype(F32)
        ok = dist >= 0
        q = q_ref[0, tile_rows[i], :]
        imps = []
        for g in range(NSA_KV_GROUPS):
            kc = kc_ref[0, :, g * d:(g + 1) * d]
            vc = vc_ref[0, :, g * d:(g + 1) * d]
            s_all = _dot_nt(_stack_heads(q, g), kc)
            p_sum = jnp.zeros((qbl, n_key), F32)
            ps = []
            for h in range(NSA_HPG):
                s = jnp.where(ok, s_all[h * qbl:(h + 1) * qbl] - _head_slope(g, h) * dist, NEG_INF)
                m = jnp.max(s, axis=-1, keepdims=True)
                e = jnp.where(ok, jnp.exp(s - m), 0.0)
                l = jnp.sum(e, axis=-1, keepdims=True)
                p = e * jnp.where(l > 0, 1.0 / l, 0.0)
                ps.append(p.astype(BF16))
                p_sum += p
            o_all = _dot(jnp.concatenate(ps, axis=0), vc)
            for h in range(NSA_HPG):
                col = (g * NSA_HPG + h) * d
                oc_ref[0, tile_rows[i], col:col + d] = o_all[h * qbl:(h + 1) * qbl]
            imps.append(_split_dot(p_sum, ov_ref[...]))
        return jnp.concatenate(imps, axis=1).T

    def top_blocks(i, imp_t):
        forced = (j == 0) | (j == qts[i]) | (j == qts[i] - 1)
        visible = j <= qts[i]
        sels = []
        for g in range(NSA_KV_GROUPS):
            score = jnp.where(visible, imp_t[g * n_slc:(g + 1) * n_slc] + jnp.where(forced, SLC_FORCE_BONUS, 0.0), NEG_INF)
            sel = jnp.zeros((n_slc, qbl), F32)
            for _ in range(min(SLC_TOPN, n_slc)):
                m = jnp.max(score, axis=0, keepdims=True)
                first = jnp.min(jnp.where(score == m, j, n_slc), axis=0, keepdims=True)
                pick = j == first
                sel = jnp.where(pick, 1.0, sel)
                score = jnp.where(pick, -3e38, score)
            sels.append(jnp.where(visible, sel, 0.0))
        sel_all = jnp.concatenate(sels, axis=0).T
        sel_ref[0, tile_rows[i], :] = sel_all.astype(BF16)
        flag_ref[0, i] = jnp.broadcast_to(jnp.max(sel_all, axis=0, keepdims=True), flag_ref.shape[2:])

    imp_ts = [importance(i) for i in tiles]
    for i in tiles:
        top_blocks(i, imp_ts[i])


def nsa_select(q3, k_cmp, v_cmp):
    b, s, _ = q3.shape
    n_qt = s // NSA_QBLOCK
    n_slc = s // SLC_LEN
    n_key = k_cmp.shape[1]
    c_lo = jnp.arange(n_key) * CMP_STRIDE
    j_lo = jnp.arange(n_slc) * SLC_LEN
    overlap = ((c_lo[:, None] < j_lo[None, :] + SLC_LEN) & (c_lo[:, None] + CMP_LEN > j_lo[None, :])).astype(BF16)
    rows = SELECT_TILES * NSA_QBLOCK
    assert s % rows == 0
    tile = lambda c: pl.BlockSpec((1, rows, c), lambda i, t: (i, t, 0))
    per_b = pl.BlockSpec((1, n_key, KV_LANES), lambda i, t: (i, 0, 0))
    return pl.pallas_call(
        _nsa_select_kernel,
        grid=(b, s // rows),
        in_specs=[tile(NSA_DIM), per_b, per_b, pl.BlockSpec((n_key, n_slc), lambda i, t: (0, 0))],
        out_specs=[tile(NSA_DIM), tile(2 * n_slc), pl.BlockSpec((1, SELECT_TILES, 8, 2 * n_slc), lambda i, t: (i, t, 0, 0))],
        out_shape=[jax.ShapeDtypeStruct((b, s, NSA_DIM), F32), jax.ShapeDtypeStruct((b, s, 2 * n_slc), BF16),
                   jax.ShapeDtypeStruct((b, n_qt, 8, 2 * n_slc), F32)],
        compiler_params=pltpu.CompilerParams(dimension_semantics=("arbitrary", "arbitrary")),
        name="nsa_select",
    )(q3, k_cmp, v_cmp, overlap)


def _nsa_attend_kernel(count_ref, list_ref, q_ref, ks_ref, vs_ref, kw_ref, vw_ref, sel_ref, oc_ref, gate_ref, out_ref,
                       ksel_ref, vsel_ref):
    bi, qt = pl.program_id(0), pl.program_id(1)
    n_qt = pl.num_programs(1)
    qbl, d = NSA_QBLOCK, NSA_HEAD_DIM
    n_slc = sel_ref.shape[2] // NSA_KV_GROUPS
    n_word = n_slc // IDS_PER_WORD
    q = q_ref[0]
    gates = jax.nn.sigmoid(gate_ref[0])

    def chunk_iotas(width):
        lane = lax.broadcasted_iota(jnp.int32, (1, width * SLC_LEN), 1)
        return dict(lane=lane, slot_of_lane=lane // SLC_LEN,
                    t_sel=qt * qbl + lax.broadcasted_iota(jnp.int32, (qbl, width * SLC_LEN), 0),
                    j_iota=lax.broadcasted_iota(jnp.int32, (n_slc, width * SLC_LEN), 0))

    iotas = {width: chunk_iotas(width) for width in set(FIRST_CHUNK + (REST_CHUNK,))}
    win_start = jnp.maximum(qt - WINDOW // qbl, 0) * qbl
    t_win = qt * qbl + lax.broadcasted_iota(jnp.int32, (qbl, N_WIN_KEYS), 0)
    dist_win = t_win - (win_start + lax.broadcasted_iota(jnp.int32, (qbl, N_WIN_KEYS), 1))
    ok_win = (dist_win >= 0) & (dist_win < WINDOW)
    dist_win = dist_win.astype(F32)

    groups = range(NSA_KV_GROUPS)
    lanes = [slice(g * d, (g + 1) * d) for g in groups]
    qs = [_stack_heads(q, g) for g in groups]
    tile_g = [(bi * n_qt + qt) * NSA_KV_GROUPS + g for g in groups]
    n_sel = [count_ref[tg] for tg in tile_g]
    sel_g = [sel_ref[0, :, g * n_slc:(g + 1) * n_slc] for g in groups]

    def sel_scores(first, width, g):
        io = iotas[width]
        j_row = jnp.full((1, width * SLC_LEN), -1, jnp.int32)
        for slot in range(width):
            valid = first + slot < n_sel[g]
            word = jnp.minimum((first + slot) // IDS_PER_WORD, n_word - 1)
            jb = (list_ref[tile_g[g] * n_word + word] >> (8 * (slot % IDS_PER_WORD))) & 0xFF
            rows = pl.ds(pl.multiple_of(jb * SLC_LEN, SLC_LEN), SLC_LEN)
            ksel_ref[g, slot * SLC_LEN:(slot + 1) * SLC_LEN, :] = ks_ref[0, rows, lanes[g]]
            vsel_ref[g, slot * SLC_LEN:(slot + 1) * SLC_LEN, :] = vs_ref[0, rows, lanes[g]]
            j_row = jnp.where(io['slot_of_lane'] == slot, jnp.where(valid, jb, -1), j_row)
        s_all = _dot_nt(qs[g], ksel_ref[g, 0:width * SLC_LEN, :])
        chosen = _dot(sel_g[g], (io['j_iota'] == j_row).astype(BF16))
        dist = io['t_sel'] - (j_row * SLC_LEN + io['lane'] % SLC_LEN)
        ok = (chosen > 0.5) & (dist >= 0)
        return s_all, ok, dist.astype(F32)

    def sel_softmax(scores, carry, g):
        s_all, ok, dist = scores
        m_old, l_old, acc = carry
        ps, ms, ls = [], [], []
        for h in range(NSA_HPG):
            rows_h = slice(h * qbl, (h + 1) * qbl)
            s = jnp.where(ok, s_all[rows_h] - _head_slope(g, h) * dist, NEG_INF)
            m_new = jnp.maximum(m_old[rows_h], jnp.max(s, axis=-1, keepdims=True))
            p = jnp.exp(s - m_new)
            alpha = jnp.exp(m_old[rows_h] - m_new)
            ls.append(alpha * l_old[rows_h] + jnp.sum(p, axis=-1, keepdims=True))
            ms.append(m_new)
            ps.append(p.astype(BF16))
        m_new = jnp.concatenate(ms, axis=0)
        alpha = jnp.exp(m_old - m_new)
        acc = alpha * acc + _dot(jnp.concatenate(ps, axis=0), vsel_ref[g, 0:s_all.shape[1], :])
        return m_new, jnp.concatenate(ls, axis=0), acc

    win_rows = pl.ds(pl.multiple_of(win_start, qbl), N_WIN_KEYS)
    s_win = [_dot_nt(qs[g], kw_ref[0, win_rows, lanes[g]]) for g in groups]
    sc0 = [sel_scores(0, FIRST_CHUNK[g], g) for g in groups]
    p_win, inv_win = [], []
    for g in groups:
        ps, inv = [], []
        for h in range(NSA_HPG):
            s = jnp.where(ok_win, s_win[g][h * qbl:(h + 1) * qbl] - _head_slope(g, h) * dist_win, NEG_INF)
            e = jnp.exp(s - jnp.max(s, axis=-1, keepdims=True))
            inv.append(1.0 / jnp.sum(e, axis=-1, keepdims=True))
            ps.append(e.astype(BF16))
        p_win.append(jnp.concatenate(ps, axis=0))
        inv_win.append(jnp.concatenate(inv, axis=0))
    init = (jnp.full((NSA_HPG * qbl, 1), M_INIT, F32), jnp.zeros((NSA_HPG * qbl, 1), F32),
            jnp.zeros((NSA_HPG * qbl, d), F32))
    carry = [sel_softmax(sc0[g], init, g) for g in groups]
    o_win = [_dot(p_win[g], vw_ref[0, win_rows, lanes[g]]) * inv_win[g] for g in groups]

    for g in groups:
        n_rest = (jnp.maximum(n_sel[g] - FIRST_CHUNK[g], 0) + REST_CHUNK - 1) // REST_CHUNK
        rest = lax.fori_loop(0, n_rest, lambda c, cr, g=g: sel_softmax(
            sel_scores(FIRST_CHUNK[g] + c * REST_CHUNK, REST_CHUNK, g), cr, g), carry[g])
        o_sel = rest[2] * (1.0 / rest[1])
        for h in range(NSA_HPG):
            hh = g * NSA_HPG + h
            col = hh * d
            rows_h = slice(h * qbl, (h + 1) * qbl)
            out_ref[0, :, col:col + d] = (gates[:, hh:hh + 1] * oc_ref[0, :, col:col + d]
                                          + gates[:, NSA_HEADS + hh:NSA_HEADS + hh + 1] * o_sel[rows_h]
                                          + gates[:, 2 * NSA_HEADS + hh:2 * NSA_HEADS + hh + 1] * o_win[g][rows_h])


def nsa_attend(counts, lists, q3, ks, vs, kw, vw, sel, o_cmp, u3, col_gate):
    b, s, _ = q3.shape
    n_qt = s // NSA_QBLOCK
    chunks = FIRST_CHUNK + (REST_CHUNK,)
    assert col_gate % 128 == 0 and s >= N_WIN_KEYS and all(c % IDS_PER_WORD == 0 for c in chunks)
    gb = col_gate // 128
    tile = lambda c: pl.BlockSpec((1, NSA_QBLOCK, c), lambda i, t, counts, lists: (i, t, 0))
    per_b = pl.BlockSpec((1, s, KV_LANES), lambda i, t, counts, lists: (i, 0, 0))
    grid_spec = pltpu.PrefetchScalarGridSpec(
        num_scalar_prefetch=2,
        grid=(b, n_qt),
        in_specs=[tile(NSA_DIM), per_b, per_b, per_b, per_b, tile(sel.shape[2]), tile(NSA_DIM),
                  pl.BlockSpec((1, NSA_QBLOCK, 128), lambda i, t, counts, lists: (i, t, gb))],
        out_specs=tile(NSA_DIM),
        scratch_shapes=[pltpu.VMEM((NSA_KV_GROUPS, max(chunks) * SLC_LEN, NSA_HEAD_DIM), BF16),
                        pltpu.VMEM((NSA_KV_GROUPS, max(chunks) * SLC_LEN, NSA_HEAD_DIM), BF16)],
    )
    return pl.pallas_call(
        _nsa_attend_kernel,
        grid_spec=grid_spec,
        out_shape=jax.ShapeDtypeStruct((b, s, NSA_DIM), F32),
        compiler_params=pltpu.CompilerParams(dimension_semantics=("arbitrary", "arbitrary")),
        name="nsa_attend",
    )(counts, lists, q3, ks, vs, kw, vw, sel, o_cmp, u3)


def _pack_union_lists(flags, n_slc):
    assert n_slc <= 256 and n_slc % IDS_PER_WORD == 0
    b, n_qt = flags.shape[:2]
    f = flags[:, :, 0, :].reshape(b, n_qt, NSA_KV_GROUPS, n_slc) > 0.5
    fi = f.astype(jnp.int32)
    counts = jnp.sum(fi, axis=-1)
    ids = jnp.arange(n_slc, dtype=jnp.int32)
    ahead = (fi.astype(F32) @ (ids[:, None] <= ids[None, :]).astype(F32)).astype(jnp.int32)
    pos = jnp.where(f, ahead - 1, counts[..., None] + ids - ahead)
    order = jnp.sum(jnp.where(pos[..., :, None] == ids, ids[:, None], 0), axis=-2)
    order = order.reshape(b, n_qt, NSA_KV_GROUPS, n_slc // IDS_PER_WORD, IDS_PER_WORD)
    words = jnp.sum(order << (8 * jnp.arange(IDS_PER_WORD, dtype=jnp.int32)), axis=-1, dtype=jnp.int32)
    return counts.reshape(-1), words.reshape(-1)


def nsa_mixer_pallas(u3, qk_gain, cmp_pe, cmp_w1, cmp_w2, col_q, col_kv, col_gate):
    b, s, d_in = u3.shape
    q, ks, vs, kw, vw = nsa_prep(u3.reshape(b * s, d_in), qk_gain, col_q, col_kv)
    k_cmp, v_cmp = nsa_compress(u3, qk_gain, cmp_pe, cmp_w1, cmp_w2, col_kv)
    q3 = q.reshape(b, s, NSA_DIM)
    r3 = lambda a: a.reshape(b, s, KV_LANES)
    o_cmp, sel, flags = nsa_select(q3, k_cmp, v_cmp)
    counts, lists = _pack_union_lists(flags, s // SLC_LEN)
    return nsa_attend(counts, lists, q3, r3(ks), r3(vs), r3(kw), r3(vw), sel, o_cmp, u3, col_gate)


GDN_TILE = 512
GDN_HALO = 8


def _dot_tn(a, b):
    return lax.dot_general(a, b, (((0,), (0,)), ((), ())), preferred_element_type=F32)


def _softplus(x):
    return jnp.maximum(x, 0.0) + jnp.log(1.0 + jnp.exp(-jnp.abs(x)))


def _l2_norm(x):
    return x * lax.rsqrt(jnp.sum(x * x, axis=-1, keepdims=True) + RMS_EPS)


def _gdn_kernel(qkv_ref, z_ref, small_ref, cw_ref, coef_ref, gain_ref, y_ref, xe_ref, state_ref):
    tt = pl.program_id(1)
    tile, c, hd = GDN_TILE, GDN_CHUNK, GDN_HEAD_DIM

    @pl.when(tt == 0)
    def _():
        xe_ref[0:GDN_HALO, :] = jnp.zeros((GDN_HALO, xe_ref.shape[1]), F32)
        state_ref[...] = jnp.zeros_like(state_ref)

    @pl.when(tt > 0)
    def _():
        xe_ref[0:GDN_HALO, :] = xe_ref[tile:tile + GDN_HALO, :]

    xe_ref[GDN_HALO:, :] = qkv_ref[0]
    conv = jnp.zeros((tile, xe_ref.shape[1]), F32)
    for j in range(GDN_CONV):
        conv += cw_ref[j:j + 1, :] * xe_ref[pl.ds(GDN_HALO - (GDN_CONV - 1) + j, tile), :]
    act = conv * jax.nn.sigmoid(conv)

    small = small_ref[0]
    beta_all = jax.nn.sigmoid(small)
    g_all = coef_ref[0:1, :] * _softplus(small + coef_ref[1:2, :])
    row = lax.broadcasted_iota(jnp.int32, (c, SMALL_LANES), 0)
    ri = lax.broadcasted_iota(jnp.int32, (c, c), 0)
    ci = lax.broadcasted_iota(jnp.int32, (c, c), 1)
    lower = ri >= ci
    strict = ri > ci

    n_chunk = tile // c
    pairs = [(n, h) for n in range(n_chunk) for h in range(GDN_HEADS)]
    gcs, gc_ts = [], []
    for n in range(n_chunk):
        gc = g_all[n * c:(n + 1) * c]
        shift = 1
        while shift < c:
            gc = gc + jnp.where(row >= shift, pltpu.roll(gc, shift, 0), 0.0)
            shift *= 2
        gcs.append(gc)
        gc_ts.append(gc.T)
    pre = []
    for n, h in pairs:
        rows = slice(n * c, (n + 1) * c)
        q = _l2_norm(act[rows, h * hd:(h + 1) * hd]) * hd ** -0.5
        k = _l2_norm(act[rows, GDN_DIM + h * hd:GDN_DIM + (h + 1) * hd])
        v = act[rows, 2 * GDN_DIM + h * hd:2 * GDN_DIM + (h + 1) * hd]
        beta = beta_all[rows, LANE_BETA + h:LANE_BETA + h + 1]
        gcol = gcs[n][:, LANE_DECAY + h:LANE_DECAY + h + 1]
        grow = gc_ts[n][LANE_DECAY + h:LANE_DECAY + h + 1, :]
        g_last = gcol[c - 1:c, :]
        decay = jnp.where(lower, jnp.exp(jnp.where(lower, gcol - grow, 0.0)), 0.0)
        kb = k * beta
        kh = k.astype(BF16)
        pre.append(dict(
            lmat=jnp.where(strict, _dot_nt(kb.astype(BF16), kh) * decay, 0.0),
            rhs=jnp.concatenate([v * beta, kb * jnp.exp(gcol)], axis=1),
            attn=(_dot_nt(q.astype(BF16), kh) * decay).astype(BF16),
            q_dec=(q * jnp.exp(gcol)).astype(BF16),
            k_dec=(k * jnp.exp(g_last - gcol)).astype(BF16),
            d_last=jnp.exp(g_last)))
    eye = (ri == ci).astype(F32)
    power = [p['lmat'].astype(BF16) for p in pre]
    t_inv = [eye - p['lmat'] for p in pre]
    for _ in range(5):
        power = [_dot(lm, lm).astype(BF16) for lm in power]
        t_inv = [t + _dot(lm, t.astype(BF16)) for lm, t in zip(power, t_inv)]
    rhs = [_dot(t.astype(BF16), p['rhs'].astype(BF16)) for t, p in zip(t_inv, pre)]

    state = [state_ref[h] for h in range(GDN_HEADS)]
    for n in range(n_chunk):
        rows = slice(n * c, (n + 1) * c)
        sb = [s.astype(BF16) for s in state]
        ps = [pre[n * GDN_HEADS + h] for h in range(GDN_HEADS)]
        rs = [rhs[n * GDN_HEADS + h] for h in range(GDN_HEADS)]
        v_new = [(r[:, :hd] - _dot(r[:, hd:].astype(BF16), s)).astype(BF16) for r, s in zip(rs, sb)]
        outs = [_dot(p['q_dec'], s) + _dot(p['attn'], vn) for p, s, vn in zip(ps, sb, v_new)]
        state = [s * p['d_last'] + _dot_tn(p['k_dec'], vn) for p, s, vn in zip(ps, state, v_new)]
        for h, o in enumerate(outs):
            o = o * lax.rsqrt(jnp.mean(o * o, axis=-1, keepdims=True) + RMS_EPS) * gain_ref[...]
            zz = z_ref[0, rows, h * hd:(h + 1) * hd]
            y_ref[0, rows, h * hd:(h + 1) * hd] = o * (zz * jax.nn.sigmoid(zz))
    for h in range(GDN_HEADS):
        state_ref[h] = state[h]


def gdn_mixer(u3, col_qkv, col_z, col_small, conv_w, a_log, dt_bias, out_gain):
    b, s, _ = u3.shape
    lane = jnp.arange(SMALL_LANES)
    in_decay = (lane >= LANE_DECAY) & (lane < LANE_DECAY + GDN_HEADS)
    idx = jnp.clip(lane - LANE_DECAY, 0, GDN_HEADS - 1)
    coef = jnp.stack([jnp.where(in_decay, -jnp.exp(a_log)[idx], 0.0), jnp.where(in_decay, dt_bias[idx], 0.0)])
    qkv_w = 3 * GDN_DIM
    assert col_qkv % qkv_w == 0 and col_z % GDN_DIM == 0 and col_small % SMALL_LANES == 0
    full = lambda r, cc: pl.BlockSpec((r, cc), lambda i, t: (0, 0))
    return pl.pallas_call(
        _gdn_kernel,
        grid=(b, s // GDN_TILE),
        in_specs=[pl.BlockSpec((1, GDN_TILE, qkv_w), lambda i, t: (i, t, col_qkv // qkv_w)),
                  pl.BlockSpec((1, GDN_TILE, GDN_DIM), lambda i, t: (i, t, col_z // GDN_DIM)),
                  pl.BlockSpec((1, GDN_TILE, SMALL_LANES), lambda i, t: (i, t, col_small // SMALL_LANES)),
                  full(GDN_CONV, qkv_w), full(2, SMALL_LANES), full(1, GDN_HEAD_DIM)],
        out_specs=pl.BlockSpec((1, GDN_TILE, GDN_DIM), lambda i, t: (i, t, 0)),
        out_shape=jax.ShapeDtypeStruct((b, s, GDN_DIM), F32),
        scratch_shapes=[pltpu.VMEM((GDN_TILE + GDN_HALO, qkv_w), F32), pltpu.VMEM((GDN_HEADS, GDN_HEAD_DIM, GDN_HEAD_DIM), F32)],
        compiler_params=pltpu.CompilerParams(dimension_semantics=("arbitrary", "arbitrary")),
        name="gdn_mixer",
    )(u3, u3, u3, conv_w, coef, out_gain.reshape(1, GDN_HEAD_DIM))


def _merge_kernel(tiles_per_seq, x_ref, g_ref, u_ref, cw_ref, yn_ref, yg_ref, wb_ref, wg_ref, bg_ref, wo_ref, o_ref, xe_ref):
    x = x_ref[...]
    tm, d = x.shape
    cd = CONV_DIM
    first_of_seq = pl.program_id(0) % tiles_per_seq == 0

    @pl.when(first_of_seq)
    def _():
        xe_ref[0:GDN_HALO, :] = jnp.zeros((GDN_HALO, cd), F32)

    @pl.when(jnp.logical_not(first_of_seq))
    def _():
        xe_ref[0:GDN_HALO, :] = xe_ref[tm:tm + GDN_HALO, :]

    xe_ref[GDN_HALO:, :] = u_ref[:, cd:2 * cd] * u_ref[:, 2 * cd:3 * cd]
    conv = jnp.zeros((tm, cd), F32)
    for j in range(CONV_WIDTH):
        conv += cw_ref[j:j + 1, :] * xe_ref[pl.ds(GDN_HALO - (CONV_WIDTH - 1) + j, tm), :]
    y_a = u_ref[:, 0:cd] * conv

    h = (x * lax.rsqrt(jnp.mean(x * x, axis=-1, keepdims=True) + RMS_EPS) * g_ref[...]).astype(BF16)
    merged = jnp.zeros(x.shape, F32)
    for r, y in enumerate((y_a, yn_ref[...], yg_ref[...])):
        gate = jax.nn.sigmoid(_dot(h, wg_ref[:, r * d:(r + 1) * d]) + bg_ref[:, r * d:(r + 1) * d])
        merged += gate * _dot(y.astype(BF16), wb_ref[r])
    o_ref[...] = x + _dot(merged.astype(BF16), wo_ref[...])


def merge_branches(x2d, seq, gain, u2d, conv_w, y_n, y_g, w_branch, w_gate, b_gate, w_out, tm=512):
    t, d = x2d.shape
    assert seq % tm == 0 and COL_CONV == 0
    row = lambda c: pl.BlockSpec((tm, c), lambda i: (i, 0))
    full = lambda *shape: pl.BlockSpec(shape, lambda i: (0,) * len(shape))
    return pl.pallas_call(
        functools.partial(_merge_kernel, seq // tm),
        grid=(t // tm,),
        in_specs=[row(d), full(1, d), row(3 * CONV_DIM), full(CONV_WIDTH, CONV_DIM), row(BRANCH_DIM), row(BRANCH_DIM),
                  full(N_BRANCH, BRANCH_DIM, d), full(d, N_BRANCH * d), full(1, N_BRANCH * d), full(d, d)],
        out_specs=row(d),
        out_shape=jax.ShapeDtypeStruct((t, d), F32),
        scratch_shapes=[pltpu.VMEM((tm + GDN_HALO, CONV_DIM), F32)],
        compiler_params=pltpu.CompilerParams(dimension_semantics=("arbitrary",), vmem_limit_bytes=VMEM_LIMIT_BIG),
        name="merge_branches",
    )(x2d, gain.reshape(1, d), u2d, conv_w, y_n, y_g, w_branch.astype(BF16), w_gate.astype(BF16), b_gate.reshape(1, -1), w_out.astype(BF16))


def hybrid_mixer(x2d, b, s, norm_gain, w_in, conv_a_w, nsa_qk_gain, cmp_pe, cmp_w1, cmp_w2, gdn_conv_w, gdn_a_log, gdn_dt_bias, gdn_out_gain, w_branch, w_gate, b_gate, w_out):
    t = b * s
    u3 = norm_proj(x2d, norm_gain, permute_in_proj(w_in)).reshape(b, s, D_U)
    y_n = nsa_mixer_pallas(u3, nsa_qk_gain, cmp_pe, cmp_w1, cmp_w2, COL_NSA_Q, COL_NSA_KV, COL_SMALL)
    y_g = gdn_mixer(u3, COL_GDN_QKV, COL_GDN_Z, COL_SMALL, gdn_conv_w, gdn_a_log, gdn_dt_bias, gdn_out_gain)
    return merge_branches(x2d, s, norm_gain, u3.reshape(t, D_U), conv_a_w, y_n.reshape(t, BRANCH_DIM), y_g.reshape(t, BRANCH_DIM), w_branch, w_gate, b_gate, w_out)


MOE_ROWS = 512
ROUTE_LANES = 128
N_ROUTER = MOE_GROUPS + N_EXPERTS
ROW_DMA_UNROLL = 8


def _moe_route_kernel(x_ref, g_ref, wr_ref, br_ref, tri_ref, h_ref, route_ref, cnt_ref, run_ref):
    @pl.when(pl.program_id(0) == 0)
    def _():
        run_ref[...] = jnp.zeros_like(run_ref)

    x = x_ref[...]
    h = x * lax.rsqrt(jnp.mean(x * x, axis=-1, keepdims=True) + RMS_EPS) * g_ref[...]
    bits = lax.bitcast_convert_type(h.astype(BF16).astype(F32), jnp.uint32)
    half = h.shape[1] // 2
    packed = (bits[:, half:] & jnp.uint32(0xFFFF0000)) | (bits[:, :half] >> 16)
    h_ref[...] = packed.reshape(h_ref.shape)
    logits = _dot(h.astype(BF16), wr_ref[...]) + br_ref[...]
    lane = lax.broadcasted_iota(jnp.int32, logits.shape, 1)
    first_of = lambda hit: jnp.min(jnp.where(hit, lane, ROUTE_LANES), axis=-1, keepdims=True)
    is_grp = lane < MOE_GROUPS
    lg = jnp.where(is_grp, logits, NEG_INF)
    m_g = jnp.max(lg, axis=-1, keepdims=True)
    grp = first_of(lg == m_g)
    p_grp = 1.0 / jnp.sum(jnp.where(is_grp, jnp.exp(lg - m_g), 0.0), axis=-1, keepdims=True)
    lo = MOE_GROUPS + grp * EXPERTS_PER_GROUP
    le = jnp.where((lane >= lo) & (lane < lo + EXPERTS_PER_GROUP), logits, NEG_INF)
    m1 = jnp.max(le, axis=-1, keepdims=True)
    i1 = first_of(le == m1)
    le2 = jnp.where(lane == i1, NEG_INF, le)
    m2 = jnp.max(le2, axis=-1, keepdims=True)
    i2 = first_of(le2 == m2)
    r = jnp.exp(m2 - m1)
    g1 = p_grp / (1.0 + r)
    g2 = p_grp * r / (1.0 + r)
    e1 = i1 - MOE_GROUPS
    e2 = i2 - MOE_GROUPS
    hit1 = lane == e1
    hit2 = lane == e2
    onehot = (hit1 | hit2).astype(BF16)
    before = _dot(tri_ref[...], onehot) + run_ref[...]
    r1 = jnp.sum(jnp.where(hit1, before, 0.0), axis=-1, keepdims=True)
    r2 = jnp.sum(jnp.where(hit2, before, 0.0), axis=-1, keepdims=True)
    run_ref[...] += jnp.sum(onehot.astype(F32), axis=0, keepdims=True)
    rec = jnp.zeros(logits.shape, F32)
    for k, v in enumerate((e1.astype(F32), e2.astype(F32), r1, r2, g1, g2)):
        rec = jnp.where(lane == k, v, rec)
    route_ref[...] = rec
    cnt_ref[...] = jnp.broadcast_to(run_ref[...], cnt_ref.shape)


def moe_route(x2d, gain, w_rg, b_rg, w_re, b_re, tm=512):
    t, d = x2d.shape
    pad = ROUTE_LANES - N_ROUTER
    wr = jnp.pad(jnp.concatenate([w_rg, w_re], axis=1), ((0, 0), (0, pad))).astype(BF16)
    br = jnp.pad(jnp.concatenate([b_rg, b_re]), (0, pad)).reshape(1, ROUTE_LANES)
    tri = (jnp.arange(tm)[:, None] > jnp.arange(tm)[None, :]).astype(BF16)
    full = lambda r, c: pl.BlockSpec((r, c), lambda i: (0, 0))
    return pl.pallas_call(
        _moe_route_kernel,
        grid=(t // tm,),
        in_specs=[pl.BlockSpec((tm, d), lambda i: (i, 0)), full(1, d), full(d, ROUTE_LANES), full(1, ROUTE_LANES), full(tm, tm)],
        out_specs=[pl.BlockSpec((tm, 1, d // 2), lambda i: (i, 0, 0)), pl.BlockSpec((tm, ROUTE_LANES), lambda i: (i, 0)), full(8, ROUTE_LANES)],
        out_shape=[jax.ShapeDtypeStruct((t, 1, d // 2), jnp.uint32), jax.ShapeDtypeStruct((t, ROUTE_LANES), F32), jax.ShapeDtypeStruct((8, ROUTE_LANES), F32)],
        scratch_shapes=[pltpu.VMEM((1, ROUTE_LANES), F32)],
        compiler_params=pltpu.CompilerParams(dimension_semantics=("arbitrary",)),
        name="moe_route",
    )(x2d, gain.reshape(1, d), wr, br, tri)


def _row_copy(src_ref, src_row, dst_ref, dst_row, sem):
    return pltpu.make_async_copy(src_ref.at[src_row], dst_ref.at[dst_row], sem)


def _moe_dispatch_kernel(dest_ref, h_ref, buf_in_ref, buf_ref, sem):
    del buf_in_ref
    tm = h_ref.shape[0]
    base = pl.program_id(0) * tm

    def send(r, carry):
        for k in range(TOPK_IN_GROUP):
            _row_copy(h_ref, r, buf_ref, dest_ref[(base + r) * TOPK_IN_GROUP + k], sem).start(priority=k)
        return carry

    lax.fori_loop(0, tm, send, 0, unroll=ROW_DMA_UNROLL)
    for k in range(TOPK_IN_GROUP):
        pltpu.make_async_copy(h_ref, buf_ref.at[pl.ds(0, tm)], sem).wait()


def moe_dispatch(dest, h3, n_rows, tm=512):
    t, _, d = h3.shape
    grid_spec = pltpu.PrefetchScalarGridSpec(
        num_scalar_prefetch=1, grid=(t // tm,),
        in_specs=[pl.BlockSpec((tm, 1, d), lambda i, dest: (i, 0, 0)), pl.BlockSpec(memory_space=pl.ANY)],
        out_specs=pl.BlockSpec(memory_space=pl.ANY),
        scratch_shapes=[pltpu.SemaphoreType.DMA(())],
    )
    return pl.pallas_call(
        _moe_dispatch_kernel, grid_spec=grid_spec,
        out_shape=jax.ShapeDtypeStruct((n_rows, 1, d), h3.dtype),
        input_output_aliases={2: 0},
        compiler_params=pltpu.CompilerParams(dimension_semantics=("arbitrary",), has_side_effects=True),
        name="moe_dispatch",
    )(dest, h3, jnp.zeros((n_rows, 1, d), h3.dtype))


def _moe_ffn_kernel(blk_e_ref, n_used_ref, x_ref, wg_ref, wu_ref, wd_ref, y_ref):
    del blk_e_ref
    used = pl.program_id(0) < n_used_ref[0]

    @pl.when(used)
    def _():
        w = x_ref[...].reshape(x_ref.shape[0], x_ref.shape[2])
        lo = lax.bitcast_convert_type(w << 16, F32)
        hi = lax.bitcast_convert_type(w & jnp.uint32(0xFFFF0000), F32)
        xb = jnp.concatenate([lo, hi], axis=1).astype(BF16)
        a = _dot(xb, wg_ref[0])
        mid = a * jax.nn.sigmoid(a) * _dot(xb, wu_ref[0])
        y_ref[...] = _dot(mid.astype(BF16), wd_ref[0]).reshape(y_ref.shape)

    @pl.when(jnp.logical_not(used))
    def _():
        y_ref[...] = jnp.zeros_like(y_ref)


def moe_ffn(blk_expert, n_used, buf3, w_eg, w_eu, w_ed):
    n_rows, _, packed_w = buf3.shape
    d, ff = w_eg.shape[1:]
    w_in_spec = pl.BlockSpec((1, d, ff), lambda b, blk_e, n_used: (blk_e[b], 0, 0))
    grid_spec = pltpu.PrefetchScalarGridSpec(
        num_scalar_prefetch=2, grid=(n_rows // MOE_ROWS,),
        in_specs=[pl.BlockSpec((MOE_ROWS, 1, packed_w), lambda b, blk_e, n_used: (b, 0, 0)), w_in_spec, w_in_spec,
                  pl.BlockSpec((1, ff, d), lambda b, blk_e, n_used: (blk_e[b], 0, 0))],
        out_specs=pl.BlockSpec((MOE_ROWS, 1, d), lambda b, blk_e, n_used: (b, 0, 0)),
    )
    return pl.pallas_call(
        _moe_ffn_kernel, grid_spec=grid_spec,
        out_shape=jax.ShapeDtypeStruct((n_rows, 1, d), F32),
        compiler_params=pltpu.CompilerParams(dimension_semantics=("arbitrary",)),
        name="moe_ffn",
    )(blk_expert, n_used, buf3, w_eg.astype(BF16), w_eu.astype(BF16), w_ed.astype(BF16))


def _moe_combine_kernel(dest_ref, y_ref, rec_ref, x_ref, out_ref, ya_ref, yb_ref, sem):
    tm = out_ref.shape[0]
    base = pl.program_id(0) * tm

    def fetch(r, carry):
        _row_copy(y_ref, dest_ref[(base + r) * TOPK_IN_GROUP], ya_ref, r, sem).start(priority=0)
        _row_copy(y_ref, dest_ref[(base + r) * TOPK_IN_GROUP + 1], yb_ref, r, sem).start(priority=1)
        return carry

    lax.fori_loop(0, tm, fetch, 0, unroll=ROW_DMA_UNROLL)
    pltpu.make_async_copy(y_ref.at[pl.ds(0, tm)], ya_ref, sem).wait()
    pltpu.make_async_copy(y_ref.at[pl.ds(0, tm)], yb_ref, sem).wait()
    rec = rec_ref[...]
    ya = ya_ref[...].reshape(out_ref.shape)
    yb = yb_ref[...].reshape(out_ref.shape)
    out_ref[...] = x_ref[...] + rec[:, 4:5] * ya + rec[:, 5:6] * yb


def moe_combine(dest, y3, rec, x2d, tm=512):
    t, d = x2d.shape
    grid_spec = pltpu.PrefetchScalarGridSpec(
        num_scalar_prefetch=1, grid=(t // tm,),
        in_specs=[pl.BlockSpec(memory_space=pl.ANY), pl.BlockSpec((tm, ROUTE_LANES), lambda i, dest: (i, 0)),
                  pl.BlockSpec((tm, d), lambda i, dest: (i, 0))],
        out_specs=pl.BlockSpec((tm, d), lambda i, dest: (i, 0)),
        scratch_shapes=[pltpu.VMEM((tm, 1, d), F32), pltpu.VMEM((tm, 1, d), F32), pltpu.SemaphoreType.DMA(())],
    )
    return pl.pallas_call(
        _moe_combine_kernel, grid_spec=grid_spec,
        out_shape=jax.ShapeDtypeStruct((t, d), F32),
        compiler_params=pltpu.CompilerParams(dimension_semantics=("arbitrary",)),
        name="moe_combine",
    )(dest, y3, rec, x2d)


def hier_moe_pallas(x2d, gain, w_rg, b_rg, w_re, b_re, w_eg, w_eu, w_ed):
    t, d = x2d.shape
    h, rec, cnt = moe_route(x2d, gain, w_rg, b_rg, w_re, b_re)
    counts = cnt[0, :N_EXPERTS].astype(jnp.int32)
    n_blk = (counts + MOE_ROWS - 1) // MOE_ROWS
    blk_end = jnp.cumsum(n_blk)
    pad_start = (blk_end - n_blk) * MOE_ROWS
    experts = rec[:, 0:2].astype(jnp.int32)
    dest = (pad_start[experts] + rec[:, 2:4].astype(jnp.int32)).reshape(-1)
    total_blk = t * TOPK_IN_GROUP // MOE_ROWS + N_EXPERTS
    blk_expert = jnp.sum(jnp.arange(total_blk)[:, None] >= blk_end[None, :], axis=1, dtype=jnp.int32)
    blk_expert = jnp.minimum(blk_expert, N_EXPERTS - 1)
    buf = moe_dispatch(dest, h, total_blk * MOE_ROWS)
    y = moe_ffn(blk_expert, blk_end[-1:].astype(jnp.int32), buf, w_eg, w_eu, w_ed)
    return moe_combine(dest, y, rec, x2d)


def kernel(x, norm_mix, w_in, conv_a_w, nsa_qk_gain, cmp_pe, cmp_w1, cmp_w2, gdn_conv_w, gdn_a_log, gdn_dt_bias, gdn_out_gain, w_branch, w_gate, b_gate, w_out, norm_ffn, w_router_group, b_router_group, w_router_expert, b_router_expert, w_expert_gate, w_expert_up, w_expert_down):
    b, s, dm = x.shape
    x = x.reshape(b * s, dm)
    for l in range(DEPTH):
        x = hybrid_mixer(x, b, s, norm_mix[l], w_in[l], conv_a_w[l], nsa_qk_gain[l], cmp_pe[l], cmp_w1[l], cmp_w2[l], gdn_conv_w[l], gdn_a_log[l], gdn_dt_bias[l], gdn_out_gain[l], w_branch[l], w_gate[l], b_gate[l], w_out[l])
        x = hier_moe_pallas(x, norm_ffn[l], w_router_group[l], b_router_group[l], w_router_expert[l], b_router_expert[l], w_expert_gate[l], w_expert_up[l], w_expert_down[l])
    return x.reshape(b, s, dm)
```
